```python
import functools
import jax, jax.numpy as jnp
from jax import lax
import numpy as np

D_MODEL = 1024
BATCH = 32
SEQ = 256
DEPTH = 2
DEC_BATCH = 4
DEC_SEQ = 4096
PAST_LEN = 512

GRID_W = 64
HEAD_DIM = 64
H_ATTN = 6
KV_ATTN = 2
H_MLSTM = 4
H_NBHD = 6
D_FF = 4 * D_MODEL
Q_BLOCK = 128
MLSTM_CHUNK = 64
NA_ROWS = 8
NA_COLS = 16
ROPE_THETA = 10000.0
EPS = 1e-6
N_MOD = 6

W_ATTN = H_ATTN * HEAD_DIM
W_KV = KV_ATTN * HEAD_DIM
W_MLSTM = H_MLSTM * HEAD_DIM
W_NBHD = H_NBHD * HEAD_DIM
MIX_WIDTH = W_ATTN + W_MLSTM + W_NBHD
N_GATES = 4 * H_MLSTM
IN_SPLITS = (W_ATTN, W_KV, W_KV, W_MLSTM, W_MLSTM, W_MLSTM, W_MLSTM, N_GATES, W_NBHD, W_NBHD, W_NBHD)
IN_WIDTH = sum(IN_SPLITS)

kernel_name = "hybrid_diffusion_gqa_mlstm_natten_step"


def rms_norm(x, g):
    xf = x.astype(jnp.float32)
    y = xf * lax.rsqrt(jnp.mean(xf * xf, axis=-1, keepdims=True) + EPS)
    return (y * g.astype(jnp.float32)).astype(x.dtype)


def split_heads(a, n_heads):
    return a.reshape(a.shape[:-1] + (n_heads, HEAD_DIM))


def split_projection(z):
    parts, start = [], 0
    for w in IN_SPLITS:
        parts.append(z[..., start:start + w])
        start += w
    return parts


def axial_rope(x, row, col):
    half = HEAD_DIM // 2
    quarter = half // 2
    inv_freq = ROPE_THETA ** (-jnp.arange(quarter, dtype=jnp.float32) / quarter)

    def rotate(xp, pos):
        ang = pos.astype(jnp.float32)[:, None] * inv_freq[None, :]
        cos = jnp.cos(ang)[None, :, None, :]
        sin = jnp.sin(ang)[None, :, None, :]
        x1, x2 = xp[..., :quarter], xp[..., quarter:]
        return jnp.concatenate([x1 * cos - x2 * sin, x2 * cos + x1 * sin], axis=-1)

    xf = x.astype(jnp.float32)
    return jnp.concatenate([rotate(xf[..., :half], row), rotate(xf[..., half:], col)], axis=-1).astype(x.dtype)


def gqa_attention(q, k, v):
    B, Sq, H, Dh = q.shape
    n_kv = k.shape[2]
    G = H // n_kv
    nb = Sq // Q_BLOCK
    qb = jnp.moveaxis(q.reshape(B, nb, Q_BLOCK, n_kv, G, Dh), 1, 0)
    scale = Dh ** -0.5

    def block(qblk):
        s = jnp.einsum('bqngd,bsnd->bngqs', qblk, k, preferred_element_type=jnp.float32) * scale
        p = jax.nn.softmax(s, axis=-1).astype(v.dtype)
        return jnp.einsum('bngqs,bsnd->bqngd', p, v)

    o = lax.map(block, qb)
    return jnp.moveaxis(o, 0, 1).reshape(B, Sq, H * Dh)


def neighbourhood_attention(q, k, v, k_ctx, v_ctx, rel_bias):
    B, S, H, Dh = q.shape
    rows = S // GRID_W
    kr = min(NA_ROWS, rows)
    scale = Dh ** -0.5
    qg = q.reshape(B, rows, GRID_W, H, Dh)
    kg = k.reshape(B, rows, GRID_W, H, Dh)
    vg = v.reshape(B, rows, GRID_W, H, Dh)
    col = jnp.arange(GRID_W)
    col_start = jnp.clip(col - NA_COLS // 2, 0, GRID_W - NA_COLS)
    col_idx = col_start[:, None] + jnp.arange(NA_COLS)[None, :]
    dc = col_idx - col[:, None] + (NA_COLS - 1)
    n_nb = kr * NA_COLS

    def one_row(args):
        q_r, r = args
        rs = jnp.clip(r - kr // 2, 0, rows - kr)
        k_slab = lax.dynamic_slice_in_dim(kg, rs, kr, axis=1)
        v_slab = lax.dynamic_slice_in_dim(vg, rs, kr, axis=1)
        k_nb = k_slab[:, :, col_idx]
        v_nb = v_slab[:, :, col_idx]
        dr = rs + jnp.arange(kr) - r + (NA_ROWS - 1)
        bias = rel_bias[:, dr][:, :, dc].transpose(0, 2, 1, 3).astype(jnp.float32)
        s_nb = jnp.einsum('bwhd,bawchd->bhwac', q_r, k_nb, preferred_element_type=jnp.float32) * scale + bias[None]
        s_ctx = jnp.einsum('bwhd,bphd->bhwp', q_r, k_ctx, preferred_element_type=jnp.float32) * scale
        s = jnp.concatenate([s_nb.reshape(B, H, GRID_W, n_nb), s_ctx], axis=-1)
        p = jax.nn.softmax(s, axis=-1).astype(v.dtype)
        p_nb = p[..., :n_nb].reshape(B, H, GRID_W, kr, NA_COLS)
        p_ctx = p[..., n_nb:]
        return (jnp.einsum('bhwac,bawchd->bwhd', p_nb, v_nb)
                + jnp.einsum('bhwp,bphd->bwhd', p_ctx, v_ctx))

    o = lax.map(one_row, (jnp.moveaxis(qg, 1, 0), jnp.arange(rows)))
    return jnp.moveaxis(o, 0, 1).reshape(B, S, H * Dh)


def mlstm_chunkwise(q, k, v, i_pre, log_f, C0, n0, m0):
    B, S, H, Dh = q.shape
    L = MLSTM_CHUNK
    nc = S // L

    def chunks(a):
        return jnp.moveaxis(a.reshape((B, nc, L) + a.shape[2:]), 1, 0)

    causal = jnp.tril(jnp.ones((L, L), dtype=bool))

    def step(carry, xs):
        C, n, m = carry
        qc, kc, vc, ic, fc = xs
        b = jnp.cumsum(fc, axis=1).transpose(0, 2, 1)
        ig = ic.transpose(0, 2, 1)
        d = jnp.where(causal, b[..., :, None] - b[..., None, :] + ig[..., None, :], -jnp.inf)
        inter = b + m[..., None]
        m_t = jnp.maximum(inter, jnp.max(d, axis=-1))
        w = jnp.einsum('blhd,bshd->bhls', qc, kc) * jnp.exp(d - m_t[..., None])
        w_inter = jnp.exp(inter - m_t)
        num = (jnp.einsum('bhls,bshd->blhd', w, vc)
               + jnp.einsum('bhl,blhk,bhkd->blhd', w_inter, qc, C))
        den = jnp.sum(w, axis=-1) + w_inter * jnp.einsum('blhk,bhk->bhl', qc, n)
        den = jnp.maximum(jnp.abs(den), jnp.exp(-m_t))
        h = num / den.transpose(0, 2, 1)[..., None]
        b_last = b[..., -1]
        g = b_last[..., None] - b + ig
        m_new = jnp.maximum(b_last + m, jnp.max(g, axis=-1))
        a = jnp.exp(g - m_new[..., None])
        decay = jnp.exp(b_last + m - m_new)
        C_new = decay[..., None, None] * C + jnp.einsum('bhs,bshk,bshd->bhkd', a, kc, vc)
        n_new = decay[..., None] * n + jnp.einsum('bhs,bshk->bhk', a, kc)
        return (C_new, n_new, m_new), h

    (C, n, m), h = lax.scan(step, (C0, n0, m0),
                            (chunks(q), chunks(k), chunks(v), chunks(i_pre), chunks(log_f)))
    return jnp.moveaxis(h, 0, 1).reshape(B, S, H, Dh), C, n, m


def mlstm_bidirectional(q, k, v, gates, C0, n0, m0):
    f32 = jnp.float32
    q = q.astype(f32)
    k = k.astype(f32) * HEAD_DIM ** -0.5
    v = v.astype(f32)
    g = gates.astype(f32).reshape(gates.shape[:-1] + (4, H_MLSTM))
    i_f, lf_f = g[..., 0, :], jax.nn.log_sigmoid(g[..., 1, :])
    i_b, lf_b = g[..., 2, :], jax.nn.log_sigmoid(g[..., 3, :])
    C0, n0, m0 = C0.astype(f32), n0.astype(f32), m0.astype(f32)
    h_f, Cf, nf, mf = mlstm_chunkwise(q, k, v, i_f, lf_f, C0[:, 0], n0[:, 0], m0[:, 0])
    rev = lambda a: jnp.flip(a, axis=1)
    h_b, Cb, nb, mb = mlstm_chunkwise(rev(q), rev(k), rev(v), rev(i_b), rev(lf_b), C0[:, 1], n0[:, 1], m0[:, 1])
    return (h_f + rev(h_b), jnp.stack([Cf, Cb], axis=1), jnp.stack([nf, nb], axis=1),
            jnp.stack([mf, mb], axis=1))


def mlstm_output(h, o_pre, out_norm):
    hn = rms_norm(h, out_norm)
    return (jax.nn.sigmoid(o_pre.astype(jnp.float32)) * hn.reshape(hn.shape[:2] + (W_MLSTM,))).astype(o_pre.dtype)


def context_mixer(h, w_in, q_norm, k_norm, gate_bias, out_norm, w_out):
    B = h.shape[0]
    qa, ka, va, qb, kb, vb, ob, gb, qc, kc, vc = split_projection(h @ w_in)
    qa = rms_norm(split_heads(qa, H_ATTN), q_norm)
    ka = rms_norm(split_heads(ka, KV_ATTN), k_norm)
    va = split_heads(va, KV_ATTN)
    out_a = gqa_attention(qa, ka, va)
    zC = jnp.zeros((B, 2, H_MLSTM, HEAD_DIM, HEAD_DIM), jnp.float32)
    zn = jnp.zeros((B, 2, H_MLSTM, HEAD_DIM), jnp.float32)
    zm = jnp.zeros((B, 2, H_MLSTM), jnp.float32)
    hb, C, n, m = mlstm_bidirectional(split_heads(qb, H_MLSTM), split_heads(kb, H_MLSTM),
                                      split_heads(vb, H_MLSTM), gb + gate_bias, zC, zn, zm)
    out_b = mlstm_output(hb, ob, out_norm)
    kc = split_heads(kc, H_NBHD)
    vc = split_heads(vc, H_NBHD)
    out_c = gqa_attention(split_heads(qc, H_NBHD), kc, vc)
    out = jnp.concatenate([out_a, out_b, out_c], axis=-1) @ w_out
    return out, (ka, va, kc, vc, C, n, m)


def latent_mixer(h, ck_a, cv_a, ck_c, cv_c, sC, sn, sm, w_in, q_norm, k_norm, gate_bias, out_norm, rel_bias, w_out):
    S = h.shape[1]
    pos = jnp.arange(S)
    row, col = pos // GRID_W, pos % GRID_W
    qa, ka, va, qb, kb, vb, ob, gb, qc, kc, vc = split_projection(h @ w_in)
    qa = axial_rope(rms_norm(split_heads(qa, H_ATTN), q_norm), row, col)
    ka = axial_rope(rms_norm(split_heads(ka, KV_ATTN), k_norm), row, col)
    out_a = gqa_attention(qa, jnp.concatenate([ka, ck_a.astype(ka.dtype)], axis=1),
                          jnp.concatenate([split_heads(va, KV_ATTN), cv_a.astype(va.dtype)], axis=1))
    hb, _, _, _ = mlstm_bidirectional(split_heads(qb, H_MLSTM), split_heads(kb, H_MLSTM),
                                      split_heads(vb, H_MLSTM), gb + gate_bias, sC, sn, sm)
    out_b = mlstm_output(hb, ob, out_norm)
    out_c = neighbourhood_attention(split_heads(qc, H_NBHD), split_heads(kc, H_NBHD), split_heads(vc, H_NBHD),
                                    ck_c.astype(qc.dtype), cv_c.astype(qc.dtype), rel_bias)
    out = jnp.concatenate([out_a, out_b, out_c], axis=-1) @ w_out
    return out, ()


def modulation(cond, w_ada, b_ada):
    return jnp.split(jax.nn.silu(cond) @ w_ada + b_ada, N_MOD, axis=-1)


def squared_relu_mlp(h, w_up, w_down):
    return jnp.square(jax.nn.relu(h @ w_up)) @ w_down


def trunk_layer(x, mods, mixer, g_pre_mix, g_post_mix, g_pre_ffn, g_post_ffn, w_up, w_down):
    sh1, sc1, gt1, sh2, sc2, gt2 = mods
    mix_out, extras = mixer(rms_norm(x, g_pre_mix) * (1 + sc1) + sh1)
    x = x + gt1 * rms_norm(mix_out, g_post_mix)
    h = rms_norm(x, g_pre_ffn) * (1 + sc2) + sh2
    x = x + gt2 * rms_norm(squared_relu_mlp(h, w_up, w_down), g_post_ffn)
    return x, extras


def setup_inputs(seed: int = 0) -> dict:
    key = jax.random.key(seed)
    ks = jax.random.split(key, 26)
    f32 = jnp.float32
    nrm = lambda k, shape, s=1.0: s * jax.random.normal(k, shape, f32)
    fbias = jnp.linspace(3.0, 6.0, H_MLSTM, dtype=f32)
    zb = jnp.zeros((H_MLSTM,), f32)
    gate_base = jnp.concatenate([zb, fbias, zb, fbias])
    return {
        "x_prompt": nrm(ks[0], (BATCH, SEQ, D_MODEL)),
        "x_sample": nrm(ks[1], (DEC_BATCH, DEC_SEQ, D_MODEL)),
        "c": nrm(ks[2], (DEC_BATCH, D_MODEL)),
        "cache_k_attn": nrm(ks[3], (DEC_BATCH, DEPTH, PAST_LEN, KV_ATTN, HEAD_DIM)),
        "cache_v_attn": nrm(ks[4], (DEC_BATCH, DEPTH, PAST_LEN, KV_ATTN, HEAD_DIM)),
        "cache_k_nbhd": nrm(ks[5], (DEC_BATCH, DEPTH, PAST_LEN, H_NBHD, HEAD_DIM)),
        "cache_v_nbhd": nrm(ks[6], (DEC_BATCH, DEPTH, PAST_LEN, H_NBHD, HEAD_DIM)),
        "state_mlstm_C": nrm(ks[7], (DEC_BATCH, DEPTH, 2, H_MLSTM, HEAD_DIM, HEAD_DIM), 0.1),
        "state_mlstm_n": nrm(ks[8], (DEC_BATCH, DEPTH, 2, H_MLSTM, HEAD_DIM), 0.1),
        "state_mlstm_m": nrm(ks[9], (DEC_BATCH, DEPTH, 2, H_MLSTM)),
        "c_ctx": nrm(ks[10], (D_MODEL,)),
        "w_ada": nrm(ks[11], (DEPTH, D_MODEL, N_MOD * D_MODEL), 0.5 * D_MODEL ** -0.5),
        "b_ada": nrm(ks[12], (DEPTH, N_MOD * D_MODEL), 0.01),
        "g_pre_mix": 1.0 + nrm(ks[13], (DEPTH, D_MODEL), 0.02),
        "g_post_mix": 1.0 + nrm(ks[14], (DEPTH, D_MODEL), 0.02),
        "g_pre_ffn": 1.0 + nrm(ks[15], (DEPTH, D_MODEL), 0.02),
        "g_post_ffn": 1.0 + nrm(ks[16], (DEPTH, D_MODEL), 0.02),
        "w_in": nrm(ks[17], (DEPTH, D_MODEL, IN_WIDTH), D_MODEL ** -0.5),
        "q_norm_attn": 1.0 + nrm(ks[18], (DEPTH, HEAD_DIM), 0.02),
        "k_norm_attn": 1.0 + nrm(ks[19], (DEPTH, HEAD_DIM), 0.02),
        "mlstm_gate_bias": gate_base[None, :] + nrm(ks[20], (DEPTH, N_GATES), 0.1),
        "mlstm_out_norm": 1.0 + nrm(ks[21], (DEPTH, H_MLSTM, HEAD_DIM), 0.02),
        "nbhd_rel_bias": nrm(ks[22], (DEPTH, H_NBHD, 2 * NA_ROWS - 1, 2 * NA_COLS - 1), 0.1),
        "w_out": nrm(ks[23], (DEPTH, MIX_WIDTH, D_MODEL), MIX_WIDTH ** -0.5),
        "w_ffn_up": nrm(ks[24], (DEPTH, D_MODEL, D_FF), D_MODEL ** -0.5),
        "w_ffn_down": nrm(ks[25], (DEPTH, D_FF, D_MODEL), D_FF ** -0.5),
    }


def reference(x_prompt, x_sample, c, cache_k_attn, cache_v_attn, cache_k_nbhd, cache_v_nbhd,
              state_mlstm_C, state_mlstm_n, state_mlstm_m, c_ctx, w_ada, b_ada,
              g_pre_mix, g_post_mix, g_pre_ffn, g_post_ffn, w_in, q_norm_attn, k_norm_attn,
              mlstm_gate_bias, mlstm_out_norm, nbhd_rel_bias, w_out, w_ffn_up, w_ffn_down):
    xp = x_prompt
    ctx = []
    for l in range(DEPTH):
        mods = modulation(c_ctx, w_ada[l], b_ada[l])
        mixer = functools.partial(context_mixer, w_in=w_in[l], q_norm=q_norm_attn[l], k_norm=k_norm_attn[l],
                                  gate_bias=mlstm_gate_bias[l], out_norm=mlstm_out_norm[l], w_out=w_out[l])
        xp, extras = trunk_layer(xp, mods, mixer, g_pre_mix[l], g_post_mix[l], g_pre_ffn[l], g_post_ffn[l],
                                 w_ffn_up[l], w_ffn_down[l])
        ctx.append(extras)
    new_k_attn = jnp.stack([e[0] for e in ctx], axis=1)
    new_v_attn = jnp.stack([e[1] for e in ctx], axis=1)
    new_k_nbhd = jnp.stack([e[2] for e in ctx], axis=1)
    new_v_nbhd = jnp.stack([e[3] for e in ctx], axis=1)
    new_mlstm_C = jnp.stack([e[4] for e in ctx], axis=1)
    new_mlstm_n = jnp.stack([e[5] for e in ctx], axis=1)
    new_mlstm_m = jnp.stack([e[6] for e in ctx], axis=1)

    xs = x_sample
    for l in range(DEPTH):
        mods = [m[:, None, :] for m in modulation(c, w_ada[l], b_ada[l])]
        mixer = functools.partial(latent_mixer, ck_a=cache_k_attn[:, l], cv_a=cache_v_attn[:, l],
                                  ck_c=cache_k_nbhd[:, l], cv_c=cache_v_nbhd[:, l],
                                  sC=state_mlstm_C[:, l], sn=state_mlstm_n[:, l], sm=state_mlstm_m[:, l],
                                  w_in=w_in[l], q_norm=q_norm_attn[l], k_norm=k_norm_attn[l],
                                  gate_bias=mlstm_gate_bias[l], out_norm=mlstm_out_norm[l],
                                  rel_bias=nbhd_rel_bias[l], w_out=w_out[l])
        xs, _ = trunk_layer(xs, mods, mixer, g_pre_mix[l], g_post_mix[l], g_pre_ffn[l], g_post_ffn[l],
                            w_ffn_up[l], w_ffn_down[l])

    return (xp, xs, new_k_attn, new_v_attn, new_k_nbhd, new_v_nbhd, new_mlstm_C, new_mlstm_n, new_mlstm_m)
```

```python
import functools

import jax
import jax.numpy as jnp
import numpy as np
from jax import lax
from jax.experimental import pallas as pl
from jax.experimental.pallas import tpu as pltpu

F32 = jnp.float32
BF16 = jnp.bfloat16

D_MODEL = 1024
DEPTH = 2
GRID_W = 64
HEAD_DIM = 64
H_ATTN = 6
KV_ATTN = 2
H_MLSTM = 4
H_NBHD = 6
D_FF = 4 * D_MODEL
NA_ROWS = 8
NA_COLS = 16
ROPE_THETA = 10000.0
EPS = 1e-6
N_MOD = 6
W_ATTN = H_ATTN * HEAD_DIM
W_KV = KV_ATTN * HEAD_DIM
W_MLSTM = H_MLSTM * HEAD_DIM
W_NBHD = H_NBHD * HEAD_DIM
N_GATES = 4 * H_MLSTM

LANES = 128
V7X_VMEM_BYTES = 64 * 1024 * 1024
VMEM_CAP_BYTES = 56 * 1024 * 1024

TM_PROJ = 512
TM_FFN = 1024
TF_FFN = 1024
TQ_ATTN = 256
CK_ATTN = 512
L_CHUNK = 128
NB_GROUP = 8
NB_SLAB = 16
NEG = -1e30

_COLS = {}
_off = 0
for _name, _w in (("qa", W_ATTN), ("ka", W_KV), ("va", W_KV), ("qb", W_MLSTM), ("kb", W_MLSTM),
                  ("vb", W_MLSTM), ("ob", W_MLSTM), ("gf", LANES), ("gb", LANES),
                  ("qc", W_NBHD), ("kc", W_NBHD), ("vc", W_NBHD)):
    _COLS[_name] = (_off, _off + _w)
    _off += _w
IN_PAD = _off


def _vmem_limit(nbytes):
    return int(min(max(nbytes, 16 * 1024 * 1024), VMEM_CAP_BYTES))


def _dot(a, b):
    return jnp.dot(a, b, preferred_element_type=F32)


def _dot_nt(a, b):
    return lax.dot_general(a, b, (((1,), (1,)), ((), ())), preferred_element_type=F32)


def _lane_lo(shape):
    return (lax.broadcasted_iota(jnp.int32, shape, len(shape) - 1) % LANES) < HEAD_DIM


def _rms(x, g):
    ms = jnp.mean(x * x, axis=-1, keepdims=True)
    return (x * lax.rsqrt(ms + EPS)) * g


def _pair_rms(x, g):
    lo = _lane_lo(x.shape)
    x2 = x * x
    s_lo = jnp.sum(jnp.where(lo, x2, 0.0), axis=-1, keepdims=True)
    s_hi = jnp.sum(jnp.where(lo, 0.0, x2), axis=-1, keepdims=True)
    r = jnp.where(lo, lax.rsqrt(s_lo / HEAD_DIM + EPS), lax.rsqrt(s_hi / HEAD_DIM + EPS))
    return (x * r) * g


def _sigmoid(x):
    return 1.0 / (1.0 + jnp.exp(-x))


def _mods_kernel(c_ref, w_ref, b_ref, o_ref):
    c = c_ref[...]
    s = (c * _sigmoid(c)).astype(BF16)
    o_ref[0] = _dot(s, w_ref[0].astype(BF16)) + b_ref[0]


def _modulation(cond, w_ada, b_ada):
    n = cond.shape[0]
    tn = D_MODEL
    return pl.pallas_call(
        _mods_kernel,
        grid=(DEPTH, N_MOD * D_MODEL // tn),
        in_specs=[pl.BlockSpec((n, D_MODEL), lambda l, j: (0, 0)),
                  pl.BlockSpec((1, D_MODEL, tn), lambda l, j: (l, 0, j)),
                  pl.BlockSpec((1, 1, tn), lambda l, j: (l, 0, j))],
        out_specs=pl.BlockSpec((1, n, tn), lambda l, j: (l, 0, j)),
        out_shape=jax.ShapeDtypeStruct((DEPTH, n, N_MOD * D_MODEL), F32),
        compiler_params=pltpu.CompilerParams(
            dimension_semantics=("arbitrary", "arbitrary"),
            vmem_limit_bytes=_vmem_limit(4 * D_MODEL * tn * 4)),
        name="modulation",
    )(cond, w_ada, b_ada.reshape(DEPTH, 1, N_MOD * D_MODEL))


def _inproj_kernel(*refs, rope):
    if rope:
        (x_ref, mod_ref, g_ref, w_ref, qn_ref, kn_ref, gbias_ref, cos_ref, sin_ref,
         qa_ref, ka_ref, va_ref, qb_ref, kb_ref, vb_ref, ob_ref, gf_ref, gb_ref,
         qc_ref, kc_ref, vc_ref) = refs
    else:
        (x_ref, mod_ref, g_ref, w_ref, qn_ref, kn_ref, gbias_ref,
         qa_ref, ka_ref, va_ref, qb_ref, kb_ref, vb_ref, ob_ref, gf_ref, gb_ref,
         qc_ref, kc_ref, vc_ref) = refs
    x = x_ref[...]
    mod = mod_ref[0]
    sh1 = mod[:, 0:D_MODEL]
    sc1 = mod[:, D_MODEL:2 * D_MODEL]
    hb = (_rms(x, g_ref[...]) * (1.0 + sc1) + sh1).astype(BF16)

    def proj(name, j=0, w=None):
        lo, hi = _COLS[name]
        lo = lo + j
        hi = hi if w is None else lo + w
        return _dot(hb, w_ref[:, lo:hi])

    scale = HEAD_DIM ** -0.5

    def rotary(t):
        first = (lax.broadcasted_iota(jnp.int32, t.shape, 1) % 32) < 16
        partner = jnp.where(first, pltpu.roll(t, LANES - 16, 1), pltpu.roll(t, 16, 1))
        return t * cos_ref[...] + partner * sin_ref[...]

    for j in range(W_ATTN // LANES):
        t = _pair_rms(proj("qa", j * LANES, LANES), qn_ref[...])
        if rope:
            t = rotary(t)
        qa_ref[:, j * LANES:(j + 1) * LANES] = (t * scale).astype(qa_ref.dtype)
    t = _pair_rms(proj("ka"), kn_ref[...])
    if rope:
        t = rotary(t)
    ka_ref[...] = t.astype(ka_ref.dtype)
    va_ref[...] = proj("va").astype(va_ref.dtype)
    qb_ref[...] = proj("qb").astype(qb_ref.dtype)
    kb_ref[...] = (proj("kb") * scale).astype(kb_ref.dtype)
    vb_ref[...] = proj("vb").astype(vb_ref.dtype)
    ob_ref[...] = proj("ob").astype(ob_ref.dtype)
    for name, out, j in (("gf", gf_ref, 0), ("gb", gb_ref, 1)):
        gt = proj(name) + gbias_ref[:, j * LANES:(j + 1) * LANES]
        lane = lax.broadcasted_iota(jnp.int32, gt.shape, 1)
        is_f = (lane >= H_MLSTM) & (lane < 2 * H_MLSTM)
        logsig = jnp.minimum(gt, 0.0) - jnp.log1p(jnp.exp(-jnp.abs(gt)))
        out[...] = jnp.where(is_f, logsig, gt)
    qc_ref[...] = (proj("qc") * scale).astype(qc_ref.dtype)
    kc_ref[...] = proj("kc").astype(kc_ref.dtype)
    vc_ref[...] = proj("vc").astype(vc_ref.dtype)


def _inproj(x, mods, g_pre, w_in_p, qn, kn, gbias, rope_tabs, *, rows_per_cond, kv_dtype, name):
    T = x.shape[0]
    tm = TM_PROJ
    bpc = rows_per_cond // tm
    rope = rope_tabs is not None
    row = lambda i: (i, 0)
    const = lambda i: (0, 0)
    in_specs = [pl.BlockSpec((tm, D_MODEL), row),
                pl.BlockSpec((1, 1, N_MOD * D_MODEL), lambda i: (i // bpc, 0, 0)),
                pl.BlockSpec((1, D_MODEL), const),
                pl.BlockSpec((D_MODEL, IN_PAD), const),
                pl.BlockSpec((1, LANES), const),
                pl.BlockSpec((1, LANES), const),
                pl.BlockSpec((1, 2 * LANES), const)]
    args = [x, mods, g_pre, w_in_p, qn, kn, gbias]
    if rope:
        nblk = rope_tabs[0].shape[0] // tm
        in_specs += [pl.BlockSpec((tm, LANES), lambda i: (i % nblk, 0))] * 2
        args += list(rope_tabs)
    widths = [("qa", W_ATTN, BF16), ("ka", W_KV, kv_dtype), ("va", W_KV, kv_dtype),
              ("qb", W_MLSTM, BF16), ("kb", W_MLSTM, BF16), ("vb", W_MLSTM, BF16),
              ("ob", W_MLSTM, F32), ("gf", LANES, F32), ("gb", LANES, F32),
              ("qc", W_NBHD, BF16), ("kc", W_NBHD, kv_dtype), ("vc", W_NBHD, kv_dtype)]
    out_specs = [pl.BlockSpec((tm, w), row) for _, w, _ in widths]
    out_shape = [jax.ShapeDtypeStruct((T, w), dt) for _, w, dt in widths]
    est = 2 * (tm * D_MODEL * 4 + D_MODEL * IN_PAD * 2 + tm * IN_PAD * 4) + 3 * tm * IN_PAD * 4
    outs = pl.pallas_call(
        functools.partial(_inproj_kernel, rope=rope),
        grid=(T // tm,),
        in_specs=in_specs, out_specs=out_specs, out_shape=out_shape,
        compiler_params=pltpu.CompilerParams(dimension_semantics=("arbitrary",),
                                             vmem_limit_bytes=_vmem_limit(est)),
        name=name,
    )(*args)
    return dict(zip([n for n, _, _ in widths], outs))


def _attn_kernel(q_ref, k_ref, v_ref, o_ref, s_scr, *, groups, tq, sk, ck):
    nch = sk // ck
    outs = {}
    for heads, kg, kh in groups:
        pieces = []
        for h in heads:
            qg = q_ref[0, :, (h // 2) * LANES:(h // 2 + 1) * LANES].astype(F32)
            if h % 2 != kh:
                qg = pltpu.roll(qg, HEAD_DIM, 1)
            keep = _lane_lo(qg.shape) if kh == 0 else jnp.logical_not(_lane_lo(qg.shape))
            pieces.append(jnp.where(keep, qg, 0.0).astype(BF16))
        qs = pieces[0] if len(pieces) == 1 else jnp.concatenate(pieces, axis=0)
        m_rows = len(heads) * tq
        ksl = slice(kg * LANES, (kg + 1) * LANES)

        def scores(j, m, qs=qs, m_rows=m_rows, ksl=ksl):
            r0 = pl.multiple_of(j * ck, ck)
            s = _dot_nt(qs, k_ref[0, pl.ds(r0, ck), ksl].astype(BF16))
            s_scr[j, 0:m_rows, :] = s
            return jnp.maximum(m, jnp.max(s, axis=1, keepdims=True))

        m = lax.fori_loop(0, nch, scores, jnp.full((m_rows, 1), -jnp.inf, F32))

        def weighted(j, carry, m=m, m_rows=m_rows, ksl=ksl):
            l, acc = carry
            r0 = pl.multiple_of(j * ck, ck)
            e = jnp.exp(s_scr[j, 0:m_rows, :] - m)
            acc = acc + _dot(e.astype(BF16), v_ref[0, pl.ds(r0, ck), ksl].astype(BF16))
            return l + jnp.sum(e, axis=1, keepdims=True), acc

        l, acc = lax.fori_loop(0, nch, weighted,
                               (jnp.zeros((m_rows, 1), F32), jnp.zeros((m_rows, LANES), F32)))
        o = acc / l
        for gi, h in enumerate(heads):
            outs[h] = (o[gi * tq:(gi + 1) * tq], kh)
    n_heads = len(outs)
    for t in range(n_heads // 2):
        (oe, khe), (oo, kho) = outs[2 * t], outs[2 * t + 1]
        if khe != 0:
            oe = pltpu.roll(oe, HEAD_DIM, 1)
        if kho != 1:
            oo = pltpu.roll(oo, HEAD_DIM, 1)
        o_ref[0, :, t * LANES:(t + 1) * LANES] = jnp.where(_lane_lo(oe.shape), oe, oo).astype(o_ref.dtype)


def _attention(q, k, v, *, groups, name):
    B, Sq, W = q.shape
    Sk, KW = k.shape[1], k.shape[2]
    tq = min(TQ_ATTN, Sq)
    ck = min(CK_ATTN, Sk)
    m_max = max(len(g[0]) for g in groups) * tq
    est = (2 * (2 * tq * W * 2 + 2 * Sk * KW * k.dtype.itemsize) + m_max * Sk * 4
           + 6 * m_max * ck * 4)
    return pl.pallas_call(
        functools.partial(_attn_kernel, groups=groups, tq=tq, sk=Sk, ck=ck),
        grid=(B, Sq // tq),
        in_specs=[pl.BlockSpec((1, tq, W), lambda b, i: (b, i, 0)),
                  pl.BlockSpec((1, Sk, KW), lambda b, i: (b, 0, 0)),
                  pl.BlockSpec((1, Sk, KW), lambda b, i: (b, 0, 0))],
        out_specs=pl.BlockSpec((1, tq, W), lambda b, i: (b, i, 0)),
        out_shape=jax.ShapeDtypeStruct((B, Sq, W), BF16),
        scratch_shapes=[pltpu.VMEM((Sk // ck, m_max, ck), F32)],
        compiler_params=pltpu.CompilerParams(dimension_semantics=("arbitrary", "arbitrary"),
                                             vmem_limit_bytes=_vmem_limit(est)),
        name=name,
    )(q, k, v)


_GROUPS_GQA = (((0, 1, 2), 0, 0), ((3, 4, 5), 0, 1))
_GROUPS_MHA = tuple(((h,), h // 2, h % 2) for h in range(H_NBHD))


def _nbhd_window(r, rows):
    kr = min(NA_ROWS, rows)
    return min(max(r - kr // 2, 0), rows - kr), kr


def _nbhd_patterns(rows):
    n_groups = rows // NB_GROUP
    pats = []
    for g in (0, 1, n_groups - 1):
        r0 = g * NB_GROUP
        pats.append((r0, min(max(r0 - NA_ROWS // 2, 0), rows - NB_SLAB)))
    return pats


def _nbhd_kernel(rb_ref, q_ref, k_ref, v_ref, kc_ref, vc_ref, o_ref, bias_scr, *, rows):
    hp = pl.program_id(1)
    g = pl.program_id(2)
    n_groups = rows // NB_GROUP
    n_dr = 2 * NA_ROWS - 1
    n_dc = 2 * NA_COLS - 1
    tq = NB_GROUP * GRID_W
    pats = _nbhd_patterns(rows)

    @pl.when(g == 0)
    def _build_bias():
        shape = (GRID_W, LANES)
        w = lax.broadcasted_iota(jnp.int32, shape, 0)
        lane = lax.broadcasted_iota(jnp.int32, shape, 1)
        cc = lane % GRID_W
        second = lane >= GRID_W
        cs = jnp.clip(w - NA_COLS // 2, 0, GRID_W - NA_COLS)
        col_ok = (cc >= cs) & (cc < cs + NA_COLS)
        dc = cc - w + (NA_COLS - 1)
        for hh in range(2):
            base = (2 * hp + hh) * (n_dr * n_dc)
            tiles = {}
            for d in range(-1, n_dr):
                acc = jnp.zeros(shape, F32)
                for j in range(n_dc):
                    va = rb_ref[base + d * n_dc + j] if d >= 0 else 0.0
                    vb = rb_ref[base + (d + 1) * n_dc + j] if d + 1 < n_dr else 0.0
                    acc = acc + jnp.where(dc == j, jnp.where(second, vb, va), 0.0)
                tiles[d] = acc
            for pi, (r0, slab0) in enumerate(pats):
                for i in range(NB_GROUP):
                    r = r0 + i
                    rs, kr = _nbhd_window(r, rows)
                    for ap in range(NB_SLAB // 2):
                        kra = slab0 + 2 * ap
                        ok_a = rs <= kra < rs + kr
                        ok_b = rs <= kra + 1 < rs + kr
                        if not (ok_a or ok_b):
                            blk = jnp.full(shape, NEG, F32)
                        else:
                            d = kra - r + (NA_ROWS - 1)
                            row_ok = (jnp.logical_not(second) if ok_a and not ok_b else
                                      second if ok_b and not ok_a else None)
                            ok = col_ok if row_ok is None else (col_ok & row_ok)
                            blk = jnp.where(ok, tiles[d], NEG)
                        bias_scr[hh, pi, i * GRID_W:(i + 1) * GRID_W, ap * LANES:(ap + 1) * LANES] = blk

    pat = jnp.where(g == 0, 0, jnp.where(g == n_groups - 1, 2, 1))
    slab0 = jnp.clip(g * NB_GROUP - NA_ROWS // 2, 0, rows - NB_SLAB)
    t0 = pl.multiple_of(slab0 * GRID_W, GRID_W)
    nk = NB_SLAB * GRID_W
    kslab = k_ref[0, pl.ds(t0, nk), :]
    vslab = v_ref[0, pl.ds(t0, nk), :]
    kctx = kc_ref[0].astype(BF16)
    vctx = vc_ref[0].astype(BF16)
    q = q_ref[0].astype(F32)
    lo = _lane_lo(q.shape)
    outs = []
    for hh in range(2):
        keep = lo if hh == 0 else jnp.logical_not(lo)
        qm = jnp.where(keep, q, 0.0).astype(BF16)
        s_nb = _dot_nt(qm, kslab) + bias_scr[hh, pat]
        s_cx = _dot_nt(qm, kctx)
        m = jnp.maximum(jnp.max(s_nb, axis=1, keepdims=True), jnp.max(s_cx, axis=1, keepdims=True))
        e_nb = jnp.exp(s_nb - m)
        e_cx = jnp.exp(s_cx - m)
        l = jnp.sum(e_nb, axis=1, keepdims=True) + jnp.sum(e_cx, axis=1, keepdims=True)
        acc = _dot(e_nb.astype(BF16), vslab) + _dot(e_cx.astype(BF16), vctx)
        outs.append(acc / l)
    o_ref[0] = jnp.where(lo, outs[0], outs[1]).astype(o_ref.dtype)


def _nbhd_attention(q, k, v, k_ctx, v_ctx, rel_bias_flat, *, name):
    B, S, W = q.shape
    P = k_ctx.shape[1]
    rows = S // GRID_W
    tq = NB_GROUP * GRID_W
    nk = NB_SLAB * GRID_W
    est = (2 * (2 * tq * LANES * 2 + 2 * S * LANES * 2 + 2 * P * LANES * 4)
           + 2 * 3 * tq * nk * 4 + 8 * tq * (nk + P) * 4)
    return pl.pallas_call(
        functools.partial(_nbhd_kernel, rows=rows),
        grid=(B, W // LANES, rows // NB_GROUP),
        in_specs=[pl.BlockSpec(memory_space=pltpu.SMEM),
                  pl.BlockSpec((1, tq, LANES), lambda b, p, g: (b, g, p)),
                  pl.BlockSpec((1, S, LANES), lambda b, p, g: (b, 0, p)),
                  pl.BlockSpec((1, S, LANES), lambda b, p, g: (b, 0, p)),
                  pl.BlockSpec((1, P, LANES), lambda b, p, g: (b, 0, p)),
                  pl.BlockSpec((1, P, LANES), lambda b, p, g: (b, 0, p))],
        out_specs=pl.BlockSpec((1, tq, LANES), lambda b, p, g: (b, g, p)),
        out_shape=jax.ShapeDtypeStruct((B, S, W), BF16),
        scratch_shapes=[pltpu.VMEM((2, 3, tq, nk), F32)],
        compiler_params=pltpu.CompilerParams(dimension_semantics=("arbitrary",) * 3,
                                             vmem_limit_bytes=_vmem_limit(est)),
        name=name,
    )(rel_bias_flat, q, k, v, k_ctx, v_ctx)


def _mlstm_kernel(q_ref, k_ref, v_ref, g_ref, ob_ref, c0_ref, n0_ref, m0_ref, on_ref,
                  out_ref, cf_ref, nf_ref, mf_ref, hf_scr, c_scr, n_scr, m_scr, *, nc):
    d = pl.program_id(1)
    L = L_CHUNK
    row = lax.broadcasted_iota(jnp.int32, (L, L), 0)
    col = lax.broadcasted_iota(jnp.int32, (L, L), 1)
    sign = 1 - 2 * d
    mask = (col - row) * sign <= 0
    maskf = mask.astype(F32)
    lo = _lane_lo((L, LANES))
    same_head = (row < HEAD_DIM) == (col < HEAD_DIM)

    c_scr[...] = c0_ref[0, 0]
    n_scr[...] = n0_ref[0, 0]
    m_scr[...] = m0_ref[0, 0]

    def chunk(ci, carry):
        c = jnp.where(d == 0, ci, nc - 1 - ci)
        r0 = pl.multiple_of(c * L, L)
        rows = pl.ds(r0, L)
        gt = g_ref[0, 0, rows, :]
        bn = jnp.dot(maskf, gt, precision=lax.Precision.HIGHEST, preferred_element_type=F32)
        gtT = gt.T
        bnT = bn.T
        b_end = jnp.where(d == 0, bn[L - 1:L, :], bn[0:1, :])
        for p in range(H_MLSTM // 2):
            lanes = slice(p * LANES, (p + 1) * LANES)
            qp = q_ref[0, rows, lanes]
            kp = k_ref[0, rows, lanes]
            vp = v_ref[0, rows, lanes]
            c_prev = c_scr[p]
            n_prev = n_scr[p]
            q_c = _dot(qp, c_prev.astype(BF16))
            q_n = qp.astype(F32) * n_prev
            h_half, a_half, decays = [], [], []
            for hh in range(2):
                h = 2 * p + hh
                keep = lo if hh == 0 else jnp.logical_not(lo)
                b_col = bn[:, H_MLSTM + h:H_MLSTM + h + 1]
                i_col = gt[:, h:h + 1]
                b_row = bnT[H_MLSTM + h:H_MLSTM + h + 1, :]
                i_row = gtT[h:h + 1, :]
                m_prev = m_scr[h][0:1, 0:1]
                dmat = jnp.where(mask, b_col - b_row + i_row, -jnp.inf)
                inter = b_col + m_prev
                m_t = jnp.maximum(inter, jnp.max(dmat, axis=1, keepdims=True))
                qm = jnp.where(keep, qp, jnp.zeros_like(qp))
                w = _dot_nt(qm, kp) * jnp.exp(dmat - m_t)
                num = _dot(w.astype(BF16), vp)
                w_inter = jnp.exp(inter - m_t)
                qn_h = jnp.sum(jnp.where(keep, q_n, 0.0), axis=1, keepdims=True)
                den = jnp.sum(w, axis=1, keepdims=True) + w_inter * qn_h
                den = jnp.maximum(jnp.abs(den), jnp.exp(-m_t))
                h_half.append((num + w_inter * q_c) / den)
                b_last = b_end[:, H_MLSTM + h:H_MLSTM + h + 1]
                g_row = b_last - b_row + i_row
                m_new = jnp.maximum(b_last + m_prev, jnp.max(g_row, axis=1, keepdims=True))
                a_half.append(jnp.exp(b_last - b_col + i_col - m_new))
                decays.append(jnp.exp(b_last + m_prev - m_new))
                m_scr[h] = jnp.broadcast_to(m_new, (8, LANES))
            h_pair = jnp.where(lo, h_half[0], h_half[1])
            ka = kp.astype(F32) * jnp.where(lo, a_half[0], a_half[1])
            upd = _dot(ka.T.astype(BF16), vp)
            dec_rows = jnp.where(row < HEAD_DIM, decays[0], decays[1])
            c_scr[p] = dec_rows * c_prev + jnp.where(same_head, upd, 0.0)
            dec_lanes = jnp.where(lo[0:1], decays[0], decays[1])
            n_scr[p] = dec_lanes * n_prev + jnp.sum(ka, axis=0, keepdims=True)

            @pl.when(d == 0)
            def _store_fwd():
                hf_scr[rows, lanes] = h_pair

            @pl.when(d == 1)
            def _finish():
                tot = hf_scr[rows, lanes] + h_pair
                hn = _pair_rms(tot, on_ref[:, lanes])
                out_ref[0, rows, lanes] = (_sigmoid(ob_ref[0, rows, lanes]) * hn).astype(out_ref.dtype)
        return carry

    lax.fori_loop(0, nc, chunk, 0)
    cf_ref[0, 0] = c_scr[...]
    nf_ref[0, 0] = n_scr[...]
    mf_ref[0, 0] = m_scr[...]


def _mlstm(q, k, v, gates, ob, c0, n0, m0, out_norm, *, name):
    B, S, W = q.shape
    nc = S // L_CHUNK
    seq = lambda b, d: (b, 0, 0)
    st5 = lambda b, d: (b, d, 0, 0, 0)
    est = (2 * (3 * S * W * 2 + S * LANES * 4 + S * W * 4 + S * W * 2) + S * W * 4
           + 64 * L_CHUNK * LANES * 4 + 8 * 1024 * 1024)
    return pl.pallas_call(
        functools.partial(_mlstm_kernel, nc=nc),
        grid=(B, 2),
        in_specs=[pl.BlockSpec((1, S, W), seq), pl.BlockSpec((1, S, W), seq), pl.BlockSpec((1, S, W), seq),
                  pl.BlockSpec((1, 1, S, LANES), lambda b, d: (b, d, 0, 0)),
                  pl.BlockSpec((1, S, W), seq),
                  pl.BlockSpec((1, 1, 2, LANES, LANES), st5),
                  pl.BlockSpec((1, 1, 2, 1, LANES), st5),
                  pl.BlockSpec((1, 1, H_MLSTM, 8, LANES), st5),
                  pl.BlockSpec((1, W), lambda b, d: (0, 0))],
        out_specs=[pl.BlockSpec((1, S, W), seq),
                   pl.BlockSpec((1, 1, 2, LANES, LANES), st5),
                   pl.BlockSpec((1, 1, 2, 1, LANES), st5),
                   pl.BlockSpec((1, 1, H_MLSTM, 8, LANES), st5)],
        out_shape=[jax.ShapeDtypeStruct((B, S, W), BF16),
                   jax.ShapeDtypeStruct((B, 2, 2, LANES, LANES), F32),
                   jax.ShapeDtypeStruct((B, 2, 2, 1, LANES), F32),
                   jax.ShapeDtypeStruct((B, 2, H_MLSTM, 8, LANES), F32)],
        scratch_shapes=[pltpu.VMEM((S, W), F32),
                        pltpu.VMEM((2, LANES, LANES), F32),
                        pltpu.VMEM((2, 1, LANES), F32),
                        pltpu.VMEM((H_MLSTM, 8, LANES), F32)],
        compiler_params=pltpu.CompilerParams(dimension_semantics=("arbitrary", "arbitrary"),
                                             vmem_limit_bytes=_vmem_limit(est)),
        name=name,
    )(q, k, v, gates, ob, c0, n0, m0, out_norm)


def _outproj_kernel(a_ref, b_ref, c_ref, x_ref, mod_ref, w_ref, gpost_ref, gpre_ref, x1_ref, h2_ref):
    mo = (_dot(a_ref[...], w_ref[0:W_ATTN, :])
          + _dot(b_ref[...], w_ref[W_ATTN:W_ATTN + W_MLSTM, :])
          + _dot(c_ref[...], w_ref[W_ATTN + W_MLSTM:, :]))
    mod = mod_ref[0]
    gt1 = mod[:, 2 * D_MODEL:3 * D_MODEL]
    sh2 = mod[:, 3 * D_MODEL:4 * D_MODEL]
    sc2 = mod[:, 4 * D_MODEL:5 * D_MODEL]
    x1 = x_ref[...] + gt1 * _rms(mo, gpost_ref[...])
    x1_ref[...] = x1
    h2_ref[...] = (_rms(x1, gpre_ref[...]) * (1.0 + sc2) + sh2).astype(h2_ref.dtype)


def _outproj(oa, ob, oc, x, mods, w_out, g_post, g_pre, *, rows_per_cond, name):
    T = x.shape[0]
    tm = TM_PROJ
    bpc = rows_per_cond // tm
    row = lambda i: (i, 0)
    const = lambda i: (0, 0)
    est = 2 * (tm * D_MODEL * (2 + 4 + 4 + 2) + D_MODEL * D_MODEL * 2) + 4 * tm * D_MODEL * 4
    return pl.pallas_call(
        _outproj_kernel,
        grid=(T // tm,),
        in_specs=[pl.BlockSpec((tm, W_ATTN), row), pl.BlockSpec((tm, W_MLSTM), row),
                  pl.BlockSpec((tm, W_NBHD), row), pl.BlockSpec((tm, D_MODEL), row),
                  pl.BlockSpec((1, 1, N_MOD * D_MODEL), lambda i: (i // bpc, 0, 0)),
                  pl.BlockSpec((D_MODEL, D_MODEL), const),
                  pl.BlockSpec((1, D_MODEL), const), pl.BlockSpec((1, D_MODEL), const)],
        out_specs=[pl.BlockSpec((tm, D_MODEL), row), pl.BlockSpec((tm, D_MODEL), row)],
        out_shape=[jax.ShapeDtypeStruct((T, D_MODEL), F32), jax.ShapeDtypeStruct((T, D_MODEL), BF16)],
        compiler_params=pltpu.CompilerParams(dimension_semantics=("arbitrary",),
                                             vmem_limit_bytes=_vmem_limit(est)),
        name=name,
    )(oa, ob, oc, x, mods, w_out, g_post, g_pre)


def _ffn_kernel(h_ref, x_ref, mod_ref, wu_ref, wd_ref, g_ref, o_ref, acc_ref):
    j = pl.program_id(1)

    @pl.when(j == 0)
    def _zero():
        acc_ref[...] = jnp.zeros_like(acc_ref)

    u = jnp.maximum(_dot(h_ref[...], wu_ref[...]), 0.0)
    acc_ref[...] += _dot((u * u).astype(BF16), wd_ref[...])

    @pl.when(j == pl.num_programs(1) - 1)
    def _finish():
        gt2 = mod_ref[0][:, 5 * D_MODEL:6 * D_MODEL]
        o_ref[...] = x_ref[...] + gt2 * _rms(acc_ref[...], g_ref[...])


def _ffn(h2, x1, mods, w_up, w_down, g_post, *, rows_per_cond, name):
    T = x1.shape[0]
    tm, tf = TM_FFN, TF_FFN
    tm = min(tm, rows_per_cond)
    bpc = rows_per_cond // tm
    est = 2 * (tm * D_MODEL * (2 + 4 + 4) + 2 * D_MODEL * tf * 2) + tm * D_MODEL * 4 + 3 * tm * tf * 4
    return pl.pallas_call(
        _ffn_kernel,
        grid=(T // tm, D_FF // tf),
        in_specs=[pl.BlockSpec((tm, D_MODEL), lambda i, j: (i, 0)),
                  pl.BlockSpec((tm, D_MODEL), lambda i, j: (i, 0)),
                  pl.BlockSpec((1, 1, N_MOD * D_MODEL), lambda i, j: (i // bpc, 0, 0)),
                  pl.BlockSpec((D_MODEL, tf), lambda i, j: (0, j)),
                  pl.BlockSpec((tf, D_MODEL), lambda i, j: (j, 0)),
                  pl.BlockSpec((1, D_MODEL), lambda i, j: (0, 0))],
        out_specs=pl.BlockSpec((tm, D_MODEL), lambda i, j: (i, 0)),
        out_shape=jax.ShapeDtypeStruct((T, D_MODEL), F32),
        scratch_shapes=[pltpu.VMEM((tm, D_MODEL), F32)],
        compiler_params=pltpu.CompilerParams(dimension_semantics=("arbitrary", "arbitrary"),
                                             vmem_limit_bytes=_vmem_limit(est)),
        name=name,
    )(h2, x1, mods, w_up, w_down, g_post)


def _pad_w_in(w_in_l):
    o = W_ATTN + 2 * W_KV + 4 * W_MLSTM
    pre, gates, post = w_in_l[:, :o], w_in_l[:, o:o + N_GATES], w_in_l[:, o + N_GATES:]
    z = jnp.zeros((D_MODEL, LANES - 2 * H_MLSTM), w_in_l.dtype)
    return jnp.concatenate([pre, gates[:, :2 * H_MLSTM], z, gates[:, 2 * H_MLSTM:], z, post],
                           axis=1).astype(BF16)


def _pad_gate_bias(gb_l):
    z = jnp.zeros((LANES - 2 * H_MLSTM,), gb_l.dtype)
    return jnp.concatenate([gb_l[:2 * H_MLSTM], z, gb_l[2 * H_MLSTM:], z]).reshape(1, 2 * LANES)


def _rope_tables(S):
    quarter = HEAD_DIM // 4
    pos = jnp.arange(S)
    inv_freq = ROPE_THETA ** (-jnp.arange(quarter, dtype=F32) / quarter)

    def tabs(p):
        ang = p.astype(F32)[:, None] * inv_freq[None, :]
        return jnp.cos(ang), jnp.sin(ang)

    cr, sr = tabs(pos // GRID_W)
    cc, sc = tabs(pos % GRID_W)
    cos = jnp.concatenate([cr, cr, cc, cc], axis=1)
    sin = jnp.concatenate([-sr, sr, -sc, sc], axis=1)
    return jnp.tile(cos, (1, 2)), jnp.tile(sin, (1, 2))


def _pack_state(C, n, m):
    B = C.shape[0]
    Cp = C.reshape(B, 2, 2, 2, HEAD_DIM, HEAD_DIM)
    z = jnp.zeros_like(Cp[:, :, :, 0])
    top = jnp.concatenate([Cp[:, :, :, 0], z], axis=-1)
    bot = jnp.concatenate([z, Cp[:, :, :, 1]], axis=-1)
    Cbd = jnp.concatenate([top, bot], axis=-2)
    n_p = n.reshape(B, 2, 2, 1, LANES)
    m_p = jnp.broadcast_to(m[..., None, None], m.shape + (8, LANES))
    return Cbd, n_p, m_p


def _unpack_state(Cbd, n_p, m_p):
    B = Cbd.shape[0]
    c_even = Cbd[:, :, :, :HEAD_DIM, :HEAD_DIM]
    c_odd = Cbd[:, :, :, HEAD_DIM:, HEAD_DIM:]
    C = jnp.stack([c_even, c_odd], axis=3).reshape(B, 2, H_MLSTM, HEAD_DIM, HEAD_DIM)
    return C, n_p.reshape(B, 2, H_MLSTM, HEAD_DIM), m_p[..., 0, 0]


def _layer(x, mods, lw, *, B, S, cond_rows, rope_tabs, ctx_cache, state, name):
    T = B * S
    kv_dtype = BF16 if ctx_cache is not None else F32
    pr = _inproj(x, mods, lw["g_pre_mix"], lw["w_in"], lw["q_norm"], lw["k_norm"], lw["gate_bias"],
                 rope_tabs, rows_per_cond=cond_rows, kv_dtype=kv_dtype, name=name + "_inproj")
    seq = lambda a: a.reshape(B, S, a.shape[-1])
    qa, ka, va = seq(pr["qa"]), seq(pr["ka"]), seq(pr["va"])
    qc, kc, vc = seq(pr["qc"]), seq(pr["kc"]), seq(pr["vc"])
    if ctx_cache is None:
        out_a = _attention(qa, ka, va, groups=_GROUPS_GQA, name=name + "_attn_a")
        out_c = _attention(qc, kc, vc, groups=_GROUPS_MHA, name=name + "_attn_c")
    else:
        ck_a, cv_a, ck_c, cv_c = ctx_cache
        k_all = jnp.concatenate([ka, ck_a.astype(BF16)], axis=1)
        v_all = jnp.concatenate([va, cv_a.astype(BF16)], axis=1)
        out_a = _attention(qa, k_all, v_all, groups=_GROUPS_GQA, name=name + "_attn_a")
        out_c = _nbhd_attention(qc, kc, vc, ck_c, cv_c, lw["rel_bias"], name=name + "_attn_c")
    gates = jnp.stack([seq(pr["gf"]), seq(pr["gb"])], axis=1)
    c0, n0, m0 = state
    out_b, cf, nf, mf = _mlstm(seq(pr["qb"]), seq(pr["kb"]), seq(pr["vb"]), gates, seq(pr["ob"]),
                               c0, n0, m0, lw["out_norm"], name=name + "_mlstm")
    x1, h2 = _outproj(out_a.reshape(T, W_ATTN), out_b.reshape(T, W_MLSTM), out_c.reshape(T, W_NBHD),
                      x, mods, lw["w_out"], lw["g_post_mix"], lw["g_pre_ffn"],
                      rows_per_cond=cond_rows, name=name + "_outproj")
    x2 = _ffn(h2, x1, mods, lw["w_up"], lw["w_down"], lw["g_post_ffn"],
              rows_per_cond=cond_rows, name=name + "_ffn")
    return x2, (ka, va, kc, vc, cf, nf, mf)


def kernel(x_prompt, x_sample, c, cache_k_attn, cache_v_attn, cache_k_nbhd, cache_v_nbhd, state_mlstm_C, state_mlstm_n, state_mlstm_m, c_ctx, w_ada, b_ada, g_pre_mix, g_post_mix, g_pre_ffn, g_post_ffn, w_in, q_norm_attn, k_norm_attn, mlstm_gate_bias, mlstm_out_norm, nbhd_rel_bias, w_out, w_ffn_up, w_ffn_down):
    Bc, Sc, _ = x_prompt.shape
    Bl, Sl, _ = x_sample.shape
    P = cache_k_attn.shape[2]
    n_cond = 8
    cond = jnp.concatenate([c_ctx[None, :], c, jnp.zeros((n_cond - 1 - Bl, D_MODEL), F32)], axis=0)
    mods_all = _modulation(cond, w_ada, b_ada)

    layers = []
    for l in range(DEPTH):
        layers.append(dict(
            w_in=_pad_w_in(w_in[l]),
            w_out=w_out[l].astype(BF16),
            w_up=w_ffn_up[l].astype(BF16),
            w_down=w_ffn_down[l].astype(BF16),
            g_pre_mix=g_pre_mix[l].reshape(1, D_MODEL), g_post_mix=g_post_mix[l].reshape(1, D_MODEL),
            g_pre_ffn=g_pre_ffn[l].reshape(1, D_MODEL), g_post_ffn=g_post_ffn[l].reshape(1, D_MODEL),
            q_norm=jnp.tile(q_norm_attn[l], 2).reshape(1, LANES),
            k_norm=jnp.tile(k_norm_attn[l], 2).reshape(1, LANES),
            gate_bias=_pad_gate_bias(mlstm_gate_bias[l]),
            out_norm=mlstm_out_norm[l].reshape(1, W_MLSTM),
            rel_bias=nbhd_rel_bias[l].reshape(-1),
        ))

    xp = x_prompt.reshape(Bc * Sc, D_MODEL)
    zero_state = _pack_state(jnp.zeros((Bc, 2, H_MLSTM, HEAD_DIM, HEAD_DIM), F32),
                             jnp.zeros((Bc, 2, H_MLSTM, HEAD_DIM), F32),
                             jnp.zeros((Bc, 2, H_MLSTM), F32))
    ctx = []
    for l in range(DEPTH):
        mods = mods_all[l, 0:1].reshape(1, 1, N_MOD * D_MODEL)
        xp, extras = _layer(xp, mods, layers[l], B=Bc, S=Sc, cond_rows=Bc * Sc, rope_tabs=None,
                            ctx_cache=None, state=zero_state, name=f"ctx{l}")
        ctx.append(extras)
    new_k_attn = jnp.stack([e[0].reshape(Bc, Sc, KV_ATTN, HEAD_DIM) for e in ctx], axis=1)
    new_v_attn = jnp.stack([e[1].reshape(Bc, Sc, KV_ATTN, HEAD_DIM) for e in ctx], axis=1)
    new_k_nbhd = jnp.stack([e[2].reshape(Bc, Sc, H_NBHD, HEAD_DIM) for e in ctx], axis=1)
    new_v_nbhd = jnp.stack([e[3].reshape(Bc, Sc, H_NBHD, HEAD_DIM) for e in ctx], axis=1)
    states = [_unpack_state(e[4], e[5], e[6]) for e in ctx]
    new_C = jnp.stack([s[0] for s in states], axis=1)
    new_n = jnp.stack([s[1] for s in states], axis=1)
    new_m = jnp.stack([s[2] for s in states], axis=1)

    xs = x_sample.reshape(Bl * Sl, D_MODEL)
    rope_tabs = _rope_tables(Sl)
    for l in range(DEPTH):
        mods = mods_all[l, 1:1 + Bl].reshape(Bl, 1, N_MOD * D_MODEL)
        cache = (cache_k_attn[:, l].reshape(Bl, P, W_KV), cache_v_attn[:, l].reshape(Bl, P, W_KV),
                 cache_k_nbhd[:, l].reshape(Bl, P, W_NBHD), cache_v_nbhd[:, l].reshape(Bl, P, W_NBHD))
        state = _pack_state(state_mlstm_C[:, l], state_mlstm_n[:, l], state_mlstm_m[:, l])
        xs, _ = _layer(xs, mods, layers[l], B=Bl, S=Sl, cond_rows=Sl, rope_tabs=rope_tabs,
                       ctx_cache=cache, state=state, name=f"lat{l}")

    return (xp.reshape(Bc, Sc, D_MODEL), xs.reshape(Bl, Sl, D_MODEL),
            new_k_attn, new_v_attn, new_k_nbhd, new_v_nbhd, new_C, new_n, new_m)
```

```python
import functools

import jax
import jax.numpy as jnp
import numpy as np
from jax import lax
from jax.experimental import pallas as pl
from jax.experimental.pallas import tpu as pltpu

F32 = jnp.float32
BF16 = jnp.bfloat16

D_MODEL = 1024
DEPTH = 2
GRID_W = 64
HEAD_DIM = 64
H_ATTN = 6
KV_ATTN = 2
H_MLSTM = 4
H_NBHD = 6
D_FF = 4 * D_MODEL
NA_ROWS = 8
NA_COLS = 16
ROPE_THETA = 10000.0
EPS = 1e-6
N_MOD = 6
W_ATTN = H_ATTN * HEAD_DIM
W_KV = KV_ATTN * HEAD_DIM
W_MLSTM = H_MLSTM * HEAD_DIM
W_NBHD = H_NBHD * HEAD_DIM
N_GATES = 4 * H_MLSTM

LANES = 128
V7X_VMEM_BYTES = 64 * 1024 * 1024
VMEM_CAP_BYTES = 56 * 1024 * 1024

TM_PROJ = 512
TM_FFN = 1024
TF_FFN = 1024
TQ_ATTN = 512
CK_ATTN = 512
L_CHUNK = 128
NB_GROUP = 8
NB_SLAB = 16
NEG = -1e30
LOG2E = 1.4426950408889634

_COLS = {}
_off = 0
for _name, _w in (("qa", W_ATTN), ("ka", W_KV), ("va", W_KV), ("qb", W_MLSTM), ("kb", W_MLSTM),
                  ("vb", W_MLSTM), ("ob", W_MLSTM), ("gf", LANES), ("gb", LANES),
                  ("qc", W_NBHD), ("kc", W_NBHD), ("vc", W_NBHD)):
    _COLS[_name] = (_off, _off + _w)
    _off += _w
IN_PAD = _off


def _vmem_limit(nbytes):
    return int(min(max(nbytes, 16 * 1024 * 1024), VMEM_CAP_BYTES))


def _dot(a, b):
    return jnp.dot(a, b, preferred_element_type=F32)


def _dot_nt(a, b):
    return lax.dot_general(a, b, (((1,), (1,)), ((), ())), preferred_element_type=F32)


def _lane_lo(shape):
    return (lax.broadcasted_iota(jnp.int32, shape, len(shape) - 1) % LANES) < HEAD_DIM


def _rms(x, g):
    ms = jnp.mean(x * x, axis=-1, keepdims=True)
    return (x * lax.rsqrt(ms + EPS)) * g


def _pair_rms(x, g):
    lo = _lane_lo(x.shape)
    x2 = x * x
    s_lo = jnp.sum(jnp.where(lo, x2, 0.0), axis=-1, keepdims=True)
    s_hi = jnp.sum(jnp.where(lo, 0.0, x2), axis=-1, keepdims=True)
    r = jnp.where(lo, lax.rsqrt(s_lo / HEAD_DIM + EPS), lax.rsqrt(s_hi / HEAD_DIM + EPS))
    return (x * r) * g


def _sigmoid(x):
    return 1.0 / (1.0 + jnp.exp(-x))


def _mods_kernel(c_ref, w_ref, b_ref, o_ref):
    c = c_ref[...]
    s = (c * _sigmoid(c)).astype(BF16)
    o_ref[0] = _dot(s, w_ref[0].astype(BF16)) + b_ref[0]


def _modulation(cond, w_ada, b_ada):
    n = cond.shape[0]
    tn = D_MODEL
    return pl.pallas_call(
        _mods_kernel,
        grid=(DEPTH, N_MOD * D_MODEL // tn),
        in_specs=[pl.BlockSpec((n, D_MODEL), lambda l, j: (0, 0)),
                  pl.BlockSpec((1, D_MODEL, tn), lambda l, j: (l, 0, j)),
                  pl.BlockSpec((1, 1, tn), lambda l, j: (l, 0, j))],
        out_specs=pl.BlockSpec((1, n, tn), lambda l, j: (l, 0, j)),
        out_shape=jax.ShapeDtypeStruct((DEPTH, n, N_MOD * D_MODEL), F32),
        compiler_params=pltpu.CompilerParams(
            dimension_semantics=("arbitrary", "arbitrary"),
            vmem_limit_bytes=_vmem_limit(4 * D_MODEL * tn * 4)),
        name="modulation",
    )(cond, w_ada, b_ada.reshape(DEPTH, 1, N_MOD * D_MODEL))


def _inproj_kernel(*refs, rope):
    if rope:
        (x_ref, mod_ref, g_ref, w_ref, qn_ref, kn_ref, gbias_ref, cos_ref, sin_ref,
         qa_ref, ka_ref, va_ref, qb_ref, kb_ref, vb_ref, ob_ref, gf_ref, gb_ref,
         qc_ref, kc_ref, vc_ref) = refs
    else:
        (x_ref, mod_ref, g_ref, w_ref, qn_ref, kn_ref, gbias_ref,
         qa_ref, ka_ref, va_ref, qb_ref, kb_ref, vb_ref, ob_ref, gf_ref, gb_ref,
         qc_ref, kc_ref, vc_ref) = refs
    x = x_ref[...]
    mod = mod_ref[0]
    sh1 = mod[:, 0:D_MODEL]
    sc1 = mod[:, D_MODEL:2 * D_MODEL]
    hb = (_rms(x, g_ref[...]) * (1.0 + sc1) + sh1).astype(BF16)

    def proj(name, j=0, w=None):
        lo, hi = _COLS[name]
        lo = lo + j
        hi = hi if w is None else lo + w
        return _dot(hb, w_ref[:, lo:hi])

    scale = HEAD_DIM ** -0.5
    q_scale = scale * LOG2E

    def rotary(t):
        first = (lax.broadcasted_iota(jnp.int32, t.shape, 1) % 32) < 16
        partner = jnp.where(first, pltpu.roll(t, LANES - 16, 1), pltpu.roll(t, 16, 1))
        return t * cos_ref[...] + partner * sin_ref[...]

    for j in range(W_ATTN // LANES):
        t = _pair_rms(proj("qa", j * LANES, LANES), qn_ref[...])
        if rope:
            t = rotary(t)
        qa_ref[:, j * LANES:(j + 1) * LANES] = (t * q_scale).astype(qa_ref.dtype)
    t = _pair_rms(proj("ka"), kn_ref[...])
    if rope:
        t = rotary(t)
    ka_ref[...] = t.astype(ka_ref.dtype)
    va_ref[...] = proj("va").astype(va_ref.dtype)
    qb_ref[...] = proj("qb").astype(qb_ref.dtype)
    kb_ref[...] = (proj("kb") * scale).astype(kb_ref.dtype)
    vb_ref[...] = proj("vb").astype(vb_ref.dtype)
    ob_ref[...] = proj("ob").astype(ob_ref.dtype)
    for name, out, j in (("gf", gf_ref, 0), ("gb", gb_ref, 1)):
        gt = proj(name) + gbias_ref[:, j * LANES:(j + 1) * LANES]
        lane = lax.broadcasted_iota(jnp.int32, gt.shape, 1)
        is_f = (lane >= H_MLSTM) & (lane < 2 * H_MLSTM)
        logsig = jnp.minimum(gt, 0.0) - jnp.log1p(jnp.exp(-jnp.abs(gt)))
        out[...] = jnp.where(is_f, logsig, gt)
    qc_ref[...] = (proj("qc") * q_scale).astype(qc_ref.dtype)
    kc_ref[...] = proj("kc").astype(kc_ref.dtype)
    vc_ref[...] = proj("vc").astype(vc_ref.dtype)


def _inproj(x, mods, g_pre, w_in_p, qn, kn, gbias, rope_tabs, *, rows_per_cond, kv_dtype, name):
    T = x.shape[0]
    tm = TM_PROJ
    bpc = rows_per_cond // tm
    rope = rope_tabs is not None
    row = lambda i: (i, 0)
    const = lambda i: (0, 0)
    in_specs = [pl.BlockSpec((tm, D_MODEL), row),
                pl.BlockSpec((1, 1, N_MOD * D_MODEL), lambda i: (i // bpc, 0, 0)),
                pl.BlockSpec((1, D_MODEL), const),
                pl.BlockSpec((D_MODEL, IN_PAD), const),
                pl.BlockSpec((1, LANES), const),
                pl.BlockSpec((1, LANES), const),
                pl.BlockSpec((1, 2 * LANES), const)]
    args = [x, mods, g_pre, w_in_p, qn, kn, gbias]
    if rope:
        nblk = rope_tabs[0].shape[0] // tm
        in_specs += [pl.BlockSpec((tm, LANES), lambda i: (i % nblk, 0))] * 2
        args += list(rope_tabs)
    widths = [("qa", W_ATTN, BF16), ("ka", W_KV, kv_dtype), ("va", W_KV, kv_dtype),
              ("qb", W_MLSTM, BF16), ("kb", W_MLSTM, BF16), ("vb", W_MLSTM, BF16),
              ("ob", W_MLSTM, F32), ("gf", LANES, F32), ("gb", LANES, F32),
              ("qc", W_NBHD, BF16), ("kc", W_NBHD, kv_dtype), ("vc", W_NBHD, kv_dtype)]
    out_specs = [pl.BlockSpec((tm, w), row) for _, w, _ in widths]
    out_shape = [jax.ShapeDtypeStruct((T, w), dt) for _, w, dt in widths]
    est = 2 * (tm * D_MODEL * 4 + D_MODEL * IN_PAD * 2 + tm * IN_PAD * 4) + 3 * tm * IN_PAD * 4
    outs = pl.pallas_call(
        functools.partial(_inproj_kernel, rope=rope),
        grid=(T // tm,),
        in_specs=in_specs, out_specs=out_specs, out_shape=out_shape,
        compiler_params=pltpu.CompilerParams(dimension_semantics=("arbitrary",),
                                             vmem_limit_bytes=_vmem_limit(est)),
        name=name,
    )(*args)
    return dict(zip([n for n, _, _ in widths], outs))


def _attn_kernel(q_ref, k_ref, vt_ref, o_ref, qs_scr, m_scr, l_scr, acc_scr, *, heads, nch):
    for h, (kg, kh) in enumerate(heads):
        qg = q_ref[0, :, (h // 2) * LANES:(h // 2 + 1) * LANES].astype(F32)
        if h % 2 != kh:
            qg = pltpu.roll(qg, HEAD_DIM, 1)
        keep = _lane_lo(qg.shape) if kh == 0 else jnp.logical_not(_lane_lo(qg.shape))
        qs_scr[h] = jnp.where(keep, qg, 0.0).astype(BF16)
    m_scr[...] = jnp.full(m_scr.shape, -jnp.inf, F32)
    l_scr[...] = jnp.zeros(l_scr.shape, F32)
    acc_scr[...] = jnp.zeros(acc_scr.shape, F32)

    def scores(j, h):
        kg = heads[h][0]
        kj = k_ref[0, j, :, kg * LANES:(kg + 1) * LANES].astype(BF16)
        return _dot_nt(kj, qs_scr[h])

    def chunk(j, carry):
        st_next = scores(j, 0)
        for h, (kg, kh) in enumerate(heads):
            st = st_next
            if h + 1 < len(heads):
                st_next = scores(j, h + 1)
            m_old = m_scr[h]
            m_new = jnp.maximum(m_old, jnp.max(st, axis=0, keepdims=True))
            e = jnp.exp2(st - m_new)
            alpha = jnp.exp2(m_old - m_new)
            l_scr[h] = alpha * l_scr[h] + jnp.sum(e, axis=0, keepdims=True)
            r = kg * LANES + kh * HEAD_DIM
            vt = vt_ref[0, j, r:r + HEAD_DIM, :].astype(BF16)
            acc_scr[h] = alpha * acc_scr[h] + _dot(vt, e.astype(BF16))
            m_scr[h] = m_new
        return carry

    lax.fori_loop(0, nch, chunk, 0)
    for t in range(len(heads) // 2):
        ot = jnp.concatenate([acc_scr[2 * t] / l_scr[2 * t], acc_scr[2 * t + 1] / l_scr[2 * t + 1]], axis=0)
        o_ref[0, :, t * LANES:(t + 1) * LANES] = ot.T.astype(o_ref.dtype)


def _attention(q, k, v, *, heads, name):
    B, Sq, W = q.shape
    Sk, KW = k.shape[1], k.shape[2]
    tq = min(TQ_ATTN, Sq)
    ck = min(CK_ATTN, Sk)
    nch = Sk // ck
    nh = len(heads)
    k4 = k.reshape(B, nch, ck, KW)
    vt4 = v.reshape(B, nch, ck, KW).transpose(0, 1, 3, 2)
    isz = k.dtype.itemsize
    est = (2 * (2 * tq * W * 2 + 2 * Sk * KW * isz) + nh * tq * (LANES * 2 + HEAD_DIM * 4 + 64)
           + 8 * nh * ck * tq * 4)
    return pl.pallas_call(
        functools.partial(_attn_kernel, heads=heads, nch=nch),
        grid=(B, Sq // tq),
        in_specs=[pl.BlockSpec((1, tq, W), lambda b, i: (b, i, 0)),
                  pl.BlockSpec((1, nch, ck, KW), lambda b, i: (b, 0, 0, 0)),
                  pl.BlockSpec((1, nch, KW, ck), lambda b, i: (b, 0, 0, 0))],
        out_specs=pl.BlockSpec((1, tq, W), lambda b, i: (b, i, 0)),
        out_shape=jax.ShapeDtypeStruct((B, Sq, W), BF16),
        scratch_shapes=[pltpu.VMEM((nh, tq, LANES), BF16),
                        pltpu.VMEM((nh, 1, tq), F32),
                        pltpu.VMEM((nh, 1, tq), F32),
                        pltpu.VMEM((nh, HEAD_DIM, tq), F32)],
        compiler_params=pltpu.CompilerParams(dimension_semantics=("arbitrary", "arbitrary"),
                                             vmem_limit_bytes=_vmem_limit(est)),
        name=name,
    )(q, k4, vt4)


_HEADS_GQA = tuple((0, h // (H_ATTN // KV_ATTN)) for h in range(H_ATTN))
_HEADS_MHA = tuple((h // 2, h % 2) for h in range(H_NBHD))


def _nbhd_window(r, rows):
    kr = min(NA_ROWS, rows)
    return min(max(r - kr // 2, 0), rows - kr), kr


def _nbhd_patterns(rows):
    n_groups = rows // NB_GROUP
    pats = []
    for g in (0, 1, n_groups - 1):
        r0 = g * NB_GROUP
        pats.append((r0, min(max(r0 - NA_ROWS // 2, 0), rows - NB_SLAB)))
    return pats


def _nbhd_kernel(rb_ref, q_ref, k_ref, v_ref, kc_ref, vc_ref, o_ref, bias_scr, *, rows):
    hp = pl.program_id(1)
    g = pl.program_id(2)
    n_groups = rows // NB_GROUP
    n_dr = 2 * NA_ROWS - 1
    n_dc = 2 * NA_COLS - 1
    tq = NB_GROUP * GRID_W
    pats = _nbhd_patterns(rows)

    @pl.when(g == 0)
    def _build_bias():
        shape = (GRID_W, LANES)
        w = lax.broadcasted_iota(jnp.int32, shape, 0)
        lane = lax.broadcasted_iota(jnp.int32, shape, 1)
        cc = lane % GRID_W
        second = lane >= GRID_W
        cs = jnp.clip(w - NA_COLS // 2, 0, GRID_W - NA_COLS)
        col_ok = (cc >= cs) & (cc < cs + NA_COLS)
        dc = cc - w + (NA_COLS - 1)
        for hh in range(2):
            base = (2 * hp + hh) * (n_dr * n_dc)
            tiles = {}
            for d in range(-1, n_dr):
                acc = jnp.zeros(shape, F32)
                for j in range(n_dc):
                    va = rb_ref[base + d * n_dc + j] * LOG2E if d >= 0 else 0.0
                    vb = rb_ref[base + (d + 1) * n_dc + j] * LOG2E if d + 1 < n_dr else 0.0
                    acc = acc + jnp.where(dc == j, jnp.where(second, vb, va), 0.0)
                tiles[d] = acc
            for pi, (r0, slab0) in enumerate(pats):
                for i in range(NB_GROUP):
                    r = r0 + i
                    rs, kr = _nbhd_window(r, rows)
                    for ap in range(NB_SLAB // 2):
                        kra = slab0 + 2 * ap
                        ok_a = rs <= kra < rs + kr
                        ok_b = rs <= kra + 1 < rs + kr
                        if not (ok_a or ok_b):
                            blk = jnp.full(shape, NEG, F32)
                        else:
                            d = kra - r + (NA_ROWS - 1)
                            row_ok = (jnp.logical_not(second) if ok_a and not ok_b else
                                      second if ok_b and not ok_a else None)
                            ok = col_ok if row_ok is None else (col_ok & row_ok)
                            blk = jnp.where(ok, tiles[d], NEG)
                        bias_scr[hh, pi, i * GRID_W:(i + 1) * GRID_W, ap * LANES:(ap + 1) * LANES] = blk

    pat = jnp.where(g == 0, 0, jnp.where(g == n_groups - 1, 2, 1))
    slab0 = jnp.clip(g * NB_GROUP - NA_ROWS // 2, 0, rows - NB_SLAB)
    t0 = pl.multiple_of(slab0 * GRID_W, GRID_W)
    nk = NB_SLAB * GRID_W
    kslab = k_ref[0, pl.ds(t0, nk), :]
    vslab = v_ref[0, pl.ds(t0, nk), :]
    kctx = kc_ref[0].astype(BF16)
    vctx = vc_ref[0].astype(BF16)
    q = q_ref[0].astype(F32)
    lo = _lane_lo(q.shape)
    outs = []
    for hh in range(2):
        keep = lo if hh == 0 else jnp.logical_not(lo)
        qm = jnp.where(keep, q, 0.0).astype(BF16)
        s_nb = _dot_nt(qm, kslab) + bias_scr[hh, pat]
        s_cx = _dot_nt(qm, kctx)
        m = jnp.maximum(jnp.max(s_nb, axis=1, keepdims=True), jnp.max(s_cx, axis=1, keepdims=True))
        e_nb = jnp.exp2(s_nb - m)
        e_cx = jnp.exp2(s_cx - m)
        l = jnp.sum(e_nb, axis=1, keepdims=True) + jnp.sum(e_cx, axis=1, keepdims=True)
        acc = _dot(e_nb.astype(BF16), vslab) + _dot(e_cx.astype(BF16), vctx)
        outs.append(acc / l)
    o_ref[0] = jnp.where(lo, outs[0], outs[1]).astype(o_ref.dtype)


def _nbhd_attention(q, k, v, k_ctx, v_ctx, rel_bias_flat, *, name):
    B, S, W = q.shape
    P = k_ctx.shape[1]
    rows = S // GRID_W
    tq = NB_GROUP * GRID_W
    nk = NB_SLAB * GRID_W
    est = (2 * (2 * tq * LANES * 2 + 2 * S * LANES * 2 + 2 * P * LANES * 4)
           + 2 * 3 * tq * nk * 4 + 8 * tq * (nk + P) * 4)
    return pl.pallas_call(
        functools.partial(_nbhd_kernel, rows=rows),
        grid=(B, W // LANES, rows // NB_GROUP),
        in_specs=[pl.BlockSpec(memory_space=pltpu.SMEM),
                  pl.BlockSpec((1, tq, LANES), lambda b, p, g: (b, g, p)),
                  pl.BlockSpec((1, S, LANES), lambda b, p, g: (b, 0, p)),
                  pl.BlockSpec((1, S, LANES), lambda b, p, g: (b, 0, p)),
                  pl.BlockSpec((1, P, LANES), lambda b, p, g: (b, 0, p)),
                  pl.BlockSpec((1, P, LANES), lambda b, p, g: (b, 0, p))],
        out_specs=pl.BlockSpec((1, tq, LANES), lambda b, p, g: (b, g, p)),
        out_shape=jax.ShapeDtypeStruct((B, S, W), BF16),
        scratch_shapes=[pltpu.VMEM((2, 3, tq, nk), F32)],
        compiler_params=pltpu.CompilerParams(dimension_semantics=("arbitrary",) * 3,
                                             vmem_limit_bytes=_vmem_limit(est)),
        name=name,
    )(rel_bias_flat, q, k, v, k_ctx, v_ctx)


def _mlstm_kernel(q_ref, k_ref, v_ref, g_ref, ob_ref, c0_ref, n0_ref, m0_ref, on_ref,
                  out_ref, cf_ref, nf_ref, mf_ref, hf_scr, c_scr, n_scr, m_scr, *, nc):
    d = pl.program_id(1)
    L = L_CHUNK
    row = lax.broadcasted_iota(jnp.int32, (L, L), 0)
    col = lax.broadcasted_iota(jnp.int32, (L, L), 1)
    sign = 1 - 2 * d
    mask = (col - row) * sign <= 0
    maskf = mask.astype(F32)
    lo = _lane_lo((L, LANES))
    same_head = (row < HEAD_DIM) == (col < HEAD_DIM)

    c_scr[...] = c0_ref[0, 0]
    n_scr[...] = n0_ref[0, 0]
    m_scr[...] = m0_ref[0, 0]

    def chunk(ci, carry):
        c = jnp.where(d == 0, ci, nc - 1 - ci)
        r0 = pl.multiple_of(c * L, L)
        rows = pl.ds(r0, L)
        gt = g_ref[0, 0, rows, :]
        bn = jnp.dot(maskf, gt, precision=lax.Precision.HIGHEST, preferred_element_type=F32)
        gtT = gt.T
        bnT = bn.T
        b_end = jnp.where(d == 0, bn[L - 1:L, :], bn[0:1, :])
        for p in range(H_MLSTM // 2):
            lanes = slice(p * LANES, (p + 1) * LANES)
            qp = q_ref[0, rows, lanes]
            kp = k_ref[0, rows, lanes]
            vp = v_ref[0, rows, lanes]
            c_prev = c_scr[p]
            n_prev = n_scr[p]
            q_c = _dot(qp, c_prev.astype(BF16))
            q_n = qp.astype(F32) * n_prev
            h_half, a_half, decays = [], [], []
            for hh in range(2):
                h = 2 * p + hh
                keep = lo if hh == 0 else jnp.logical_not(lo)
                b_col = bn[:, H_MLSTM + h:H_MLSTM + h + 1]
                i_col = gt[:, h:h + 1]
                b_row = bnT[H_MLSTM + h:H_MLSTM + h + 1, :]
                i_row = gtT[h:h + 1, :]
                m_prev = m_scr[h][0:1, 0:1]
                dmat = jnp.where(mask, b_col - b_row + i_row, -jnp.inf)
                inter = b_col + m_prev
                m_t = jnp.maximum(inter, jnp.max(dmat, axis=1, keepdims=True))
                qm = jnp.where(keep, qp, jnp.zeros_like(qp))
                w = _dot_nt(qm, kp) * jnp.exp(dmat - m_t)
                num = _dot(w.astype(BF16), vp)
                w_inter = jnp.exp(inter - m_t)
                qn_h = jnp.sum(jnp.where(keep, q_n, 0.0), axis=1, keepdims=True)
                den = jnp.sum(w, axis=1, keepdims=True) + w_inter * qn_h
                den = jnp.maximum(jnp.abs(den), jnp.exp(-m_t))
                h_half.append((num + w_inter * q_c) / den)
                b_last = b_end[:, H_MLSTM + h:H_MLSTM + h + 1]
                g_row = b_last - b_row + i_row
                m_new = jnp.maximum(b_last + m_prev, jnp.max(g_row, axis=1, keepdims=True))
                a_half.append(jnp.exp(b_last - b_col + i_col - m_new))
                decays.append(jnp.exp(b_last + m_prev - m_new))
                m_scr[h] = jnp.broadcast_to(m_new, (8, LANES))
            h_pair = jnp.where(lo, h_half[0], h_half[1])
            ka = kp.astype(F32) * jnp.where(lo, a_half[0], a_half[1])
            upd = _dot(ka.T.astype(BF16), vp)
            dec_rows = jnp.where(row < HEAD_DIM, decays[0], decays[1])
            c_scr[p] = dec_rows * c_prev + jnp.where(same_head, upd, 0.0)
            dec_lanes = jnp.where(lo[0:1], decays[0], decays[1])
            n_scr[p] = dec_lanes * n_prev + jnp.sum(ka, axis=0, keepdims=True)

            @pl.when(d == 0)
            def _store_fwd():
                hf_scr[rows, lanes] = h_pair

            @pl.when(d == 1)
            def _finish():
                tot = hf_scr[rows, lanes] + h_pair
                hn = _pair_rms(tot, on_ref[:, lanes])
                out_ref[0, rows, lanes] = (_sigmoid(ob_ref[0, rows, lanes]) * hn).astype(out_ref.dtype)
        return carry

    lax.fori_loop(0, nc, chunk, 0)
    cf_ref[0, 0] = c_scr[...]
    nf_ref[0, 0] = n_scr[...]
    mf_ref[0, 0] = m_scr[...]


def _mlstm(q, k, v, gates, ob, c0, n0, m0, out_norm, *, name):
    B, S, W = q.shape
    nc = S // L_CHUNK
    seq = lambda b, d: (b, 0, 0)
    st5 = lambda b, d: (b, d, 0, 0, 0)
    est = (2 * (3 * S * W * 2 + S * LANES * 4 + S * W * 4 + S * W * 2) + S * W * 4
           + 64 * L_CHUNK * LANES * 4 + 8 * 1024 * 1024)
    return pl.pallas_call(
        functools.partial(_mlstm_kernel, nc=nc),
        grid=(B, 2),
        in_specs=[pl.BlockSpec((1, S, W), seq), pl.BlockSpec((1, S, W), seq), pl.BlockSpec((1, S, W), seq),
                  pl.BlockSpec((1, 1, S, LANES), lambda b, d: (b, d, 0, 0)),
                  pl.BlockSpec((1, S, W), seq),
                  pl.BlockSpec((1, 1, 2, LANES, LANES), st5),
                  pl.BlockSpec((1, 1, 2, 1, LANES), st5),
                  pl.BlockSpec((1, 1, H_MLSTM, 8, LANES), st5),
                  pl.BlockSpec((1, W), lambda b, d: (0, 0))],
        out_specs=[pl.BlockSpec((1, S, W), seq),
                   pl.BlockSpec((1, 1, 2, LANES, LANES), st5),
                   pl.BlockSpec((1, 1, 2, 1, LANES), st5),
                   pl.BlockSpec((1, 1, H_MLSTM, 8, LANES), st5)],
        out_shape=[jax.ShapeDtypeStruct((B, S, W), BF16),
                   jax.ShapeDtypeStruct((B, 2, 2, LANES, LANES), F32),
                   jax.ShapeDtypeStruct((B, 2, 2, 1, LANES), F32),
                   jax.ShapeDtypeStruct((B, 2, H_MLSTM, 8, LANES), F32)],
        scratch_shapes=[pltpu.VMEM((S, W), F32),
                        pltpu.VMEM((2, LANES, LANES), F32),
                        pltpu.VMEM((2, 1, LANES), F32),
                        pltpu.VMEM((H_MLSTM, 8, LANES), F32)],
        compiler_params=pltpu.CompilerParams(dimension_semantics=("arbitrary", "arbitrary"),
                                             vmem_limit_bytes=_vmem_limit(est)),
        name=name,
    )(q, k, v, gates, ob, c0, n0, m0, out_norm)


def _outproj_kernel(a_ref, b_ref, c_ref, x_ref, mod_ref, w_ref, gpost_ref, gpre_ref, x1_ref, h2_ref):
    mo = (_dot(a_ref[...], w_ref[0:W_ATTN, :])
          + _dot(b_ref[...], w_ref[W_ATTN:W_ATTN + W_MLSTM, :])
          + _dot(c_ref[...], w_ref[W_ATTN + W_MLSTM:, :]))
    mod = mod_ref[0]
    gt1 = mod[:, 2 * D_MODEL:3 * D_MODEL]
    sh2 = mod[:, 3 * D_MODEL:4 * D_MODEL]
    sc2 = mod[:, 4 * D_MODEL:5 * D_MODEL]
    x1 = x_ref[...] + gt1 * _rms(mo, gpost_ref[...])
    x1_ref[...] = x1
    h2_ref[...] = (_rms(x1, gpre_ref[...]) * (1.0 + sc2) + sh2).astype(h2_ref.dtype)


def _outproj(oa, ob, oc, x, mods, w_out, g_post, g_pre, *, rows_per_cond, name):
    T = x.shape[0]
    tm = TM_PROJ
    bpc = rows_per_cond // tm
    row = lambda i: (i, 0)
    const = lambda i: (0, 0)
    est = 2 * (tm * D_MODEL * (2 + 4 + 4 + 2) + D_MODEL * D_MODEL * 2) + 4 * tm * D_MODEL * 4
    return pl.pallas_call(
        _outproj_kernel,
        grid=(T // tm,),
        in_specs=[pl.BlockSpec((tm, W_ATTN), row), pl.BlockSpec((tm, W_MLSTM), row),
                  pl.BlockSpec((tm, W_NBHD), row), pl.BlockSpec((tm, D_MODEL), row),
                  pl.BlockSpec((1, 1, N_MOD * D_MODEL), lambda i: (i // bpc, 0, 0)),
                  pl.BlockSpec((D_MODEL, D_MODEL), const),
                  pl.BlockSpec((1, D_MODEL), const), pl.BlockSpec((1, D_MODEL), const)],
        out_specs=[pl.BlockSpec((tm, D_MODEL), row), pl.BlockSpec((tm, D_MODEL), row)],
        out_shape=[jax.ShapeDtypeStruct((T, D_MODEL), F32), jax.ShapeDtypeStruct((T, D_MODEL), BF16)],
        compiler_params=pltpu.CompilerParams(dimension_semantics=("arbitrary",),
                                             vmem_limit_bytes=_vmem_limit(est)),
        name=name,
    )(oa, ob, oc, x, mods, w_out, g_post, g_pre)


def _ffn_kernel(h_ref, x_ref, mod_ref, wu_ref, wd_ref, g_ref, o_ref, acc_ref):
    j = pl.program_id(1)

    @pl.when(j == 0)
    def _zero():
        acc_ref[...] = jnp.zeros_like(acc_ref)

    u = jnp.maximum(_dot(h_ref[...], wu_ref[...]), 0.0)
    acc_ref[...] += _dot((u * u).astype(BF16), wd_ref[...])

    @pl.when(j == pl.num_programs(1) - 1)
    def _finish():
        gt2 = mod_ref[0][:, 5 * D_MODEL:6 * D_MODEL]
        o_ref[...] = x_ref[...] + gt2 * _rms(acc_ref[...], g_ref[...])


def _ffn(h2, x1, mods, w_up, w_down, g_post, *, rows_per_cond, name):
    T = x1.shape[0]
    tm, tf = TM_FFN, TF_FFN
    tm = min(tm, rows_per_cond)
    bpc = rows_per_cond // tm
    est = 2 * (tm * D_MODEL * (2 + 4 + 4) + 2 * D_MODEL * tf * 2) + tm * D_MODEL * 4 + 3 * tm * tf * 4
    return pl.pallas_call(
        _ffn_kernel,
        grid=(T // tm, D_FF // tf),
        in_specs=[pl.BlockSpec((tm, D_MODEL), lambda i, j: (i, 0)),
                  pl.BlockSpec((tm, D_MODEL), lambda i, j: (i, 0)),
                  pl.BlockSpec((1, 1, N_MOD * D_MODEL), lambda i, j: (i // bpc, 0, 0)),
                  pl.BlockSpec((D_MODEL, tf), lambda i, j: (0, j)),
                  pl.BlockSpec((tf, D_MODEL), lambda i, j: (j, 0)),
                  pl.BlockSpec((1, D_MODEL), lambda i, j: (0, 0))],
        out_specs=pl.BlockSpec((tm, D_MODEL), lambda i, j: (i, 0)),
        out_shape=jax.ShapeDtypeStruct((T, D_MODEL), F32),
        scratch_shapes=[pltpu.VMEM((tm, D_MODEL), F32)],
        compiler_params=pltpu.CompilerParams(dimension_semantics=("arbitrary", "arbitrary"),
                                             vmem_limit_bytes=_vmem_limit(est)),
        name=name,
    )(h2, x1, mods, w_up, w_down, g_post)


def _pad_w_in(w_in_l):
    o = W_ATTN + 2 * W_KV + 4 * W_MLSTM
    pre, gates, post = w_in_l[:, :o], w_in_l[:, o:o + N_GATES], w_in_l[:, o + N_GATES:]
    z = jnp.zeros((D_MODEL, LANES - 2 * H_MLSTM), w_in_l.dtype)
    return jnp.concatenate([pre, gates[:, :2 * H_MLSTM], z, gates[:, 2 * H_MLSTM:], z, post],
                           axis=1).astype(BF16)


def _pad_gate_bias(gb_l):
    z = jnp.zeros((LANES - 2 * H_MLSTM,), gb_l.dtype)
    return jnp.concatenate([gb_l[:2 * H_MLSTM], z, gb_l[2 * H_MLSTM:], z]).reshape(1, 2 * LANES)


def _rope_tables(S):
    quarter = HEAD_DIM // 4
    pos = jnp.arange(S)
    inv_freq = ROPE_THETA ** (-jnp.arange(quarter, dtype=F32) / quarter)

    def tabs(p):
        ang = p.astype(F32)[:, None] * inv_freq[None, :]
        return jnp.cos(ang), jnp.sin(ang)

    cr, sr = tabs(pos // GRID_W)
    cc, sc = tabs(pos % GRID_W)
    cos = jnp.concatenate([cr, cr, cc, cc], axis=1)
    sin = jnp.concatenate([-sr, sr, -sc, sc], axis=1)
    return jnp.tile(cos, (1, 2)), jnp.tile(sin, (1, 2))


def _pack_state(C, n, m):
    B = C.shape[0]
    Cp = C.reshape(B, 2, 2, 2, HEAD_DIM, HEAD_DIM)
    z = jnp.zeros_like(Cp[:, :, :, 0])
    top = jnp.concatenate([Cp[:, :, :, 0], z], axis=-1)
    bot = jnp.concatenate([z, Cp[:, :, :, 1]], axis=-1)
    Cbd = jnp.concatenate([top, bot], axis=-2)
    n_p = n.reshape(B, 2, 2, 1, LANES)
    m_p = jnp.broadcast_to(m[..., None, None], m.shape + (8, LANES))
    return Cbd, n_p, m_p


def _unpack_state(Cbd, n_p, m_p):
    B = Cbd.shape[0]
    c_even = Cbd[:, :, :, :HEAD_DIM, :HEAD_DIM]
    c_odd = Cbd[:, :, :, HEAD_DIM:, HEAD_DIM:]
    C = jnp.stack([c_even, c_odd], axis=3).reshape(B, 2, H_MLSTM, HEAD_DIM, HEAD_DIM)
    return C, n_p.reshape(B, 2, H_MLSTM, HEAD_DIM), m_p[..., 0, 0]


def _layer(x, mods, lw, *, B, S, cond_rows, rope_tabs, ctx_cache, state, name):
    T = B * S
    kv_dtype = BF16 if ctx_cache is not None else F32
    pr = _inproj(x, mods, lw["g_pre_mix"], lw["w_in"], lw["q_norm"], lw["k_norm"], lw["gate_bias"],
                 rope_tabs, rows_per_cond=cond_rows, kv_dtype=kv_dtype, name=name + "_inproj")
    seq = lambda a: a.reshape(B, S, a.shape[-1])
    qa, ka, va = seq(pr["qa"]), seq(pr["ka"]), seq(pr["va"])
    qc, kc, vc = seq(pr["qc"]), seq(pr["kc"]), seq(pr["vc"])
    if ctx_cache is None:
        out_a = _attention(qa, ka, va, heads=_HEADS_GQA, name=name + "_attn_a")
        out_c = _attention(qc, kc, vc, heads=_HEADS_MHA, name=name + "_attn_c")
    else:
        ck_a, cv_a, ck_c, cv_c = ctx_cache
        k_all = jnp.concatenate([ka, ck_a.astype(BF16)], axis=1)
        v_all = jnp.concatenate([va, cv_a.astype(BF16)], axis=1)
        out_a = _attention(qa, k_all, v_all, heads=_HEADS_GQA, name=name + "_attn_a")
        out_c = _nbhd_attention(qc, kc, vc, ck_c, cv_c, lw["rel_bias"], name=name + "_attn_c")
    gates = jnp.stack([seq(pr["gf"]), seq(pr["gb"])], axis=1)
    c0, n0, m0 = state
    out_b, cf, nf, mf = _mlstm(seq(pr["qb"]), seq(pr["kb"]), seq(pr["vb"]), gates, seq(pr["ob"]),
                               c0, n0, m0, lw["out_norm"], name=name + "_mlstm")
    x1, h2 = _outproj(out_a.reshape(T, W_ATTN), out_b.reshape(T, W_MLSTM), out_c.reshape(T, W_NBHD),
                      x, mods, lw["w_out"], lw["g_post_mix"], lw["g_pre_ffn"],
                      rows_per_cond=cond_rows, name=name + "_outproj")
    x2 = _ffn(h2, x1, mods, lw["w_up"], lw["w_down"], lw["g_post_ffn"],
              rows_per_cond=cond_rows, name=name + "_ffn")
    return x2, (ka, va, kc, vc, cf, nf, mf)


def kernel(x_prompt, x_sample, c, cache_k_attn, cache_v_attn, cache_k_nbhd, cache_v_nbhd, state_mlstm_C, state_mlstm_n, state_mlstm_m, c_ctx, w_ada, b_ada, g_pre_mix, g_post_mix, g_pre_ffn, g_post_ffn, w_in, q_norm_attn, k_norm_attn, mlstm_gate_bias, mlstm_out_norm, nbhd_rel_bias, w_out, w_ffn_up, w_ffn_down):
    Bc, Sc, _ = x_prompt.shape
    Bl, Sl, _ = x_sample.shape
    P = cache_k_attn.shape[2]
    n_cond = 8
    cond = jnp.concatenate([c_ctx[None, :], c, jnp.zeros((n_cond - 1 - Bl, D_MODEL), F32)], axis=0)
    mods_all = _modulation(cond, w_ada, b_ada)

    layers = []
    for l in range(DEPTH):
        layers.append(dict(
            w_in=_pad_w_in(w_in[l]),
            w_out=w_out[l].astype(BF16),
            w_up=w_ffn_up[l].astype(BF16),
            w_down=w_ffn_down[l].astype(BF16),
            g_pre_mix=g_pre_mix[l].reshape(1, D_MODEL), g_post_mix=g_post_mix[l].reshape(1, D_MODEL),
            g_pre_ffn=g_pre_ffn[l].reshape(1, D_MODEL), g_post_ffn=g_post_ffn[l].reshape(1, D_MODEL),
            q_norm=jnp.tile(q_norm_attn[l], 2).reshape(1, LANES),
            k_norm=jnp.tile(k_norm_attn[l], 2).reshape(1, LANES),
            gate_bias=_pad_gate_bias(mlstm_gate_bias[l]),
            out_norm=mlstm_out_norm[l].reshape(1, W_MLSTM),
            rel_bias=nbhd_rel_bias[l].reshape(-1),
        ))

    xp = x_prompt.reshape(Bc * Sc, D_MODEL)
    zero_state = _pack_state(jnp.zeros((Bc, 2, H_MLSTM, HEAD_DIM, HEAD_DIM), F32),
                             jnp.zeros((Bc, 2, H_MLSTM, HEAD_DIM), F32),
                             jnp.zeros((Bc, 2, H_MLSTM), F32))
    ctx = []
    for l in range(DEPTH):
        mods = mods_all[l, 0:1].reshape(1, 1, N_MOD * D_MODEL)
        xp, extras = _layer(xp, mods, layers[l], B=Bc, S=Sc, cond_rows=Bc * Sc, rope_tabs=None,
                            ctx_cache=None, state=zero_state, name=f"ctx{l}")
        ctx.append(extras)
    new_k_attn = jnp.stack([e[0].reshape(Bc, Sc, KV_ATTN, HEAD_DIM) for e in ctx], axis=1)
    new_v_attn = jnp.stack([e[1].reshape(Bc, Sc, KV_ATTN, HEAD_DIM) for e in ctx], axis=1)
    new_k_nbhd = jnp.stack([e[2].reshape(Bc, Sc, H_NBHD, HEAD_DIM) for e in ctx], axis=1)
    new_v_nbhd = jnp.stack([e[3].reshape(Bc, Sc, H_NBHD, HEAD_DIM) for e in ctx], axis=1)
    states = [_unpack_state(e[4], e[5], e[6]) for e in ctx]
    new_C = jnp.stack([s[0] for s in states], axis=1)
    new_n = jnp.stack([s[1] for s in states], axis=1)
    new_m = jnp.stack([s[2] for s in states], axis=1)

    xs = x_sample.reshape(Bl * Sl, D_MODEL)
    rope_tabs = _rope_tables(Sl)
    for l in range(DEPTH):
        mods = mods_all[l, 1:1 + Bl].reshape(Bl, 1, N_MOD * D_MODEL)
        cache = (cache_k_attn[:, l].reshape(Bl, P, W_KV), cache_v_attn[:, l].reshape(Bl, P, W_KV),
                 cache_k_nbhd[:, l].reshape(Bl, P, W_NBHD), cache_v_nbhd[:, l].reshape(Bl, P, W_NBHD))
        state = _pack_state(state_mlstm_C[:, l], state_mlstm_n[:, l], state_mlstm_m[:, l])
        xs, _ = _layer(xs, mods, layers[l], B=Bl, S=Sl, cond_rows=Sl, rope_tabs=rope_tabs,
                       ctx_cache=cache, state=state, name=f"lat{l}")

    return (xp.reshape(Bc, Sc, D_MODEL), xs.reshape(Bl, Sl, D_MODEL),
            new_k_attn, new_v_attn, new_k_nbhd, new_v_nbhd, new_C, new_n, new_m)
```

```python
import functools

import jax
import jax.numpy as jnp
import numpy as np
from jax import lax
from jax.experimental import pallas as pl
from jax.experimental.pallas import tpu as pltpu

F32 = jnp.float32
BF16 = jnp.bfloat16

D_MODEL = 1024
DEPTH = 2
GRID_W = 64
HEAD_DIM = 64
H_ATTN = 6
KV_ATTN = 2
H_MLSTM = 4
H_NBHD = 6
D_FF = 4 * D_MODEL
NA_ROWS = 8
NA_COLS = 16
ROPE_THETA = 10000.0
EPS = 1e-6
N_MOD = 6
W_ATTN = H_ATTN * HEAD_DIM
W_KV = KV_ATTN * HEAD_DIM
W_MLSTM = H_MLSTM * HEAD_DIM
W_NBHD = H_NBHD * HEAD_DIM
N_GATES = 4 * H_MLSTM

LANES = 128
V7X_VMEM_BYTES = 64 * 1024 * 1024
VMEM_CAP_BYTES = 56 * 1024 * 1024

TM_PROJ = 512
TM_FFN = 1024
TF_FFN = 1024
TQ_ATTN = 512
CK_ATTN = 512
L_CHUNK = 128
MLSTM_GROUP = 2
MLSTM_GROUP_A = 4
NB_GROUP = 8
NB_SLAB = 16
NEG = -1e30
LOG2E = 1.4426950408889634

_COLS = {}
_off = 0
for _name, _w in (("qa", W_ATTN), ("ka", W_KV), ("va", W_KV), ("qb", W_MLSTM), ("kb", W_MLSTM),
                  ("vb", W_MLSTM), ("ob", W_MLSTM), ("gf", LANES), ("gb", LANES),
                  ("qc", W_NBHD), ("kc", W_NBHD), ("vc", W_NBHD)):
    _COLS[_name] = (_off, _off + _w)
    _off += _w
IN_PAD = _off


def _vmem_limit(nbytes):
    return int(min(max(nbytes, 16 * 1024 * 1024), VMEM_CAP_BYTES))


def _dot(a, b):
    return jnp.dot(a, b, preferred_element_type=F32)


def _dot_nt(a, b):
    return lax.dot_general(a, b, (((1,), (1,)), ((), ())), preferred_element_type=F32)


def _lane_lo(shape):
    return (lax.broadcasted_iota(jnp.int32, shape, len(shape) - 1) % LANES) < HEAD_DIM


def _rms(x, g):
    ms = jnp.mean(x * x, axis=-1, keepdims=True)
    return (x * lax.rsqrt(ms + EPS)) * g


def _pair_rms(x, g):
    lo = _lane_lo(x.shape)
    x2 = x * x
    s_lo = jnp.sum(jnp.where(lo, x2, 0.0), axis=-1, keepdims=True)
    s_hi = jnp.sum(jnp.where(lo, 0.0, x2), axis=-1, keepdims=True)
    r = jnp.where(lo, lax.rsqrt(s_lo / HEAD_DIM + EPS), lax.rsqrt(s_hi / HEAD_DIM + EPS))
    return (x * r) * g


def _sigmoid(x):
    return 1.0 / (1.0 + jnp.exp(-x))


def _mods_kernel(c_ref, w_ref, b_ref, o_ref):
    c = c_ref[...]
    s = (c * _sigmoid(c)).astype(BF16)
    o_ref[0] = _dot(s, w_ref[0].astype(BF16)) + b_ref[0]


def _modulation(cond, w_ada, b_ada):
    n = cond.shape[0]
    tn = D_MODEL
    return pl.pallas_call(
        _mods_kernel,
        grid=(DEPTH, N_MOD * D_MODEL // tn),
        in_specs=[pl.BlockSpec((n, D_MODEL), lambda l, j: (0, 0)),
                  pl.BlockSpec((1, D_MODEL, tn), lambda l, j: (l, 0, j)),
                  pl.BlockSpec((1, 1, tn), lambda l, j: (l, 0, j))],
        out_specs=pl.BlockSpec((1, n, tn), lambda l, j: (l, 0, j)),
        out_shape=jax.ShapeDtypeStruct((DEPTH, n, N_MOD * D_MODEL), F32),
        compiler_params=pltpu.CompilerParams(
            dimension_semantics=("arbitrary", "arbitrary"),
            vmem_limit_bytes=_vmem_limit(4 * D_MODEL * tn * 4)),
        name="modulation",
    )(cond, w_ada, b_ada.reshape(DEPTH, 1, N_MOD * D_MODEL))


def _inproj_kernel(*refs, rope):
    if rope:
        (x_ref, mod_ref, g_ref, w_ref, qn_ref, kn_ref, gbias_ref, cos_ref, sin_ref,
         qa_ref, ka_ref, va_ref, qb_ref, kb_ref, vb_ref, ob_ref, gate_ref,
         qc_ref, kc_ref, vc_ref) = refs
    else:
        (x_ref, mod_ref, g_ref, w_ref, qn_ref, kn_ref, gbias_ref,
         qa_ref, ka_ref, va_ref, qb_ref, kb_ref, vb_ref, ob_ref, gate_ref,
         qc_ref, kc_ref, vc_ref) = refs
    x = x_ref[...]
    mod = mod_ref[0]
    sh1 = mod[:, 0:D_MODEL]
    sc1 = mod[:, D_MODEL:2 * D_MODEL]
    hb = (_rms(x, g_ref[...]) * (1.0 + sc1) + sh1).astype(BF16)

    def proj(name, j=0, w=None):
        lo, hi = _COLS[name]
        lo = lo + j
        hi = hi if w is None else lo + w
        return _dot(hb, w_ref[:, lo:hi])

    scale = HEAD_DIM ** -0.5
    q_scale = scale * LOG2E

    def rotary(t):
        first = (lax.broadcasted_iota(jnp.int32, t.shape, 1) % 32) < 16
        partner = jnp.where(first, pltpu.roll(t, LANES - 16, 1), pltpu.roll(t, 16, 1))
        return t * cos_ref[...] + partner * sin_ref[...]

    for j in range(W_ATTN // LANES):
        t = _pair_rms(proj("qa", j * LANES, LANES), qn_ref[...])
        if rope:
            t = rotary(t)
        qa_ref[:, j * LANES:(j + 1) * LANES] = (t * q_scale).astype(qa_ref.dtype)
    t = _pair_rms(proj("ka"), kn_ref[...])
    if rope:
        t = rotary(t)
    ka_ref[...] = t.astype(ka_ref.dtype)
    va_ref[...] = proj("va").astype(va_ref.dtype)
    qb_ref[...] = proj("qb").astype(qb_ref.dtype)
    kb_ref[...] = (proj("kb") * scale).astype(kb_ref.dtype)
    vb_ref[...] = proj("vb").astype(vb_ref.dtype)
    ob_ref[...] = proj("ob").astype(ob_ref.dtype)
    for j, name in enumerate(("gf", "gb")):
        gt = proj(name) + gbias_ref[:, j * LANES:(j + 1) * LANES]
        lane = lax.broadcasted_iota(jnp.int32, gt.shape, 1)
        is_f = (lane >= H_MLSTM) & (lane < 2 * H_MLSTM)
        logsig = jnp.minimum(gt, 0.0) - jnp.log1p(jnp.exp(-jnp.abs(gt)))
        gate_ref[:, j * LANES:(j + 1) * LANES] = jnp.where(is_f, logsig, gt)
    qc_ref[...] = (proj("qc") * q_scale).astype(qc_ref.dtype)
    kc_ref[...] = proj("kc").astype(kc_ref.dtype)
    vc_ref[...] = proj("vc").astype(vc_ref.dtype)


def _inproj(x, mods, g_pre, w_in_p, qn, kn, gbias, rope_tabs, *, rows_per_cond, kv_dtype, name):
    T = x.shape[0]
    tm = TM_PROJ
    bpc = rows_per_cond // tm
    rope = rope_tabs is not None
    row = lambda i: (i, 0)
    const = lambda i: (0, 0)
    in_specs = [pl.BlockSpec((tm, D_MODEL), row),
                pl.BlockSpec((1, 1, N_MOD * D_MODEL), lambda i: (i // bpc, 0, 0)),
                pl.BlockSpec((1, D_MODEL), const),
                pl.BlockSpec((D_MODEL, IN_PAD), const),
                pl.BlockSpec((1, LANES), const),
                pl.BlockSpec((1, LANES), const),
                pl.BlockSpec((1, 2 * LANES), const)]
    args = [x, mods, g_pre, w_in_p, qn, kn, gbias]
    if rope:
        nblk = rope_tabs[0].shape[0] // tm
        in_specs += [pl.BlockSpec((tm, LANES), lambda i: (i % nblk, 0))] * 2
        args += list(rope_tabs)
    widths = [("qa", W_ATTN, BF16), ("ka", W_KV, kv_dtype), ("va", W_KV, kv_dtype),
              ("qb", W_MLSTM, BF16), ("kb", W_MLSTM, BF16), ("vb", W_MLSTM, BF16),
              ("ob", W_MLSTM, F32), ("g", 2 * LANES, F32),
              ("qc", W_NBHD, BF16), ("kc", W_NBHD, kv_dtype), ("vc", W_NBHD, kv_dtype)]
    out_specs = [pl.BlockSpec((tm, w), row) for _, w, _ in widths]
    out_shape = [jax.ShapeDtypeStruct((T, w), dt) for _, w, dt in widths]
    est = 2 * (tm * D_MODEL * 4 + D_MODEL * IN_PAD * 2 + tm * IN_PAD * 4) + 3 * tm * IN_PAD * 4
    outs = pl.pallas_call(
        functools.partial(_inproj_kernel, rope=rope),
        grid=(T // tm,),
        in_specs=in_specs, out_specs=out_specs, out_shape=out_shape,
        compiler_params=pltpu.CompilerParams(dimension_semantics=("arbitrary",),
                                             vmem_limit_bytes=_vmem_limit(est)),
        name=name,
    )(*args)
    return dict(zip([n for n, _, _ in widths], outs))


def _attn_kernel(q_ref, k_ref, vt_ref, o_ref, qs_scr, m_scr, l_scr, acc_scr, *, heads, nch):
    for h, (kg, kh) in enumerate(heads):
        qg = q_ref[0, :, (h // 2) * LANES:(h // 2 + 1) * LANES].astype(F32)
        if h % 2 != kh:
            qg = pltpu.roll(qg, HEAD_DIM, 1)
        keep = _lane_lo(qg.shape) if kh == 0 else jnp.logical_not(_lane_lo(qg.shape))
        qs_scr[h] = jnp.where(keep, qg, 0.0).astype(BF16)
    m_scr[...] = jnp.full(m_scr.shape, -jnp.inf, F32)
    l_scr[...] = jnp.zeros(l_scr.shape, F32)
    acc_scr[...] = jnp.zeros(acc_scr.shape, F32)

    def scores(j, h):
        kg = heads[h][0]
        kj = k_ref[0, j, :, kg * LANES:(kg + 1) * LANES].astype(BF16)
        return _dot_nt(kj, qs_scr[h])

    def chunk(j, carry):
        st_next = scores(j, 0)
        for h, (kg, kh) in enumerate(heads):
            st = st_next
            if h + 1 < len(heads):
                st_next = scores(j, h + 1)
            m_old = m_scr[h]
            m_new = jnp.maximum(m_old, jnp.max(st, axis=0, keepdims=True))
            e = jnp.exp2(st - m_new)
            alpha = jnp.exp2(m_old - m_new)
            l_scr[h] = alpha * l_scr[h] + jnp.sum(e, axis=0, keepdims=True)
            r = kg * LANES + kh * HEAD_DIM
            vt = vt_ref[0, j, r:r + HEAD_DIM, :].astype(BF16)
            acc_scr[h] = alpha * acc_scr[h] + _dot(vt, e.astype(BF16))
            m_scr[h] = m_new
        return carry

    lax.fori_loop(0, nch, chunk, 0)
    for t in range(len(heads) // 2):
        ot = jnp.concatenate([acc_scr[2 * t] / l_scr[2 * t], acc_scr[2 * t + 1] / l_scr[2 * t + 1]], axis=0)
        o_ref[0, :, t * LANES:(t + 1) * LANES] = ot.T.astype(o_ref.dtype)


def _attention(q, k, v, *, heads, name):
    B, Sq, W = q.shape
    Sk, KW = k.shape[1], k.shape[2]
    tq = min(TQ_ATTN, Sq)
    ck = min(CK_ATTN, Sk)
    nch = Sk // ck
    nh = len(heads)
    k4 = k.reshape(B, nch, ck, KW)
    vt4 = v.reshape(B, nch, ck, KW).transpose(0, 1, 3, 2)
    isz = k.dtype.itemsize
    est = (2 * (2 * tq * W * 2 + 2 * Sk * KW * isz) + nh * tq * (LANES * 2 + HEAD_DIM * 4 + 64)
           + 8 * nh * ck * tq * 4)
    return pl.pallas_call(
        functools.partial(_attn_kernel, heads=heads, nch=nch),
        grid=(B, Sq // tq),
        in_specs=[pl.BlockSpec((1, tq, W), lambda b, i: (b, i, 0)),
                  pl.BlockSpec((1, nch, ck, KW), lambda b, i: (b, 0, 0, 0)),
                  pl.BlockSpec((1, nch, KW, ck), lambda b, i: (b, 0, 0, 0))],
        out_specs=pl.BlockSpec((1, tq, W), lambda b, i: (b, i, 0)),
        out_shape=jax.ShapeDtypeStruct((B, Sq, W), BF16),
        scratch_shapes=[pltpu.VMEM((nh, tq, LANES), BF16),
                        pltpu.VMEM((nh, 1, tq), F32),
                        pltpu.VMEM((nh, 1, tq), F32),
                        pltpu.VMEM((nh, HEAD_DIM, tq), F32)],
        compiler_params=pltpu.CompilerParams(dimension_semantics=("arbitrary", "arbitrary"),
                                             vmem_limit_bytes=_vmem_limit(est)),
        name=name,
    )(q, k4, vt4)


_HEADS_GQA = tuple((0, h // (H_ATTN // KV_ATTN)) for h in range(H_ATTN))
_HEADS_MHA = tuple((h // 2, h % 2) for h in range(H_NBHD))


def _nbhd_window(r, rows):
    kr = min(NA_ROWS, rows)
    return min(max(r - kr // 2, 0), rows - kr), kr


def _nbhd_patterns(rows):
    n_groups = rows // NB_GROUP
    pats = []
    for g in (0, 1, n_groups - 1):
        r0 = g * NB_GROUP
        pats.append((r0, min(max(r0 - NA_ROWS // 2, 0), rows - NB_SLAB)))
    return pats


def _nbhd_kernel(rb_ref, q_ref, k_ref, v_ref, kc_ref, vc_ref, o_ref, bias_scr, *, rows):
    hp = pl.program_id(1)
    g = pl.program_id(2)
    n_groups = rows // NB_GROUP
    n_dr = 2 * NA_ROWS - 1
    n_dc = 2 * NA_COLS - 1
    tq = NB_GROUP * GRID_W
    pats = _nbhd_patterns(rows)

    @pl.when(g == 0)
    def _build_bias():
        shape = (GRID_W, LANES)
        w = lax.broadcasted_iota(jnp.int32, shape, 0)
        lane = lax.broadcasted_iota(jnp.int32, shape, 1)
        cc = lane % GRID_W
        second = lane >= GRID_W
        cs = jnp.clip(w - NA_COLS // 2, 0, GRID_W - NA_COLS)
        col_ok = (cc >= cs) & (cc < cs + NA_COLS)
        dc = cc - w + (NA_COLS - 1)
        for hh in range(2):
            base = (2 * hp + hh) * (n_dr * n_dc)
            tiles = {}
            for d in range(-1, n_dr):
                acc = jnp.zeros(shape, F32)
                for j in range(n_dc):
                    va = rb_ref[base + d * n_dc + j] * LOG2E if d >= 0 else 0.0
                    vb = rb_ref[base + (d + 1) * n_dc + j] * LOG2E if d + 1 < n_dr else 0.0
                    acc = acc + jnp.where(dc == j, jnp.where(second, vb, va), 0.0)
                tiles[d] = acc
            for pi, (r0, slab0) in enumerate(pats):
                for i in range(NB_GROUP):
                    r = r0 + i
                    rs, kr = _nbhd_window(r, rows)
                    for ap in range(NB_SLAB // 2):
                        kra = slab0 + 2 * ap
                        ok_a = rs <= kra < rs + kr
                        ok_b = rs <= kra + 1 < rs + kr
                        if not (ok_a or ok_b):
                            blk = jnp.full(shape, NEG, F32)
                        else:
                            d = kra - r + (NA_ROWS - 1)
                            row_ok = (jnp.logical_not(second) if ok_a and not ok_b else
                                      second if ok_b and not ok_a else None)
                            ok = col_ok if row_ok is None else (col_ok & row_ok)
                            blk = jnp.where(ok, tiles[d], NEG)
                        bias_scr[hh, pi, i * GRID_W:(i + 1) * GRID_W, ap * LANES:(ap + 1) * LANES] = blk

    pat = jnp.where(g == 0, 0, jnp.where(g == n_groups - 1, 2, 1))
    slab0 = jnp.clip(g * NB_GROUP - NA_ROWS // 2, 0, rows - NB_SLAB)
    t0 = pl.multiple_of(slab0 * GRID_W, GRID_W)
    nk = NB_SLAB * GRID_W
    kslab = k_ref[0, pl.ds(t0, nk), :]
    vslab = v_ref[0, pl.ds(t0, nk), :]
    kctx = kc_ref[0].astype(BF16)
    vctx = vc_ref[0].astype(BF16)
    q = q_ref[0].astype(F32)
    lo = _lane_lo(q.shape)
    outs = []
    for hh in range(2):
        keep = lo if hh == 0 else jnp.logical_not(lo)
        qm = jnp.where(keep, q, 0.0).astype(BF16)
        s_nb = _dot_nt(qm, kslab) + bias_scr[hh, pat]
        s_cx = _dot_nt(qm, kctx)
        m = jnp.maximum(jnp.max(s_nb, axis=1, keepdims=True), jnp.max(s_cx, axis=1, keepdims=True))
        e_nb = jnp.exp2(s_nb - m)
        e_cx = jnp.exp2(s_cx - m)
        l = jnp.sum(e_nb, axis=1, keepdims=True) + jnp.sum(e_cx, axis=1, keepdims=True)
        acc = _dot(e_nb.astype(BF16), vslab) + _dot(e_cx.astype(BF16), vctx)
        outs.append(acc / l)
    o_ref[0] = jnp.where(lo, outs[0], outs[1]).astype(o_ref.dtype)


def _nbhd_attention(q, k, v, k_ctx, v_ctx, rel_bias_flat, *, name):
    B, S, W = q.shape
    P = k_ctx.shape[1]
    rows = S // GRID_W
    tq = NB_GROUP * GRID_W
    nk = NB_SLAB * GRID_W
    est = (2 * (2 * tq * LANES * 2 + 2 * S * LANES * 2 + 2 * P * LANES * 4)
           + 2 * 3 * tq * nk * 4 + 8 * tq * (nk + P) * 4)
    return pl.pallas_call(
        functools.partial(_nbhd_kernel, rows=rows),
        grid=(B, W // LANES, rows // NB_GROUP),
        in_specs=[pl.BlockSpec(memory_space=pltpu.SMEM),
                  pl.BlockSpec((1, tq, LANES), lambda b, p, g: (b, g, p)),
                  pl.BlockSpec((1, S, LANES), lambda b, p, g: (b, 0, p)),
                  pl.BlockSpec((1, S, LANES), lambda b, p, g: (b, 0, p)),
                  pl.BlockSpec((1, P, LANES), lambda b, p, g: (b, 0, p)),
                  pl.BlockSpec((1, P, LANES), lambda b, p, g: (b, 0, p))],
        out_specs=pl.BlockSpec((1, tq, LANES), lambda b, p, g: (b, g, p)),
        out_shape=jax.ShapeDtypeStruct((B, S, W), BF16),
        scratch_shapes=[pltpu.VMEM((2, 3, tq, nk), F32)],
        compiler_params=pltpu.CompilerParams(dimension_semantics=("arbitrary",) * 3,
                                             vmem_limit_bytes=_vmem_limit(est)),
        name=name,
    )(rel_bias_flat, q, k, v, k_ctx, v_ctx)


def _split3(x):
    hi = x.astype(BF16)
    r = x - hi.astype(F32)
    mid = r.astype(BF16)
    return hi, mid, (r - mid.astype(F32)).astype(BF16)


def _mlstm_kernel(q_ref, k_ref, v_ref, g_ref, ob_ref, s0_ref, m0_ref, on_ref,
                  out_ref, sf_ref, mf_ref,
                  h_scr, nat_scr, rows_scr, stat_scr, mprev_scr, un_scr, st_scr, *, nc, grp_a, grp):
    d = pl.program_id(1)
    L = L_CHUNK
    NP = H_MLSTM // 2
    row = lax.broadcasted_iota(jnp.int32, (L, L), 0)
    col = lax.broadcasted_iota(jnp.int32, (L, L), 1)
    sign = 1 - 2 * d
    mask = (col - row) * sign <= 0
    maskb = mask.astype(BF16)
    mask3 = jnp.concatenate([maskb, maskb, maskb], axis=1)
    lane = lax.broadcasted_iota(jnp.int32, (L, LANES), 1)
    lo = lane < HEAD_DIM
    top = row < HEAD_DIM
    row2 = lax.broadcasted_iota(jnp.int32, (L, 2 * LANES), 0)
    col2 = lax.broadcasted_iota(jnp.int32, (L, 2 * LANES), 1)
    keep_state = (row2 < HEAD_DIM) == ((col2 % LANES) < HEAD_DIM)
    top2 = row2 < HEAD_DIM
    ones_b = jnp.ones((L, LANES), BF16)
    ones_lo = lo.astype(BF16)
    ones_hi = jnp.logical_not(lo).astype(BF16)

    def chunk_rows(c):
        return pl.ds(pl.multiple_of(c * L, L), L)

    def pass_a(it, carry):
        cs = [it * grp_a + u for u in range(grp_a)]
        gts = [g_ref[0, chunk_rows(c), :] for c in cs]
        bns = [_dot(mask3, jnp.concatenate(_split3(gt), axis=0)) for gt in gts]
        a_all = []
        for c, gt, bn in zip(cs, gts, bns):
            nat = jnp.where(lane < H_MLSTM, gt, bn)
            nat_scr[chunk_rows(c), :] = nat * (-LOG2E)
            nat_t = nat.T
            b_rows = nat_t[H_MLSTM:2 * H_MLSTM]
            c_rows = nat_t[0:H_MLSTM] - b_rows
            rows_scr[c] = jnp.concatenate([c_rows * LOG2E, b_rows], axis=0)
            c_max = jnp.max(c_rows, axis=1, keepdims=True)
            b_tot = jnp.where(d == 0, b_rows[:, L - 1:L], b_rows[:, 0:1])
            stat_scr[c] = jnp.concatenate([jnp.broadcast_to(c_max, (H_MLSTM, LANES)),
                                           jnp.broadcast_to(b_tot, (H_MLSTM, LANES))], axis=0)
            a_all.append(jnp.exp(c_rows - c_max))
        for c, a_rows in zip(cs, a_all):
            rows = chunk_rows(c)
            for p in range(NP):
                lanes = slice(p * LANES, (p + 1) * LANES)
                k_t = k_ref[0, rows, lanes].astype(F32).T
                a_sel = jnp.where(top, a_rows[2 * p:2 * p + 1], a_rows[2 * p + 1:2 * p + 2])
                vv = jnp.concatenate([v_ref[0, rows, lanes], ones_b], axis=1)
                un = _dot((k_t * a_sel).astype(BF16), vv)
                un_scr[c, p] = jnp.where(keep_state, un, 0.0)
        return carry

    lax.fori_loop(0, nc // grp_a, pass_a, 0)

    st_scr[...] = s0_ref[0, 0]

    def pass_b(ci, m):
        c = jnp.where(d == 0, ci, nc - 1 - ci)
        st = stat_scr[c]
        c_max, b_tot = st[0:H_MLSTM], st[H_MLSTM:]
        m_new = jnp.maximum(b_tot + m, b_tot + c_max)
        d_old = jnp.exp(b_tot + m - m_new)
        d_new = jnp.exp(b_tot + c_max - m_new)
        mprev_scr[c] = jnp.concatenate([m, m], axis=0) * LOG2E
        for p in range(NP):
            def rows_of(t, p=p):
                even = jnp.concatenate([t[2 * p:2 * p + 1]] * 2, axis=1)
                odd = jnp.concatenate([t[2 * p + 1:2 * p + 2]] * 2, axis=1)
                return jnp.where(top2, even, odd)
            s_prev = st_scr[p]
            st_scr[p] = rows_of(d_old) * s_prev + rows_of(d_new) * un_scr[c, p]
            un_scr[c, p] = s_prev
        return m_new

    m_fin = lax.fori_loop(0, nc, pass_b, m0_ref[0, 0][0:H_MLSTM])
    sf_ref[0, 0] = st_scr[...]
    mf_ref[0, 0] = jnp.concatenate([m_fin, m_fin], axis=0)

    def pass_c(it, carry):
        cs = [it * grp + u for u in range(grp)]
        units = [(u, p) for u in range(grp) for p in range(NP)]
        early = {}
        for u, p in units:
            c = cs[u]
            rows = chunk_rows(c)
            lanes = slice(p * LANES, (p + 1) * LANES)
            qp = q_ref[0, rows, lanes]
            kp = k_ref[0, rows, lanes]
            s_in = un_scr[c, p].astype(BF16)
            qms = [jnp.where(lo if hh == 0 else jnp.logical_not(lo), qp, jnp.zeros_like(qp)) for hh in range(2)]
            early[u, p] = ([_dot_nt(qm, kp) for qm in qms],
                           _dot(qp, s_in))
        mid = {}
        for u, p in units:
            r_t = rows_scr[cs[u]]
            m_in = mprev_scr[cs[u]]
            for hh in range(2):
                h = 2 * p + hh
                cm = jnp.where(mask, r_t[h:h + 1, :], -jnp.inf)
                m_prev = m_in[h:h + 1, :]
                mu = jnp.maximum(jnp.broadcast_to(jnp.max(cm, axis=1, keepdims=True), (L, LANES)), m_prev)
                w = early[u, p][0][hh] * jnp.exp2(cm - mu)
                mid[u, p, hh] = (w.astype(BF16), mu, m_prev)
        for u, p in units:
            rows = chunk_rows(cs[u])
            lanes = slice(p * LANES, (p + 1) * LANES)
            vp = v_ref[0, rows, lanes]
            zero = jnp.zeros_like(vp)
            vv = jnp.concatenate([jnp.concatenate([jnp.where(lo, vp, zero), ones_lo], axis=1),
                                  jnp.concatenate([jnp.where(lo, zero, vp), ones_hi], axis=1)], axis=0)
            w2 = jnp.concatenate([mid[u, p, 0][0], mid[u, p, 1][0]], axis=1)
            nd = _dot(w2, vv)
            nat = nat_scr[rows, :]
            nb = [jnp.broadcast_to(nat[:, H_MLSTM + 2 * p + hh:H_MLSTM + 2 * p + hh + 1], (L, LANES))
                  for hh in range(2)]
            fs = early[u, p][1]
            mu = jnp.where(lo, mid[u, p, 0][1], mid[u, p, 1][1])
            m_prev = jnp.where(lo[0:1], mid[u, p, 0][2], mid[u, p, 1][2])
            w_inter = jnp.exp2(m_prev - mu)
            den = nd[:, LANES:] + w_inter * fs[:, LANES:]
            den = jnp.maximum(jnp.abs(den), jnp.exp2(jnp.where(lo, nb[0], nb[1]) - mu))
            h_scr[d, rows, lanes] = (nd[:, :LANES] + w_inter * fs[:, :LANES]) / den
        return carry

    lax.fori_loop(0, nc // grp, pass_c, 0)

    @pl.when(d == 1)
    def _finish():
        def rows_block(c, carry):
            rows = chunk_rows(c)
            for p in range(NP):
                lanes = slice(p * LANES, (p + 1) * LANES)
                hn = _pair_rms(h_scr[0, rows, lanes] + h_scr[1, rows, lanes], on_ref[:, lanes])
                out_ref[0, rows, lanes] = (_sigmoid(ob_ref[0, rows, lanes]) * hn).astype(out_ref.dtype)
            return carry

        lax.fori_loop(0, nc, rows_block, 0)


def _mlstm(q, k, v, gates, ob, s0, m0, out_norm, *, name):
    assert L_CHUNK == LANES
    B, S, W = q.shape
    nc = S // L_CHUNK
    grp = min(MLSTM_GROUP, nc)
    grp_a = min(MLSTM_GROUP_A, nc)
    npair = H_MLSTM // 2
    seq = lambda b, d: (b, 0, 0)
    est = (2 * (3 * S * W * 2 + S * LANES * 4 + S * W * 4 + S * W * 2) + 2 * S * W * 4 + S * LANES * 4
           + nc * npair * LANES * 2 * LANES * 4 + 12 * 1024 * 1024)
    return pl.pallas_call(
        functools.partial(_mlstm_kernel, nc=nc, grp_a=grp_a, grp=grp),
        grid=(B, 2),
        in_specs=[pl.BlockSpec((1, S, W), seq), pl.BlockSpec((1, S, W), seq), pl.BlockSpec((1, S, W), seq),
                  pl.BlockSpec((1, S, LANES), lambda b, d: (b, 0, d)),
                  pl.BlockSpec((1, S, W), seq),
                  pl.BlockSpec((1, 1, npair, LANES, 2 * LANES), lambda b, d: (b, d, 0, 0, 0)),
                  pl.BlockSpec((1, 1, 8, LANES), lambda b, d: (b, d, 0, 0)),
                  pl.BlockSpec((1, W), lambda b, d: (0, 0))],
        out_specs=[pl.BlockSpec((1, S, W), seq),
                   pl.BlockSpec((1, 1, npair, LANES, 2 * LANES), lambda b, d: (b, d, 0, 0, 0)),
                   pl.BlockSpec((1, 1, 8, LANES), lambda b, d: (b, d, 0, 0))],
        out_shape=[jax.ShapeDtypeStruct((B, S, W), BF16),
                   jax.ShapeDtypeStruct((B, 2, npair, LANES, 2 * LANES), F32),
                   jax.ShapeDtypeStruct((B, 2, 8, LANES), F32)],
        scratch_shapes=[pltpu.VMEM((2, S, W), F32),
                        pltpu.VMEM((S, LANES), F32),
                        pltpu.VMEM((nc, 8, L_CHUNK), F32),
                        pltpu.VMEM((nc, 8, LANES), F32),
                        pltpu.VMEM((nc, 8, LANES), F32),
                        pltpu.VMEM((nc, npair, LANES, 2 * LANES), F32),
                        pltpu.VMEM((npair, LANES, 2 * LANES), F32)],
        compiler_params=pltpu.CompilerParams(dimension_semantics=("arbitrary", "arbitrary"),
                                             vmem_limit_bytes=_vmem_limit(est)),
        name=name,
    )(q, k, v, gates, ob, s0, m0, out_norm)


def _outproj_kernel(a_ref, b_ref, c_ref, x_ref, mod_ref, w_ref, gpost_ref, gpre_ref, x1_ref, h2_ref):
    mo = (_dot(a_ref[...], w_ref[0:W_ATTN, :])
          + _dot(b_ref[...], w_ref[W_ATTN:W_ATTN + W_MLSTM, :])
          + _dot(c_ref[...], w_ref[W_ATTN + W_MLSTM:, :]))
    mod = mod_ref[0]
    gt1 = mod[:, 2 * D_MODEL:3 * D_MODEL]
    sh2 = mod[:, 3 * D_MODEL:4 * D_MODEL]
    sc2 = mod[:, 4 * D_MODEL:5 * D_MODEL]
    x1 = x_ref[...] + gt1 * _rms(mo, gpost_ref[...])
    x1_ref[...] = x1
    h2_ref[...] = (_rms(x1, gpre_ref[...]) * (1.0 + sc2) + sh2).astype(h2_ref.dtype)


def _outproj(oa, ob, oc, x, mods, w_out, g_post, g_pre, *, rows_per_cond, name):
    T = x.shape[0]
    tm = TM_PROJ
    bpc = rows_per_cond // tm
    row = lambda i: (i, 0)
    const = lambda i: (0, 0)
    est = 2 * (tm * D_MODEL * (2 + 4 + 4 + 2) + D_MODEL * D_MODEL * 2) + 4 * tm * D_MODEL * 4
    return pl.pallas_call(
        _outproj_kernel,
        grid=(T // tm,),
        in_specs=[pl.BlockSpec((tm, W_ATTN), row), pl.BlockSpec((tm, W_MLSTM), row),
                  pl.BlockSpec((tm, W_NBHD), row), pl.BlockSpec((tm, D_MODEL), row),
                  pl.BlockSpec((1, 1, N_MOD * D_MODEL), lambda i: (i // bpc, 0, 0)),
                  pl.BlockSpec((D_MODEL, D_MODEL), const),
                  pl.BlockSpec((1, D_MODEL), const), pl.BlockSpec((1, D_MODEL), const)],
        out_specs=[pl.BlockSpec((tm, D_MODEL), row), pl.BlockSpec((tm, D_MODEL), row)],
        out_shape=[jax.ShapeDtypeStruct((T, D_MODEL), F32), jax.ShapeDtypeStruct((T, D_MODEL), BF16)],
        compiler_params=pltpu.CompilerParams(dimension_semantics=("arbitrary",),
                                             vmem_limit_bytes=_vmem_limit(est)),
        name=name,
    )(oa, ob, oc, x, mods, w_out, g_post, g_pre)


def _ffn_kernel(h_ref, x_ref, mod_ref, wu_ref, wd_ref, g_ref, o_ref, acc_ref):
    j = pl.program_id(1)

    @pl.when(j == 0)
    def _zero():
        acc_ref[...] = jnp.zeros_like(acc_ref)

    u = jnp.maximum(_dot(h_ref[...], wu_ref[...]), 0.0)
    acc_ref[...] += _dot((u * u).astype(BF16), wd_ref[...])

    @pl.when(j == pl.num_programs(1) - 1)
    def _finish():
        gt2 = mod_ref[0][:, 5 * D_MODEL:6 * D_MODEL]
        o_ref[...] = x_ref[...] + gt2 * _rms(acc_ref[...], g_ref[...])


def _ffn(h2, x1, mods, w_up, w_down, g_post, *, rows_per_cond, name):
    T = x1.shape[0]
    tm, tf = TM_FFN, TF_FFN
    tm = min(tm, rows_per_cond)
    bpc = rows_per_cond // tm
    est = 2 * (tm * D_MODEL * (2 + 4 + 4) + 2 * D_MODEL * tf * 2) + tm * D_MODEL * 4 + 3 * tm * tf * 4
    return pl.pallas_call(
        _ffn_kernel,
        grid=(T // tm, D_FF // tf),
        in_specs=[pl.BlockSpec((tm, D_MODEL), lambda i, j: (i, 0)),
                  pl.BlockSpec((tm, D_MODEL), lambda i, j: (i, 0)),
                  pl.BlockSpec((1, 1, N_MOD * D_MODEL), lambda i, j: (i // bpc, 0, 0)),
                  pl.BlockSpec((D_MODEL, tf), lambda i, j: (0, j)),
                  pl.BlockSpec((tf, D_MODEL), lambda i, j: (j, 0)),
                  pl.BlockSpec((1, D_MODEL), lambda i, j: (0, 0))],
        out_specs=pl.BlockSpec((tm, D_MODEL), lambda i, j: (i, 0)),
        out_shape=jax.ShapeDtypeStruct((T, D_MODEL), F32),
        scratch_shapes=[pltpu.VMEM((tm, D_MODEL), F32)],
        compiler_params=pltpu.CompilerParams(dimension_semantics=("arbitrary", "arbitrary"),
                                             vmem_limit_bytes=_vmem_limit(est)),
        name=name,
    )(h2, x1, mods, w_up, w_down, g_post)


def _pad_w_in(w_in_l):
    o = W_ATTN + 2 * W_KV + 4 * W_MLSTM
    pre, gates, post = w_in_l[:, :o], w_in_l[:, o:o + N_GATES], w_in_l[:, o + N_GATES:]
    z = jnp.zeros((D_MODEL, LANES - 2 * H_MLSTM), w_in_l.dtype)
    return jnp.concatenate([pre, gates[:, :2 * H_MLSTM], z, gates[:, 2 * H_MLSTM:], z, post],
                           axis=1).astype(BF16)


def _pad_gate_bias(gb_l):
    z = jnp.zeros((LANES - 2 * H_MLSTM,), gb_l.dtype)
    return jnp.concatenate([gb_l[:2 * H_MLSTM], z, gb_l[2 * H_MLSTM:], z]).reshape(1, 2 * LANES)


def _rope_tables(S):
    quarter = HEAD_DIM // 4
    pos = jnp.arange(S)
    inv_freq = ROPE_THETA ** (-jnp.arange(quarter, dtype=F32) / quarter)

    def tabs(p):
        ang = p.astype(F32)[:, None] * inv_freq[None, :]
        return jnp.cos(ang), jnp.sin(ang)

    cr, sr = tabs(pos // GRID_W)
    cc, sc = tabs(pos % GRID_W)
    cos = jnp.concatenate([cr, cr, cc, cc], axis=1)
    sin = jnp.concatenate([-sr, sr, -sc, sc], axis=1)
    return jnp.tile(cos, (1, 2)), jnp.tile(sin, (1, 2))


def _pack_state(C, n, m):
    B = C.shape[0]
    Cp = C.reshape(B, 2, 2, 2, HEAD_DIM, HEAD_DIM)
    z = jnp.zeros_like(Cp[:, :, :, 0])
    top = jnp.concatenate([Cp[:, :, :, 0], z], axis=-1)
    bot = jnp.concatenate([z, Cp[:, :, :, 1]], axis=-1)
    Cbd = jnp.concatenate([top, bot], axis=-2)
    n_rep = jnp.broadcast_to(n.reshape(B, 2, 2, LANES, 1), (B, 2, 2, LANES, LANES))
    same_head = (jnp.arange(LANES)[:, None] < HEAD_DIM) == (jnp.arange(LANES)[None, :] < HEAD_DIM)
    n_rep = jnp.where(same_head, n_rep, 0.0)
    m_rows = jnp.broadcast_to(m[..., None], m.shape + (LANES,))
    return jnp.concatenate([Cbd, n_rep], axis=-1), jnp.concatenate([m_rows, m_rows], axis=-2)


def _unpack_state(s_p, m_p):
    B = s_p.shape[0]
    c_even = s_p[:, :, :, :HEAD_DIM, :HEAD_DIM]
    c_odd = s_p[:, :, :, HEAD_DIM:, HEAD_DIM:LANES]
    C = jnp.stack([c_even, c_odd], axis=3).reshape(B, 2, H_MLSTM, HEAD_DIM, HEAD_DIM)
    n = jnp.concatenate([s_p[..., :HEAD_DIM, LANES], s_p[..., HEAD_DIM:, LANES + HEAD_DIM]], axis=-1)
    return C, n.reshape(B, 2, H_MLSTM, HEAD_DIM), m_p[:, :, :H_MLSTM, 0]


def _layer(x, mods, lw, *, B, S, cond_rows, rope_tabs, ctx_cache, state, name):
    T = B * S
    kv_dtype = BF16 if ctx_cache is not None else F32
    pr = _inproj(x, mods, lw["g_pre_mix"], lw["w_in"], lw["q_norm"], lw["k_norm"], lw["gate_bias"],
                 rope_tabs, rows_per_cond=cond_rows, kv_dtype=kv_dtype, name=name + "_inproj")
    seq = lambda a: a.reshape(B, S, a.shape[-1])
    qa, ka, va = seq(pr["qa"]), seq(pr["ka"]), seq(pr["va"])
    qc, kc, vc = seq(pr["qc"]), seq(pr["kc"]), seq(pr["vc"])
    if ctx_cache is None:
        out_a = _attention(qa, ka, va, heads=_HEADS_GQA, name=name + "_attn_a")
        out_c = _attention(qc, kc, vc, heads=_HEADS_MHA, name=name + "_attn_c")
    else:
        ck_a, cv_a, ck_c, cv_c = ctx_cache
        k_all = jnp.concatenate([ka, ck_a.astype(BF16)], axis=1)
        v_all = jnp.concatenate([va, cv_a.astype(BF16)], axis=1)
        out_a = _attention(qa, k_all, v_all, heads=_HEADS_GQA, name=name + "_attn_a")
        out_c = _nbhd_attention(qc, kc, vc, ck_c, cv_c, lw["rel_bias"], name=name + "_attn_c")
    s0, m0 = state
    out_b, sf, mf = _mlstm(seq(pr["qb"]), seq(pr["kb"]), seq(pr["vb"]), seq(pr["g"]), seq(pr["ob"]),
                           s0, m0, lw["out_norm"], name=name + "_mlstm")
    x1, h2 = _outproj(out_a.reshape(T, W_ATTN), out_b.reshape(T, W_MLSTM), out_c.reshape(T, W_NBHD),
                      x, mods, lw["w_out"], lw["g_post_mix"], lw["g_pre_ffn"],
                      rows_per_cond=cond_rows, name=name + "_outproj")
    x2 = _ffn(h2, x1, mods, lw["w_up"], lw["w_down"], lw["g_post_ffn"],
              rows_per_cond=cond_rows, name=name + "_ffn")
    return x2, (ka, va, kc, vc, sf, mf)


def kernel(x_prompt, x_sample, c, cache_k_attn, cache_v_attn, cache_k_nbhd, cache_v_nbhd, state_mlstm_C, state_mlstm_n, state_mlstm_m, c_ctx, w_ada, b_ada, g_pre_mix, g_post_mix, g_pre_ffn, g_post_ffn, w_in, q_norm_attn, k_norm_attn, mlstm_gate_bias, mlstm_out_norm, nbhd_rel_bias, w_out, w_ffn_up, w_ffn_down):
    Bc, Sc, _ = x_prompt.shape
    Bl, Sl, _ = x_sample.shape
    P = cache_k_attn.shape[2]
    n_cond = 8
    cond = jnp.concatenate([c_ctx[None, :], c, jnp.zeros((n_cond - 1 - Bl, D_MODEL), F32)], axis=0)
    mods_all = _modulation(cond, w_ada, b_ada)

    layers = []
    for l in range(DEPTH):
        layers.append(dict(
            w_in=_pad_w_in(w_in[l]),
            w_out=w_out[l].astype(BF16),
            w_up=w_ffn_up[l].astype(BF16),
            w_down=w_ffn_down[l].astype(BF16),
            g_pre_mix=g_pre_mix[l].reshape(1, D_MODEL), g_post_mix=g_post_mix[l].reshape(1, D_MODEL),
            g_pre_ffn=g_pre_ffn[l].reshape(1, D_MODEL), g_post_ffn=g_post_ffn[l].reshape(1, D_MODEL),
            q_norm=jnp.tile(q_norm_attn[l], 2).reshape(1, LANES),
            k_norm=jnp.tile(k_norm_attn[l], 2).reshape(1, LANES),
            gate_bias=_pad_gate_bias(mlstm_gate_bias[l]),
            out_norm=mlstm_out_norm[l].reshape(1, W_MLSTM),
            rel_bias=nbhd_rel_bias[l].reshape(-1),
        ))

    xp = x_prompt.reshape(Bc * Sc, D_MODEL)
    zero_state = _pack_state(jnp.zeros((Bc, 2, H_MLSTM, HEAD_DIM, HEAD_DIM), F32),
                             jnp.zeros((Bc, 2, H_MLSTM, HEAD_DIM), F32),
                             jnp.zeros((Bc, 2, H_MLSTM), F32))
    ctx = []
    for l in range(DEPTH):
        mods = mods_all[l, 0:1].reshape(1, 1, N_MOD * D_MODEL)
        xp, extras = _layer(xp, mods, layers[l], B=Bc, S=Sc, cond_rows=Bc * Sc, rope_tabs=None,
                            ctx_cache=None, state=zero_state, name=f"ctx{l}")
        ctx.append(extras)
    new_k_attn = jnp.stack([e[0].reshape(Bc, Sc, KV_ATTN, HEAD_DIM) for e in ctx], axis=1)
    new_v_attn = jnp.stack([e[1].reshape(Bc, Sc, KV_ATTN, HEAD_DIM) for e in ctx], axis=1)
    new_k_nbhd = jnp.stack([e[2].reshape(Bc, Sc, H_NBHD, HEAD_DIM) for e in ctx], axis=1)
    new_v_nbhd = jnp.stack([e[3].reshape(Bc, Sc, H_NBHD, HEAD_DIM) for e in ctx], axis=1)
    states = [_unpack_state(e[4], e[5]) for e in ctx]
    new_C = jnp.stack([s[0] for s in states], axis=1)
    new_n = jnp.stack([s[1] for s in states], axis=1)
    new_m = jnp.stack([s[2] for s in states], axis=1)

    xs = x_sample.reshape(Bl * Sl, D_MODEL)
    rope_tabs = _rope_tables(Sl)
    for l in range(DEPTH):
        mods = mods_all[l, 1:1 + Bl].reshape(Bl, 1, N_MOD * D_MODEL)
        cache = (cache_k_attn[:, l].reshape(Bl, P, W_KV), cache_v_attn[:, l].reshape(Bl, P, W_KV),
                 cache_k_nbhd[:, l].reshape(Bl, P, W_NBHD), cache_v_nbhd[:, l].reshape(Bl, P, W_NBHD))
        state = _pack_state(state_mlstm_C[:, l], state_mlstm_n[:, l], state_mlstm_m[:, l])
        xs, _ = _layer(xs, mods, layers[l], B=Bl, S=Sl, cond_rows=Sl, rope_tabs=rope_tabs,
                       ctx_cache=cache, state=state, name=f"lat{l}")

    return (xp.reshape(Bc, Sc, D_MODEL), xs.reshape(Bl, Sl, D_MODEL),
            new_k_attn, new_v_attn, new_k_nbhd, new_v_nbhd, new_C, new_n, new_m)
```

```python
import functools

import jax
import jax.numpy as jnp
import numpy as np
from jax import lax
from jax.experimental import pallas as pl
from jax.experimental.pallas import tpu as pltpu

F32 = jnp.float32
BF16 = jnp.bfloat16

D_MODEL = 1024
DEPTH = 2
GRID_W = 64
HEAD_DIM = 64
H_ATTN = 6
KV_ATTN = 2
H_MLSTM = 4
H_NBHD = 6
D_FF = 4 * D_MODEL
NA_ROWS = 8
NA_COLS = 16
ROPE_THETA = 10000.0
EPS = 1e-6
N_MOD = 6
W_ATTN = H_ATTN * HEAD_DIM
W_KV = KV_ATTN * HEAD_DIM
W_MLSTM = H_MLSTM * HEAD_DIM
W_NBHD = H_NBHD * HEAD_DIM
N_GATES = 4 * H_MLSTM

LANES = 128
V7X_VMEM_BYTES = 64 * 1024 * 1024
VMEM_CAP_BYTES = 56 * 1024 * 1024

TM_PROJ = 512
TM_FFN = 1024
TF_FFN = 1024
TQ_ATTN = 512
CK_ATTN = 512
ATTN_LOOKAHEAD = 2
L_CHUNK = 128
MLSTM_GROUP = 2
MLSTM_GROUP_A = 4
NB_GROUP = 8
NB_SLAB = 16
NB_STEP_GROUPS = 2
NB_KV_BLOCK = 256
NEG = -1e30
LOG2E = 1.4426950408889634

_COLS = {}
_off = 0
for _name, _w in (("qa", W_ATTN), ("ka", W_KV), ("va", W_KV), ("qb", W_MLSTM), ("kb", W_MLSTM),
                  ("vb", W_MLSTM), ("ob", W_MLSTM), ("gf", LANES), ("gb", LANES),
                  ("qc", W_NBHD), ("kc", W_NBHD), ("vc", W_NBHD)):
    _COLS[_name] = (_off, _off + _w)
    _off += _w
IN_PAD = _off


def _vmem_limit(nbytes):
    return int(min(max(nbytes, 16 * 1024 * 1024), VMEM_CAP_BYTES))


def _dot(a, b):
    return jnp.dot(a, b, preferred_element_type=F32)


def _dot_nt(a, b):
    return lax.dot_general(a, b, (((1,), (1,)), ((), ())), preferred_element_type=F32)


def _lane_lo(shape):
    return (lax.broadcasted_iota(jnp.int32, shape, len(shape) - 1) % LANES) < HEAD_DIM


def _rms(x, g):
    ms = jnp.mean(x * x, axis=-1, keepdims=True)
    return (x * lax.rsqrt(ms + EPS)) * g


def _pair_rms(x, g):
    lo = _lane_lo(x.shape)
    x2 = x * x
    s_lo = jnp.sum(jnp.where(lo, x2, 0.0), axis=-1, keepdims=True)
    s_hi = jnp.sum(jnp.where(lo, 0.0, x2), axis=-1, keepdims=True)
    r = jnp.where(lo, lax.rsqrt(s_lo / HEAD_DIM + EPS), lax.rsqrt(s_hi / HEAD_DIM + EPS))
    return (x * r) * g


def _sigmoid(x):
    return 1.0 / (1.0 + jnp.exp(-x))


def _mods_kernel(c_ref, w_ref, b_ref, o_ref):
    c = c_ref[...]
    s = (c * _sigmoid(c)).astype(BF16)
    o_ref[0] = _dot(s, w_ref[0].astype(BF16)) + b_ref[0]


def _modulation(cond, w_ada, b_ada):
    n = cond.shape[0]
    tn = D_MODEL
    return pl.pallas_call(
        _mods_kernel,
        grid=(DEPTH, N_MOD * D_MODEL // tn),
        in_specs=[pl.BlockSpec((n, D_MODEL), lambda l, j: (0, 0)),
                  pl.BlockSpec((1, D_MODEL, tn), lambda l, j: (l, 0, j)),
                  pl.BlockSpec((1, 1, tn), lambda l, j: (l, 0, j))],
        out_specs=pl.BlockSpec((1, n, tn), lambda l, j: (l, 0, j)),
        out_shape=jax.ShapeDtypeStruct((DEPTH, n, N_MOD * D_MODEL), F32),
        compiler_params=pltpu.CompilerParams(
            dimension_semantics=("arbitrary", "arbitrary"),
            vmem_limit_bytes=_vmem_limit(4 * D_MODEL * tn * 4)),
        name="modulation",
    )(cond, w_ada, b_ada.reshape(DEPTH, 1, N_MOD * D_MODEL))


def _inproj_kernel(*refs, rope):
    if rope:
        (x_ref, mod_ref, g_ref, w_ref, qn_ref, kn_ref, gbias_ref, cos_ref, sin_ref,
         qa_ref, ka_ref, va_ref, qb_ref, kb_ref, vb_ref, ob_ref, gate_ref,
         qc_ref, kc_ref, vc_ref) = refs
    else:
        (x_ref, mod_ref, g_ref, w_ref, qn_ref, kn_ref, gbias_ref,
         qa_ref, ka_ref, va_ref, qb_ref, kb_ref, vb_ref, ob_ref, gate_ref,
         qc_ref, kc_ref, vc_ref) = refs
    x = x_ref[...]
    mod = mod_ref[0]
    sh1 = mod[:, 0:D_MODEL]
    sc1 = mod[:, D_MODEL:2 * D_MODEL]
    hb = (_rms(x, g_ref[...]) * (1.0 + sc1) + sh1).astype(BF16)

    def proj(name, j=0, w=None):
        lo, hi = _COLS[name]
        lo = lo + j
        hi = hi if w is None else lo + w
        return _dot(hb, w_ref[:, lo:hi])

    scale = HEAD_DIM ** -0.5
    q_scale = scale * LOG2E

    def rotary(t):
        first = (lax.broadcasted_iota(jnp.int32, t.shape, 1) % 32) < 16
        partner = jnp.where(first, pltpu.roll(t, LANES - 16, 1), pltpu.roll(t, 16, 1))
        return t * cos_ref[...] + partner * sin_ref[...]

    for j in range(W_ATTN // LANES):
        t = _pair_rms(proj("qa", j * LANES, LANES), qn_ref[...])
        if rope:
            t = rotary(t)
        qa_ref[:, j * LANES:(j + 1) * LANES] = (t * q_scale).astype(qa_ref.dtype)
    t = _pair_rms(proj("ka"), kn_ref[...])
    if rope:
        t = rotary(t)
    ka_ref[...] = t.astype(ka_ref.dtype)
    va_ref[...] = proj("va").astype(va_ref.dtype)
    qb_ref[...] = proj("qb").astype(qb_ref.dtype)
    kb_ref[...] = (proj("kb") * scale).astype(kb_ref.dtype)
    vb_ref[...] = proj("vb").astype(vb_ref.dtype)
    ob_ref[...] = proj("ob").astype(ob_ref.dtype)
    for j, name in enumerate(("gf", "gb")):
        gt = proj(name) + gbias_ref[:, j * LANES:(j + 1) * LANES]
        lane = lax.broadcasted_iota(jnp.int32, gt.shape, 1)
        is_f = (lane >= H_MLSTM) & (lane < 2 * H_MLSTM)
        logsig = jnp.minimum(gt, 0.0) - jnp.log1p(jnp.exp(-jnp.abs(gt)))
        gate_ref[:, j * LANES:(j + 1) * LANES] = jnp.where(is_f, logsig, gt)
    qc_ref[...] = (proj("qc") * q_scale).astype(qc_ref.dtype)
    kc_ref[...] = proj("kc").astype(kc_ref.dtype)
    vc_ref[...] = proj("vc").astype(vc_ref.dtype)


def _inproj(x, mods, g_pre, w_in_p, qn, kn, gbias, rope_tabs, *, rows_per_cond, kv_dtype, name):
    T = x.shape[0]
    tm = TM_PROJ
    bpc = rows_per_cond // tm
    rope = rope_tabs is not None
    row = lambda i: (i, 0)
    const = lambda i: (0, 0)
    in_specs = [pl.BlockSpec((tm, D_MODEL), row),
                pl.BlockSpec((1, 1, N_MOD * D_MODEL), lambda i: (i // bpc, 0, 0)),
                pl.BlockSpec((1, D_MODEL), const),
                pl.BlockSpec((D_MODEL, IN_PAD), const),
                pl.BlockSpec((1, LANES), const),
                pl.BlockSpec((1, LANES), const),
                pl.BlockSpec((1, 2 * LANES), const)]
    args = [x, mods, g_pre, w_in_p, qn, kn, gbias]
    if rope:
        nblk = rope_tabs[0].shape[0] // tm
        in_specs += [pl.BlockSpec((tm, LANES), lambda i: (i % nblk, 0))] * 2
        args += list(rope_tabs)
    widths = [("qa", W_ATTN, BF16), ("ka", W_KV, kv_dtype), ("va", W_KV, kv_dtype),
              ("qb", W_MLSTM, BF16), ("kb", W_MLSTM, BF16), ("vb", W_MLSTM, BF16),
              ("ob", W_MLSTM, F32), ("g", 2 * LANES, F32),
              ("qc", W_NBHD, BF16), ("kc", W_NBHD, kv_dtype), ("vc", W_NBHD, kv_dtype)]
    out_specs = [pl.BlockSpec((tm, w), row) for _, w, _ in widths]
    out_shape = [jax.ShapeDtypeStruct((T, w), dt) for _, w, dt in widths]
    est = 2 * (tm * D_MODEL * 4 + D_MODEL * IN_PAD * 2 + tm * IN_PAD * 4) + 3 * tm * IN_PAD * 4
    outs = pl.pallas_call(
        functools.partial(_inproj_kernel, rope=rope),
        grid=(T // tm,),
        in_specs=in_specs, out_specs=out_specs, out_shape=out_shape,
        compiler_params=pltpu.CompilerParams(dimension_semantics=("arbitrary",),
                                             vmem_limit_bytes=_vmem_limit(est)),
        name=name,
    )(*args)
    return dict(zip([n for n, _, _ in widths], outs))


def _attn_kernel(q_ref, k_ref, vt_ref, o_ref, qs_scr, m_scr, l_scr, acc_scr, s_ring, cmax_scr, *, heads, nch):
    nh = len(heads)
    ring = ATTN_LOOKAHEAD + 1
    assert nh % ring == 0
    for h, (kg, kh) in enumerate(heads):
        qg = q_ref[0, :, (h // 2) * LANES:(h // 2 + 1) * LANES].astype(F32)
        if h % 2 != kh:
            qg = pltpu.roll(qg, HEAD_DIM, 1)
        keep = _lane_lo(qg.shape) if kh == 0 else jnp.logical_not(_lane_lo(qg.shape))
        qs_scr[h] = jnp.where(keep, qg, 0.0).astype(BF16)
    m_scr[...] = jnp.full(m_scr.shape, -jnp.inf, F32)
    l_scr[...] = jnp.zeros(l_scr.shape, F32)
    acc_scr[...] = jnp.zeros(acc_scr.shape, F32)

    def scores(j, item):
        h = item % nh
        kg = heads[h][0]
        kj = k_ref[0, j, :, kg * LANES:(kg + 1) * LANES].astype(BF16)
        st = _dot_nt(kj, qs_scr[h])
        s_ring[item % ring] = st
        cmax_scr[item % ring] = jnp.max(st, axis=0, keepdims=True)

    def chunk(j, carry):
        j_next = jnp.minimum(j + 1, nch - 1)
        for h, (kg, kh) in enumerate(heads):
            ahead = h + ATTN_LOOKAHEAD
            scores(j if ahead < nh else j_next, ahead)
            m_old = m_scr[h]
            m_new = jnp.maximum(m_old, cmax_scr[h % ring])
            e = jnp.exp2(s_ring[h % ring] - m_new)
            alpha = jnp.exp2(m_old - m_new)
            l_scr[h] = alpha * l_scr[h] + jnp.sum(e, axis=0, keepdims=True)
            r = kg * LANES + kh * HEAD_DIM
            vt = vt_ref[0, j, r:r + HEAD_DIM, :].astype(BF16)
            acc_scr[h] = alpha * acc_scr[h] + _dot(vt, e.astype(BF16))
            m_scr[h] = m_new
        return carry

    for item in range(ATTN_LOOKAHEAD):
        scores(0, item)
    lax.fori_loop(0, nch, chunk, 0)
    for t in range(len(heads) // 2):
        ot = jnp.concatenate([acc_scr[2 * t] / l_scr[2 * t], acc_scr[2 * t + 1] / l_scr[2 * t + 1]], axis=0)
        o_ref[0, :, t * LANES:(t + 1) * LANES] = ot.T.astype(o_ref.dtype)


def _attention(q, k, v, *, heads, name):
    B, Sq, W = q.shape
    Sk, KW = k.shape[1], k.shape[2]
    tq = min(TQ_ATTN, Sq)
    ck = min(CK_ATTN, Sk)
    nch = Sk // ck
    nh = len(heads)
    k4 = k.reshape(B, nch, ck, KW)
    vt4 = v.reshape(B, nch, ck, KW).transpose(0, 1, 3, 2)
    isz = k.dtype.itemsize
    est = (2 * (2 * tq * W * 2 + 2 * Sk * KW * isz) + nh * tq * (LANES * 2 + HEAD_DIM * 4 + 64)
           + 8 * nh * ck * tq * 4)
    return pl.pallas_call(
        functools.partial(_attn_kernel, heads=heads, nch=nch),
        grid=(B, Sq // tq),
        in_specs=[pl.BlockSpec((1, tq, W), lambda b, i: (b, i, 0)),
                  pl.BlockSpec((1, nch, ck, KW), lambda b, i: (b, 0, 0, 0)),
                  pl.BlockSpec((1, nch, KW, ck), lambda b, i: (b, 0, 0, 0))],
        out_specs=pl.BlockSpec((1, tq, W), lambda b, i: (b, i, 0)),
        out_shape=jax.ShapeDtypeStruct((B, Sq, W), BF16),
        scratch_shapes=[pltpu.VMEM((nh, tq, LANES), BF16),
                        pltpu.VMEM((nh, 1, tq), F32),
                        pltpu.VMEM((nh, 1, tq), F32),
                        pltpu.VMEM((nh, HEAD_DIM, tq), F32),
                        pltpu.VMEM((ATTN_LOOKAHEAD + 1, ck, tq), F32),
                        pltpu.VMEM((ATTN_LOOKAHEAD + 1, 1, tq), F32)],
        compiler_params=pltpu.CompilerParams(dimension_semantics=("arbitrary", "arbitrary"),
                                             vmem_limit_bytes=_vmem_limit(est)),
        name=name,
    )(q, k4, vt4)


_HEADS_GQA = tuple((0, h // (H_ATTN // KV_ATTN)) for h in range(H_ATTN))
_HEADS_MHA = tuple((h // 2, h % 2) for h in range(H_NBHD))


def _nbhd_window(r, rows):
    kr = min(NA_ROWS, rows)
    return min(max(r - kr // 2, 0), rows - kr), kr


def _nbhd_patterns(rows):
    n_groups = rows // NB_GROUP
    pats = []
    for g in (0, 1, n_groups - 1):
        r0 = g * NB_GROUP
        pats.append((r0, min(max(r0 - NA_ROWS // 2, 0), rows - NB_SLAB)))
    return pats


def _nbhd_kernel(rb_ref, q_ref, k_ref, vt_ref, kc_ref, vct_ref, o_ref,
                 bias_scr, qs_scr, m_scr, l_scr, acc_scr, s_ring, cmax_scr, *, rows):
    hp = pl.program_id(0)
    b = pl.program_id(1)
    gs = pl.program_id(2)
    n_groups = rows // NB_GROUP
    n_dr = 2 * NA_ROWS - 1
    n_dc = 2 * NA_COLS - 1
    tq = NB_GROUP * GRID_W
    ck = tq
    pats = _nbhd_patterns(rows)
    ring = ATTN_LOOKAHEAD + 1
    n_chunks = NB_SLAB * GRID_W // ck + 1
    assert n_chunks % ring == 0 and kc_ref.shape[1] == ck
    blk_rows = NB_KV_BLOCK // GRID_W

    @pl.when((b == 0) & (gs == 0))
    def _build_bias():
        shape = (GRID_W, LANES)
        w = lax.broadcasted_iota(jnp.int32, shape, 0)
        lane = lax.broadcasted_iota(jnp.int32, shape, 1)
        cc = lane % GRID_W
        second = lane >= GRID_W
        cs = jnp.clip(w - NA_COLS // 2, 0, GRID_W - NA_COLS)
        col_ok = (cc >= cs) & (cc < cs + NA_COLS)
        dc = cc - w + (NA_COLS - 1)
        for hh in range(2):
            base = (2 * hp + hh) * (n_dr * n_dc)
            tiles = {}
            for d in range(-1, n_dr):
                acc = jnp.zeros(shape, F32)
                for j in range(n_dc):
                    va = rb_ref[base + d * n_dc + j] * LOG2E if d >= 0 else 0.0
                    vb = rb_ref[base + (d + 1) * n_dc + j] * LOG2E if d + 1 < n_dr else 0.0
                    acc = acc + jnp.where(dc == j, jnp.where(second, vb, va), 0.0)
                tiles[d] = acc
            def query_row_tile(r, kra):
                rs, kr = _nbhd_window(r, rows)
                ok_a = rs <= kra < rs + kr
                ok_b = rs <= kra + 1 < rs + kr
                if not (ok_a or ok_b):
                    return jnp.full(shape, NEG, F32)
                row_ok = (jnp.logical_not(second) if ok_a and not ok_b else
                          second if ok_b and not ok_a else None)
                ok = col_ok if row_ok is None else (col_ok & row_ok)
                return jnp.where(ok, tiles[kra - r + (NA_ROWS - 1)], NEG)

            for pi, (r0, slab0) in enumerate(pats):
                for ip in range(NB_GROUP // 2):
                    for ap in range(NB_SLAB // 2):
                        kra = slab0 + 2 * ap
                        two_rows = jnp.concatenate([query_row_tile(r0 + 2 * ip, kra),
                                                    query_row_tile(r0 + 2 * ip + 1, kra)], axis=0)
                        bias_scr[hh, pi, ap * LANES:(ap + 1) * LANES, ip * LANES:(ip + 1) * LANES] = two_rows.T

    lo = _lane_lo((tq, LANES))
    n_items = NB_STEP_GROUPS * 2 * n_chunks
    for s in range(NB_STEP_GROUPS):
        q = q_ref[0, s * tq:(s + 1) * tq, :].astype(F32)
        for hh in range(2):
            qs_scr[2 * s + hh] = jnp.where(lo if hh == 0 else jnp.logical_not(lo), q, 0.0).astype(BF16)
    m_scr[...] = jnp.full(m_scr.shape, -jnp.inf, F32)
    l_scr[...] = jnp.zeros(l_scr.shape, F32)
    acc_scr[...] = jnp.zeros(acc_scr.shape, F32)

    def group_of(s):
        g = gs * NB_STEP_GROUPS + s
        pat = jnp.where(g == 0, 0, jnp.where(g == n_groups - 1, 2, 1))
        slab0 = jnp.clip(g * NB_GROUP - NA_ROWS // 2, 0, rows - NB_SLAB)
        return pat, slab0 // blk_rows

    def scores(item):
        s, hh, c = item // (2 * n_chunks), (item // n_chunks) % 2, item % n_chunks
        qm = qs_scr[2 * s + hh]
        if c < n_chunks - 1:
            pat, blk0 = group_of(s)
            kc = k_ref[0, pl.ds(blk0 + c * (ck // NB_KV_BLOCK), ck // NB_KV_BLOCK)].reshape(ck, LANES)
            st = _dot_nt(kc, qm) + bias_scr[hh, pat, c * ck:(c + 1) * ck, :]
        else:
            st = _dot_nt(kc_ref[0].astype(BF16), qm)
        s_ring[item % ring] = st
        cmax_scr[item % ring] = jnp.max(st, axis=0, keepdims=True)

    for item in range(ATTN_LOOKAHEAD):
        scores(item)
    for item in range(n_items):
        if item + ATTN_LOOKAHEAD < n_items:
            scores(item + ATTN_LOOKAHEAD)
        s, hh, c = item // (2 * n_chunks), (item // n_chunks) % 2, item % n_chunks
        u = 2 * s + hh
        m_old = m_scr[u]
        m_new = jnp.maximum(m_old, cmax_scr[item % ring])
        e = jnp.exp2(s_ring[item % ring] - m_new)
        alpha = jnp.exp2(m_old - m_new)
        l_scr[u] = alpha * l_scr[u] + jnp.sum(e, axis=0, keepdims=True)
        if c < n_chunks - 1:
            _, blk0 = group_of(s)
            nb = ck // NB_KV_BLOCK
            vt = jnp.concatenate([vt_ref[0, blk0 + c * nb + i, hh * HEAD_DIM:(hh + 1) * HEAD_DIM, :]
                                  for i in range(nb)], axis=1)
        else:
            vt = vct_ref[0, hh * HEAD_DIM:(hh + 1) * HEAD_DIM, :].astype(BF16)
        acc_scr[u] = alpha * acc_scr[u] + _dot(vt, e.astype(BF16))
        m_scr[u] = m_new
    for s in range(NB_STEP_GROUPS):
        ot = jnp.concatenate([acc_scr[2 * s] / l_scr[2 * s], acc_scr[2 * s + 1] / l_scr[2 * s + 1]], axis=0)
        o_ref[0, s * tq:(s + 1) * tq, :] = ot.T.astype(o_ref.dtype)


def _nbhd_attention(q, k, v, k_ctx, v_ctx, rel_bias_flat, *, name):
    B, S, W = q.shape
    P = k_ctx.shape[1]
    rows = S // GRID_W
    tq = NB_GROUP * GRID_W
    nk = NB_SLAB * GRID_W
    nblk = S // NB_KV_BLOCK
    tqs = NB_STEP_GROUPS * tq
    nu = 2 * NB_STEP_GROUPS
    ring = ATTN_LOOKAHEAD + 1
    k4 = k.reshape(B, nblk, NB_KV_BLOCK, W)
    vt4 = v.reshape(B, nblk, NB_KV_BLOCK, W).transpose(0, 1, 3, 2)
    vct = v_ctx.transpose(0, 2, 1)
    est = (2 * (2 * tqs * LANES * 2 + 2 * S * LANES * 2 + 2 * P * LANES * 4)
           + 2 * 3 * tq * nk * 4 + ring * tq * tq * 4 + 8 * tq * tq * 4 + nu * tq * 1024)
    return pl.pallas_call(
        functools.partial(_nbhd_kernel, rows=rows),
        grid=(W // LANES, B, rows // (NB_GROUP * NB_STEP_GROUPS)),
        in_specs=[pl.BlockSpec(memory_space=pltpu.SMEM),
                  pl.BlockSpec((1, tqs, LANES), lambda p, b, g: (b, g, p)),
                  pl.BlockSpec((1, nblk, NB_KV_BLOCK, LANES), lambda p, b, g: (b, 0, 0, p)),
                  pl.BlockSpec((1, nblk, LANES, NB_KV_BLOCK), lambda p, b, g: (b, 0, p, 0)),
                  pl.BlockSpec((1, P, LANES), lambda p, b, g: (b, 0, p)),
                  pl.BlockSpec((1, LANES, P), lambda p, b, g: (b, p, 0))],
        out_specs=pl.BlockSpec((1, tqs, LANES), lambda p, b, g: (b, g, p)),
        out_shape=jax.ShapeDtypeStruct((B, S, W), BF16),
        scratch_shapes=[pltpu.VMEM((2, 3, nk, tq), F32),
                        pltpu.VMEM((nu, tq, LANES), BF16),
                        pltpu.VMEM((nu, 1, tq), F32),
                        pltpu.VMEM((nu, 1, tq), F32),
                        pltpu.VMEM((nu, HEAD_DIM, tq), F32),
                        pltpu.VMEM((ring, tq, tq), F32),
                        pltpu.VMEM((ring, 1, tq), F32)],
        compiler_params=pltpu.CompilerParams(dimension_semantics=("arbitrary",) * 3,
                                             vmem_limit_bytes=_vmem_limit(est)),
        name=name,
    )(rel_bias_flat, q, k4, vt4, k_ctx, vct)


def _split3(x):
    hi = x.astype(BF16)
    r = x - hi.astype(F32)
    mid = r.astype(BF16)
    return hi, mid, (r - mid.astype(F32)).astype(BF16)


def _mlstm_kernel(q_ref, k_ref, v_ref, g_ref, ob_ref, s0_ref, m0_ref, on_ref,
                  out_ref, sf_ref, mf_ref,
                  h_scr, nat_scr, rows_scr, stat_scr, mprev_scr, un_scr, st_scr, *, nc, grp_a, grp):
    d = pl.program_id(1)
    L = L_CHUNK
    NP = H_MLSTM // 2
    row = lax.broadcasted_iota(jnp.int32, (L, L), 0)
    col = lax.broadcasted_iota(jnp.int32, (L, L), 1)
    sign = 1 - 2 * d
    mask = (col - row) * sign <= 0
    maskb = mask.astype(BF16)
    mask3 = jnp.concatenate([maskb, maskb, maskb], axis=1)
    lane = lax.broadcasted_iota(jnp.int32, (L, LANES), 1)
    lo = lane < HEAD_DIM
    top = row < HEAD_DIM
    row2 = lax.broadcasted_iota(jnp.int32, (L, 2 * LANES), 0)
    col2 = lax.broadcasted_iota(jnp.int32, (L, 2 * LANES), 1)
    keep_state = (row2 < HEAD_DIM) == ((col2 % LANES) < HEAD_DIM)
    top2 = row2 < HEAD_DIM
    ones_b = jnp.ones((L, LANES), BF16)
    ones_lo = lo.astype(BF16)
    ones_hi = jnp.logical_not(lo).astype(BF16)

    def chunk_rows(c):
        return pl.ds(pl.multiple_of(c * L, L), L)

    def pass_a(it, carry):
        cs = [it * grp_a + u for u in range(grp_a)]
        gts = [g_ref[0, chunk_rows(c), :] for c in cs]
        bns = [_dot(mask3, jnp.concatenate(_split3(gt), axis=0)) for gt in gts]
        a_all = []
        for c, gt, bn in zip(cs, gts, bns):
            nat = jnp.where(lane < H_MLSTM, gt, bn)
            nat_scr[chunk_rows(c), :] = nat * (-LOG2E)
            nat_t = nat.T
            b_rows = nat_t[H_MLSTM:2 * H_MLSTM]
            c_rows = nat_t[0:H_MLSTM] - b_rows
            rows_scr[c] = jnp.concatenate([c_rows * LOG2E, b_rows], axis=0)
            c_max = jnp.max(c_rows, axis=1, keepdims=True)
            b_tot = jnp.where(d == 0, b_rows[:, L - 1:L], b_rows[:, 0:1])
            stat_scr[c] = jnp.concatenate([jnp.broadcast_to(c_max, (H_MLSTM, LANES)),
                                           jnp.broadcast_to(b_tot, (H_MLSTM, LANES))], axis=0)
            a_all.append(jnp.exp(c_rows - c_max))
        for c, a_rows in zip(cs, a_all):
            rows = chunk_rows(c)
            for p in range(NP):
                lanes = slice(p * LANES, (p + 1) * LANES)
                k_t = k_ref[0, rows, lanes].astype(F32).T
                a_sel = jnp.where(top, a_rows[2 * p:2 * p + 1], a_rows[2 * p + 1:2 * p + 2])
                vv = jnp.concatenate([v_ref[0, rows, lanes], ones_b], axis=1)
                un = _dot((k_t * a_sel).astype(BF16), vv)
                un_scr[c, p] = jnp.where(keep_state, un, 0.0)
        return carry

    lax.fori_loop(0, nc // grp_a, pass_a, 0)

    st_scr[...] = s0_ref[0, 0]

    def pass_b(ci, m):
        c = jnp.where(d == 0, ci, nc - 1 - ci)
        st = stat_scr[c]
        c_max, b_tot = st[0:H_MLSTM], st[H_MLSTM:]
        m_new = jnp.maximum(b_tot + m, b_tot + c_max)
        d_old = jnp.exp(b_tot + m - m_new)
        d_new = jnp.exp(b_tot + c_max - m_new)
        mprev_scr[c] = jnp.concatenate([m, m], axis=0) * LOG2E
        for p in range(NP):
            def rows_of(t, p=p):
                even = jnp.concatenate([t[2 * p:2 * p + 1]] * 2, axis=1)
                odd = jnp.concatenate([t[2 * p + 1:2 * p + 2]] * 2, axis=1)
                return jnp.where(top2, even, odd)
            s_prev = st_scr[p]
            st_scr[p] = rows_of(d_old) * s_prev + rows_of(d_new) * un_scr[c, p]
            un_scr[c, p] = s_prev
        return m_new

    m_fin = lax.fori_loop(0, nc, pass_b, m0_ref[0, 0][0:H_MLSTM])
    sf_ref[0, 0] = st_scr[...]
    mf_ref[0, 0] = jnp.concatenate([m_fin, m_fin], axis=0)

    def pass_c(it, carry):
        cs = [it * grp + u for u in range(grp)]
        units = [(u, p) for u in range(grp) for p in range(NP)]
        early = {}
        for u, p in units:
            c = cs[u]
            rows = chunk_rows(c)
            lanes = slice(p * LANES, (p + 1) * LANES)
            qp = q_ref[0, rows, lanes]
            kp = k_ref[0, rows, lanes]
            s_in = un_scr[c, p].astype(BF16)
            qms = [jnp.where(lo if hh == 0 else jnp.logical_not(lo), qp, jnp.zeros_like(qp)) for hh in range(2)]
            early[u, p] = ([_dot_nt(qm, kp) for qm in qms],
                           _dot(qp, s_in))
        mid = {}
        for u, p in units:
            r_t = rows_scr[cs[u]]
            m_in = mprev_scr[cs[u]]
            for hh in range(2):
                h = 2 * p + hh
                cm = jnp.where(mask, r_t[h:h + 1, :], -jnp.inf)
                m_prev = m_in[h:h + 1, :]
                mu = jnp.maximum(jnp.broadcast_to(jnp.max(cm, axis=1, keepdims=True), (L, LANES)), m_prev)
                w = early[u, p][0][hh] * jnp.exp2(cm - mu)
                mid[u, p, hh] = (w.astype(BF16), mu, m_prev)
        for u, p in units:
            rows = chunk_rows(cs[u])
            lanes = slice(p * LANES, (p + 1) * LANES)
            vp = v_ref[0, rows, lanes]
            zero = jnp.zeros_like(vp)
            vv = jnp.concatenate([jnp.concatenate([jnp.where(lo, vp, zero), ones_lo], axis=1),
                                  jnp.concatenate([jnp.where(lo, zero, vp), ones_hi], axis=1)], axis=0)
            w2 = jnp.concatenate([mid[u, p, 0][0], mid[u, p, 1][0]], axis=1)
            nd = _dot(w2, vv)
            nat = nat_scr[rows, :]
            nb = [jnp.broadcast_to(nat[:, H_MLSTM + 2 * p + hh:H_MLSTM + 2 * p + hh + 1], (L, LANES))
                  for hh in range(2)]
            fs = early[u, p][1]
            mu = jnp.where(lo, mid[u, p, 0][1], mid[u, p, 1][1])
            m_prev = jnp.where(lo[0:1], mid[u, p, 0][2], mid[u, p, 1][2])
            w_inter = jnp.exp2(m_prev - mu)
            den = nd[:, LANES:] + w_inter * fs[:, LANES:]
            den = jnp.maximum(jnp.abs(den), jnp.exp2(jnp.where(lo, nb[0], nb[1]) - mu))
            h_scr[d, rows, lanes] = (nd[:, :LANES] + w_inter * fs[:, :LANES]) / den
        return carry

    lax.fori_loop(0, nc // grp, pass_c, 0)

    @pl.when(d == 1)
    def _finish():
        def rows_block(c, carry):
            rows = chunk_rows(c)
            for p in range(NP):
                lanes = slice(p * LANES, (p + 1) * LANES)
                hn = _pair_rms(h_scr[0, rows, lanes] + h_scr[1, rows, lanes], on_ref[:, lanes])
                out_ref[0, rows, lanes] = (_sigmoid(ob_ref[0, rows, lanes]) * hn).astype(out_ref.dtype)
            return carry

        lax.fori_loop(0, nc, rows_block, 0)


def _mlstm(q, k, v, gates, ob, s0, m0, out_norm, *, name):
    assert L_CHUNK == LANES
    B, S, W = q.shape
    nc = S // L_CHUNK
    grp = min(MLSTM_GROUP, nc)
    grp_a = min(MLSTM_GROUP_A, nc)
    npair = H_MLSTM // 2
    seq = lambda b, d: (b, 0, 0)
    est = (2 * (3 * S * W * 2 + S * LANES * 4 + S * W * 4 + S * W * 2) + 2 * S * W * 4 + S * LANES * 4
           + nc * npair * LANES * 2 * LANES * 4 + 12 * 1024 * 1024)
    return pl.pallas_call(
        functools.partial(_mlstm_kernel, nc=nc, grp_a=grp_a, grp=grp),
        grid=(B, 2),
        in_specs=[pl.BlockSpec((1, S, W), seq), pl.BlockSpec((1, S, W), seq), pl.BlockSpec((1, S, W), seq),
                  pl.BlockSpec((1, S, LANES), lambda b, d: (b, 0, d)),
                  pl.BlockSpec((1, S, W), seq),
                  pl.BlockSpec((1, 1, npair, LANES, 2 * LANES), lambda b, d: (b, d, 0, 0, 0)),
                  pl.BlockSpec((1, 1, 8, LANES), lambda b, d: (b, d, 0, 0)),
                  pl.BlockSpec((1, W), lambda b, d: (0, 0))],
        out_specs=[pl.BlockSpec((1, S, W), seq),
                   pl.BlockSpec((1, 1, npair, LANES, 2 * LANES), lambda b, d: (b, d, 0, 0, 0)),
                   pl.BlockSpec((1, 1, 8, LANES), lambda b, d: (b, d, 0, 0))],
        out_shape=[jax.ShapeDtypeStruct((B, S, W), BF16),
                   jax.ShapeDtypeStruct((B, 2, npair, LANES, 2 * LANES), F32),
                   jax.ShapeDtypeStruct((B, 2, 8, LANES), F32)],
        scratch_shapes=[pltpu.VMEM((2, S, W), F32),
                        pltpu.VMEM((S, LANES), F32),
                        pltpu.VMEM((nc, 8, L_CHUNK), F32),
                        pltpu.VMEM((nc, 8, LANES), F32),
                        pltpu.VMEM((nc, 8, LANES), F32),
                        pltpu.VMEM((nc, npair, LANES, 2 * LANES), F32),
                        pltpu.VMEM((npair, LANES, 2 * LANES), F32)],
        compiler_params=pltpu.CompilerParams(dimension_semantics=("arbitrary", "arbitrary"),
                                             vmem_limit_bytes=_vmem_limit(est)),
        name=name,
    )(q, k, v, gates, ob, s0, m0, out_norm)


def _outproj_kernel(a_ref, b_ref, c_ref, x_ref, mod_ref, w_ref, gpost_ref, gpre_ref, x1_ref, h2_ref):
    mo = (_dot(a_ref[...], w_ref[0:W_ATTN, :])
          + _dot(b_ref[...], w_ref[W_ATTN:W_ATTN + W_MLSTM, :])
          + _dot(c_ref[...], w_ref[W_ATTN + W_MLSTM:, :]))
    mod = mod_ref[0]
    gt1 = mod[:, 2 * D_MODEL:3 * D_MODEL]
    sh2 = mod[:, 3 * D_MODEL:4 * D_MODEL]
    sc2 = mod[:, 4 * D_MODEL:5 * D_MODEL]
    x1 = x_ref[...] + gt1 * _rms(mo, gpost_ref[...])
    x1_ref[...] = x1
    h2_ref[...] = (_rms(x1, gpre_ref[...]) * (1.0 + sc2) + sh2).astype(h2_ref.dtype)


def _outproj(oa, ob, oc, x, mods, w_out, g_post, g_pre, *, rows_per_cond, name):
    T = x.shape[0]
    tm = TM_PROJ
    bpc = rows_per_cond // tm
    row = lambda i: (i, 0)
    const = lambda i: (0, 0)
    est = 2 * (tm * D_MODEL * (2 + 4 + 4 + 2) + D_MODEL * D_MODEL * 2) + 4 * tm * D_MODEL * 4
    return pl.pallas_call(
        _outproj_kernel,
        grid=(T // tm,),
        in_specs=[pl.BlockSpec((tm, W_ATTN), row), pl.BlockSpec((tm, W_MLSTM), row),
                  pl.BlockSpec((tm, W_NBHD), row), pl.BlockSpec((tm, D_MODEL), row),
                  pl.BlockSpec((1, 1, N_MOD * D_MODEL), lambda i: (i // bpc, 0, 0)),
                  pl.BlockSpec((D_MODEL, D_MODEL), const),
                  pl.BlockSpec((1, D_MODEL), const), pl.BlockSpec((1, D_MODEL), const)],
        out_specs=[pl.BlockSpec((tm, D_MODEL), row), pl.BlockSpec((tm, D_MODEL), row)],
        out_shape=[jax.ShapeDtypeStruct((T, D_MODEL), F32), jax.ShapeDtypeStruct((T, D_MODEL), BF16)],
        compiler_params=pltpu.CompilerParams(dimension_semantics=("arbitrary",),
                                             vmem_limit_bytes=_vmem_limit(est)),
        name=name,
    )(oa, ob, oc, x, mods, w_out, g_post, g_pre)


def _ffn_kernel(h_ref, x_ref, mod_ref, wu_ref, wd_ref, g_ref, o_ref, acc_ref):
    j = pl.program_id(1)

    @pl.when(j == 0)
    def _zero():
        acc_ref[...] = jnp.zeros_like(acc_ref)

    u = jnp.maximum(_dot(h_ref[...], wu_ref[...]), 0.0)
    acc_ref[...] += _dot((u * u).astype(BF16), wd_ref[...])

    @pl.when(j == pl.num_programs(1) - 1)
    def _finish():
        gt2 = mod_ref[0][:, 5 * D_MODEL:6 * D_MODEL]
        o_ref[...] = x_ref[...] + gt2 * _rms(acc_ref[...], g_ref[...])


def _ffn(h2, x1, mods, w_up, w_down, g_post, *, rows_per_cond, name):
    T = x1.shape[0]
    tm, tf = TM_FFN, TF_FFN
    tm = min(tm, rows_per_cond)
    bpc = rows_per_cond // tm
    est = 2 * (tm * D_MODEL * (2 + 4 + 4) + 2 * D_MODEL * tf * 2) + tm * D_MODEL * 4 + 3 * tm * tf * 4
    return pl.pallas_call(
        _ffn_kernel,
        grid=(T // tm, D_FF // tf),
        in_specs=[pl.BlockSpec((tm, D_MODEL), lambda i, j: (i, 0)),
                  pl.BlockSpec((tm, D_MODEL), lambda i, j: (i, 0)),
                  pl.BlockSpec((1, 1, N_MOD * D_MODEL), lambda i, j: (i // bpc, 0, 0)),
                  pl.BlockSpec((D_MODEL, tf), lambda i, j: (0, j)),
                  pl.BlockSpec((tf, D_MODEL), lambda i, j: (j, 0)),
                  pl.BlockSpec((1, D_MODEL), lambda i, j: (0, 0))],
        out_specs=pl.BlockSpec((tm, D_MODEL), lambda i, j: (i, 0)),
        out_shape=jax.ShapeDtypeStruct((T, D_MODEL), F32),
        scratch_shapes=[pltpu.VMEM((tm, D_MODEL), F32)],
        compiler_params=pltpu.CompilerParams(dimension_semantics=("arbitrary", "arbitrary"),
                                             vmem_limit_bytes=_vmem_limit(est)),
        name=name,
    )(h2, x1, mods, w_up, w_down, g_post)


def _pad_w_in(w_in_l):
    o = W_ATTN + 2 * W_KV + 4 * W_MLSTM
    pre, gates, post = w_in_l[:, :o], w_in_l[:, o:o + N_GATES], w_in_l[:, o + N_GATES:]
    z = jnp.zeros((D_MODEL, LANES - 2 * H_MLSTM), w_in_l.dtype)
    return jnp.concatenate([pre, gates[:, :2 * H_MLSTM], z, gates[:, 2 * H_MLSTM:], z, post],
                           axis=1).astype(BF16)


def _pad_gate_bias(gb_l):
    z = jnp.zeros((LANES - 2 * H_MLSTM,), gb_l.dtype)
    return jnp.concatenate([gb_l[:2 * H_MLSTM], z, gb_l[2 * H_MLSTM:], z]).reshape(1, 2 * LANES)


def _rope_tables(S):
    quarter = HEAD_DIM // 4
    pos = jnp.arange(S)
    inv_freq = ROPE_THETA ** (-jnp.arange(quarter, dtype=F32) / quarter)

    def tabs(p):
        ang = p.astype(F32)[:, None] * inv_freq[None, :]
        return jnp.cos(ang), jnp.sin(ang)

    cr, sr = tabs(pos // GRID_W)
    cc, sc = tabs(pos % GRID_W)
    cos = jnp.concatenate([cr, cr, cc, cc], axis=1)
    sin = jnp.concatenate([-sr, sr, -sc, sc], axis=1)
    return jnp.tile(cos, (1, 2)), jnp.tile(sin, (1, 2))


def _pack_state(C, n, m):
    B = C.shape[0]
    Cp = C.reshape(B, 2, 2, 2, HEAD_DIM, HEAD_DIM)
    z = jnp.zeros_like(Cp[:, :, :, 0])
    top = jnp.concatenate([Cp[:, :, :, 0], z], axis=-1)
    bot = jnp.concatenate([z, Cp[:, :, :, 1]], axis=-1)
    Cbd = jnp.concatenate([top, bot], axis=-2)
    n_rep = jnp.broadcast_to(n.reshape(B, 2, 2, LANES, 1), (B, 2, 2, LANES, LANES))
    same_head = (jnp.arange(LANES)[:, None] < HEAD_DIM) == (jnp.arange(LANES)[None, :] < HEAD_DIM)
    n_rep = jnp.where(same_head, n_rep, 0.0)
    m_rows = jnp.broadcast_to(m[..., None], m.shape + (LANES,))
    return jnp.concatenate([Cbd, n_rep], axis=-1), jnp.concatenate([m_rows, m_rows], axis=-2)


def _unpack_state(s_p, m_p):
    B = s_p.shape[0]
    c_even = s_p[:, :, :, :HEAD_DIM, :HEAD_DIM]
    c_odd = s_p[:, :, :, HEAD_DIM:, HEAD_DIM:LANES]
    C = jnp.stack([c_even, c_odd], axis=3).reshape(B, 2, H_MLSTM, HEAD_DIM, HEAD_DIM)
    n = jnp.concatenate([s_p[..., :HEAD_DIM, LANES], s_p[..., HEAD_DIM:, LANES + HEAD_DIM]], axis=-1)
    return C, n.reshape(B, 2, H_MLSTM, HEAD_DIM), m_p[:, :, :H_MLSTM, 0]


def _layer(x, mods, lw, *, B, S, cond_rows, rope_tabs, ctx_cache, state, name):
    T = B * S
    kv_dtype = BF16 if ctx_cache is not None else F32
    pr = _inproj(x, mods, lw["g_pre_mix"], lw["w_in"], lw["q_norm"], lw["k_norm"], lw["gate_bias"],
                 rope_tabs, rows_per_cond=cond_rows, kv_dtype=kv_dtype, name=name + "_inproj")
    seq = lambda a: a.reshape(B, S, a.shape[-1])
    qa, ka, va = seq(pr["qa"]), seq(pr["ka"]), seq(pr["va"])
    qc, kc, vc = seq(pr["qc"]), seq(pr["kc"]), seq(pr["vc"])
    if ctx_cache is None:
        out_a = _attention(qa, ka, va, heads=_HEADS_GQA, name=name + "_attn_a")
        out_c = _attention(qc, kc, vc, heads=_HEADS_MHA, name=name + "_attn_c")
    else:
        ck_a, cv_a, ck_c, cv_c = ctx_cache
        k_all = jnp.concatenate([ka, ck_a.astype(BF16)], axis=1)
        v_all = jnp.concatenate([va, cv_a.astype(BF16)], axis=1)
        out_a = _attention(qa, k_all, v_all, heads=_HEADS_GQA, name=name + "_attn_a")
        out_c = _nbhd_attention(qc, kc, vc, ck_c, cv_c, lw["rel_bias"], name=name + "_attn_c")
    s0, m0 = state
    out_b, sf, mf = _mlstm(seq(pr["qb"]), seq(pr["kb"]), seq(pr["vb"]), seq(pr["g"]), seq(pr["ob"]),
                           s0, m0, lw["out_norm"], name=name + "_mlstm")
    x1, h2 = _outproj(out_a.reshape(T, W_ATTN), out_b.reshape(T, W_MLSTM), out_c.reshape(T, W_NBHD),
                      x, mods, lw["w_out"], lw["g_post_mix"], lw["g_pre_ffn"],
                      rows_per_cond=cond_rows, name=name + "_outproj")
    x2 = _ffn(h2, x1, mods, lw["w_up"], lw["w_down"], lw["g_post_ffn"],
              rows_per_cond=cond_rows, name=name + "_ffn")
    return x2, (ka, va, kc, vc, sf, mf)


def kernel(x_prompt, x_sample, c, cache_k_attn, cache_v_attn, cache_k_nbhd, cache_v_nbhd, state_mlstm_C, state_mlstm_n, state_mlstm_m, c_ctx, w_ada, b_ada, g_pre_mix, g_post_mix, g_pre_ffn, g_post_ffn, w_in, q_norm_attn, k_norm_attn, mlstm_gate_bias, mlstm_out_norm, nbhd_rel_bias, w_out, w_ffn_up, w_ffn_down):
    Bc, Sc, _ = x_prompt.shape
    Bl, Sl, _ = x_sample.shape
    P = cache_k_attn.shape[2]
    n_cond = 8
    cond = jnp.concatenate([c_ctx[None, :], c, jnp.zeros((n_cond - 1 - Bl, D_MODEL), F32)], axis=0)
    mods_all = _modulation(cond, w_ada, b_ada)

    layers = []
    for l in range(DEPTH):
        layers.append(dict(
            w_in=_pad_w_in(w_in[l]),
            w_out=w_out[l].astype(BF16),
            w_up=w_ffn_up[l].astype(BF16),
            w_down=w_ffn_down[l].astype(BF16),
            g_pre_mix=g_pre_mix[l].reshape(1, D_MODEL), g_post_mix=g_post_mix[l].reshape(1, D_MODEL),
            g_pre_ffn=g_pre_ffn[l].reshape(1, D_MODEL), g_post_ffn=g_post_ffn[l].reshape(1, D_MODEL),
            q_norm=jnp.tile(q_norm_attn[l], 2).reshape(1, LANES),
            k_norm=jnp.tile(k_norm_attn[l], 2).reshape(1, LANES),
            gate_bias=_pad_gate_bias(mlstm_gate_bias[l]),
            out_norm=mlstm_out_norm[l].reshape(1, W_MLSTM),
            rel_bias=nbhd_rel_bias[l].reshape(-1),
        ))

    xp = x_prompt.reshape(Bc * Sc, D_MODEL)
    zero_state = _pack_state(jnp.zeros((Bc, 2, H_MLSTM, HEAD_DIM, HEAD_DIM), F32),
                             jnp.zeros((Bc, 2, H_MLSTM, HEAD_DIM), F32),
                             jnp.zeros((Bc, 2, H_MLSTM), F32))
    ctx = []
    for l in range(DEPTH):
        mods = mods_all[l, 0:1].reshape(1, 1, N_MOD * D_MODEL)
        xp, extras = _layer(xp, mods, layers[l], B=Bc, S=Sc, cond_rows=Bc * Sc, rope_tabs=None,
                            ctx_cache=None, state=zero_state, name=f"ctx{l}")
        ctx.append(extras)
    new_k_attn = jnp.stack([e[0].reshape(Bc, Sc, KV_ATTN, HEAD_DIM) for e in ctx], axis=1)
    new_v_attn = jnp.stack([e[1].reshape(Bc, Sc, KV_ATTN, HEAD_DIM) for e in ctx], axis=1)
    new_k_nbhd = jnp.stack([e[2].reshape(Bc, Sc, H_NBHD, HEAD_DIM) for e in ctx], axis=1)
    new_v_nbhd = jnp.stack([e[3].reshape(Bc, Sc, H_NBHD, HEAD_DIM) for e in ctx], axis=1)
    states = [_unpack_state(e[4], e[5]) for e in ctx]
    new_C = jnp.stack([s[0] for s in states], axis=1)
    new_n = jnp.stack([s[1] for s in states], axis=1)
    new_m = jnp.stack([s[2] for s in states], axis=1)

    xs = x_sample.reshape(Bl * Sl, D_MODEL)
    rope_tabs = _rope_tables(Sl)
    for l in range(DEPTH):
        mods = mods_all[l, 1:1 + Bl].reshape(Bl, 1, N_MOD * D_MODEL)
        cache = (cache_k_attn[:, l].reshape(Bl, P, W_KV), cache_v_attn[:, l].reshape(Bl, P, W_KV),
                 cache_k_nbhd[:, l].reshape(Bl, P, W_NBHD), cache_v_nbhd[:, l].reshape(Bl, P, W_NBHD))
        state = _pack_state(state_mlstm_C[:, l], state_mlstm_n[:, l], state_mlstm_m[:, l])
        xs, _ = _layer(xs, mods, layers[l], B=Bl, S=Sl, cond_rows=Sl, rope_tabs=rope_tabs,
                       ctx_cache=cache, state=state, name=f"lat{l}")

    return (xp.reshape(Bc, Sc, D_MODEL), xs.reshape(Bl, Sl, D_MODEL),
            new_k_attn, new_v_attn, new_k_nbhd, new_v_nbhd, new_C, new_n, new_m)
```

```python
import functools

import jax
import jax.numpy as jnp
import numpy as np
from jax import lax
from jax.experimental import pallas as pl
from jax.experimental.pallas import tpu as pltpu

F32 = jnp.float32
BF16 = jnp.bfloat16

D_MODEL = 1024
DEPTH = 2
GRID_W = 64
HEAD_DIM = 64
H_ATTN = 6
KV_ATTN = 2
H_MLSTM = 4
H_NBHD = 6
D_FF = 4 * D_MODEL
NA_ROWS = 8
NA_COLS = 16
ROPE_THETA = 10000.0
EPS = 1e-6
N_MOD = 6
W_ATTN = H_ATTN * HEAD_DIM
W_KV = KV_ATTN * HEAD_DIM
W_MLSTM = H_MLSTM * HEAD_DIM
W_NBHD = H_NBHD * HEAD_DIM
N_GATES = 4 * H_MLSTM

LANES = 128
V7X_VMEM_BYTES = 64 * 1024 * 1024
VMEM_CAP_BYTES = 56 * 1024 * 1024

TM_PROJ = 512
PROJ_CHUNK = 512
TM_FFN = 1024
TF_FFN = 1024
TQ_ATTN = 512
CK_ATTN = 512
ATTN_LOOKAHEAD = 2
L_CHUNK = 128
MLSTM_GROUP = 2
MLSTM_GROUP_A = 4
NB_GROUP = 8
NB_SLAB = 16
NB_STEP_GROUPS = 2
NB_KV_BLOCK = 256
NEG = -1e30
LOG2E = 1.4426950408889634

_COLS = {}
_off = 0
for _name, _w in (("qa", W_ATTN), ("ka", W_KV), ("va", W_KV), ("qb", W_MLSTM), ("kb", W_MLSTM),
                  ("vb", W_MLSTM), ("ob", W_MLSTM), ("gf", LANES), ("gb", LANES),
                  ("qc", W_NBHD), ("kc", W_NBHD), ("vc", W_NBHD)):
    _COLS[_name] = (_off, _off + _w)
    _off += _w
IN_PAD = _off


def _vmem_limit(nbytes):
    return int(min(max(nbytes, 16 * 1024 * 1024), VMEM_CAP_BYTES))


def _dot(a, b):
    return jnp.dot(a, b, preferred_element_type=F32)


def _dot_nt(a, b):
    return lax.dot_general(a, b, (((1,), (1,)), ((), ())), preferred_element_type=F32)


def _lane_lo(shape):
    return (lax.broadcasted_iota(jnp.int32, shape, len(shape) - 1) % LANES) < HEAD_DIM


def _rms(x, g):
    ms = jnp.mean(x * x, axis=-1, keepdims=True)
    return (x * lax.rsqrt(ms + EPS)) * g


def _pair_rms(x, g):
    lo = _lane_lo(x.shape)
    x2 = x * x
    s_lo = jnp.sum(jnp.where(lo, x2, 0.0), axis=-1, keepdims=True)
    s_hi = jnp.sum(jnp.where(lo, 0.0, x2), axis=-1, keepdims=True)
    r = jnp.where(lo, lax.rsqrt(s_lo / HEAD_DIM + EPS), lax.rsqrt(s_hi / HEAD_DIM + EPS))
    return (x * r) * g


def _sigmoid(x):
    return 1.0 / (1.0 + jnp.exp(-x))


def _mods_kernel(c_ref, w_ref, b_ref, o_ref):
    c = c_ref[...]
    s = (c * _sigmoid(c)).astype(BF16)
    o_ref[0] = _dot(s, w_ref[0].astype(BF16)) + b_ref[0]


def _modulation(cond, w_ada, b_ada):
    n = cond.shape[0]
    tn = D_MODEL
    return pl.pallas_call(
        _mods_kernel,
        grid=(DEPTH, N_MOD * D_MODEL // tn),
        in_specs=[pl.BlockSpec((n, D_MODEL), lambda l, j: (0, 0)),
                  pl.BlockSpec((1, D_MODEL, tn), lambda l, j: (l, 0, j)),
                  pl.BlockSpec((1, 1, tn), lambda l, j: (l, 0, j))],
        out_specs=pl.BlockSpec((1, n, tn), lambda l, j: (l, 0, j)),
        out_shape=jax.ShapeDtypeStruct((DEPTH, n, N_MOD * D_MODEL), F32),
        compiler_params=pltpu.CompilerParams(
            dimension_semantics=("arbitrary", "arbitrary"),
            vmem_limit_bytes=_vmem_limit(4 * D_MODEL * tn * 4)),
        name="modulation",
    )(cond, w_ada, b_ada.reshape(DEPTH, 1, N_MOD * D_MODEL))


def _inproj_kernel(*refs, rope):
    if rope:
        (x_ref, mod_ref, g_ref, w_ref, qn_ref, kn_ref, gbias_ref, cos_ref, sin_ref,
         qa_ref, ka_ref, va_ref, qb_ref, kb_ref, vb_ref, ob_ref, gate_ref,
         qc_ref, kc_ref, vc_ref) = refs
    else:
        (x_ref, mod_ref, g_ref, w_ref, qn_ref, kn_ref, gbias_ref,
         qa_ref, ka_ref, va_ref, qb_ref, kb_ref, vb_ref, ob_ref, gate_ref,
         qc_ref, kc_ref, vc_ref) = refs
    x = x_ref[...]
    mod = mod_ref[0]
    sh1 = mod[:, 0:D_MODEL]
    sc1 = mod[:, D_MODEL:2 * D_MODEL]
    hb = (_rms(x, g_ref[...]) * (1.0 + sc1) + sh1).astype(BF16)

    z = [_dot(hb, w_ref[:, c0:c0 + PROJ_CHUNK]) for c0 in range(0, IN_PAD, PROJ_CHUNK)]

    def proj(name, j=0, w=None):
        lo, hi = _COLS[name]
        lo = lo + j
        hi = hi if w is None else lo + w
        parts = []
        while lo < hi:
            c, o = divmod(lo, PROJ_CHUNK)
            n = min(hi - lo, PROJ_CHUNK - o)
            parts.append(z[c][:, o:o + n])
            lo += n
        return parts[0] if len(parts) == 1 else jnp.concatenate(parts, axis=1)

    scale = HEAD_DIM ** -0.5
    q_scale = scale * LOG2E

    def rotary(t):
        first = (lax.broadcasted_iota(jnp.int32, t.shape, 1) % 32) < 16
        partner = jnp.where(first, pltpu.roll(t, LANES - 16, 1), pltpu.roll(t, 16, 1))
        return t * cos_ref[...] + partner * sin_ref[...]

    for j in range(W_ATTN // LANES):
        t = _pair_rms(proj("qa", j * LANES, LANES), qn_ref[...])
        if rope:
            t = rotary(t)
        qa_ref[:, j * LANES:(j + 1) * LANES] = (t * q_scale).astype(qa_ref.dtype)
    t = _pair_rms(proj("ka"), kn_ref[...])
    if rope:
        t = rotary(t)
    ka_ref[...] = t.astype(ka_ref.dtype)
    va_ref[...] = proj("va").astype(va_ref.dtype)
    qb_ref[...] = proj("qb").astype(qb_ref.dtype)
    kb_ref[...] = (proj("kb") * scale).astype(kb_ref.dtype)
    vb_ref[...] = proj("vb").astype(vb_ref.dtype)
    ob_ref[...] = proj("ob").astype(ob_ref.dtype)
    for j, name in enumerate(("gf", "gb")):
        gt = proj(name) + gbias_ref[:, j * LANES:(j + 1) * LANES]
        lane = lax.broadcasted_iota(jnp.int32, gt.shape, 1)
        is_f = (lane >= H_MLSTM) & (lane < 2 * H_MLSTM)
        logsig = jnp.minimum(gt, 0.0) - jnp.log1p(jnp.exp(-jnp.abs(gt)))
        gate_ref[:, j * LANES:(j + 1) * LANES] = jnp.where(is_f, logsig, gt)
    qc_ref[...] = (proj("qc") * q_scale).astype(qc_ref.dtype)
    kc_ref[...] = proj("kc").astype(kc_ref.dtype)
    vc_ref[...] = proj("vc").astype(vc_ref.dtype)


def _inproj(x, mods, g_pre, w_in_p, qn, kn, gbias, rope_tabs, *, rows_per_cond, kv_dtype, name):
    T = x.shape[0]
    tm = TM_PROJ
    bpc = rows_per_cond // tm
    rope = rope_tabs is not None
    row = lambda i: (i, 0)
    const = lambda i: (0, 0)
    in_specs = [pl.BlockSpec((tm, D_MODEL), row),
                pl.BlockSpec((1, 1, N_MOD * D_MODEL), lambda i: (i // bpc, 0, 0)),
                pl.BlockSpec((1, D_MODEL), const),
                pl.BlockSpec((D_MODEL, IN_PAD), const),
                pl.BlockSpec((1, LANES), const),
                pl.BlockSpec((1, LANES), const),
                pl.BlockSpec((1, 2 * LANES), const)]
    args = [x, mods, g_pre, w_in_p, qn, kn, gbias]
    if rope:
        nblk = rope_tabs[0].shape[0] // tm
        in_specs += [pl.BlockSpec((tm, LANES), lambda i: (i % nblk, 0))] * 2
        args += list(rope_tabs)
    widths = [("qa", W_ATTN, BF16), ("ka", W_KV, kv_dtype), ("va", W_KV, kv_dtype),
              ("qb", W_MLSTM, BF16), ("kb", W_MLSTM, BF16), ("vb", W_MLSTM, BF16),
              ("ob", W_MLSTM, F32), ("g", 2 * LANES, F32),
              ("qc", W_NBHD, BF16), ("kc", W_NBHD, kv_dtype), ("vc", W_NBHD, kv_dtype)]
    out_specs = [pl.BlockSpec((tm, w), row) for _, w, _ in widths]
    out_shape = [jax.ShapeDtypeStruct((T, w), dt) for _, w, dt in widths]
    est = 2 * (tm * D_MODEL * 4 + D_MODEL * IN_PAD * 2 + tm * IN_PAD * 4) + 3 * tm * IN_PAD * 4
    outs = pl.pallas_call(
        functools.partial(_inproj_kernel, rope=rope),
        grid=(T // tm,),
        in_specs=in_specs, out_specs=out_specs, out_shape=out_shape,
        compiler_params=pltpu.CompilerParams(dimension_semantics=("arbitrary",),
                                             vmem_limit_bytes=_vmem_limit(est)),
        name=name,
    )(*args)
    return dict(zip([n for n, _, _ in widths], outs))


def _attn_kernel(q_ref, k_ref, vt_ref, o_ref, qs_scr, m_scr, l_scr, acc_scr, s_ring, cmax_scr, *, heads, nch):
    nh = len(heads)
    ring = ATTN_LOOKAHEAD + 1
    assert nh % ring == 0
    for h, (kg, kh) in enumerate(heads):
        qg = q_ref[0, :, (h // 2) * LANES:(h // 2 + 1) * LANES].astype(F32)
        if h % 2 != kh:
            qg = pltpu.roll(qg, HEAD_DIM, 1)
        keep = _lane_lo(qg.shape) if kh == 0 else jnp.logical_not(_lane_lo(qg.shape))
        qs_scr[h] = jnp.where(keep, qg, 0.0).astype(BF16)
    m_scr[...] = jnp.full(m_scr.shape, -jnp.inf, F32)
    l_scr[...] = jnp.zeros(l_scr.shape, F32)
    acc_scr[...] = jnp.zeros(acc_scr.shape, F32)

    def scores(j, item):
        h = item % nh
        kg = heads[h][0]
        kj = k_ref[0, j, :, kg * LANES:(kg + 1) * LANES].astype(BF16)
        st = _dot_nt(kj, qs_scr[h])
        s_ring[item % ring] = st
        cmax_scr[item % ring] = jnp.max(st, axis=0, keepdims=True)

    def chunk(j, carry):
        j_next = jnp.minimum(j + 1, nch - 1)
        for h, (kg, kh) in enumerate(heads):
            ahead = h + ATTN_LOOKAHEAD
            scores(j if ahead < nh else j_next, ahead)
            m_old = m_scr[h]
            m_new = jnp.maximum(m_old, cmax_scr[h % ring])
            e = jnp.exp2(s_ring[h % ring] - m_new)
            alpha = jnp.exp2(m_old - m_new)
            l_scr[h] = alpha * l_scr[h] + jnp.sum(e, axis=0, keepdims=True)
            r = kg * LANES + kh * HEAD_DIM
            vt = vt_ref[0, j, r:r + HEAD_DIM, :].astype(BF16)
            acc_scr[h] = alpha * acc_scr[h] + _dot(vt, e.astype(BF16))
            m_scr[h] = m_new
        return carry

    for item in range(ATTN_LOOKAHEAD):
        scores(0, item)
    lax.fori_loop(0, nch, chunk, 0)
    for t in range(len(heads) // 2):
        ot = jnp.concatenate([acc_scr[2 * t] / l_scr[2 * t], acc_scr[2 * t + 1] / l_scr[2 * t + 1]], axis=0)
        o_ref[0, :, t * LANES:(t + 1) * LANES] = ot.T.astype(o_ref.dtype)


def _attention(q, k, v, *, heads, name):
    B, Sq, W = q.shape
    Sk, KW = k.shape[1], k.shape[2]
    tq = min(TQ_ATTN, Sq)
    ck = min(CK_ATTN, Sk)
    nch = Sk // ck
    nh = len(heads)
    k4 = k.reshape(B, nch, ck, KW)
    vt4 = v.reshape(B, nch, ck, KW).transpose(0, 1, 3, 2)
    isz = k.dtype.itemsize
    est = (2 * (2 * tq * W * 2 + 2 * Sk * KW * isz) + nh * tq * (LANES * 2 + HEAD_DIM * 4 + 64)
           + 8 * nh * ck * tq * 4)
    return pl.pallas_call(
        functools.partial(_attn_kernel, heads=heads, nch=nch),
        grid=(B, Sq // tq),
        in_specs=[pl.BlockSpec((1, tq, W), lambda b, i: (b, i, 0)),
                  pl.BlockSpec((1, nch, ck, KW), lambda b, i: (b, 0, 0, 0)),
                  pl.BlockSpec((1, nch, KW, ck), lambda b, i: (b, 0, 0, 0))],
        out_specs=pl.BlockSpec((1, tq, W), lambda b, i: (b, i, 0)),
        out_shape=jax.ShapeDtypeStruct((B, Sq, W), BF16),
        scratch_shapes=[pltpu.VMEM((nh, tq, LANES), BF16),
                        pltpu.VMEM((nh, 1, tq), F32),
                        pltpu.VMEM((nh, 1, tq), F32),
                        pltpu.VMEM((nh, HEAD_DIM, tq), F32),
                        pltpu.VMEM((ATTN_LOOKAHEAD + 1, ck, tq), F32),
                        pltpu.VMEM((ATTN_LOOKAHEAD + 1, 1, tq), F32)],
        compiler_params=pltpu.CompilerParams(dimension_semantics=("arbitrary", "arbitrary"),
                                             vmem_limit_bytes=_vmem_limit(est)),
        name=name,
    )(q, k4, vt4)


_HEADS_GQA = tuple((0, h // (H_ATTN // KV_ATTN)) for h in range(H_ATTN))
_HEADS_MHA = tuple((h // 2, h % 2) for h in range(H_NBHD))


def _nbhd_window(r, rows):
    kr = min(NA_ROWS, rows)
    return min(max(r - kr // 2, 0), rows - kr), kr


def _nbhd_patterns(rows):
    n_groups = rows // NB_GROUP
    pats = []
    for g in (0, 1, n_groups - 1):
        r0 = g * NB_GROUP
        pats.append((r0, min(max(r0 - NA_ROWS // 2, 0), rows - NB_SLAB)))
    return pats


def _nbhd_kernel(rb_ref, q_ref, k_ref, vt_ref, kc_ref, vct_ref, o_ref,
                 bias_scr, qs_scr, m_scr, l_scr, acc_scr, s_ring, cmax_scr, *, rows):
    hp = pl.program_id(0)
    b = pl.program_id(1)
    gs = pl.program_id(2)
    n_groups = rows // NB_GROUP
    n_dr = 2 * NA_ROWS - 1
    n_dc = 2 * NA_COLS - 1
    tq = NB_GROUP * GRID_W
    ck = tq
    pats = _nbhd_patterns(rows)
    ring = ATTN_LOOKAHEAD + 1
    n_chunks = NB_SLAB * GRID_W // ck + 1
    assert n_chunks % ring == 0 and kc_ref.shape[1] == ck
    blk_rows = NB_KV_BLOCK // GRID_W

    @pl.when((b == 0) & (gs == 0))
    def _build_bias():
        shape = (GRID_W, LANES)
        w = lax.broadcasted_iota(jnp.int32, shape, 0)
        lane = lax.broadcasted_iota(jnp.int32, shape, 1)
        cc = lane % GRID_W
        second = lane >= GRID_W
        cs = jnp.clip(w - NA_COLS // 2, 0, GRID_W - NA_COLS)
        col_ok = (cc >= cs) & (cc < cs + NA_COLS)
        dc = cc - w + (NA_COLS - 1)
        for hh in range(2):
            base = (2 * hp + hh) * (n_dr * n_dc)
            tiles = {}
            for d in range(-1, n_dr):
                acc = jnp.zeros(shape, F32)
                for j in range(n_dc):
                    va = rb_ref[base + d * n_dc + j] * LOG2E if d >= 0 else 0.0
                    vb = rb_ref[base + (d + 1) * n_dc + j] * LOG2E if d + 1 < n_dr else 0.0
                    acc = acc + jnp.where(dc == j, jnp.where(second, vb, va), 0.0)
                tiles[d] = acc
            def query_row_tile(r, kra):
                rs, kr = _nbhd_window(r, rows)
                ok_a = rs <= kra < rs + kr
                ok_b = rs <= kra + 1 < rs + kr
                if not (ok_a or ok_b):
                    return jnp.full(shape, NEG, F32)
                row_ok = (jnp.logical_not(second) if ok_a and not ok_b else
                          second if ok_b and not ok_a else None)
                ok = col_ok if row_ok is None else (col_ok & row_ok)
                return jnp.where(ok, tiles[kra - r + (NA_ROWS - 1)], NEG)

            for pi, (r0, slab0) in enumerate(pats):
                for ip in range(NB_GROUP // 2):
                    for ap in range(NB_SLAB // 2):
                        kra = slab0 + 2 * ap
                        two_rows = jnp.concatenate([query_row_tile(r0 + 2 * ip, kra),
                                                    query_row_tile(r0 + 2 * ip + 1, kra)], axis=0)
                        bias_scr[hh, pi, ap * LANES:(ap + 1) * LANES, ip * LANES:(ip + 1) * LANES] = two_rows.T

    lo = _lane_lo((tq, LANES))
    n_items = NB_STEP_GROUPS * 2 * n_chunks
    for s in range(NB_STEP_GROUPS):
        q = q_ref[0, s * tq:(s + 1) * tq, :].astype(F32)
        for hh in range(2):
            qs_scr[2 * s + hh] = jnp.where(lo if hh == 0 else jnp.logical_not(lo), q, 0.0).astype(BF16)
    m_scr[...] = jnp.full(m_scr.shape, -jnp.inf, F32)
    l_scr[...] = jnp.zeros(l_scr.shape, F32)
    acc_scr[...] = jnp.zeros(acc_scr.shape, F32)

    def group_of(s):
        g = gs * NB_STEP_GROUPS + s
        pat = jnp.where(g == 0, 0, jnp.where(g == n_groups - 1, 2, 1))
        slab0 = jnp.clip(g * NB_GROUP - NA_ROWS // 2, 0, rows - NB_SLAB)
        return pat, slab0 // blk_rows

    def scores(item):
        s, hh, c = item // (2 * n_chunks), (item // n_chunks) % 2, item % n_chunks
        qm = qs_scr[2 * s + hh]
        if c < n_chunks - 1:
            pat, blk0 = group_of(s)
            kc = k_ref[0, pl.ds(blk0 + c * (ck // NB_KV_BLOCK), ck // NB_KV_BLOCK)].reshape(ck, LANES)
            st = _dot_nt(kc, qm) + bias_scr[hh, pat, c * ck:(c + 1) * ck, :]
        else:
            st = _dot_nt(kc_ref[0].astype(BF16), qm)
        s_ring[item % ring] = st
        cmax_scr[item % ring] = jnp.max(st, axis=0, keepdims=True)

    for item in range(ATTN_LOOKAHEAD):
        scores(item)
    for item in range(n_items):
        if item + ATTN_LOOKAHEAD < n_items:
            scores(item + ATTN_LOOKAHEAD)
        s, hh, c = item // (2 * n_chunks), (item // n_chunks) % 2, item % n_chunks
        u = 2 * s + hh
        m_old = m_scr[u]
        m_new = jnp.maximum(m_old, cmax_scr[item % ring])
        e = jnp.exp2(s_ring[item % ring] - m_new)
        alpha = jnp.exp2(m_old - m_new)
        l_scr[u] = alpha * l_scr[u] + jnp.sum(e, axis=0, keepdims=True)
        if c < n_chunks - 1:
            _, blk0 = group_of(s)
            nb = ck // NB_KV_BLOCK
            vt = jnp.concatenate([vt_ref[0, blk0 + c * nb + i, hh * HEAD_DIM:(hh + 1) * HEAD_DIM, :]
                                  for i in range(nb)], axis=1)
        else:
            vt = vct_ref[0, hh * HEAD_DIM:(hh + 1) * HEAD_DIM, :].astype(BF16)
        acc_scr[u] = alpha * acc_scr[u] + _dot(vt, e.astype(BF16))
        m_scr[u] = m_new
    for s in range(NB_STEP_GROUPS):
        ot = jnp.concatenate([acc_scr[2 * s] / l_scr[2 * s], acc_scr[2 * s + 1] / l_scr[2 * s + 1]], axis=0)
        o_ref[0, s * tq:(s + 1) * tq, :] = ot.T.astype(o_ref.dtype)


def _nbhd_attention(q, k, v, k_ctx, v_ctx, rel_bias_flat, *, name):
    B, S, W = q.shape
    P = k_ctx.shape[1]
    rows = S // GRID_W
    tq = NB_GROUP * GRID_W
    nk = NB_SLAB * GRID_W
    nblk = S // NB_KV_BLOCK
    tqs = NB_STEP_GROUPS * tq
    nu = 2 * NB_STEP_GROUPS
    ring = ATTN_LOOKAHEAD + 1
    k4 = k.reshape(B, nblk, NB_KV_BLOCK, W)
    vt4 = v.reshape(B, nblk, NB_KV_BLOCK, W).transpose(0, 1, 3, 2)
    vct = v_ctx.transpose(0, 2, 1)
    est = (2 * (2 * tqs * LANES * 2 + 2 * S * LANES * 2 + 2 * P * LANES * 4)
           + 2 * 3 * tq * nk * 4 + ring * tq * tq * 4 + 8 * tq * tq * 4 + nu * tq * 1024)
    return pl.pallas_call(
        functools.partial(_nbhd_kernel, rows=rows),
        grid=(W // LANES, B, rows // (NB_GROUP * NB_STEP_GROUPS)),
        in_specs=[pl.BlockSpec(memory_space=pltpu.SMEM),
                  pl.BlockSpec((1, tqs, LANES), lambda p, b, g: (b, g, p)),
                  pl.BlockSpec((1, nblk, NB_KV_BLOCK, LANES), lambda p, b, g: (b, 0, 0, p)),
                  pl.BlockSpec((1, nblk, LANES, NB_KV_BLOCK), lambda p, b, g: (b, 0, p, 0)),
                  pl.BlockSpec((1, P, LANES), lambda p, b, g: (b, 0, p)),
                  pl.BlockSpec((1, LANES, P), lambda p, b, g: (b, p, 0))],
        out_specs=pl.BlockSpec((1, tqs, LANES), lambda p, b, g: (b, g, p)),
        out_shape=jax.ShapeDtypeStruct((B, S, W), BF16),
        scratch_shapes=[pltpu.VMEM((2, 3, nk, tq), F32),
                        pltpu.VMEM((nu, tq, LANES), BF16),
                        pltpu.VMEM((nu, 1, tq), F32),
                        pltpu.VMEM((nu, 1, tq), F32),
                        pltpu.VMEM((nu, HEAD_DIM, tq), F32),
                        pltpu.VMEM((ring, tq, tq), F32),
                        pltpu.VMEM((ring, 1, tq), F32)],
        compiler_params=pltpu.CompilerParams(dimension_semantics=("arbitrary",) * 3,
                                             vmem_limit_bytes=_vmem_limit(est)),
        name=name,
    )(rel_bias_flat, q, k4, vt4, k_ctx, vct)


def _split3(x):
    hi = x.astype(BF16)
    r = x - hi.astype(F32)
    mid = r.astype(BF16)
    return hi, mid, (r - mid.astype(F32)).astype(BF16)


def _mlstm_kernel(q_ref, k_ref, v_ref, g_ref, ob_ref, s0_ref, m0_ref, on_ref,
                  out_ref, sf_ref, mf_ref,
                  h_scr, nat_scr, rows_scr, stat_scr, mprev_scr, un_scr, st_scr, *, nc, grp_a, grp):
    d = pl.program_id(1)
    L = L_CHUNK
    NP = H_MLSTM // 2
    row = lax.broadcasted_iota(jnp.int32, (L, L), 0)
    col = lax.broadcasted_iota(jnp.int32, (L, L), 1)
    sign = 1 - 2 * d
    mask = (col - row) * sign <= 0
    maskb = mask.astype(BF16)
    mask3 = jnp.concatenate([maskb, maskb, maskb], axis=1)
    lane = lax.broadcasted_iota(jnp.int32, (L, LANES), 1)
    lo = lane < HEAD_DIM
    top = row < HEAD_DIM
    row2 = lax.broadcasted_iota(jnp.int32, (L, 2 * LANES), 0)
    col2 = lax.broadcasted_iota(jnp.int32, (L, 2 * LANES), 1)
    keep_state = (row2 < HEAD_DIM) == ((col2 % LANES) < HEAD_DIM)
    top2 = row2 < HEAD_DIM
    ones_b = jnp.ones((L, LANES), BF16)
    ones_lo = lo.astype(BF16)
    ones_hi = jnp.logical_not(lo).astype(BF16)

    def chunk_rows(c):
        return pl.ds(pl.multiple_of(c * L, L), L)

    def pass_a(it, carry):
        cs = [it * grp_a + u for u in range(grp_a)]
        gts = [g_ref[0, chunk_rows(c), :] for c in cs]
        bns = [_dot(mask3, jnp.concatenate(_split3(gt), axis=0)) for gt in gts]
        a_all = []
        for c, gt, bn in zip(cs, gts, bns):
            nat = jnp.where(lane < H_MLSTM, gt, bn)
            nat_scr[chunk_rows(c), :] = nat * (-LOG2E)
            nat_t = nat.T
            b_rows = nat_t[H_MLSTM:2 * H_MLSTM]
            c_rows = nat_t[0:H_MLSTM] - b_rows
            rows_scr[c] = jnp.concatenate([c_rows * LOG2E, b_rows], axis=0)
            c_max = jnp.max(c_rows, axis=1, keepdims=True)
            b_tot = jnp.where(d == 0, b_rows[:, L - 1:L], b_rows[:, 0:1])
            stat_scr[c] = jnp.concatenate([jnp.broadcast_to(c_max, (H_MLSTM, LANES)),
                                           jnp.broadcast_to(b_tot, (H_MLSTM, LANES))], axis=0)
            a_all.append(jnp.exp(c_rows - c_max))
        for c, a_rows in zip(cs, a_all):
            rows = chunk_rows(c)
            for p in range(NP):
                lanes = slice(p * LANES, (p + 1) * LANES)
                k_t = k_ref[0, rows, lanes].astype(F32).T
                a_sel = jnp.where(top, a_rows[2 * p:2 * p + 1], a_rows[2 * p + 1:2 * p + 2])
                vv = jnp.concatenate([v_ref[0, rows, lanes], ones_b], axis=1)
                un = _dot((k_t * a_sel).astype(BF16), vv)
                un_scr[c, p] = jnp.where(keep_state, un, 0.0)
        return carry

    lax.fori_loop(0, nc // grp_a, pass_a, 0)

    st_scr[...] = s0_ref[0, 0]

    def pass_b(ci, m):
        c = jnp.where(d == 0, ci, nc - 1 - ci)
        st = stat_scr[c]
        c_max, b_tot = st[0:H_MLSTM], st[H_MLSTM:]
        m_new = jnp.maximum(b_tot + m, b_tot + c_max)
        d_old = jnp.exp(b_tot + m - m_new)
        d_new = jnp.exp(b_tot + c_max - m_new)
        mprev_scr[c] = jnp.concatenate([m, m], axis=0) * LOG2E
        for p in range(NP):
            def rows_of(t, p=p):
                even = jnp.concatenate([t[2 * p:2 * p + 1]] * 2, axis=1)
                odd = jnp.concatenate([t[2 * p + 1:2 * p + 2]] * 2, axis=1)
                return jnp.where(top2, even, odd)
            s_prev = st_scr[p]
            st_scr[p] = rows_of(d_old) * s_prev + rows_of(d_new) * un_scr[c, p]
            un_scr[c, p] = s_prev
        return m_new

    m_fin = lax.fori_loop(0, nc, pass_b, m0_ref[0, 0][0:H_MLSTM])
    sf_ref[0, 0] = st_scr[...]
    mf_ref[0, 0] = jnp.concatenate([m_fin, m_fin], axis=0)

    def pass_c(it, carry):
        cs = [it * grp + u for u in range(grp)]
        units = [(u, p) for u in range(grp) for p in range(NP)]
        early = {}
        for u, p in units:
            c = cs[u]
            rows = chunk_rows(c)
            lanes = slice(p * LANES, (p + 1) * LANES)
            qp = q_ref[0, rows, lanes]
            kp = k_ref[0, rows, lanes]
            s_in = un_scr[c, p].astype(BF16)
            qms = [jnp.where(lo if hh == 0 else jnp.logical_not(lo), qp, jnp.zeros_like(qp)) for hh in range(2)]
            early[u, p] = ([_dot_nt(qm, kp) for qm in qms],
                           _dot(qp, s_in))
        mid = {}
        for u, p in units:
            r_t = rows_scr[cs[u]]
            m_in = mprev_scr[cs[u]]
            for hh in range(2):
                h = 2 * p + hh
                cm = jnp.where(mask, r_t[h:h + 1, :], -jnp.inf)
                m_prev = m_in[h:h + 1, :]
                mu = jnp.maximum(jnp.broadcast_to(jnp.max(cm, axis=1, keepdims=True), (L, LANES)), m_prev)
                w = early[u, p][0][hh] * jnp.exp2(cm - mu)
                mid[u, p, hh] = (w.astype(BF16), mu, m_prev)
        for u, p in units:
            rows = chunk_rows(cs[u])
            lanes = slice(p * LANES, (p + 1) * LANES)
            vp = v_ref[0, rows, lanes]
            zero = jnp.zeros_like(vp)
            vv = jnp.concatenate([jnp.concatenate([jnp.where(lo, vp, zero), ones_lo], axis=1),
                                  jnp.concatenate([jnp.where(lo, zero, vp), ones_hi], axis=1)], axis=0)
            w2 = jnp.concatenate([mid[u, p, 0][0], mid[u, p, 1][0]], axis=1)
            nd = _dot(w2, vv)
            nat = nat_scr[rows, :]
            nb = [jnp.broadcast_to(nat[:, H_MLSTM + 2 * p + hh:H_MLSTM + 2 * p + hh + 1], (L, LANES))
                  for hh in range(2)]
            fs = early[u, p][1]
            mu = jnp.where(lo, mid[u, p, 0][1], mid[u, p, 1][1])
            m_prev = jnp.where(lo[0:1], mid[u, p, 0][2], mid[u, p, 1][2])
            w_inter = jnp.exp2(m_prev - mu)
            den = nd[:, LANES:] + w_inter * fs[:, LANES:]
            den = jnp.maximum(jnp.abs(den), jnp.exp2(jnp.where(lo, nb[0], nb[1]) - mu))
            h_scr[d, rows, lanes] = (nd[:, :LANES] + w_inter * fs[:, :LANES]) / den
        return carry

    lax.fori_loop(0, nc // grp, pass_c, 0)

    @pl.when(d == 1)
    def _finish():
        def rows_block(c, carry):
            rows = chunk_rows(c)
            for p in range(NP):
                lanes = slice(p * LANES, (p + 1) * LANES)
                hn = _pair_rms(h_scr[0, rows, lanes] + h_scr[1, rows, lanes], on_ref[:, lanes])
                out_ref[0, rows, lanes] = (_sigmoid(ob_ref[0, rows, lanes]) * hn).astype(out_ref.dtype)
            return carry

        lax.fori_loop(0, nc, rows_block, 0)


def _mlstm(q, k, v, gates, ob, s0, m0, out_norm, *, name):
    assert L_CHUNK == LANES
    B, S, W = q.shape
    nc = S // L_CHUNK
    grp = min(MLSTM_GROUP, nc)
    grp_a = min(MLSTM_GROUP_A, nc)
    npair = H_MLSTM // 2
    seq = lambda b, d: (b, 0, 0)
    est = (2 * (3 * S * W * 2 + S * LANES * 4 + S * W * 4 + S * W * 2) + 2 * S * W * 4 + S * LANES * 4
           + nc * npair * LANES * 2 * LANES * 4 + 12 * 1024 * 1024)
    return pl.pallas_call(
        functools.partial(_mlstm_kernel, nc=nc, grp_a=grp_a, grp=grp),
        grid=(B, 2),
        in_specs=[pl.BlockSpec((1, S, W), seq), pl.BlockSpec((1, S, W), seq), pl.BlockSpec((1, S, W), seq),
                  pl.BlockSpec((1, S, LANES), lambda b, d: (b, 0, d)),
                  pl.BlockSpec((1, S, W), seq),
                  pl.BlockSpec((1, 1, npair, LANES, 2 * LANES), lambda b, d: (b, d, 0, 0, 0)),
                  pl.BlockSpec((1, 1, 8, LANES), lambda b, d: (b, d, 0, 0)),
                  pl.BlockSpec((1, W), lambda b, d: (0, 0))],
        out_specs=[pl.BlockSpec((1, S, W), seq),
                   pl.BlockSpec((1, 1, npair, LANES, 2 * LANES), lambda b, d: (b, d, 0, 0, 0)),
                   pl.BlockSpec((1, 1, 8, LANES), lambda b, d: (b, d, 0, 0))],
        out_shape=[jax.ShapeDtypeStruct((B, S, W), BF16),
                   jax.ShapeDtypeStruct((B, 2, npair, LANES, 2 * LANES), F32),
                   jax.ShapeDtypeStruct((B, 2, 8, LANES), F32)],
        scratch_shapes=[pltpu.VMEM((2, S, W), F32),
                        pltpu.VMEM((S, LANES), F32),
                        pltpu.VMEM((nc, 8, L_CHUNK), F32),
                        pltpu.VMEM((nc, 8, LANES), F32),
                        pltpu.VMEM((nc, 8, LANES), F32),
                        pltpu.VMEM((nc, npair, LANES, 2 * LANES), F32),
                        pltpu.VMEM((npair, LANES, 2 * LANES), F32)],
        compiler_params=pltpu.CompilerParams(dimension_semantics=("arbitrary", "arbitrary"),
                                             vmem_limit_bytes=_vmem_limit(est)),
        name=name,
    )(q, k, v, gates, ob, s0, m0, out_norm)


def _outproj_kernel(a_ref, b_ref, c_ref, x_ref, mod_ref, w_ref, gpost_ref, gpre_ref, x1_ref, h2_ref):
    mo = _dot(jnp.concatenate([a_ref[...], b_ref[...], c_ref[...]], axis=1), w_ref[...])
    mod = mod_ref[0]
    gt1 = mod[:, 2 * D_MODEL:3 * D_MODEL]
    sh2 = mod[:, 3 * D_MODEL:4 * D_MODEL]
    sc2 = mod[:, 4 * D_MODEL:5 * D_MODEL]
    x1 = x_ref[...] + gt1 * _rms(mo, gpost_ref[...])
    x1_ref[...] = x1
    h2_ref[...] = (_rms(x1, gpre_ref[...]) * (1.0 + sc2) + sh2).astype(h2_ref.dtype)


def _outproj(oa, ob, oc, x, mods, w_out, g_post, g_pre, *, rows_per_cond, name):
    T = x.shape[0]
    tm = TM_PROJ
    bpc = rows_per_cond // tm
    row = lambda i: (i, 0)
    const = lambda i: (0, 0)
    est = 2 * (tm * D_MODEL * (2 + 4 + 4 + 2) + D_MODEL * D_MODEL * 2) + 4 * tm * D_MODEL * 4
    return pl.pallas_call(
        _outproj_kernel,
        grid=(T // tm,),
        in_specs=[pl.BlockSpec((tm, W_ATTN), row), pl.BlockSpec((tm, W_MLSTM), row),
                  pl.BlockSpec((tm, W_NBHD), row), pl.BlockSpec((tm, D_MODEL), row),
                  pl.BlockSpec((1, 1, N_MOD * D_MODEL), lambda i: (i // bpc, 0, 0)),
                  pl.BlockSpec((D_MODEL, D_MODEL), const),
                  pl.BlockSpec((1, D_MODEL), const), pl.BlockSpec((1, D_MODEL), const)],
        out_specs=[pl.BlockSpec((tm, D_MODEL), row), pl.BlockSpec((tm, D_MODEL), row)],
        out_shape=[jax.ShapeDtypeStruct((T, D_MODEL), F32), jax.ShapeDtypeStruct((T, D_MODEL), BF16)],
        compiler_params=pltpu.CompilerParams(dimension_semantics=("arbitrary",),
                                             vmem_limit_bytes=_vmem_limit(est)),
        name=name,
    )(oa, ob, oc, x, mods, w_out, g_post, g_pre)


def _ffn_kernel(h_ref, x_ref, mod_ref, wu_ref, wd_ref, g_ref, o_ref, acc_ref):
    j = pl.program_id(1)

    @pl.when(j == 0)
    def _zero():
        acc_ref[...] = jnp.zeros_like(acc_ref)

    u = jnp.maximum(_dot(h_ref[...], wu_ref[...]), 0.0)
    acc_ref[...] += _dot((u * u).astype(BF16), wd_ref[...])

    @pl.when(j == pl.num_programs(1) - 1)
    def _finish():
        gt2 = mod_ref[0][:, 5 * D_MODEL:6 * D_MODEL]
        o_ref[...] = x_ref[...] + gt2 * _rms(acc_ref[...], g_ref[...])


def _ffn(h2, x1, mods, w_up, w_down, g_post, *, rows_per_cond, name):
    T = x1.shape[0]
    tm, tf = TM_FFN, TF_FFN
    tm = min(tm, rows_per_cond)
    bpc = rows_per_cond // tm
    est = 2 * (tm * D_MODEL * (2 + 4 + 4) + 2 * D_MODEL * tf * 2) + tm * D_MODEL * 4 + 3 * tm * tf * 4
    return pl.pallas_call(
        _ffn_kernel,
        grid=(T // tm, D_FF // tf),
        in_specs=[pl.BlockSpec((tm, D_MODEL), lambda i, j: (i, 0)),
                  pl.BlockSpec((tm, D_MODEL), lambda i, j: (i, 0)),
                  pl.BlockSpec((1, 1, N_MOD * D_MODEL), lambda i, j: (i // bpc, 0, 0)),
                  pl.BlockSpec((D_MODEL, tf), lambda i, j: (0, j)),
                  pl.BlockSpec((tf, D_MODEL), lambda i, j: (j, 0)),
                  pl.BlockSpec((1, D_MODEL), lambda i, j: (0, 0))],
        out_specs=pl.BlockSpec((tm, D_MODEL), lambda i, j: (i, 0)),
        out_shape=jax.ShapeDtypeStruct((T, D_MODEL), F32),
        scratch_shapes=[pltpu.VMEM((tm, D_MODEL), F32)],
        compiler_params=pltpu.CompilerParams(dimension_semantics=("arbitrary", "arbitrary"),
                                             vmem_limit_bytes=_vmem_limit(est)),
        name=name,
    )(h2, x1, mods, w_up, w_down, g_post)


def _pad_w_in(w_in_l):
    o = W_ATTN + 2 * W_KV + 4 * W_MLSTM
    pre, gates, post = w_in_l[:, :o], w_in_l[:, o:o + N_GATES], w_in_l[:, o + N_GATES:]
    z = jnp.zeros((D_MODEL, LANES - 2 * H_MLSTM), w_in_l.dtype)
    return jnp.concatenate([pre, gates[:, :2 * H_MLSTM], z, gates[:, 2 * H_MLSTM:], z, post],
                           axis=1).astype(BF16)


def _pad_gate_bias(gb_l):
    z = jnp.zeros((LANES - 2 * H_MLSTM,), gb_l.dtype)
    return jnp.concatenate([gb_l[:2 * H_MLSTM], z, gb_l[2 * H_MLSTM:], z]).reshape(1, 2 * LANES)


def _rope_tables(S):
    quarter = HEAD_DIM // 4
    pos = jnp.arange(S)
    inv_freq = ROPE_THETA ** (-jnp.arange(quarter, dtype=F32) / quarter)

    def tabs(p):
        ang = p.astype(F32)[:, None] * inv_freq[None, :]
        return jnp.cos(ang), jnp.sin(ang)

    cr, sr = tabs(pos // GRID_W)
    cc, sc = tabs(pos % GRID_W)
    cos = jnp.concatenate([cr, cr, cc, cc], axis=1)
    sin = jnp.concatenate([-sr, sr, -sc, sc], axis=1)
    return jnp.tile(cos, (1, 2)), jnp.tile(sin, (1, 2))


def _pack_state(C, n, m):
    B = C.shape[0]
    Cp = C.reshape(B, 2, 2, 2, HEAD_DIM, HEAD_DIM)
    z = jnp.zeros_like(Cp[:, :, :, 0])
    top = jnp.concatenate([Cp[:, :, :, 0], z], axis=-1)
    bot = jnp.concatenate([z, Cp[:, :, :, 1]], axis=-1)
    Cbd = jnp.concatenate([top, bot], axis=-2)
    n_rep = jnp.broadcast_to(n.reshape(B, 2, 2, LANES, 1), (B, 2, 2, LANES, LANES))
    same_head = (jnp.arange(LANES)[:, None] < HEAD_DIM) == (jnp.arange(LANES)[None, :] < HEAD_DIM)
    n_rep = jnp.where(same_head, n_rep, 0.0)
    m_rows = jnp.broadcast_to(m[..., None], m.shape + (LANES,))
    return jnp.concatenate([Cbd, n_rep], axis=-1), jnp.concatenate([m_rows, m_rows], axis=-2)


def _unpack_state(s_p, m_p):
    B = s_p.shape[0]
    c_even = s_p[:, :, :, :HEAD_DIM, :HEAD_DIM]
    c_odd = s_p[:, :, :, HEAD_DIM:, HEAD_DIM:LANES]
    C = jnp.stack([c_even, c_odd], axis=3).reshape(B, 2, H_MLSTM, HEAD_DIM, HEAD_DIM)
    n = jnp.concatenate([s_p[..., :HEAD_DIM, LANES], s_p[..., HEAD_DIM:, LANES + HEAD_DIM]], axis=-1)
    return C, n.reshape(B, 2, H_MLSTM, HEAD_DIM), m_p[:, :, :H_MLSTM, 0]


def _layer(x, mods, lw, *, B, S, cond_rows, rope_tabs, ctx_cache, state, name):
    T = B * S
    kv_dtype = BF16 if ctx_cache is not None else F32
    pr = _inproj(x, mods, lw["g_pre_mix"], lw["w_in"], lw["q_norm"], lw["k_norm"], lw["gate_bias"],
                 rope_tabs, rows_per_cond=cond_rows, kv_dtype=kv_dtype, name=name + "_inproj")
    seq = lambda a: a.reshape(B, S, a.shape[-1])
    qa, ka, va = seq(pr["qa"]), seq(pr["ka"]), seq(pr["va"])
    qc, kc, vc = seq(pr["qc"]), seq(pr["kc"]), seq(pr["vc"])
    if ctx_cache is None:
        out_a = _attention(qa, ka, va, heads=_HEADS_GQA, name=name + "_attn_a")
        out_c = _attention(qc, kc, vc, heads=_HEADS_MHA, name=name + "_attn_c")
    else:
        ck_a, cv_a, ck_c, cv_c = ctx_cache
        k_all = jnp.concatenate([ka, ck_a.astype(BF16)], axis=1)
        v_all = jnp.concatenate([va, cv_a.astype(BF16)], axis=1)
        out_a = _attention(qa, k_all, v_all, heads=_HEADS_GQA, name=name + "_attn_a")
        out_c = _nbhd_attention(qc, kc, vc, ck_c, cv_c, lw["rel_bias"], name=name + "_attn_c")
    s0, m0 = state
    out_b, sf, mf = _mlstm(seq(pr["qb"]), seq(pr["kb"]), seq(pr["vb"]), seq(pr["g"]), seq(pr["ob"]),
                           s0, m0, lw["out_norm"], name=name + "_mlstm")
    x1, h2 = _outproj(out_a.reshape(T, W_ATTN), out_b.reshape(T, W_MLSTM), out_c.reshape(T, W_NBHD),
                      x, mods, lw["w_out"], lw["g_post_mix"], lw["g_pre_ffn"],
                      rows_per_cond=cond_rows, name=name + "_outproj")
    x2 = _ffn(h2, x1, mods, lw["w_up"], lw["w_down"], lw["g_post_ffn"],
              rows_per_cond=cond_rows, name=name + "_ffn")
    return x2, (ka, va, kc, vc, sf, mf)


def kernel(x_prompt, x_sample, c, cache_k_attn, cache_v_attn, cache_k_nbhd, cache_v_nbhd, state_mlstm_C, state_mlstm_n, state_mlstm_m, c_ctx, w_ada, b_ada, g_pre_mix, g_post_mix, g_pre_ffn, g_post_ffn, w_in, q_norm_attn, k_norm_attn, mlstm_gate_bias, mlstm_out_norm, nbhd_rel_bias, w_out, w_ffn_up, w_ffn_down):
    Bc, Sc, _ = x_prompt.shape
    Bl, Sl, _ = x_sample.shape
    P = cache_k_attn.shape[2]
    n_cond = 8
    cond = jnp.concatenate([c_ctx[None, :], c, jnp.zeros((n_cond - 1 - Bl, D_MODEL), F32)], axis=0)
    mods_all = _modulation(cond, w_ada, b_ada)

    layers = []
    for l in range(DEPTH):
        layers.append(dict(
            w_in=_pad_w_in(w_in[l]),
            w_out=w_out[l].astype(BF16),
            w_up=w_ffn_up[l].astype(BF16),
            w_down=w_ffn_down[l].astype(BF16),
            g_pre_mix=g_pre_mix[l].reshape(1, D_MODEL), g_post_mix=g_post_mix[l].reshape(1, D_MODEL),
            g_pre_ffn=g_pre_ffn[l].reshape(1, D_MODEL), g_post_ffn=g_post_ffn[l].reshape(1, D_MODEL),
            q_norm=jnp.tile(q_norm_attn[l], 2).reshape(1, LANES),
            k_norm=jnp.tile(k_norm_attn[l], 2).reshape(1, LANES),
            gate_bias=_pad_gate_bias(mlstm_gate_bias[l]),
            out_norm=mlstm_out_norm[l].reshape(1, W_MLSTM),
            rel_bias=nbhd_rel_bias[l].reshape(-1),
        ))

    xp = x_prompt.reshape(Bc * Sc, D_MODEL)
    zero_state = _pack_state(jnp.zeros((Bc, 2, H_MLSTM, HEAD_DIM, HEAD_DIM), F32),
                             jnp.zeros((Bc, 2, H_MLSTM, HEAD_DIM), F32),
                             jnp.zeros((Bc, 2, H_MLSTM), F32))
    ctx = []
    for l in range(DEPTH):
        mods = mods_all[l, 0:1].reshape(1, 1, N_MOD * D_MODEL)
        xp, extras = _layer(xp, mods, layers[l], B=Bc, S=Sc, cond_rows=Bc * Sc, rope_tabs=None,
                            ctx_cache=None, state=zero_state, name=f"ctx{l}")
        ctx.append(extras)
    new_k_attn = jnp.stack([e[0].reshape(Bc, Sc, KV_ATTN, HEAD_DIM) for e in ctx], axis=1)
    new_v_attn = jnp.stack([e[1].reshape(Bc, Sc, KV_ATTN, HEAD_DIM) for e in ctx], axis=1)
    new_k_nbhd = jnp.stack([e[2].reshape(Bc, Sc, H_NBHD, HEAD_DIM) for e in ctx], axis=1)
    new_v_nbhd = jnp.stack([e[3].reshape(Bc, Sc, H_NBHD, HEAD_DIM) for e in ctx], axis=1)
    states = [_unpack_state(e[4], e[5]) for e in ctx]
    new_C = jnp.stack([s[0] for s in states], axis=1)
    new_n = jnp.stack([s[1] for s in states], axis=1)
    new_m = jnp.stack([s[2] for s in states], axis=1)

    xs = x_sample.reshape(Bl * Sl, D_MODEL)
    rope_tabs = _rope_tables(Sl)
    for l in range(DEPTH):
        mods = mods_all[l, 1:1 + Bl].reshape(Bl, 1, N_MOD * D_MODEL)
        cache = (cache_k_attn[:, l].reshape(Bl, P, W_KV), cache_v_attn[:, l].reshape(Bl, P, W_KV),
                 cache_k_nbhd[:, l].reshape(Bl, P, W_NBHD), cache_v_nbhd[:, l].reshape(Bl, P, W_NBHD))
        state = _pack_state(state_mlstm_C[:, l], state_mlstm_n[:, l], state_mlstm_m[:, l])
        xs, _ = _layer(xs, mods, layers[l], B=Bl, S=Sl, cond_rows=Sl, rope_tabs=rope_tabs,
                       ctx_cache=cache, state=state, name=f"lat{l}")

    return (xp.reshape(Bc, Sc, D_MODEL), xs.reshape(Bl, Sl, D_MODEL),
            new_k_attn, new_v_attn, new_k_nbhd, new_v_nbhd, new_C, new_n, new_m)
```

```python
import functools

import jax
import jax.numpy as jnp
import numpy as np
from jax import lax
from jax.experimental import pallas as pl
from jax.experimental.pallas import tpu as pltpu

F32 = jnp.float32
BF16 = jnp.bfloat16

D_MODEL = 1024
DEPTH = 2
GRID_W = 64
HEAD_DIM = 64
H_ATTN = 6
KV_ATTN = 2
H_MLSTM = 4
H_NBHD = 6
D_FF = 4 * D_MODEL
NA_ROWS = 8
NA_COLS = 16
ROPE_THETA = 10000.0
EPS = 1e-6
N_MOD = 6
W_ATTN = H_ATTN * HEAD_DIM
W_KV = KV_ATTN * HEAD_DIM
W_MLSTM = H_MLSTM * HEAD_DIM
W_NBHD = H_NBHD * HEAD_DIM
N_GATES = 4 * H_MLSTM

LANES = 128
V7X_VMEM_BYTES = 64 * 1024 * 1024
VMEM_CAP_BYTES = 56 * 1024 * 1024

TM_PROJ = 512
PROJ_CHUNK = 512
TM_FFN = 1024
TF_FFN = 1024
TQ_ATTN = 512
CK_ATTN = 512
ATTN_LOOKAHEAD = 2
L_CHUNK = 128
MLSTM_GROUP = 2
MLSTM_GROUP_A = 4
NB_GROUP = 8
NB_SLAB = 16
NB_STEP_GROUPS = 4
NB_KV_BLOCK = 256
NEG = -1e30
LOG2E = 1.4426950408889634

_COLS = {}
_off = 0
for _name, _w in (("qa", W_ATTN), ("ka", W_KV), ("va", W_KV), ("qb", W_MLSTM), ("kb", W_MLSTM),
                  ("vb", W_MLSTM), ("ob", W_MLSTM), ("gf", LANES), ("gb", LANES),
                  ("qc", W_NBHD), ("kc", W_NBHD), ("vc", W_NBHD)):
    _COLS[_name] = (_off, _off + _w)
    _off += _w
IN_PAD = _off


def _vmem_limit(nbytes):
    return int(min(max(nbytes, 16 * 1024 * 1024), VMEM_CAP_BYTES))


def _dot(a, b):
    return jnp.dot(a, b, preferred_element_type=F32)


def _dot_nt(a, b):
    return lax.dot_general(a, b, (((1,), (1,)), ((), ())), preferred_element_type=F32)


def _lane_lo(shape):
    return (lax.broadcasted_iota(jnp.int32, shape, len(shape) - 1) % LANES) < HEAD_DIM


def _rms(x, g):
    ms = jnp.mean(x * x, axis=-1, keepdims=True)
    return (x * lax.rsqrt(ms + EPS)) * g


def _pair_rms(x, g):
    lo = _lane_lo(x.shape)
    x2 = x * x
    s_lo = jnp.sum(jnp.where(lo, x2, 0.0), axis=-1, keepdims=True)
    s_hi = jnp.sum(jnp.where(lo, 0.0, x2), axis=-1, keepdims=True)
    r = jnp.where(lo, lax.rsqrt(s_lo / HEAD_DIM + EPS), lax.rsqrt(s_hi / HEAD_DIM + EPS))
    return (x * r) * g


def _sigmoid(x):
    return 1.0 / (1.0 + jnp.exp(-x))


def _mods_kernel(c_ref, w_ref, b_ref, o_ref):
    c = c_ref[...]
    s = (c * _sigmoid(c)).astype(BF16)
    o_ref[0] = _dot(s, w_ref[0].astype(BF16)) + b_ref[0]


def _modulation(cond, w_ada, b_ada):
    n = cond.shape[0]
    tn = D_MODEL
    return pl.pallas_call(
        _mods_kernel,
        grid=(DEPTH, N_MOD * D_MODEL // tn),
        in_specs=[pl.BlockSpec((n, D_MODEL), lambda l, j: (0, 0)),
                  pl.BlockSpec((1, D_MODEL, tn), lambda l, j: (l, 0, j)),
                  pl.BlockSpec((1, 1, tn), lambda l, j: (l, 0, j))],
        out_specs=pl.BlockSpec((1, n, tn), lambda l, j: (l, 0, j)),
        out_shape=jax.ShapeDtypeStruct((DEPTH, n, N_MOD * D_MODEL), F32),
        compiler_params=pltpu.CompilerParams(
            dimension_semantics=("arbitrary", "arbitrary"),
            vmem_limit_bytes=_vmem_limit(4 * D_MODEL * tn * 4)),
        name="modulation",
    )(cond, w_ada, b_ada.reshape(DEPTH, 1, N_MOD * D_MODEL))


def _inproj_kernel(*refs, rope):
    if rope:
        (x_ref, mod_ref, g_ref, w_ref, qn_ref, kn_ref, gbias_ref, cos_ref, sin_ref,
         qa_ref, ka_ref, va_ref, qb_ref, kb_ref, vb_ref, ob_ref, gate_ref,
         qc_ref, kc_ref, vc_ref, vat_ref, vct_ref) = refs
    else:
        (x_ref, mod_ref, g_ref, w_ref, qn_ref, kn_ref, gbias_ref,
         qa_ref, ka_ref, va_ref, qb_ref, kb_ref, vb_ref, ob_ref, gate_ref,
         qc_ref, kc_ref, vc_ref, vat_ref, vct_ref) = refs
    x = x_ref[...]
    mod = mod_ref[0]
    sh1 = mod[:, 0:D_MODEL]
    sc1 = mod[:, D_MODEL:2 * D_MODEL]
    hb = (_rms(x, g_ref[...]) * (1.0 + sc1) + sh1).astype(BF16)

    z = [_dot(hb, w_ref[:, c0:c0 + PROJ_CHUNK]) for c0 in range(0, IN_PAD, PROJ_CHUNK)]

    def proj(name, j=0, w=None):
        lo, hi = _COLS[name]
        lo = lo + j
        hi = hi if w is None else lo + w
        parts = []
        while lo < hi:
            c, o = divmod(lo, PROJ_CHUNK)
            n = min(hi - lo, PROJ_CHUNK - o)
            parts.append(z[c][:, o:o + n])
            lo += n
        return parts[0] if len(parts) == 1 else jnp.concatenate(parts, axis=1)

    scale = HEAD_DIM ** -0.5
    q_scale = scale * LOG2E

    def rotary(t):
        first = (lax.broadcasted_iota(jnp.int32, t.shape, 1) % 32) < 16
        partner = jnp.where(first, pltpu.roll(t, LANES - 16, 1), pltpu.roll(t, 16, 1))
        return t * cos_ref[...] + partner * sin_ref[...]

    for j in range(W_ATTN // LANES):
        t = _pair_rms(proj("qa", j * LANES, LANES), qn_ref[...])
        if rope:
            t = rotary(t)
        qa_ref[:, j * LANES:(j + 1) * LANES] = (t * q_scale).astype(qa_ref.dtype)
    t = _pair_rms(proj("ka"), kn_ref[...])
    if rope:
        t = rotary(t)
    ka_ref[...] = t.astype(ka_ref.dtype)

    def store_v(v, v_ref, vt_ref):
        v_ref[...] = v.astype(v_ref.dtype)
        nblk, _, blk = vt_ref.shape
        for u in range(nblk):
            vt_ref[u] = v[u * blk:(u + 1) * blk].T.astype(vt_ref.dtype)

    store_v(proj("va"), va_ref, vat_ref)
    qb_ref[...] = proj("qb").astype(qb_ref.dtype)
    kb_ref[...] = (proj("kb") * scale).astype(kb_ref.dtype)
    vb_ref[...] = proj("vb").astype(vb_ref.dtype)
    ob_ref[...] = proj("ob").astype(ob_ref.dtype)
    for j, name in enumerate(("gf", "gb")):
        gt = proj(name) + gbias_ref[:, j * LANES:(j + 1) * LANES]
        lane = lax.broadcasted_iota(jnp.int32, gt.shape, 1)
        is_f = (lane >= H_MLSTM) & (lane < 2 * H_MLSTM)
        logsig = jnp.minimum(gt, 0.0) - jnp.log1p(jnp.exp(-jnp.abs(gt)))
        gate_ref[:, j * LANES:(j + 1) * LANES] = jnp.where(is_f, logsig, gt)
    qc_ref[...] = (proj("qc") * q_scale).astype(qc_ref.dtype)
    kc_ref[...] = proj("kc").astype(kc_ref.dtype)
    store_v(proj("vc"), vc_ref, vct_ref)


def _inproj(x, mods, g_pre, w_in_p, qn, kn, gbias, rope_tabs, *, rows_per_cond, kv_dtype, vt_blocks, name):
    T = x.shape[0]
    tm = TM_PROJ
    bpc = rows_per_cond // tm
    rope = rope_tabs is not None
    row = lambda i: (i, 0)
    const = lambda i: (0, 0)
    in_specs = [pl.BlockSpec((tm, D_MODEL), row),
                pl.BlockSpec((1, 1, N_MOD * D_MODEL), lambda i: (i // bpc, 0, 0)),
                pl.BlockSpec((1, D_MODEL), const),
                pl.BlockSpec((D_MODEL, IN_PAD), const),
                pl.BlockSpec((1, LANES), const),
                pl.BlockSpec((1, LANES), const),
                pl.BlockSpec((1, 2 * LANES), const)]
    args = [x, mods, g_pre, w_in_p, qn, kn, gbias]
    if rope:
        nblk = rope_tabs[0].shape[0] // tm
        in_specs += [pl.BlockSpec((tm, LANES), lambda i: (i % nblk, 0))] * 2
        args += list(rope_tabs)
    widths = [("qa", W_ATTN, BF16), ("ka", W_KV, kv_dtype), ("va", W_KV, kv_dtype),
              ("qb", W_MLSTM, BF16), ("kb", W_MLSTM, BF16), ("vb", W_MLSTM, BF16),
              ("ob", W_MLSTM, F32), ("g", 2 * LANES, F32),
              ("qc", W_NBHD, BF16), ("kc", W_NBHD, kv_dtype), ("vc", W_NBHD, kv_dtype)]
    out_specs = [pl.BlockSpec((tm, w), row) for _, w, _ in widths]
    out_shape = [jax.ShapeDtypeStruct((T, w), dt) for _, w, dt in widths]
    for w, blk in zip((W_KV, W_NBHD), vt_blocks):
        out_specs.append(pl.BlockSpec((tm // blk, w, blk), lambda i: (i, 0, 0)))
        out_shape.append(jax.ShapeDtypeStruct((T // blk, w, blk), BF16))
    est = 2 * (tm * D_MODEL * 4 + D_MODEL * IN_PAD * 2 + tm * IN_PAD * 4) + 3 * tm * IN_PAD * 4
    outs = pl.pallas_call(
        functools.partial(_inproj_kernel, rope=rope),
        grid=(T // tm,),
        in_specs=in_specs, out_specs=out_specs, out_shape=out_shape,
        compiler_params=pltpu.CompilerParams(dimension_semantics=("arbitrary",),
                                             vmem_limit_bytes=_vmem_limit(est)),
        name=name,
    )(*args)
    return dict(zip([n for n, _, _ in widths] + ["vat", "vct"], outs))


def _attn_kernel(*refs, heads, part_chunks):
    n_parts = len(part_chunks)
    assert n_parts in (1, 2)
    q_ref, kv_refs = refs[0], refs[1:1 + 2 * n_parts]
    o_ref, qs_scr, m_scr, l_scr, acc_scr, s_ring, cmax_scr = refs[1 + 2 * n_parts:]
    nch = sum(part_chunks)

    def chunk_of(which, j, rows, cols):
        first = kv_refs[which][0, jnp.minimum(j, part_chunks[0] - 1), rows, cols].astype(BF16)
        if n_parts == 1:
            return first
        j2 = jnp.clip(j - part_chunks[0], 0, part_chunks[1] - 1)
        return jnp.where(j < part_chunks[0], first, kv_refs[2 + which][0, j2, rows, cols].astype(BF16))

    nh = len(heads)
    ring = ATTN_LOOKAHEAD + 1
    assert nh % ring == 0
    for h, (kg, kh) in enumerate(heads):
        qg = q_ref[0, :, (h // 2) * LANES:(h // 2 + 1) * LANES].astype(F32)
        if h % 2 != kh:
            qg = pltpu.roll(qg, HEAD_DIM, 1)
        keep = _lane_lo(qg.shape) if kh == 0 else jnp.logical_not(_lane_lo(qg.shape))
        qs_scr[h] = jnp.where(keep, qg, 0.0).astype(BF16)
    m_scr[...] = jnp.full(m_scr.shape, -jnp.inf, F32)
    l_scr[...] = jnp.zeros(l_scr.shape, F32)
    acc_scr[...] = jnp.zeros(acc_scr.shape, F32)

    def scores(j, item):
        h = item % nh
        kg = heads[h][0]
        kj = chunk_of(0, j, slice(None), slice(kg * LANES, (kg + 1) * LANES))
        st = _dot_nt(kj, qs_scr[h])
        s_ring[item % ring] = st
        cmax_scr[item % ring] = jnp.max(st, axis=0, keepdims=True)

    def chunk(j, carry):
        j_next = jnp.minimum(j + 1, nch - 1)
        for h, (kg, kh) in enumerate(heads):
            ahead = h + ATTN_LOOKAHEAD
            scores(j if ahead < nh else j_next, ahead)
            m_old = m_scr[h]
            m_new = jnp.maximum(m_old, cmax_scr[h % ring])
            e = jnp.exp2(s_ring[h % ring] - m_new)
            alpha = jnp.exp2(m_old - m_new)
            l_scr[h] = alpha * l_scr[h] + jnp.sum(e, axis=0, keepdims=True)
            r = kg * LANES + kh * HEAD_DIM
            vt = chunk_of(1, j, slice(r, r + HEAD_DIM), slice(None))
            acc_scr[h] = alpha * acc_scr[h] + _dot(vt, e.astype(BF16))
            m_scr[h] = m_new
        return carry

    for item in range(ATTN_LOOKAHEAD):
        scores(0, item)
    lax.fori_loop(0, nch, chunk, 0)
    for t in range(len(heads) // 2):
        ot = jnp.concatenate([acc_scr[2 * t] / l_scr[2 * t], acc_scr[2 * t + 1] / l_scr[2 * t + 1]], axis=0)
        o_ref[0, :, t * LANES:(t + 1) * LANES] = ot.T.astype(o_ref.dtype)


def _attention(q, parts, *, heads, name):
    B, Sq, W = q.shape
    ck, KW = parts[0][0].shape[2], parts[0][0].shape[3]
    tq = min(TQ_ATTN, Sq)
    nh = len(heads)
    part_chunks = tuple(k4.shape[1] for k4, _ in parts)
    in_specs = [pl.BlockSpec((1, tq, W), lambda b, i: (b, i, 0))]
    args = [q]
    kv_bytes = 0
    for k4, vt4 in parts:
        assert k4.shape[2:] == (ck, KW) and vt4.shape[2:] == (KW, ck)
        in_specs += [pl.BlockSpec((1,) + k4.shape[1:], lambda b, i: (b, 0, 0, 0)),
                     pl.BlockSpec((1,) + vt4.shape[1:], lambda b, i: (b, 0, 0, 0))]
        args += [k4, vt4]
        kv_bytes += k4[0].size * k4.dtype.itemsize + vt4[0].size * vt4.dtype.itemsize
    est = (2 * (2 * tq * W * 2 + kv_bytes) + nh * tq * (LANES * 2 + HEAD_DIM * 4 + 64)
           + 8 * nh * ck * tq * 4)
    return pl.pallas_call(
        functools.partial(_attn_kernel, heads=heads, part_chunks=part_chunks),
        grid=(B, Sq // tq),
        in_specs=in_specs,
        out_specs=pl.BlockSpec((1, tq, W), lambda b, i: (b, i, 0)),
        out_shape=jax.ShapeDtypeStruct((B, Sq, W), BF16),
        scratch_shapes=[pltpu.VMEM((nh, tq, LANES), BF16),
                        pltpu.VMEM((nh, 1, tq), F32),
                        pltpu.VMEM((nh, 1, tq), F32),
                        pltpu.VMEM((nh, HEAD_DIM, tq), F32),
                        pltpu.VMEM((ATTN_LOOKAHEAD + 1, ck, tq), F32),
                        pltpu.VMEM((ATTN_LOOKAHEAD + 1, 1, tq), F32)],
        compiler_params=pltpu.CompilerParams(dimension_semantics=("arbitrary", "arbitrary"),
                                             vmem_limit_bytes=_vmem_limit(est)),
        name=name,
    )(*args)


_HEADS_GQA = tuple((0, h // (H_ATTN // KV_ATTN)) for h in range(H_ATTN))
_HEADS_MHA = tuple((h // 2, h % 2) for h in range(H_NBHD))


def _nbhd_window(r, rows):
    kr = min(NA_ROWS, rows)
    return min(max(r - kr // 2, 0), rows - kr), kr


def _nbhd_patterns(rows):
    n_groups = rows // NB_GROUP
    pats = []
    for g in (0, 1, n_groups - 1):
        r0 = g * NB_GROUP
        pats.append((r0, min(max(r0 - NA_ROWS // 2, 0), rows - NB_SLAB)))
    return pats


def _nbhd_kernel(rb_ref, q_ref, k_ref, vt_ref, kc_ref, vct_ref, o_ref,
                 bias_scr, qs_scr, m_scr, l_scr, acc_scr, s_ring, cmax_scr, *, rows):
    hp = pl.program_id(0)
    b = pl.program_id(1)
    gs = pl.program_id(2)
    n_groups = rows // NB_GROUP
    n_dr = 2 * NA_ROWS - 1
    n_dc = 2 * NA_COLS - 1
    tq = NB_GROUP * GRID_W
    ck = tq
    pats = _nbhd_patterns(rows)
    ring = ATTN_LOOKAHEAD + 1
    n_chunks = NB_SLAB * GRID_W // ck + 1
    assert n_chunks % ring == 0 and kc_ref.shape[1] == ck
    blk_rows = NB_KV_BLOCK // GRID_W

    @pl.when((b == 0) & (gs == 0))
    def _build_bias():
        shape = (GRID_W, LANES)
        w = lax.broadcasted_iota(jnp.int32, shape, 0)
        lane = lax.broadcasted_iota(jnp.int32, shape, 1)
        cc = lane % GRID_W
        second = lane >= GRID_W
        cs = jnp.clip(w - NA_COLS // 2, 0, GRID_W - NA_COLS)
        col_ok = (cc >= cs) & (cc < cs + NA_COLS)
        dc = cc - w + (NA_COLS - 1)
        for hh in range(2):
            base = (2 * hp + hh) * (n_dr * n_dc)
            tiles = {}
            for d in range(-1, n_dr):
                acc = jnp.zeros(shape, F32)
                for j in range(n_dc):
                    va = rb_ref[base + d * n_dc + j] * LOG2E if d >= 0 else 0.0
                    vb = rb_ref[base + (d + 1) * n_dc + j] * LOG2E if d + 1 < n_dr else 0.0
                    acc = acc + jnp.where(dc == j, jnp.where(second, vb, va), 0.0)
                tiles[d] = acc
            def query_row_tile(r, kra):
                rs, kr = _nbhd_window(r, rows)
                ok_a = rs <= kra < rs + kr
                ok_b = rs <= kra + 1 < rs + kr
                if not (ok_a or ok_b):
                    return jnp.full(shape, NEG, F32)
                row_ok = (jnp.logical_not(second) if ok_a and not ok_b else
                          second if ok_b and not ok_a else None)
                ok = col_ok if row_ok is None else (col_ok & row_ok)
                return jnp.where(ok, tiles[kra - r + (NA_ROWS - 1)], NEG)

            for pi, (r0, slab0) in enumerate(pats):
                for ip in range(NB_GROUP // 2):
                    for ap in range(NB_SLAB // 2):
                        kra = slab0 + 2 * ap
                        two_rows = jnp.concatenate([query_row_tile(r0 + 2 * ip, kra),
                                                    query_row_tile(r0 + 2 * ip + 1, kra)], axis=0)
                        bias_scr[hh, pi, ap * LANES:(ap + 1) * LANES, ip * LANES:(ip + 1) * LANES] = two_rows.T

    lo = _lane_lo((tq, LANES))
    assert (2 * n_chunks) % ring == 0
    for s in range(NB_STEP_GROUPS):
        q = q_ref[0, s * tq:(s + 1) * tq, :].astype(F32)
        for hh in range(2):
            qs_scr[2 * s + hh] = jnp.where(lo if hh == 0 else jnp.logical_not(lo), q, 0.0).astype(BF16)
    m_scr[...] = jnp.full(m_scr.shape, -jnp.inf, F32)
    l_scr[...] = jnp.zeros(l_scr.shape, F32)
    acc_scr[...] = jnp.zeros(acc_scr.shape, F32)

    def group_of(s):
        g = gs * NB_STEP_GROUPS + s
        pat = jnp.where(g == 0, 0, jnp.where(g == n_groups - 1, 2, 1))
        slab0 = jnp.clip(g * NB_GROUP - NA_ROWS // 2, 0, rows - NB_SLAB)
        return pat, slab0 // blk_rows

    per_group = 2 * n_chunks
    nb = ck // NB_KV_BLOCK

    def scores(s, item):
        hh, c = (item % per_group) // n_chunks, item % n_chunks
        qm = qs_scr[2 * s + hh]
        if c < n_chunks - 1:
            pat, blk0 = group_of(s)
            kc = k_ref[0, pl.ds(blk0 + c * nb, nb)].reshape(ck, LANES)
            st = _dot_nt(kc, qm) + bias_scr[hh, pat, c * ck:(c + 1) * ck, :]
        else:
            st = _dot_nt(kc_ref[0].astype(BF16), qm)
        s_ring[item % ring] = st
        cmax_scr[item % ring] = jnp.max(st, axis=0, keepdims=True)

    def group(s, carry):
        s_next = jnp.minimum(s + 1, NB_STEP_GROUPS - 1)
        for item in range(per_group):
            ahead = item + ATTN_LOOKAHEAD
            scores(s if ahead < per_group else s_next, ahead)
            hh, c = item // n_chunks, item % n_chunks
            u = 2 * s + hh
            m_old = m_scr[u]
            m_new = jnp.maximum(m_old, cmax_scr[item % ring])
            e = jnp.exp2(s_ring[item % ring] - m_new)
            alpha = jnp.exp2(m_old - m_new)
            l_scr[u] = alpha * l_scr[u] + jnp.sum(e, axis=0, keepdims=True)
            if c < n_chunks - 1:
                _, blk0 = group_of(s)
                vt = jnp.concatenate([vt_ref[0, blk0 + c * nb + i, hh * HEAD_DIM:(hh + 1) * HEAD_DIM, :]
                                      for i in range(nb)], axis=1)
            else:
                vt = vct_ref[0, hh * HEAD_DIM:(hh + 1) * HEAD_DIM, :].astype(BF16)
            acc_scr[u] = alpha * acc_scr[u] + _dot(vt, e.astype(BF16))
            m_scr[u] = m_new
        ot = jnp.concatenate([acc_scr[2 * s] / l_scr[2 * s], acc_scr[2 * s + 1] / l_scr[2 * s + 1]], axis=0)
        o_ref[0, pl.ds(pl.multiple_of(s * tq, tq), tq), :] = ot.T.astype(o_ref.dtype)
        return carry

    for item in range(ATTN_LOOKAHEAD):
        scores(0, item)
    lax.fori_loop(0, NB_STEP_GROUPS, group, 0)


def _nbhd_attention(q, k4, vt4, k_ctx, vct, rel_bias_flat, *, name):
    B, S, W = q.shape
    P = k_ctx.shape[1]
    rows = S // GRID_W
    tq = NB_GROUP * GRID_W
    nk = NB_SLAB * GRID_W
    nblk = S // NB_KV_BLOCK
    tqs = NB_STEP_GROUPS * tq
    nu = 2 * NB_STEP_GROUPS
    ring = ATTN_LOOKAHEAD + 1
    est =(2 * (2 * tqs * LANES * 2 + 2 * S * LANES * 2 + 2 * P * LANES * 4)
           + 2 * 3 * tq * nk * 4 + ring * tq * tq * 4 + 8 * tq * tq * 4 + nu * tq * 1024)
    return pl.pallas_call(
        functools.partial(_nbhd_kernel, rows=rows),
        grid=(W // LANES, B, rows // (NB_GROUP * NB_STEP_GROUPS)),
        in_specs=[pl.BlockSpec(memory_space=pltpu.SMEM),
                  pl.BlockSpec((1, tqs, LANES), lambda p, b, g: (b, g, p)),
                  pl.BlockSpec((1, nblk, NB_KV_BLOCK, LANES), lambda p, b, g: (b, 0, 0, p)),
                  pl.BlockSpec((1, nblk, LANES, NB_KV_BLOCK), lambda p, b, g: (b, 0, p, 0)),
                  pl.BlockSpec((1, P, LANES), lambda p, b, g: (b, 0, p)),
                  pl.BlockSpec((1, LANES, P), lambda p, b, g: (b, p, 0))],
        out_specs=pl.BlockSpec((1, tqs, LANES), lambda p, b, g: (b, g, p)),
        out_shape=jax.ShapeDtypeStruct((B, S, W), BF16),
        scratch_shapes=[pltpu.VMEM((2, 3, nk, tq), F32),
                        pltpu.VMEM((nu, tq, LANES), BF16),
                        pltpu.VMEM((nu, 1, tq), F32),
                        pltpu.VMEM((nu, 1, tq), F32),
                        pltpu.VMEM((nu, HEAD_DIM, tq), F32),
                        pltpu.VMEM((ring, tq, tq), F32),
                        pltpu.VMEM((ring, 1, tq), F32)],
        compiler_params=pltpu.CompilerParams(dimension_semantics=("arbitrary",) * 3,
                                             vmem_limit_bytes=_vmem_limit(est)),
        name=name,
    )(rel_bias_flat, q, k4, vt4, k_ctx, vct)


def _split3(x):
    hi = x.astype(BF16)
    r = x - hi.astype(F32)
    mid = r.astype(BF16)
    return hi, mid, (r - mid.astype(F32)).astype(BF16)


def _mlstm_kernel(q_ref, k_ref, v_ref, g_ref, ob_ref, s0_ref, m0_ref, on_ref,
                  out_ref, sf_ref, mf_ref,
                  h_scr, nat_scr, rows_scr, stat_scr, mprev_scr, un_scr, st_scr, *, nc, grp_a, grp):
    d = pl.program_id(1)
    L = L_CHUNK
    NP = H_MLSTM // 2
    row = lax.broadcasted_iota(jnp.int32, (L, L), 0)
    col = lax.broadcasted_iota(jnp.int32, (L, L), 1)
    sign = 1 - 2 * d
    mask = (col - row) * sign <= 0
    maskb = mask.astype(BF16)
    mask3 = jnp.concatenate([maskb, maskb, maskb], axis=1)
    lane = lax.broadcasted_iota(jnp.int32, (L, LANES), 1)
    lo = lane < HEAD_DIM
    top = row < HEAD_DIM
    row2 = lax.broadcasted_iota(jnp.int32, (L, 2 * LANES), 0)
    col2 = lax.broadcasted_iota(jnp.int32, (L, 2 * LANES), 1)
    keep_state = (row2 < HEAD_DIM) == ((col2 % LANES) < HEAD_DIM)
    top2 = row2 < HEAD_DIM
    ones_b = jnp.ones((L, LANES), BF16)
    ones_lo = lo.astype(BF16)
    ones_hi = jnp.logical_not(lo).astype(BF16)

    def chunk_rows(c):
        return pl.ds(pl.multiple_of(c * L, L), L)

    def pass_a(it, carry):
        cs = [it * grp_a + u for u in range(grp_a)]
        gts = [g_ref[0, chunk_rows(c), :] for c in cs]
        bns = [_dot(mask3, jnp.concatenate(_split3(gt), axis=0)) for gt in gts]
        a_all = []
        for c, gt, bn in zip(cs, gts, bns):
            nat = jnp.where(lane < H_MLSTM, gt, bn)
            nat_scr[chunk_rows(c), :] = nat * (-LOG2E)
            nat_t = nat.T
            b_rows = nat_t[H_MLSTM:2 * H_MLSTM]
            c_rows = nat_t[0:H_MLSTM] - b_rows
            rows_scr[c] = jnp.concatenate([c_rows * LOG2E, b_rows], axis=0)
            c_max = jnp.max(c_rows, axis=1, keepdims=True)
            b_tot = jnp.where(d == 0, b_rows[:, L - 1:L], b_rows[:, 0:1])
            stat_scr[c] = jnp.concatenate([jnp.broadcast_to(c_max, (H_MLSTM, LANES)),
                                           jnp.broadcast_to(b_tot, (H_MLSTM, LANES))], axis=0)
            a_all.append(jnp.exp(c_rows - c_max))
        for c, a_rows in zip(cs, a_all):
            rows = chunk_rows(c)
            for p in range(NP):
                lanes = slice(p * LANES, (p + 1) * LANES)
                k_t = k_ref[0, rows, lanes].astype(F32).T
                a_sel = jnp.where(top, a_rows[2 * p:2 * p + 1], a_rows[2 * p + 1:2 * p + 2])
                vv = jnp.concatenate([v_ref[0, rows, lanes], ones_b], axis=1)
                un = _dot((k_t * a_sel).astype(BF16), vv)
                un_scr[c, p] = jnp.where(keep_state, un, 0.0)
        return carry

    lax.fori_loop(0, nc // grp_a, pass_a, 0)

    st_scr[...] = s0_ref[0, 0]

    def pass_b(ci, m):
        c = jnp.where(d == 0, ci, nc - 1 - ci)
        st = stat_scr[c]
        c_max, b_tot = st[0:H_MLSTM], st[H_MLSTM:]
        m_new = jnp.maximum(b_tot + m, b_tot + c_max)
        d_old = jnp.exp(b_tot + m - m_new)
        d_new = jnp.exp(b_tot + c_max - m_new)
        mprev_scr[c] = jnp.concatenate([m, m], axis=0) * LOG2E
        for p in range(NP):
            def rows_of(t, p=p):
                even = jnp.concatenate([t[2 * p:2 * p + 1]] * 2, axis=1)
                odd = jnp.concatenate([t[2 * p + 1:2 * p + 2]] * 2, axis=1)
                return jnp.where(top2, even, odd)
            s_prev = st_scr[p]
            st_scr[p] = rows_of(d_old) * s_prev + rows_of(d_new) * un_scr[c, p]
            un_scr[c, p] = s_prev
        return m_new

    m_fin = lax.fori_loop(0, nc, pass_b, m0_ref[0, 0][0:H_MLSTM])
    sf_ref[0, 0] = st_scr[...]
    mf_ref[0, 0] = jnp.concatenate([m_fin, m_fin], axis=0)

    def pass_c(it, carry):
        cs = [it * grp + u for u in range(grp)]
        units = [(u, p) for u in range(grp) for p in range(NP)]
        early = {}
        for u, p in units:
            c = cs[u]
            rows = chunk_rows(c)
            lanes = slice(p * LANES, (p + 1) * LANES)
            qp = q_ref[0, rows, lanes]
            kp = k_ref[0, rows, lanes]
            s_in = un_scr[c, p].astype(BF16)
            qms = [jnp.where(lo if hh == 0 else jnp.logical_not(lo), qp, jnp.zeros_like(qp)) for hh in range(2)]
            early[u, p] = ([_dot_nt(qm, kp) for qm in qms],
                           _dot(qp, s_in))
        mid = {}
        for u, p in units:
            r_t = rows_scr[cs[u]]
            m_in = mprev_scr[cs[u]]
            for hh in range(2):
                h = 2 * p + hh
                cm = jnp.where(mask, r_t[h:h + 1, :], -jnp.inf)
                m_prev = m_in[h:h + 1, :]
                mu = jnp.maximum(jnp.broadcast_to(jnp.max(cm, axis=1, keepdims=True), (L, LANES)), m_prev)
                w = early[u, p][0][hh] * jnp.exp2(cm - mu)
                mid[u, p, hh] = (w.astype(BF16), mu, m_prev)
        for u, p in units:
            rows = chunk_rows(cs[u])
            lanes = slice(p * LANES, (p + 1) * LANES)
            vp = v_ref[0, rows, lanes]
            zero = jnp.zeros_like(vp)
            vv = jnp.concatenate([jnp.concatenate([jnp.where(lo, vp, zero), ones_lo], axis=1),
                                  jnp.concatenate([jnp.where(lo, zero, vp), ones_hi], axis=1)], axis=0)
            w2 = jnp.concatenate([mid[u, p, 0][0], mid[u, p, 1][0]], axis=1)
            nd = _dot(w2, vv)
            nat = nat_scr[rows, :]
            nb = [jnp.broadcast_to(nat[:, H_MLSTM + 2 * p + hh:H_MLSTM + 2 * p + hh + 1], (L, LANES))
                  for hh in range(2)]
            fs = early[u, p][1]
            mu = jnp.where(lo, mid[u, p, 0][1], mid[u, p, 1][1])
            m_prev = jnp.where(lo[0:1], mid[u, p, 0][2], mid[u, p, 1][2])
            w_inter = jnp.exp2(m_prev - mu)
            den = nd[:, LANES:] + w_inter * fs[:, LANES:]
            den = jnp.maximum(jnp.abs(den), jnp.exp2(jnp.where(lo, nb[0], nb[1]) - mu))
            h_scr[d, rows, lanes] = (nd[:, :LANES] + w_inter * fs[:, :LANES]) / den
        return carry

    lax.fori_loop(0, nc // grp, pass_c, 0)

    @pl.when(d == 1)
    def _finish():
        def rows_block(c, carry):
            rows = chunk_rows(c)
            for p in range(NP):
                lanes = slice(p * LANES, (p + 1) * LANES)
                hn = _pair_rms(h_scr[0, rows, lanes] + h_scr[1, rows, lanes], on_ref[:, lanes])
                out_ref[0, rows, lanes] = (_sigmoid(ob_ref[0, rows, lanes]) * hn).astype(out_ref.dtype)
            return carry

        lax.fori_loop(0, nc, rows_block, 0)


def _mlstm(q, k, v, gates, ob, s0, m0, out_norm, *, name):
    assert L_CHUNK == LANES
    B, S, W = q.shape
    nc = S // L_CHUNK
    grp = min(MLSTM_GROUP, nc)
    grp_a = min(MLSTM_GROUP_A, nc)
    npair = H_MLSTM // 2
    seq = lambda b, d: (b, 0, 0)
    est = (2 * (3 * S * W * 2 + S * LANES * 4 + S * W * 4 + S * W * 2) + 2 * S * W * 4 + S * LANES * 4
           + nc * npair * LANES * 2 * LANES * 4 + 12 * 1024 * 1024)
    return pl.pallas_call(
        functools.partial(_mlstm_kernel, nc=nc, grp_a=grp_a, grp=grp),
        grid=(B, 2),
        in_specs=[pl.BlockSpec((1, S, W), seq), pl.BlockSpec((1, S, W), seq), pl.BlockSpec((1, S, W), seq),
                  pl.BlockSpec((1, S, LANES), lambda b, d: (b, 0, d)),
                  pl.BlockSpec((1, S, W), seq),
                  pl.BlockSpec((1, 1, npair, LANES, 2 * LANES), lambda b, d: (b, d, 0, 0, 0)),
                  pl.BlockSpec((1, 1, 8, LANES), lambda b, d: (b, d, 0, 0)),
                  pl.BlockSpec((1, W), lambda b, d: (0, 0))],
        out_specs=[pl.BlockSpec((1, S, W), seq),
                   pl.BlockSpec((1, 1, npair, LANES, 2 * LANES), lambda b, d: (b, d, 0, 0, 0)),
                   pl.BlockSpec((1, 1, 8, LANES), lambda b, d: (b, d, 0, 0))],
        out_shape=[jax.ShapeDtypeStruct((B, S, W), BF16),
                   jax.ShapeDtypeStruct((B, 2, npair, LANES, 2 * LANES), F32),
                   jax.ShapeDtypeStruct((B, 2, 8, LANES), F32)],
        scratch_shapes=[pltpu.VMEM((2, S, W), F32),
                        pltpu.VMEM((S, LANES), F32),
                        pltpu.VMEM((nc, 8, L_CHUNK), F32),
                        pltpu.VMEM((nc, 8, LANES), F32),
                        pltpu.VMEM((nc, 8, LANES), F32),
                        pltpu.VMEM((nc, npair, LANES, 2 * LANES), F32),
                        pltpu.VMEM((npair, LANES, 2 * LANES), F32)],
        compiler_params=pltpu.CompilerParams(dimension_semantics=("arbitrary", "arbitrary"),
                                             vmem_limit_bytes=_vmem_limit(est)),
        name=name,
    )(q, k, v, gates, ob, s0, m0, out_norm)


def _outproj_kernel(a_ref, b_ref, c_ref, x_ref, mod_ref, w_ref, gpost_ref, gpre_ref, x1_ref, h2_ref):
    mo = _dot(jnp.concatenate([a_ref[...], b_ref[...], c_ref[...]], axis=1), w_ref[...])
    mod = mod_ref[0]
    gt1 = mod[:, 2 * D_MODEL:3 * D_MODEL]
    sh2 = mod[:, 3 * D_MODEL:4 * D_MODEL]
    sc2 = mod[:, 4 * D_MODEL:5 * D_MODEL]
    x1 = x_ref[...] + gt1 * _rms(mo, gpost_ref[...])
    x1_ref[...] = x1
    h2_ref[...] = (_rms(x1, gpre_ref[...]) * (1.0 + sc2) + sh2).astype(h2_ref.dtype)


def _outproj(oa, ob, oc, x, mods, w_out, g_post, g_pre, *, rows_per_cond, name):
    T = x.shape[0]
    tm = TM_PROJ
    bpc = rows_per_cond // tm
    row = lambda i: (i, 0)
    const = lambda i: (0, 0)
    est = 2 * (tm * D_MODEL * (2 + 4 + 4 + 2) + D_MODEL * D_MODEL * 2) + 4 * tm * D_MODEL * 4
    return pl.pallas_call(
        _outproj_kernel,
        grid=(T // tm,),
        in_specs=[pl.BlockSpec((tm, W_ATTN), row), pl.BlockSpec((tm, W_MLSTM), row),
                  pl.BlockSpec((tm, W_NBHD), row), pl.BlockSpec((tm, D_MODEL), row),
                  pl.BlockSpec((1, 1, N_MOD * D_MODEL), lambda i: (i // bpc, 0, 0)),
                  pl.BlockSpec((D_MODEL, D_MODEL), const),
                  pl.BlockSpec((1, D_MODEL), const), pl.BlockSpec((1, D_MODEL), const)],
        out_specs=[pl.BlockSpec((tm, D_MODEL), row), pl.BlockSpec((tm, D_MODEL), row)],
        out_shape=[jax.ShapeDtypeStruct((T, D_MODEL), F32), jax.ShapeDtypeStruct((T, D_MODEL), BF16)],
        compiler_params=pltpu.CompilerParams(dimension_semantics=("arbitrary",),
                                             vmem_limit_bytes=_vmem_limit(est)),
        name=name,
    )(oa, ob, oc, x, mods, w_out, g_post, g_pre)


def _ffn_kernel(h_ref, x_ref, mod_ref, wu_ref, wd_ref, g_ref, o_ref, acc_ref):
    j = pl.program_id(1)

    @pl.when(j == 0)
    def _zero():
        acc_ref[...] = jnp.zeros_like(acc_ref)

    u = jnp.maximum(_dot(h_ref[...], wu_ref[...]), 0.0)
    acc_ref[...] += _dot((u * u).astype(BF16), wd_ref[...])

    @pl.when(j == pl.num_programs(1) - 1)
    def _finish():
        gt2 = mod_ref[0][:, 5 * D_MODEL:6 * D_MODEL]
        o_ref[...] = x_ref[...] + gt2 * _rms(acc_ref[...], g_ref[...])


def _ffn(h2, x1, mods, w_up, w_down, g_post, *, rows_per_cond, name):
    T = x1.shape[0]
    tm, tf = TM_FFN, TF_FFN
    tm = min(tm, rows_per_cond)
    bpc = rows_per_cond // tm
    est = 2 * (tm * D_MODEL * (2 + 4 + 4) + 2 * D_MODEL * tf * 2) + tm * D_MODEL * 4 + 3 * tm * tf * 4
    return pl.pallas_call(
        _ffn_kernel,
        grid=(T // tm, D_FF // tf),
        in_specs=[pl.BlockSpec((tm, D_MODEL), lambda i, j: (i, 0)),
                  pl.BlockSpec((tm, D_MODEL), lambda i, j: (i, 0)),
                  pl.BlockSpec((1, 1, N_MOD * D_MODEL), lambda i, j: (i // bpc, 0, 0)),
                  pl.BlockSpec((D_MODEL, tf), lambda i, j: (0, j)),
                  pl.BlockSpec((tf, D_MODEL), lambda i, j: (j, 0)),
                  pl.BlockSpec((1, D_MODEL), lambda i, j: (0, 0))],
        out_specs=pl.BlockSpec((tm, D_MODEL), lambda i, j: (i, 0)),
        out_shape=jax.ShapeDtypeStruct((T, D_MODEL), F32),
        scratch_shapes=[pltpu.VMEM((tm, D_MODEL), F32)],
        compiler_params=pltpu.CompilerParams(dimension_semantics=("arbitrary", "arbitrary"),
                                             vmem_limit_bytes=_vmem_limit(est)),
        name=name,
    )(h2, x1, mods, w_up, w_down, g_post)


def _pad_w_in(w_in_l):
    o = W_ATTN + 2 * W_KV + 4 * W_MLSTM
    pre, gates, post = w_in_l[:, :o], w_in_l[:, o:o + N_GATES], w_in_l[:, o + N_GATES:]
    z = jnp.zeros((D_MODEL, LANES - 2 * H_MLSTM), w_in_l.dtype)
    return jnp.concatenate([pre, gates[:, :2 * H_MLSTM], z, gates[:, 2 * H_MLSTM:], z, post],
                           axis=1).astype(BF16)


def _pad_gate_bias(gb_l):
    z = jnp.zeros((LANES - 2 * H_MLSTM,), gb_l.dtype)
    return jnp.concatenate([gb_l[:2 * H_MLSTM], z, gb_l[2 * H_MLSTM:], z]).reshape(1, 2 * LANES)


def _rope_tables(S):
    quarter = HEAD_DIM // 4
    pos = jnp.arange(S)
    inv_freq = ROPE_THETA ** (-jnp.arange(quarter, dtype=F32) / quarter)

    def tabs(p):
        ang = p.astype(F32)[:, None] * inv_freq[None, :]
        return jnp.cos(ang), jnp.sin(ang)

    cr, sr = tabs(pos // GRID_W)
    cc, sc = tabs(pos % GRID_W)
    cos = jnp.concatenate([cr, cr, cc, cc], axis=1)
    sin = jnp.concatenate([-sr, sr, -sc, sc], axis=1)
    return jnp.tile(cos, (1, 2)), jnp.tile(sin, (1, 2))


def _pack_state(C, n, m):
    B = C.shape[0]
    Cp = C.reshape(B, 2, 2, 2, HEAD_DIM, HEAD_DIM)
    z = jnp.zeros_like(Cp[:, :, :, 0])
    top = jnp.concatenate([Cp[:, :, :, 0], z], axis=-1)
    bot = jnp.concatenate([z, Cp[:, :, :, 1]], axis=-1)
    Cbd = jnp.concatenate([top, bot], axis=-2)
    n_rep = jnp.broadcast_to(n.reshape(B, 2, 2, LANES, 1), (B, 2, 2, LANES, LANES))
    same_head = (jnp.arange(LANES)[:, None] < HEAD_DIM) == (jnp.arange(LANES)[None, :] < HEAD_DIM)
    n_rep = jnp.where(same_head, n_rep, 0.0)
    m_rows = jnp.broadcast_to(m[..., None], m.shape + (LANES,))
    return jnp.concatenate([Cbd, n_rep], axis=-1), jnp.concatenate([m_rows, m_rows], axis=-2)


def _unpack_state(s_p, m_p):
    B = s_p.shape[0]
    c_even = s_p[:, :, :, :HEAD_DIM, :HEAD_DIM]
    c_odd = s_p[:, :, :, HEAD_DIM:, HEAD_DIM:LANES]
    C = jnp.stack([c_even, c_odd], axis=3).reshape(B, 2, H_MLSTM, HEAD_DIM, HEAD_DIM)
    n = jnp.concatenate([s_p[..., :HEAD_DIM, LANES], s_p[..., HEAD_DIM:, LANES + HEAD_DIM]], axis=-1)
    return C, n.reshape(B, 2, H_MLSTM, HEAD_DIM), m_p[:, :, :H_MLSTM, 0]


def _layer(x, mods, lw, *, B, S, cond_rows, rope_tabs, ctx_cache, state, name):
    T = B * S
    kv_dtype = BF16 if ctx_cache is not None else F32
    ck_a = min(CK_ATTN, S)
    pr = _inproj(x, mods, lw["g_pre_mix"], lw["w_in"], lw["q_norm"], lw["k_norm"], lw["gate_bias"],
                 rope_tabs, rows_per_cond=cond_rows, kv_dtype=kv_dtype,
                 vt_blocks=(ck_a, NB_KV_BLOCK), name=name + "_inproj")
    seq = lambda a: a.reshape(B, S, a.shape[-1])
    qa, ka, va = seq(pr["qa"]), seq(pr["ka"]), seq(pr["va"])
    qc, kc, vc = seq(pr["qc"]), seq(pr["kc"]), seq(pr["vc"])
    chunks = lambda a, n: a.reshape(B, S // n, n, a.shape[-1])
    vat = pr["vat"].reshape(B, S // ck_a, W_KV, ck_a)
    vct = pr["vct"].reshape(B, S // NB_KV_BLOCK, W_NBHD, NB_KV_BLOCK)
    if ctx_cache is None:
        out_a = _attention(qa, [(chunks(ka, ck_a), vat)], heads=_HEADS_GQA, name=name + "_attn_a")
        out_c = _attention(qc, [(chunks(kc, NB_KV_BLOCK), vct)], heads=_HEADS_MHA, name=name + "_attn_c")
    else:
        ck_c, cv_c = ctx_cache[2], ctx_cache[3]
        P = ck_c.shape[1]
        ctx_k = ctx_cache[0].reshape(B, P // ck_a, ck_a, W_KV)
        ctx_vt = ctx_cache[1].reshape(B, P // ck_a, ck_a, W_KV).transpose(0, 1, 3, 2)
        out_a = _attention(qa, [(chunks(ka, ck_a), vat), (ctx_k, ctx_vt)], heads=_HEADS_GQA,
                           name=name + "_attn_a")
        out_c = _nbhd_attention(qc, chunks(kc, NB_KV_BLOCK), vct, ck_c, cv_c.transpose(0, 2, 1),
                                lw["rel_bias"], name=name + "_attn_c")
    s0, m0 = state
    out_b, sf, mf = _mlstm(seq(pr["qb"]), seq(pr["kb"]), seq(pr["vb"]), seq(pr["g"]), seq(pr["ob"]),
                           s0, m0, lw["out_norm"], name=name + "_mlstm")
    x1, h2 = _outproj(out_a.reshape(T, W_ATTN), out_b.reshape(T, W_MLSTM), out_c.reshape(T, W_NBHD),
                      x, mods, lw["w_out"], lw["g_post_mix"], lw["g_pre_ffn"],
                      rows_per_cond=cond_rows, name=name + "_outproj")
    x2 = _ffn(h2, x1, mods, lw["w_up"], lw["w_down"], lw["g_post_ffn"],
              rows_per_cond=cond_rows, name=name + "_ffn")
    return x2, (ka, va, kc, vc, sf, mf)


def kernel(x_prompt, x_sample, c, cache_k_attn, cache_v_attn, cache_k_nbhd, cache_v_nbhd, state_mlstm_C, state_mlstm_n, state_mlstm_m, c_ctx, w_ada, b_ada, g_pre_mix, g_post_mix, g_pre_ffn, g_post_ffn, w_in, q_norm_attn, k_norm_attn, mlstm_gate_bias, mlstm_out_norm, nbhd_rel_bias, w_out, w_ffn_up, w_ffn_down):
    Bc, Sc, _ = x_prompt.shape
    Bl, Sl, _ = x_sample.shape
    P = cache_k_attn.shape[2]
    n_cond = 8
    cond = jnp.concatenate([c_ctx[None, :], c, jnp.zeros((n_cond - 1 - Bl, D_MODEL), F32)], axis=0)
    mods_all = _modulation(cond, w_ada, b_ada)

    layers = []
    for l in range(DEPTH):
        layers.append(dict(
            w_in=_pad_w_in(w_in[l]),
            w_out=w_out[l].astype(BF16),
            w_up=w_ffn_up[l].astype(BF16),
            w_down=w_ffn_down[l].astype(BF16),
            g_pre_mix=g_pre_mix[l].reshape(1, D_MODEL), g_post_mix=g_post_mix[l].reshape(1, D_MODEL),
            g_pre_ffn=g_pre_ffn[l].reshape(1, D_MODEL), g_post_ffn=g_post_ffn[l].reshape(1, D_MODEL),
            q_norm=jnp.tile(q_norm_attn[l], 2).reshape(1, LANES),
            k_norm=jnp.tile(k_norm_attn[l], 2).reshape(1, LANES),
            gate_bias=_pad_gate_bias(mlstm_gate_bias[l]),
            out_norm=mlstm_out_norm[l].reshape(1, W_MLSTM),
            rel_bias=nbhd_rel_bias[l].reshape(-1),
        ))

    xp = x_prompt.reshape(Bc * Sc, D_MODEL)
    zero_state = _pack_state(jnp.zeros((Bc, 2, H_MLSTM, HEAD_DIM, HEAD_DIM), F32),
                             jnp.zeros((Bc, 2, H_MLSTM, HEAD_DIM), F32),
                             jnp.zeros((Bc, 2, H_MLSTM), F32))
    ctx = []
    for l in range(DEPTH):
        mods = mods_all[l, 0:1].reshape(1, 1, N_MOD * D_MODEL)
        xp, extras = _layer(xp, mods, layers[l], B=Bc, S=Sc, cond_rows=Bc * Sc, rope_tabs=None,
                            ctx_cache=None, state=zero_state, name=f"ctx{l}")
        ctx.append(extras)
    new_k_attn = jnp.stack([e[0].reshape(Bc, Sc, KV_ATTN, HEAD_DIM) for e in ctx], axis=1)
    new_v_attn = jnp.stack([e[1].reshape(Bc, Sc, KV_ATTN, HEAD_DIM) for e in ctx], axis=1)
    new_k_nbhd = jnp.stack([e[2].reshape(Bc, Sc, H_NBHD, HEAD_DIM) for e in ctx], axis=1)
    new_v_nbhd = jnp.stack([e[3].reshape(Bc, Sc, H_NBHD, HEAD_DIM) for e in ctx], axis=1)
    states = [_unpack_state(e[4], e[5]) for e in ctx]
    new_C = jnp.stack([s[0] for s in states], axis=1)
    new_n = jnp.stack([s[1] for s in states], axis=1)
    new_m = jnp.stack([s[2] for s in states], axis=1)

    xs = x_sample.reshape(Bl * Sl, D_MODEL)
    rope_tabs = _rope_tables(Sl)
    for l in range(DEPTH):
        mods = mods_all[l, 1:1 + Bl].reshape(Bl, 1, N_MOD * D_MODEL)
        cache = (cache_k_attn[:, l].reshape(Bl, P, W_KV), cache_v_attn[:, l].reshape(Bl, P, W_KV),
                 cache_k_nbhd[:, l].reshape(Bl, P, W_NBHD), cache_v_nbhd[:, l].reshape(Bl, P, W_NBHD))
        state = _pack_state(state_mlstm_C[:, l], state_mlstm_n[:, l], state_mlstm_m[:, l])
        xs, _ = _layer(xs, mods, layers[l], B=Bl, S=Sl, cond_rows=Sl, rope_tabs=rope_tabs,
                       ctx_cache=cache, state=state, name=f"lat{l}")

    return (xp.reshape(Bc, Sc, D_MODEL), xs.reshape(Bl, Sl, D_MODEL),
            new_k_attn, new_v_attn, new_k_nbhd, new_v_nbhd, new_C, new_n, new_m)
```

```python
import functools

import jax
import jax.numpy as jnp
import numpy as np
from jax import lax
from jax.experimental import pallas as pl
from jax.experimental.pallas import tpu as pltpu

F32 = jnp.float32
BF16 = jnp.bfloat16

D_MODEL = 1024
DEPTH = 2
GRID_W = 64
HEAD_DIM = 64
H_ATTN = 6
KV_ATTN = 2
H_MLSTM = 4
H_NBHD = 6
D_FF = 4 * D_MODEL
NA_ROWS = 8
NA_COLS = 16
ROPE_THETA = 10000.0
EPS = 1e-6
N_MOD = 6
W_ATTN = H_ATTN * HEAD_DIM
W_KV = KV_ATTN * HEAD_DIM
W_MLSTM = H_MLSTM * HEAD_DIM
W_NBHD = H_NBHD * HEAD_DIM
N_GATES = 4 * H_MLSTM

LANES = 128
V7X_VMEM_BYTES = 64 * 1024 * 1024
VMEM_CAP_BYTES = 56 * 1024 * 1024

TM_PROJ = 512
PROJ_CHUNK = 512
TM_FFN = 1024
TF_FFN = 1024
ROW_SPLIT = 4
TQ_ATTN = 512
CK_ATTN = 512
ATTN_LOOKAHEAD = 2
L_CHUNK = 128
MLSTM_GROUP = 2
MLSTM_GROUP_A = 4
NB_GROUP = 8
NB_SLAB = 16
NB_STEP_GROUPS = 4
NB_KV_BLOCK = 256
NEG = -1e30
LOG2E = 1.4426950408889634

_COLS = {}
_off = 0
for _name, _w in (("qa", W_ATTN), ("ka", W_KV), ("va", W_KV), ("qb", W_MLSTM), ("kb", W_MLSTM),
                  ("vb", W_MLSTM), ("ob", W_MLSTM), ("gf", LANES), ("gb", LANES),
                  ("qc", W_NBHD), ("kc", W_NBHD), ("vc", W_NBHD)):
    _COLS[_name] = (_off, _off + _w)
    _off += _w
IN_PAD = _off


def _vmem_limit(nbytes):
    return int(min(max(nbytes, 16 * 1024 * 1024), VMEM_CAP_BYTES))


def _dot(a, b):
    return jnp.dot(a, b, preferred_element_type=F32)


def _dot_nt(a, b):
    return lax.dot_general(a, b, (((1,), (1,)), ((), ())), preferred_element_type=F32)


def _lane_lo(shape):
    return (lax.broadcasted_iota(jnp.int32, shape, len(shape) - 1) % LANES) < HEAD_DIM


def _rms(x, g):
    ms = jnp.mean(x * x, axis=-1, keepdims=True)
    return (x * lax.rsqrt(ms + EPS)) * g


def _pair_rms(x, g):
    lo = _lane_lo(x.shape)
    x2 = x * x
    s_lo = jnp.sum(jnp.where(lo, x2, 0.0), axis=-1, keepdims=True)
    s_hi = jnp.sum(jnp.where(lo, 0.0, x2), axis=-1, keepdims=True)
    r = jnp.where(lo, lax.rsqrt(s_lo / HEAD_DIM + EPS), lax.rsqrt(s_hi / HEAD_DIM + EPS))
    return (x * r) * g


def _sigmoid(x):
    return 1.0 / (1.0 + jnp.exp(-x))


def _mods_kernel(c_ref, w_ref, b_ref, o_ref):
    c = c_ref[...]
    s = (c * _sigmoid(c)).astype(BF16)
    o_ref[0] = _dot(s, w_ref[0].astype(BF16)) + b_ref[0]


def _modulation(cond, w_ada, b_ada):
    n = cond.shape[0]
    tn = D_MODEL
    return pl.pallas_call(
        _mods_kernel,
        grid=(DEPTH, N_MOD * D_MODEL // tn),
        in_specs=[pl.BlockSpec((n, D_MODEL), lambda l, j: (0, 0)),
                  pl.BlockSpec((1, D_MODEL, tn), lambda l, j: (l, 0, j)),
                  pl.BlockSpec((1, 1, tn), lambda l, j: (l, 0, j))],
        out_specs=pl.BlockSpec((1, n, tn), lambda l, j: (l, 0, j)),
        out_shape=jax.ShapeDtypeStruct((DEPTH, n, N_MOD * D_MODEL), F32),
        compiler_params=pltpu.CompilerParams(
            dimension_semantics=("arbitrary", "arbitrary"),
            vmem_limit_bytes=_vmem_limit(4 * D_MODEL * tn * 4)),
        name="modulation",
    )(cond, w_ada, b_ada.reshape(DEPTH, 1, N_MOD * D_MODEL))


def _inproj_kernel(*refs, rope):
    if rope:
        (x_ref, mod_ref, g_ref, w_ref, qn_ref, kn_ref, gbias_ref, cos_ref, sin_ref,
         qa_ref, ka_ref, va_ref, qb_ref, kb_ref, vb_ref, ob_ref, gate_ref,
         qc_ref, kc_ref, vc_ref, vat_ref, vct_ref) = refs
    else:
        (x_ref, mod_ref, g_ref, w_ref, qn_ref, kn_ref, gbias_ref,
         qa_ref, ka_ref, va_ref, qb_ref, kb_ref, vb_ref, ob_ref, gate_ref,
         qc_ref, kc_ref, vc_ref, vat_ref, vct_ref) = refs
    x = x_ref[...]
    mod = mod_ref[0]
    sh1 = mod[:, 0:D_MODEL]
    sc1 = mod[:, D_MODEL:2 * D_MODEL]
    hb = (_rms(x, g_ref[...]) * (1.0 + sc1) + sh1).astype(BF16)

    z = [_dot(hb, w_ref[:, c0:c0 + PROJ_CHUNK]) for c0 in range(0, IN_PAD, PROJ_CHUNK)]

    def proj(name, j=0, w=None):
        lo, hi = _COLS[name]
        lo = lo + j
        hi = hi if w is None else lo + w
        parts = []
        while lo < hi:
            c, o = divmod(lo, PROJ_CHUNK)
            n = min(hi - lo, PROJ_CHUNK - o)
            parts.append(z[c][:, o:o + n])
            lo += n
        return parts[0] if len(parts) == 1 else jnp.concatenate(parts, axis=1)

    scale = HEAD_DIM ** -0.5
    q_scale = scale * LOG2E

    def rotary(t):
        first = (lax.broadcasted_iota(jnp.int32, t.shape, 1) % 32) < 16
        partner = jnp.where(first, pltpu.roll(t, LANES - 16, 1), pltpu.roll(t, 16, 1))
        return t * cos_ref[...] + partner * sin_ref[...]

    for j in range(W_ATTN // LANES):
        t = _pair_rms(proj("qa", j * LANES, LANES), qn_ref[...])
        if rope:
            t = rotary(t)
        qa_ref[:, j * LANES:(j + 1) * LANES] = (t * q_scale).astype(qa_ref.dtype)
    t = _pair_rms(proj("ka"), kn_ref[...])
    if rope:
        t = rotary(t)
    ka_ref[...] = t.astype(ka_ref.dtype)

    def store_v(v, v_ref, vt_ref):
        v_ref[...] = v.astype(v_ref.dtype)
        nblk, _, blk = vt_ref.shape
        for u in range(nblk):
            vt_ref[u] = v[u * blk:(u + 1) * blk].T.astype(vt_ref.dtype)

    store_v(proj("va"), va_ref, vat_ref)
    qb_ref[...] = proj("qb").astype(qb_ref.dtype)
    kb_ref[...] = (proj("kb") * scale).astype(kb_ref.dtype)
    vb_ref[...] = proj("vb").astype(vb_ref.dtype)
    ob_ref[...] = proj("ob").astype(ob_ref.dtype)
    for j, name in enumerate(("gf", "gb")):
        gt = proj(name) + gbias_ref[:, j * LANES:(j + 1) * LANES]
        lane = lax.broadcasted_iota(jnp.int32, gt.shape, 1)
        is_f = (lane >= H_MLSTM) & (lane < 2 * H_MLSTM)
        logsig = jnp.minimum(gt, 0.0) - jnp.log1p(jnp.exp(-jnp.abs(gt)))
        gate_ref[:, j * LANES:(j + 1) * LANES] = jnp.where(is_f, logsig, gt)
    qc_ref[...] = (proj("qc") * q_scale).astype(qc_ref.dtype)
    kc_ref[...] = proj("kc").astype(kc_ref.dtype)
    store_v(proj("vc"), vc_ref, vct_ref)


def _inproj(x, mods, g_pre, w_in_p, qn, kn, gbias, rope_tabs, *, rows_per_cond, kv_dtype, vt_blocks, name):
    T = x.shape[0]
    tm = TM_PROJ
    bpc = rows_per_cond // tm
    rope = rope_tabs is not None
    row = lambda i: (i, 0)
    const = lambda i: (0, 0)
    in_specs = [pl.BlockSpec((tm, D_MODEL), row),
                pl.BlockSpec((1, 1, N_MOD * D_MODEL), lambda i: (i // bpc, 0, 0)),
                pl.BlockSpec((1, D_MODEL), const),
                pl.BlockSpec((D_MODEL, IN_PAD), const),
                pl.BlockSpec((1, LANES), const),
                pl.BlockSpec((1, LANES), const),
                pl.BlockSpec((1, 2 * LANES), const)]
    args = [x, mods, g_pre, w_in_p, qn, kn, gbias]
    if rope:
        nblk = rope_tabs[0].shape[0] // tm
        in_specs += [pl.BlockSpec((tm, LANES), lambda i: (i % nblk, 0))] * 2
        args += list(rope_tabs)
    widths = [("qa", W_ATTN, BF16), ("ka", W_KV, kv_dtype), ("va", W_KV, kv_dtype),
              ("qb", W_MLSTM, BF16), ("kb", W_MLSTM, BF16), ("vb", W_MLSTM, BF16),
              ("ob", W_MLSTM, F32), ("g", 2 * LANES, F32),
              ("qc", W_NBHD, BF16), ("kc", W_NBHD, kv_dtype), ("vc", W_NBHD, kv_dtype)]
    out_specs = [pl.BlockSpec((tm, w), row) for _, w, _ in widths]
    out_shape = [jax.ShapeDtypeStruct((T, w), dt) for _, w, dt in widths]
    for w, blk in zip((W_KV, W_NBHD), vt_blocks):
        out_specs.append(pl.BlockSpec((tm // blk, w, blk), lambda i: (i, 0, 0)))
        out_shape.append(jax.ShapeDtypeStruct((T // blk, w, blk), BF16))
    est = 2 * (tm * D_MODEL * 4 + D_MODEL * IN_PAD * 2 + tm * IN_PAD * 4) + 3 * tm * IN_PAD * 4
    outs = pl.pallas_call(
        functools.partial(_inproj_kernel, rope=rope),
        grid=(T // tm,),
        in_specs=in_specs, out_specs=out_specs, out_shape=out_shape,
        compiler_params=pltpu.CompilerParams(dimension_semantics=("arbitrary",),
                                             vmem_limit_bytes=_vmem_limit(est)),
        name=name,
    )(*args)
    return dict(zip([n for n, _, _ in widths] + ["vat", "vct"], outs))


SUM_ROWS = 16


def _softmax_step(st, col_max, vt, m_scr, acc_scr, u):
    m_old = m_scr[u]
    m_new = jnp.maximum(m_old, col_max)
    p = jnp.exp2(st - m_new).astype(BF16)
    alpha = jnp.exp2(m_old - m_new)
    vt1 = jnp.concatenate([vt, jnp.ones((SUM_ROWS, vt.shape[1]), BF16)], axis=0)
    acc_scr[u] = alpha * acc_scr[u] + _dot(vt1, p)
    m_scr[u] = m_new


def _normalised_pair(acc_scr, u0, u1):
    halves = [acc_scr[u][0:HEAD_DIM] / acc_scr[u][HEAD_DIM:HEAD_DIM + 1] for u in (u0, u1)]
    return jnp.concatenate(halves, axis=0).T
def _attn_kernel(*refs, heads, part_chunks):
    n_parts = len(part_chunks)
    assert n_parts in (1, 2)
    q_ref, kv_refs = refs[0], refs[1:1 + 2 * n_parts]
    o_ref, qs_scr, m_scr, acc_scr, s_ring, cmax_scr = refs[1 + 2 * n_parts:]
    nch = sum(part_chunks)

    def chunk_of(which, j, rows, cols):
        first = kv_refs[which][0, jnp.minimum(j, part_chunks[0] - 1), rows, cols].astype(BF16)
        if n_parts == 1:
            return first
        j2 = jnp.clip(j - part_chunks[0], 0, part_chunks[1] - 1)
        return jnp.where(j < part_chunks[0], first, kv_refs[2 + which][0, j2, rows, cols].astype(BF16))

    nh = len(heads)
    ring = ATTN_LOOKAHEAD + 1
    assert nh % ring == 0
    for h, (kg, kh) in enumerate(heads):
        qg = q_ref[0, :, (h // 2) * LANES:(h // 2 + 1) * LANES].astype(F32)
        if h % 2 != kh:
            qg = pltpu.roll(qg, HEAD_DIM, 1)
        keep = _lane_lo(qg.shape) if kh == 0 else jnp.logical_not(_lane_lo(qg.shape))
        qs_scr[h] = jnp.where(keep, qg, 0.0).astype(BF16)
    m_scr[...] = jnp.full(m_scr.shape, -jnp.inf, F32)
    acc_scr[...] = jnp.zeros(acc_scr.shape, F32)

    def scores(j, item):
        h = item % nh
        kg = heads[h][0]
        kj = chunk_of(0, j, slice(None), slice(kg * LANES, (kg + 1) * LANES))
        st = _dot_nt(kj, qs_scr[h])
        s_ring[item % ring] = st
        cmax_scr[item % ring] = jnp.max(st, axis=0, keepdims=True)

    def chunk(j, carry):
        j_next = jnp.minimum(j + 1, nch - 1)
        for h, (kg, kh) in enumerate(heads):
            ahead = h + ATTN_LOOKAHEAD
            scores(j if ahead < nh else j_next, ahead)
            r = kg * LANES + kh * HEAD_DIM
            vt = chunk_of(1, j, slice(r, r + HEAD_DIM), slice(None))
            _softmax_step(s_ring[h % ring], cmax_scr[h % ring], vt, m_scr, acc_scr, h)
        return carry

    for item in range(ATTN_LOOKAHEAD):
        scores(0, item)
    lax.fori_loop(0, nch, chunk, 0)
    for t in range(len(heads) // 2):
        o_ref[0, :, t * LANES:(t + 1) * LANES] = _normalised_pair(acc_scr, 2 * t, 2 * t + 1).astype(o_ref.dtype)


def _attention(q, parts, *, heads, name):
    B, Sq, W = q.shape
    ck, KW = parts[0][0].shape[2], parts[0][0].shape[3]
    tq = min(TQ_ATTN, Sq)
    nh = len(heads)
    part_chunks = tuple(k4.shape[1] for k4, _ in parts)
    in_specs = [pl.BlockSpec((1, tq, W), lambda b, i: (b, i, 0))]
    args = [q]
    kv_bytes = 0
    for k4, vt4 in parts:
        assert k4.shape[2:] == (ck, KW) and vt4.shape[2:] == (KW, ck)
        in_specs += [pl.BlockSpec((1,) + k4.shape[1:], lambda b, i: (b, 0, 0, 0)),
                     pl.BlockSpec((1,) + vt4.shape[1:], lambda b, i: (b, 0, 0, 0))]
        args += [k4, vt4]
        kv_bytes += k4[0].size * k4.dtype.itemsize + vt4[0].size * vt4.dtype.itemsize
    est = (2 * (2 * tq * W * 2 + kv_bytes) + nh * tq * (LANES * 2 + HEAD_DIM * 4 + 64)
           + 8 * nh * ck * tq * 4)
    return pl.pallas_call(
        functools.partial(_attn_kernel, heads=heads, part_chunks=part_chunks),
        grid=(B, Sq // tq),
        in_specs=in_specs,
        out_specs=pl.BlockSpec((1, tq, W), lambda b, i: (b, i, 0)),
        out_shape=jax.ShapeDtypeStruct((B, Sq, W), BF16),
        scratch_shapes=[pltpu.VMEM((nh, tq, LANES), BF16),
                        pltpu.VMEM((nh, 1, tq), F32),
                        pltpu.VMEM((nh, HEAD_DIM + SUM_ROWS, tq), F32),
                        pltpu.VMEM((ATTN_LOOKAHEAD + 1, ck, tq), F32),
                        pltpu.VMEM((ATTN_LOOKAHEAD + 1, 1, tq), F32)],
        compiler_params=pltpu.CompilerParams(dimension_semantics=("arbitrary", "arbitrary"),
                                             vmem_limit_bytes=_vmem_limit(est)),
        name=name,
    )(*args)


_HEADS_GQA = tuple((0, h // (H_ATTN // KV_ATTN)) for h in range(H_ATTN))
_HEADS_MHA = tuple((h // 2, h % 2) for h in range(H_NBHD))


def _nbhd_window(r, rows):
    kr = min(NA_ROWS, rows)
    return min(max(r - kr // 2, 0), rows - kr), kr


def _nbhd_patterns(rows):
    n_groups = rows // NB_GROUP
    pats = []
    for g in (0, 1, n_groups - 1):
        r0 = g * NB_GROUP
        pats.append((r0, min(max(r0 - NA_ROWS // 2, 0), rows - NB_SLAB)))
    return pats


def _nbhd_kernel(rb_ref, q_ref, k_ref, vt_ref, kc_ref, vct_ref, o_ref,
                 bias_scr, qs_scr, m_scr, acc_scr, s_ring, cmax_scr, *, rows):
    hp = pl.program_id(0)
    b = pl.program_id(1)
    gs = pl.program_id(2)
    n_groups = rows // NB_GROUP
    n_dr = 2 * NA_ROWS - 1
    n_dc = 2 * NA_COLS - 1
    tq = NB_GROUP * GRID_W
    ck = tq
    pats = _nbhd_patterns(rows)
    ring = ATTN_LOOKAHEAD + 1
    n_chunks = NB_SLAB * GRID_W // ck + 1
    assert n_chunks % ring == 0 and kc_ref.shape[1] == ck
    blk_rows = NB_KV_BLOCK // GRID_W

    @pl.when((b == 0) & (gs == 0))
    def _build_bias():
        shape = (GRID_W, LANES)
        w = lax.broadcasted_iota(jnp.int32, shape, 0)
        lane = lax.broadcasted_iota(jnp.int32, shape, 1)
        cc = lane % GRID_W
        second = lane >= GRID_W
        cs = jnp.clip(w - NA_COLS // 2, 0, GRID_W - NA_COLS)
        col_ok = (cc >= cs) & (cc < cs + NA_COLS)
        dc = cc - w + (NA_COLS - 1)
        for hh in range(2):
            base = (2 * hp + hh) * (n_dr * n_dc)
            tiles = {}
            for d in range(-1, n_dr):
                acc = jnp.zeros(shape, F32)
                for j in range(n_dc):
                    va = rb_ref[base + d * n_dc + j] * LOG2E if d >= 0 else 0.0
                    vb = rb_ref[base + (d + 1) * n_dc + j] * LOG2E if d + 1 < n_dr else 0.0
                    acc = acc + jnp.where(dc == j, jnp.where(second, vb, va), 0.0)
                tiles[d] = acc
            def query_row_tile(r, kra):
                rs, kr = _nbhd_window(r, rows)
                ok_a = rs <= kra < rs + kr
                ok_b = rs <= kra + 1 < rs + kr
                if not (ok_a or ok_b):
                    return jnp.full(shape, NEG, F32)
                row_ok = (jnp.logical_not(second) if ok_a and not ok_b else
                          second if ok_b and not ok_a else None)
                ok = col_ok if row_ok is None else (col_ok & row_ok)
                return jnp.where(ok, tiles[kra - r + (NA_ROWS - 1)], NEG)

            for pi, (r0, slab0) in enumerate(pats):
                for ip in range(NB_GROUP // 2):
                    for ap in range(NB_SLAB // 2):
                        kra = slab0 + 2 * ap
                        two_rows = jnp.concatenate([query_row_tile(r0 + 2 * ip, kra),
                                                    query_row_tile(r0 + 2 * ip + 1, kra)], axis=0)
                        bias_scr[hh, pi, ap * LANES:(ap + 1) * LANES, ip * LANES:(ip + 1) * LANES] = two_rows.T

    lo = _lane_lo((tq, LANES))
    assert (2 * n_chunks) % ring == 0
    for s in range(NB_STEP_GROUPS):
        q = q_ref[0, s * tq:(s + 1) * tq, :].astype(F32)
        for hh in range(2):
            qs_scr[2 * s + hh] = jnp.where(lo if hh == 0 else jnp.logical_not(lo), q, 0.0).astype(BF16)
    m_scr[...] = jnp.full(m_scr.shape, -jnp.inf, F32)
    acc_scr[...] = jnp.zeros(acc_scr.shape, F32)

    def group_of(s):
        g = gs * NB_STEP_GROUPS + s
        pat = jnp.where(g == 0, 0, jnp.where(g == n_groups - 1, 2, 1))
        slab0 = jnp.clip(g * NB_GROUP - NA_ROWS // 2, 0, rows - NB_SLAB)
        return pat, slab0 // blk_rows

    per_group = 2 * n_chunks
    nb = ck // NB_KV_BLOCK

    def scores(s, item):
        hh, c = (item % per_group) // n_chunks, item % n_chunks
        qm = qs_scr[2 * s + hh]
        if c < n_chunks - 1:
            pat, blk0 = group_of(s)
            kc = k_ref[0, pl.ds(blk0 + c * nb, nb)].reshape(ck, LANES)
            st = _dot_nt(kc, qm) + bias_scr[hh, pat, c * ck:(c + 1) * ck, :]
        else:
            st = _dot_nt(kc_ref[0].astype(BF16), qm)
        s_ring[item % ring] = st
        cmax_scr[item % ring] = jnp.max(st, axis=0, keepdims=True)

    def group(s, carry):
        s_next = jnp.minimum(s + 1, NB_STEP_GROUPS - 1)
        for item in range(per_group):
            ahead = item + ATTN_LOOKAHEAD
            scores(s if ahead < per_group else s_next, ahead)
            hh, c = item // n_chunks, item % n_chunks
            if c < n_chunks - 1:
                _, blk0 = group_of(s)
                vt = jnp.concatenate([vt_ref[0, blk0 + c * nb + i, hh * HEAD_DIM:(hh + 1) * HEAD_DIM, :]
                                      for i in range(nb)], axis=1)
            else:
                vt = vct_ref[0, hh * HEAD_DIM:(hh + 1) * HEAD_DIM, :].astype(BF16)
            _softmax_step(s_ring[item % ring], cmax_scr[item % ring], vt, m_scr, acc_scr, 2 * s + hh)
        o_ref[0, pl.ds(pl.multiple_of(s * tq, tq), tq), :] = (
            _normalised_pair(acc_scr, 2 * s, 2 * s + 1).astype(o_ref.dtype))
        return carry

    for item in range(ATTN_LOOKAHEAD):
        scores(0, item)
    lax.fori_loop(0, NB_STEP_GROUPS, group, 0)


def _nbhd_attention(q, k4, vt4, k_ctx, vct, rel_bias_flat, *, name):
    B, S, W = q.shape
    P = k_ctx.shape[1]
    rows = S // GRID_W
    tq = NB_GROUP * GRID_W
    nk = NB_SLAB * GRID_W
    nblk = S // NB_KV_BLOCK
    tqs = NB_STEP_GROUPS * tq
    nu = 2 * NB_STEP_GROUPS
    ring = ATTN_LOOKAHEAD + 1
    est =(2 * (2 * tqs * LANES * 2 + 2 * S * LANES * 2 + 2 * P * LANES * 4)
           + 2 * 3 * tq * nk * 4 + ring * tq * tq * 4 + 8 * tq * tq * 4 + nu * tq * 1024)
    return pl.pallas_call(
        functools.partial(_nbhd_kernel, rows=rows),
        grid=(W // LANES, B, rows // (NB_GROUP * NB_STEP_GROUPS)),
        in_specs=[pl.BlockSpec(memory_space=pltpu.SMEM),
                  pl.BlockSpec((1, tqs, LANES), lambda p, b, g: (b, g, p)),
                  pl.BlockSpec((1, nblk, NB_KV_BLOCK, LANES), lambda p, b, g: (b, 0, 0, p)),
                  pl.BlockSpec((1, nblk, LANES, NB_KV_BLOCK), lambda p, b, g: (b, 0, p, 0)),
                  pl.BlockSpec((1, P, LANES), lambda p, b, g: (b, 0, p)),
                  pl.BlockSpec((1, LANES, P), lambda p, b, g: (b, p, 0))],
        out_specs=pl.BlockSpec((1, tqs, LANES), lambda p, b, g: (b, g, p)),
        out_shape=jax.ShapeDtypeStruct((B, S, W), BF16),
        scratch_shapes=[pltpu.VMEM((2, 3, nk, tq), F32),
                        pltpu.VMEM((nu, tq, LANES), BF16),
                        pltpu.VMEM((nu, 1, tq), F32),
                        pltpu.VMEM((nu, HEAD_DIM + SUM_ROWS, tq), F32),
                        pltpu.VMEM((ring, tq, tq), F32),
                        pltpu.VMEM((ring, 1, tq), F32)],
        compiler_params=pltpu.CompilerParams(dimension_semantics=("arbitrary",) * 3,
                                             vmem_limit_bytes=_vmem_limit(est)),
        name=name,
    )(rel_bias_flat, q, k4, vt4, k_ctx, vct)


def _split3(x):
    hi = x.astype(BF16)
    r = x - hi.astype(F32)
    mid = r.astype(BF16)
    return hi, mid, (r - mid.astype(F32)).astype(BF16)


def _mlstm_kernel(q_ref, k_ref, v_ref, g_ref, ob_ref, s0_ref, m0_ref, on_ref,
                  out_ref, sf_ref, mf_ref,
                  h_scr, nat_scr, rows_scr, stat_scr, mprev_scr, un_scr, st_scr, *, nc, grp_a, grp):
    d = pl.program_id(1)
    L = L_CHUNK
    NP = H_MLSTM // 2
    row = lax.broadcasted_iota(jnp.int32, (L, L), 0)
    col = lax.broadcasted_iota(jnp.int32, (L, L), 1)
    sign = 1 - 2 * d
    mask = (col - row) * sign <= 0
    maskb = mask.astype(BF16)
    mask3 = jnp.concatenate([maskb, maskb, maskb], axis=1)
    lane = lax.broadcasted_iota(jnp.int32, (L, LANES), 1)
    lo = lane < HEAD_DIM
    top = row < HEAD_DIM
    row2 = lax.broadcasted_iota(jnp.int32, (L, 2 * LANES), 0)
    col2 = lax.broadcasted_iota(jnp.int32, (L, 2 * LANES), 1)
    keep_state = (row2 < HEAD_DIM) == ((col2 % LANES) < HEAD_DIM)
    top2 = row2 < HEAD_DIM
    ones_b = jnp.ones((L, LANES), BF16)
    ones_lo = lo.astype(BF16)
    ones_hi = jnp.logical_not(lo).astype(BF16)

    def chunk_rows(c):
        return pl.ds(pl.multiple_of(c * L, L), L)

    def pass_a(it, carry):
        cs = [it * grp_a + u for u in range(grp_a)]
        gts = [g_ref[0, chunk_rows(c), :] for c in cs]
        bns = [_dot(mask3, jnp.concatenate(_split3(gt), axis=0)) for gt in gts]
        a_all = []
        for c, gt, bn in zip(cs, gts, bns):
            nat = jnp.where(lane < H_MLSTM, gt, bn)
            nat_scr[chunk_rows(c), :] = nat * (-LOG2E)
            nat_t = nat.T
            b_rows = nat_t[H_MLSTM:2 * H_MLSTM]
            c_rows = nat_t[0:H_MLSTM] - b_rows
            rows_scr[c] = jnp.concatenate([c_rows * LOG2E, b_rows], axis=0)
            c_max = jnp.max(c_rows, axis=1, keepdims=True)
            b_tot = jnp.where(d == 0, b_rows[:, L - 1:L], b_rows[:, 0:1])
            stat_scr[c] = jnp.concatenate([jnp.broadcast_to(c_max, (H_MLSTM, LANES)),
                                           jnp.broadcast_to(b_tot, (H_MLSTM, LANES))], axis=0)
            a_all.append(jnp.exp(c_rows - c_max))
        for c, a_rows in zip(cs, a_all):
            rows = chunk_rows(c)
            for p in range(NP):
                lanes = slice(p * LANES, (p + 1) * LANES)
                k_t = k_ref[0, rows, lanes].astype(F32).T
                a_sel = jnp.where(top, a_rows[2 * p:2 * p + 1], a_rows[2 * p + 1:2 * p + 2])
                vv = jnp.concatenate([v_ref[0, rows, lanes], ones_b], axis=1)
                un = _dot((k_t * a_sel).astype(BF16), vv)
                un_scr[c, p] = jnp.where(keep_state, un, 0.0)
        return carry

    lax.fori_loop(0, nc // grp_a, pass_a, 0)

    st_scr[...] = s0_ref[0, 0]

    def pass_b(ci, m):
        c = jnp.where(d == 0, ci, nc - 1 - ci)
        st = stat_scr[c]
        c_max, b_tot = st[0:H_MLSTM], st[H_MLSTM:]
        m_new = jnp.maximum(b_tot + m, b_tot + c_max)
        d_old = jnp.exp(b_tot + m - m_new)
        d_new = jnp.exp(b_tot + c_max - m_new)
        mprev_scr[c] = jnp.concatenate([m, m], axis=0) * LOG2E
        for p in range(NP):
            def rows_of(t, p=p):
                even = jnp.concatenate([t[2 * p:2 * p + 1]] * 2, axis=1)
                odd = jnp.concatenate([t[2 * p + 1:2 * p + 2]] * 2, axis=1)
                return jnp.where(top2, even, odd)
            s_prev = st_scr[p]
            st_scr[p] = rows_of(d_old) * s_prev + rows_of(d_new) * un_scr[c, p]
            un_scr[c, p] = s_prev
        return m_new

    m_fin = lax.fori_loop(0, nc, pass_b, m0_ref[0, 0][0:H_MLSTM])
    sf_ref[0, 0] = st_scr[...]
    mf_ref[0, 0] = jnp.concatenate([m_fin, m_fin], axis=0)

    def pass_c(it, carry):
        cs = [it * grp + u for u in range(grp)]
        units = [(u, p) for u in range(grp) for p in range(NP)]
        early = {}
        for u, p in units:
            c = cs[u]
            rows = chunk_rows(c)
            lanes = slice(p * LANES, (p + 1) * LANES)
            qp = q_ref[0, rows, lanes]
            kp = k_ref[0, rows, lanes]
            s_in = un_scr[c, p].astype(BF16)
            qms = [jnp.where(lo if hh == 0 else jnp.logical_not(lo), qp, jnp.zeros_like(qp)) for hh in range(2)]
            early[u, p] = ([_dot_nt(qm, kp) for qm in qms],
                           _dot(qp, s_in))
        mid = {}
        for u, p in units:
            r_t = rows_scr[cs[u]]
            m_in = mprev_scr[cs[u]]
            for hh in range(2):
                h = 2 * p + hh
                cm = jnp.where(mask, r_t[h:h + 1, :], -jnp.inf)
                m_prev = m_in[h:h + 1, :]
                mu = jnp.maximum(jnp.broadcast_to(jnp.max(cm, axis=1, keepdims=True), (L, LANES)), m_prev)
                w = early[u, p][0][hh] * jnp.exp2(cm - mu)
                mid[u, p, hh] = (w.astype(BF16), mu, m_prev)
        for u, p in units:
            rows = chunk_rows(cs[u])
            lanes = slice(p * LANES, (p + 1) * LANES)
            vp = v_ref[0, rows, lanes]
            zero = jnp.zeros_like(vp)
            vv = jnp.concatenate([jnp.concatenate([jnp.where(lo, vp, zero), ones_lo], axis=1),
                                  jnp.concatenate([jnp.where(lo, zero, vp), ones_hi], axis=1)], axis=0)
            w2 = jnp.concatenate([mid[u, p, 0][0], mid[u, p, 1][0]], axis=1)
            nd = _dot(w2, vv)
            nat = nat_scr[rows, :]
            nb = [jnp.broadcast_to(nat[:, H_MLSTM + 2 * p + hh:H_MLSTM + 2 * p + hh + 1], (L, LANES))
                  for hh in range(2)]
            fs = early[u, p][1]
            mu = jnp.where(lo, mid[u, p, 0][1], mid[u, p, 1][1])
            m_prev = jnp.where(lo[0:1], mid[u, p, 0][2], mid[u, p, 1][2])
            w_inter = jnp.exp2(m_prev - mu)
            den = nd[:, LANES:] + w_inter * fs[:, LANES:]
            den = jnp.maximum(jnp.abs(den), jnp.exp2(jnp.where(lo, nb[0], nb[1]) - mu))
            h_scr[d, rows, lanes] = (nd[:, :LANES] + w_inter * fs[:, :LANES]) / den
        return carry

    lax.fori_loop(0, nc // grp, pass_c, 0)

    @pl.when(d == 1)
    def _finish():
        def rows_block(c, carry):
            rows = chunk_rows(c)
            for p in range(NP):
                lanes = slice(p * LANES, (p + 1) * LANES)
                hn = _pair_rms(h_scr[0, rows, lanes] + h_scr[1, rows, lanes], on_ref[:, lanes])
                out_ref[0, rows, lanes] = (_sigmoid(ob_ref[0, rows, lanes]) * hn).astype(out_ref.dtype)
            return carry

        lax.fori_loop(0, nc, rows_block, 0)


def _mlstm(q, k, v, gates, ob, s0, m0, out_norm, *, name):
    assert L_CHUNK == LANES
    B, S, W = q.shape
    nc = S // L_CHUNK
    grp = min(MLSTM_GROUP, nc)
    grp_a = min(MLSTM_GROUP_A, nc)
    npair = H_MLSTM // 2
    seq = lambda b, d: (b, 0, 0)
    est = (2 * (3 * S * W * 2 + S * LANES * 4 + S * W * 4 + S * W * 2) + 2 * S * W * 4 + S * LANES * 4
           + nc * npair * LANES * 2 * LANES * 4 + 12 * 1024 * 1024)
    return pl.pallas_call(
        functools.partial(_mlstm_kernel, nc=nc, grp_a=grp_a, grp=grp),
        grid=(B, 2),
        in_specs=[pl.BlockSpec((1, S, W), seq), pl.BlockSpec((1, S, W), seq), pl.BlockSpec((1, S, W), seq),
                  pl.BlockSpec((1, S, LANES), lambda b, d: (b, 0, d)),
                  pl.BlockSpec((1, S, W), seq),
                  pl.BlockSpec((1, 1, npair, LANES, 2 * LANES), lambda b, d: (b, d, 0, 0, 0)),
                  pl.BlockSpec((1, 1, 8, LANES), lambda b, d: (b, d, 0, 0)),
                  pl.BlockSpec((1, W), lambda b, d: (0, 0))],
        out_specs=[pl.BlockSpec((1, S, W), seq),
                   pl.BlockSpec((1, 1, npair, LANES, 2 * LANES), lambda b, d: (b, d, 0, 0, 0)),
                   pl.BlockSpec((1, 1, 8, LANES), lambda b, d: (b, d, 0, 0))],
        out_shape=[jax.ShapeDtypeStruct((B, S, W), BF16),
                   jax.ShapeDtypeStruct((B, 2, npair, LANES, 2 * LANES), F32),
                   jax.ShapeDtypeStruct((B, 2, 8, LANES), F32)],
        scratch_shapes=[pltpu.VMEM((2, S, W), F32),
                        pltpu.VMEM((S, LANES), F32),
                        pltpu.VMEM((nc, 8, L_CHUNK), F32),
                        pltpu.VMEM((nc, 8, LANES), F32),
                        pltpu.VMEM((nc, 8, LANES), F32),
                        pltpu.VMEM((nc, npair, LANES, 2 * LANES), F32),
                        pltpu.VMEM((npair, LANES, 2 * LANES), F32)],
        compiler_params=pltpu.CompilerParams(dimension_semantics=("arbitrary", "arbitrary"),
                                             vmem_limit_bytes=_vmem_limit(est)),
        name=name,
    )(q, k, v, gates, ob, s0, m0, out_norm)


def _outproj_kernel(a_ref, b_ref, c_ref, x_ref, mod_ref, w_ref, gpost_ref, gpre_ref, x1_ref, h2_ref):
    mod = mod_ref[0]
    gt1 = mod[:, 2 * D_MODEL:3 * D_MODEL]
    sh2 = mod[:, 3 * D_MODEL:4 * D_MODEL]
    sc2 = mod[:, 4 * D_MODEL:5 * D_MODEL]
    tr = x_ref.shape[0] // ROW_SPLIT
    pieces = [pl.ds(s * tr, tr) for s in range(ROW_SPLIT)]
    mos = [_dot(jnp.concatenate([a_ref[r, :], b_ref[r, :], c_ref[r, :]], axis=1), w_ref[...]) for r in pieces]
    for r, mo in zip(pieces, mos):
        x1 = x_ref[r, :] + gt1 * _rms(mo, gpost_ref[...])
        x1_ref[r, :] = x1
        h2_ref[r, :] = (_rms(x1, gpre_ref[...]) * (1.0 + sc2) + sh2).astype(h2_ref.dtype)


def _outproj(oa, ob, oc, x, mods, w_out, g_post, g_pre, *, rows_per_cond, name):
    T = x.shape[0]
    tm = TM_PROJ
    bpc = rows_per_cond // tm
    row = lambda i: (i, 0)
    const = lambda i: (0, 0)
    est = 2 * (tm * D_MODEL * (2 + 4 + 4 + 2) + D_MODEL * D_MODEL * 2) + 4 * tm * D_MODEL * 4
    return pl.pallas_call(
        _outproj_kernel,
        grid=(T // tm,),
        in_specs=[pl.BlockSpec((tm, W_ATTN), row), pl.BlockSpec((tm, W_MLSTM), row),
                  pl.BlockSpec((tm, W_NBHD), row), pl.BlockSpec((tm, D_MODEL), row),
                  pl.BlockSpec((1, 1, N_MOD * D_MODEL), lambda i: (i // bpc, 0, 0)),
                  pl.BlockSpec((D_MODEL, D_MODEL), const),
                  pl.BlockSpec((1, D_MODEL), const), pl.BlockSpec((1, D_MODEL), const)],
        out_specs=[pl.BlockSpec((tm, D_MODEL), row), pl.BlockSpec((tm, D_MODEL), row)],
        out_shape=[jax.ShapeDtypeStruct((T, D_MODEL), F32), jax.ShapeDtypeStruct((T, D_MODEL), BF16)],
        compiler_params=pltpu.CompilerParams(dimension_semantics=("arbitrary",),
                                             vmem_limit_bytes=_vmem_limit(est)),
        name=name,
    )(oa, ob, oc, x, mods, w_out, g_post, g_pre)


def _ffn_kernel(h_ref, x_ref, mod_ref, wu_ref, wd_ref, g_ref, o_ref, acc_ref):
    j = pl.program_id(1)

    @pl.when(j == 0)
    def _zero():
        acc_ref[...] = jnp.zeros_like(acc_ref)

    u = jnp.maximum(_dot(h_ref[...], wu_ref[...]), 0.0)
    acc_ref[...] += _dot((u * u).astype(BF16), wd_ref[...])

    @pl.when(j == pl.num_programs(1) - 1)
    def _finish():
        gt2 = mod_ref[0][:, 5 * D_MODEL:6 * D_MODEL]
        o_ref[...] = x_ref[...] + gt2 * _rms(acc_ref[...], g_ref[...])


def _ffn(h2, x1, mods, w_up, w_down, g_post, *, rows_per_cond, name):
    T = x1.shape[0]
    tm, tf = TM_FFN, TF_FFN
    tm = min(tm, rows_per_cond)
    bpc = rows_per_cond // tm
    est = 2 * (tm * D_MODEL * (2 + 4 + 4) + 2 * D_MODEL * tf * 2) + tm * D_MODEL * 4 + 3 * tm * tf * 4
    return pl.pallas_call(
        _ffn_kernel,
        grid=(T // tm, D_FF // tf),
        in_specs=[pl.BlockSpec((tm, D_MODEL), lambda i, j: (i, 0)),
                  pl.BlockSpec((tm, D_MODEL), lambda i, j: (i, 0)),
                  pl.BlockSpec((1, 1, N_MOD * D_MODEL), lambda i, j: (i // bpc, 0, 0)),
                  pl.BlockSpec((D_MODEL, tf), lambda i, j: (0, j)),
                  pl.BlockSpec((tf, D_MODEL), lambda i, j: (j, 0)),
                  pl.BlockSpec((1, D_MODEL), lambda i, j: (0, 0))],
        out_specs=pl.BlockSpec((tm, D_MODEL), lambda i, j: (i, 0)),
        out_shape=jax.ShapeDtypeStruct((T, D_MODEL), F32),
        scratch_shapes=[pltpu.VMEM((tm, D_MODEL), F32)],
        compiler_params=pltpu.CompilerParams(dimension_semantics=("arbitrary", "arbitrary"),
                                             vmem_limit_bytes=_vmem_limit(est)),
        name=name,
    )(h2, x1, mods, w_up, w_down, g_post)


def _pad_w_in(w_in_l):
    o = W_ATTN + 2 * W_KV + 4 * W_MLSTM
    pre, gates, post = w_in_l[:, :o], w_in_l[:, o:o + N_GATES], w_in_l[:, o + N_GATES:]
    z = jnp.zeros((D_MODEL, LANES - 2 * H_MLSTM), w_in_l.dtype)
    return jnp.concatenate([pre, gates[:, :2 * H_MLSTM], z, gates[:, 2 * H_MLSTM:], z, post],
                           axis=1).astype(BF16)


def _pad_gate_bias(gb_l):
    z = jnp.zeros((LANES - 2 * H_MLSTM,), gb_l.dtype)
    return jnp.concatenate([gb_l[:2 * H_MLSTM], z, gb_l[2 * H_MLSTM:], z]).reshape(1, 2 * LANES)


def _rope_tables(S):
    quarter = HEAD_DIM // 4
    pos = jnp.arange(S)
    inv_freq = ROPE_THETA ** (-jnp.arange(quarter, dtype=F32) / quarter)

    def tabs(p):
        ang = p.astype(F32)[:, None] * inv_freq[None, :]
        return jnp.cos(ang), jnp.sin(ang)

    cr, sr = tabs(pos // GRID_W)
    cc, sc = tabs(pos % GRID_W)
    cos = jnp.concatenate([cr, cr, cc, cc], axis=1)
    sin = jnp.concatenate([-sr, sr, -sc, sc], axis=1)
    return jnp.tile(cos, (1, 2)), jnp.tile(sin, (1, 2))


def _pack_state(C, n, m):
    B = C.shape[0]
    Cp = C.reshape(B, 2, 2, 2, HEAD_DIM, HEAD_DIM)
    z = jnp.zeros_like(Cp[:, :, :, 0])
    top = jnp.concatenate([Cp[:, :, :, 0], z], axis=-1)
    bot = jnp.concatenate([z, Cp[:, :, :, 1]], axis=-1)
    Cbd = jnp.concatenate([top, bot], axis=-2)
    n_rep = jnp.broadcast_to(n.reshape(B, 2, 2, LANES, 1), (B, 2, 2, LANES, LANES))
    same_head = (jnp.arange(LANES)[:, None] < HEAD_DIM) == (jnp.arange(LANES)[None, :] < HEAD_DIM)
    n_rep = jnp.where(same_head, n_rep, 0.0)
    m_rows = jnp.broadcast_to(m[..., None], m.shape + (LANES,))
    return jnp.concatenate([Cbd, n_rep], axis=-1), jnp.concatenate([m_rows, m_rows], axis=-2)


def _unpack_state(s_p, m_p):
    B = s_p.shape[0]
    c_even = s_p[:, :, :, :HEAD_DIM, :HEAD_DIM]
    c_odd = s_p[:, :, :, HEAD_DIM:, HEAD_DIM:LANES]
    C = jnp.stack([c_even, c_odd], axis=3).reshape(B, 2, H_MLSTM, HEAD_DIM, HEAD_DIM)
    n = jnp.concatenate([s_p[..., :HEAD_DIM, LANES], s_p[..., HEAD_DIM:, LANES + HEAD_DIM]], axis=-1)
    return C, n.reshape(B, 2, H_MLSTM, HEAD_DIM), m_p[:, :, :H_MLSTM, 0]


def _layer(x, mods, lw, *, B, S, cond_rows, rope_tabs, ctx_cache, state, name):
    T = B * S
    kv_dtype = BF16 if ctx_cache is not None else F32
    ck_a = min(CK_ATTN, S)
    pr = _inproj(x, mods, lw["g_pre_mix"], lw["w_in"], lw["q_norm"], lw["k_norm"], lw["gate_bias"],
                 rope_tabs, rows_per_cond=cond_rows, kv_dtype=kv_dtype,
                 vt_blocks=(ck_a, NB_KV_BLOCK), name=name + "_inproj")
    seq = lambda a: a.reshape(B, S, a.shape[-1])
    qa, ka, va = seq(pr["qa"]), seq(pr["ka"]), seq(pr["va"])
    qc, kc, vc = seq(pr["qc"]), seq(pr["kc"]), seq(pr["vc"])
    chunks = lambda a, n: a.reshape(B, S // n, n, a.shape[-1])
    vat = pr["vat"].reshape(B, S // ck_a, W_KV, ck_a)
    vct = pr["vct"].reshape(B, S // NB_KV_BLOCK, W_NBHD, NB_KV_BLOCK)
    if ctx_cache is None:
        out_a = _attention(qa, [(chunks(ka, ck_a), vat)], heads=_HEADS_GQA, name=name + "_attn_a")
        out_c = _attention(qc, [(chunks(kc, NB_KV_BLOCK), vct)], heads=_HEADS_MHA, name=name + "_attn_c")
    else:
        ck_c, cv_c = ctx_cache[2], ctx_cache[3]
        P = ck_c.shape[1]
        ctx_k = ctx_cache[0].reshape(B, P // ck_a, ck_a, W_KV)
        ctx_vt = ctx_cache[1].reshape(B, P // ck_a, ck_a, W_KV).transpose(0, 1, 3, 2)
        out_a = _attention(qa, [(chunks(ka, ck_a), vat), (ctx_k, ctx_vt)], heads=_HEADS_GQA,
                           name=name + "_attn_a")
        out_c = _nbhd_attention(qc, chunks(kc, NB_KV_BLOCK), vct, ck_c, cv_c.transpose(0, 2, 1),
                                lw["rel_bias"], name=name + "_attn_c")
    s0, m0 = state
    out_b, sf, mf = _mlstm(seq(pr["qb"]), seq(pr["kb"]), seq(pr["vb"]), seq(pr["g"]), seq(pr["ob"]),
                           s0, m0, lw["out_norm"], name=name + "_mlstm")
    x1, h2 = _outproj(out_a.reshape(T, W_ATTN), out_b.reshape(T, W_MLSTM), out_c.reshape(T, W_NBHD),
                      x, mods, lw["w_out"], lw["g_post_mix"], lw["g_pre_ffn"],
                      rows_per_cond=cond_rows, name=name + "_outproj")
    x2 = _ffn(h2, x1, mods, lw["w_up"], lw["w_down"], lw["g_post_ffn"],
              rows_per_cond=cond_rows, name=name + "_ffn")
    return x2, (ka, va, kc, vc, sf, mf)


def kernel(x_prompt, x_sample, c, cache_k_attn, cache_v_attn, cache_k_nbhd, cache_v_nbhd, state_mlstm_C, state_mlstm_n, state_mlstm_m, c_ctx, w_ada, b_ada, g_pre_mix, g_post_mix, g_pre_ffn, g_post_ffn, w_in, q_norm_attn, k_norm_attn, mlstm_gate_bias, mlstm_out_norm, nbhd_rel_bias, w_out, w_ffn_up, w_ffn_down):
    Bc, Sc, _ = x_prompt.shape
    Bl, Sl, _ = x_sample.shape
    P = cache_k_attn.shape[2]
    n_cond = 8
    cond = jnp.concatenate([c_ctx[None, :], c, jnp.zeros((n_cond - 1 - Bl, D_MODEL), F32)], axis=0)
    mods_all = _modulation(cond, w_ada, b_ada)

    layers = []
    for l in range(DEPTH):
        layers.append(dict(
            w_in=_pad_w_in(w_in[l]),
            w_out=w_out[l].astype(BF16),
            w_up=w_ffn_up[l].astype(BF16),
            w_down=w_ffn_down[l].astype(BF16),
            g_pre_mix=g_pre_mix[l].reshape(1, D_MODEL), g_post_mix=g_post_mix[l].reshape(1, D_MODEL),
            g_pre_ffn=g_pre_ffn[l].reshape(1, D_MODEL), g_post_ffn=g_post_ffn[l].reshape(1, D_MODEL),
            q_norm=jnp.tile(q_norm_attn[l], 2).reshape(1, LANES),
            k_norm=jnp.tile(k_norm_attn[l], 2).reshape(1, LANES),
            gate_bias=_pad_gate_bias(mlstm_gate_bias[l]),
            out_norm=mlstm_out_norm[l].reshape(1, W_MLSTM),
            rel_bias=nbhd_rel_bias[l].reshape(-1),
        ))

    xp = x_prompt.reshape(Bc * Sc, D_MODEL)
    zero_state = (jnp.zeros((Bc, 2, H_MLSTM // 2, LANES, 2 * LANES), F32), jnp.zeros((Bc, 2, 8, LANES), F32))
    ctx = []
    for l in range(DEPTH):
        mods = mods_all[l, 0:1].reshape(1, 1, N_MOD * D_MODEL)
        xp, extras = _layer(xp, mods, layers[l], B=Bc, S=Sc, cond_rows=Bc * Sc, rope_tabs=None,
                            ctx_cache=None, state=zero_state, name=f"ctx{l}")
        ctx.append(extras)
    new_k_attn = jnp.stack([e[0].reshape(Bc, Sc, KV_ATTN, HEAD_DIM) for e in ctx], axis=1)
    new_v_attn = jnp.stack([e[1].reshape(Bc, Sc, KV_ATTN, HEAD_DIM) for e in ctx], axis=1)
    new_k_nbhd = jnp.stack([e[2].reshape(Bc, Sc, H_NBHD, HEAD_DIM) for e in ctx], axis=1)
    new_v_nbhd = jnp.stack([e[3].reshape(Bc, Sc, H_NBHD, HEAD_DIM) for e in ctx], axis=1)
    states = [_unpack_state(e[4], e[5]) for e in ctx]
    new_C = jnp.stack([s[0] for s in states], axis=1)
    new_n = jnp.stack([s[1] for s in states], axis=1)
    new_m = jnp.stack([s[2] for s in states], axis=1)

    xs = x_sample.reshape(Bl * Sl, D_MODEL)
    rope_tabs = _rope_tables(Sl)
    for l in range(DEPTH):
        mods = mods_all[l, 1:1 + Bl].reshape(Bl, 1, N_MOD * D_MODEL)
        cache = (cache_k_attn[:, l].reshape(Bl, P, W_KV), cache_v_attn[:, l].reshape(Bl, P, W_KV),
                 cache_k_nbhd[:, l].reshape(Bl, P, W_NBHD), cache_v_nbhd[:, l].reshape(Bl, P, W_NBHD))
        state = _pack_state(state_mlstm_C[:, l], state_mlstm_n[:, l], state_mlstm_m[:, l])
        xs, _ = _layer(xs, mods, layers[l], B=Bl, S=Sl, cond_rows=Sl, rope_tabs=rope_tabs,
                       ctx_cache=cache, state=state, name=f"lat{l}")

    return (xp.reshape(Bc, Sc, D_MODEL), xs.reshape(Bl, Sl, D_MODEL),
            new_k_attn, new_v_attn, new_k_nbhd, new_v_nbhd, new_C, new_n, new_m)
```

```python
import functools

import jax
import jax.numpy as jnp
import numpy as np
from jax import lax
from jax.experimental import pallas as pl
from jax.experimental.pallas import tpu as pltpu

F32 = jnp.float32
BF16 = jnp.bfloat16

D_MODEL = 1024
DEPTH = 2
GRID_W = 64
HEAD_DIM = 64
H_ATTN = 6
KV_ATTN = 2
H_MLSTM = 4
H_NBHD = 6
D_FF = 4 * D_MODEL
NA_ROWS = 8
NA_COLS = 16
ROPE_THETA = 10000.0
EPS = 1e-6
N_MOD = 6
W_ATTN = H_ATTN * HEAD_DIM
W_KV = KV_ATTN * HEAD_DIM
W_MLSTM = H_MLSTM * HEAD_DIM
W_NBHD = H_NBHD * HEAD_DIM
N_GATES = 4 * H_MLSTM

LANES = 128
V7X_VMEM_BYTES = 64 * 1024 * 1024
VMEM_CAP_BYTES = 56 * 1024 * 1024

TM_PROJ = 512
PROJ_CHUNK = 512
TM_FFN = 1024
TF_FFN = 1024
ROW_SPLIT = 4
TQ_ATTN = 512
CK_ATTN = 512
ATTN_LOOKAHEAD = 2
SEQS_PER_STEP = 8
L_CHUNK = 128
MLSTM_GROUP = 2
MLSTM_GROUP_A = 4
MLSTM_STEP_CHUNKS = 16
NB_GROUP = 8
NB_SLAB = 16
NB_STEP_GROUPS = 4
NB_KV_BLOCK = 256
NEG = -1e30
LOG2E = 1.4426950408889634

_COLS = {}
_off = 0
for _name, _w in (("qa", W_ATTN), ("ka", W_KV), ("va", W_KV), ("qb", W_MLSTM), ("kb", W_MLSTM),
                  ("vb", W_MLSTM), ("ob", W_MLSTM), ("gf", LANES), ("gb", LANES),
                  ("qc", W_NBHD), ("kc", W_NBHD), ("vc", W_NBHD)):
    _COLS[_name] = (_off, _off + _w)
    _off += _w
IN_PAD = _off


def _vmem_limit(nbytes):
    return int(min(max(nbytes, 16 * 1024 * 1024), VMEM_CAP_BYTES))


def _dot(a, b):
    return jnp.dot(a, b, preferred_element_type=F32)


def _dot_nt(a, b):
    return lax.dot_general(a, b, (((1,), (1,)), ((), ())), preferred_element_type=F32)


def _lane_lo(shape):
    return (lax.broadcasted_iota(jnp.int32, shape, len(shape) - 1) % LANES) < HEAD_DIM


def _rms(x, g):
    ms = jnp.mean(x * x, axis=-1, keepdims=True)
    return (x * lax.rsqrt(ms + EPS)) * g


def _pair_rms(x, g):
    lo = _lane_lo(x.shape)
    x2 = x * x
    s_lo = jnp.sum(jnp.where(lo, x2, 0.0), axis=-1, keepdims=True)
    s_hi = jnp.sum(jnp.where(lo, 0.0, x2), axis=-1, keepdims=True)
    r = jnp.where(lo, lax.rsqrt(s_lo / HEAD_DIM + EPS), lax.rsqrt(s_hi / HEAD_DIM + EPS))
    return (x * r) * g


def _sigmoid(x):
    return 1.0 / (1.0 + jnp.exp(-x))


def _mods_kernel(c_ref, w_ref, b_ref, o_ref):
    c = c_ref[...]
    s = (c * _sigmoid(c)).astype(BF16)
    o_ref[0] = _dot(s, w_ref[0].astype(BF16)) + b_ref[0]


def _modulation(cond, w_ada, b_ada):
    n = cond.shape[0]
    tn = D_MODEL
    return pl.pallas_call(
        _mods_kernel,
        grid=(DEPTH, N_MOD * D_MODEL // tn),
        in_specs=[pl.BlockSpec((n, D_MODEL), lambda l, j: (0, 0)),
                  pl.BlockSpec((1, D_MODEL, tn), lambda l, j: (l, 0, j)),
                  pl.BlockSpec((1, 1, tn), lambda l, j: (l, 0, j))],
        out_specs=pl.BlockSpec((1, n, tn), lambda l, j: (l, 0, j)),
        out_shape=jax.ShapeDtypeStruct((DEPTH, n, N_MOD * D_MODEL), F32),
        compiler_params=pltpu.CompilerParams(
            dimension_semantics=("arbitrary", "arbitrary"),
            vmem_limit_bytes=_vmem_limit(4 * D_MODEL * tn * 4)),
        name="modulation",
    )(cond, w_ada, b_ada.reshape(DEPTH, 1, N_MOD * D_MODEL))


def _inproj_kernel(*refs, rope):
    if rope:
        (x_ref, mod_ref, g_ref, w_ref, qn_ref, kn_ref, gbias_ref, cos_ref, sin_ref,
         qa_ref, ka_ref, va_ref, qb_ref, kb_ref, vb_ref, ob_ref, gate_ref,
         qc_ref, kc_ref, vc_ref, vat_ref, vct_ref) = refs
    else:
        (x_ref, mod_ref, g_ref, w_ref, qn_ref, kn_ref, gbias_ref,
         qa_ref, ka_ref, va_ref, qb_ref, kb_ref, vb_ref, ob_ref, gate_ref,
         qc_ref, kc_ref, vc_ref, vat_ref, vct_ref) = refs
    x = x_ref[...]
    mod = mod_ref[0]
    sh1 = mod[:, 0:D_MODEL]
    sc1 = mod[:, D_MODEL:2 * D_MODEL]
    hb = (_rms(x, g_ref[...]) * (1.0 + sc1) + sh1).astype(BF16)

    z = [_dot(hb, w_ref[:, c0:c0 + PROJ_CHUNK]) for c0 in range(0, IN_PAD, PROJ_CHUNK)]

    def proj(name, j=0, w=None):
        lo, hi = _COLS[name]
        lo = lo + j
        hi = hi if w is None else lo + w
        parts = []
        while lo < hi:
            c, o = divmod(lo, PROJ_CHUNK)
            n = min(hi - lo, PROJ_CHUNK - o)
            parts.append(z[c][:, o:o + n])
            lo += n
        return parts[0] if len(parts) == 1 else jnp.concatenate(parts, axis=1)

    scale = HEAD_DIM ** -0.5
    q_scale = scale * LOG2E

    def rotary(t):
        first = (lax.broadcasted_iota(jnp.int32, t.shape, 1) % 32) < 16
        partner = jnp.where(first, pltpu.roll(t, LANES - 16, 1), pltpu.roll(t, 16, 1))
        return t * cos_ref[...] + partner * sin_ref[...]

    for j in range(W_ATTN // LANES):
        t = _pair_rms(proj("qa", j * LANES, LANES), qn_ref[...])
        if rope:
            t = rotary(t)
        qa_ref[:, j * LANES:(j + 1) * LANES] = (t * q_scale).astype(qa_ref.dtype)
    t = _pair_rms(proj("ka"), kn_ref[...])
    if rope:
        t = rotary(t)
    ka_ref[...] = t.astype(ka_ref.dtype)

    def store_v(v, v_ref, vt_ref):
        v_ref[...] = v.astype(v_ref.dtype)
        nblk, _, blk = vt_ref.shape
        for u in range(nblk):
            vt_ref[u] = v[u * blk:(u + 1) * blk].T.astype(vt_ref.dtype)

    store_v(proj("va"), va_ref, vat_ref)
    qb_ref[...] = proj("qb").astype(qb_ref.dtype)
    kb_ref[...] = (proj("kb") * scale).astype(kb_ref.dtype)
    vb_ref[...] = proj("vb").astype(vb_ref.dtype)
    ob_ref[...] = proj("ob").astype(ob_ref.dtype)
    for j, name in enumerate(("gf", "gb")):
        gt = proj(name) + gbias_ref[:, j * LANES:(j + 1) * LANES]
        lane = lax.broadcasted_iota(jnp.int32, gt.shape, 1)
        is_f = (lane >= H_MLSTM) & (lane < 2 * H_MLSTM)
        logsig = jnp.minimum(gt, 0.0) - jnp.log1p(jnp.exp(-jnp.abs(gt)))
        gate_ref[:, j * LANES:(j + 1) * LANES] = jnp.where(is_f, logsig, gt)
    qc_ref[...] = (proj("qc") * q_scale).astype(qc_ref.dtype)
    kc_ref[...] = proj("kc").astype(kc_ref.dtype)
    store_v(proj("vc"), vc_ref, vct_ref)


def _inproj(x, mods, g_pre, w_in_p, qn, kn, gbias, rope_tabs, *, rows_per_cond, kv_dtype, vt_blocks, name):
    T = x.shape[0]
    tm = TM_PROJ
    bpc = rows_per_cond // tm
    rope = rope_tabs is not None
    row = lambda i: (i, 0)
    const = lambda i: (0, 0)
    in_specs = [pl.BlockSpec((tm, D_MODEL), row),
                pl.BlockSpec((1, 1, N_MOD * D_MODEL), lambda i: (i // bpc, 0, 0)),
                pl.BlockSpec((1, D_MODEL), const),
                pl.BlockSpec((D_MODEL, IN_PAD), const),
                pl.BlockSpec((1, LANES), const),
                pl.BlockSpec((1, LANES), const),
                pl.BlockSpec((1, 2 * LANES), const)]
    args = [x, mods, g_pre, w_in_p, qn, kn, gbias]
    if rope:
        nblk = rope_tabs[0].shape[0] // tm
        in_specs += [pl.BlockSpec((tm, LANES), lambda i: (i % nblk, 0))] * 2
        args += list(rope_tabs)
    widths = [("qa", W_ATTN, BF16), ("ka", W_KV, kv_dtype), ("va", W_KV, kv_dtype),
              ("qb", W_MLSTM, BF16), ("kb", W_MLSTM, BF16), ("vb", W_MLSTM, BF16),
              ("ob", W_MLSTM, F32), ("g", 2 * LANES, F32),
              ("qc", W_NBHD, BF16), ("kc", W_NBHD, kv_dtype), ("vc", W_NBHD, kv_dtype)]
    out_specs = [pl.BlockSpec((tm, w), row) for _, w, _ in widths]
    out_shape = [jax.ShapeDtypeStruct((T, w), dt) for _, w, dt in widths]
    for w, blk in zip((W_KV, W_NBHD), vt_blocks):
        out_specs.append(pl.BlockSpec((tm // blk, w, blk), lambda i: (i, 0, 0)))
        out_shape.append(jax.ShapeDtypeStruct((T // blk, w, blk), BF16))
    est = 2 * (tm * D_MODEL * 4 + D_MODEL * IN_PAD * 2 + tm * IN_PAD * 4) + 3 * tm * IN_PAD * 4
    outs = pl.pallas_call(
        functools.partial(_inproj_kernel, rope=rope),
        grid=(T // tm,),
        in_specs=in_specs, out_specs=out_specs, out_shape=out_shape,
        compiler_params=pltpu.CompilerParams(dimension_semantics=("arbitrary",),
                                             vmem_limit_bytes=_vmem_limit(est)),
        name=name,
    )(*args)
    return dict(zip([n for n, _, _ in widths] + ["vat", "vct"], outs))


SUM_ROWS = 16


def _softmax_step(st, col_max, vt, m_scr, acc_scr, u):
    m_old = m_scr[u]
    m_new = jnp.maximum(m_old, col_max)
    p = jnp.exp2(st - m_new).astype(BF16)
    alpha = jnp.exp2(m_old - m_new)
    vt1 = jnp.concatenate([vt, jnp.ones((SUM_ROWS, vt.shape[1]), BF16)], axis=0)
    acc_scr[u] = alpha * acc_scr[u] + _dot(vt1, p)
    m_scr[u] = m_new


def _normalised_pair(acc_scr, u0, u1):
    halves = [acc_scr[u][0:HEAD_DIM] / acc_scr[u][HEAD_DIM:HEAD_DIM + 1] for u in (u0, u1)]
    return jnp.concatenate(halves, axis=0).T
def _attn_kernel(*refs, heads, part_chunks):
    n_parts = len(part_chunks)
    assert n_parts in (1, 2)
    q_ref, kv_refs = refs[0], refs[1:1 + 2 * n_parts]
    o_ref, qs_scr, m_scr, acc_scr, s_ring, cmax_scr = refs[1 + 2 * n_parts:]
    nch = sum(part_chunks)

    def chunk_of(which, j, rows, cols):
        first = kv_refs[which][0, jnp.minimum(j, part_chunks[0] - 1), rows, cols].astype(BF16)
        if n_parts == 1:
            return first
        j2 = jnp.clip(j - part_chunks[0], 0, part_chunks[1] - 1)
        return jnp.where(j < part_chunks[0], first, kv_refs[2 + which][0, j2, rows, cols].astype(BF16))

    nh = len(heads)
    ring = ATTN_LOOKAHEAD + 1
    assert nh % ring == 0
    for h, (kg, kh) in enumerate(heads):
        qs_scr[h] = _masked_queries(q_ref[0, :, (h // 2) * LANES:(h // 2 + 1) * LANES], h, kh)
    m_scr[...] = jnp.full(m_scr.shape, -jnp.inf, F32)
    acc_scr[...] = jnp.zeros(acc_scr.shape, F32)

    def scores(j, item):
        h = item % nh
        kg = heads[h][0]
        kj = chunk_of(0, j, slice(None), slice(kg * LANES, (kg + 1) * LANES))
        st = _dot_nt(kj, qs_scr[h])
        s_ring[item % ring] = st
        cmax_scr[item % ring] = jnp.max(st, axis=0, keepdims=True)

    def chunk(j, carry):
        j_next = jnp.minimum(j + 1, nch - 1)
        for h, (kg, kh) in enumerate(heads):
            ahead = h + ATTN_LOOKAHEAD
            scores(j if ahead < nh else j_next, ahead)
            r = kg * LANES + kh * HEAD_DIM
            vt = chunk_of(1, j, slice(r, r + HEAD_DIM), slice(None))
            _softmax_step(s_ring[h % ring], cmax_scr[h % ring], vt, m_scr, acc_scr, h)
        return carry

    for item in range(ATTN_LOOKAHEAD):
        scores(0, item)
    lax.fori_loop(0, nch, chunk, 0)
    for t in range(len(heads) // 2):
        o_ref[0, :, t * LANES:(t + 1) * LANES] = _normalised_pair(acc_scr, 2 * t, 2 * t + 1).astype(o_ref.dtype)


def _masked_queries(q_tile, h, kh):
    qg = q_tile.astype(F32)
    if h % 2 != kh:
        qg = pltpu.roll(qg, HEAD_DIM, 1)
    keep = _lane_lo(qg.shape) if kh == 0 else jnp.logical_not(_lane_lo(qg.shape))
    return jnp.where(keep, qg, 0.0).astype(BF16)


def _attn_seqs_kernel(q_ref, k_ref, vt_ref, o_ref, qs_scr, s_ring, cmax_scr, *, heads):
    nh = len(heads)
    n_seq = q_ref.shape[0]
    ring = nh
    look = nh - 1

    def prepare(s, carry):
        for h, (kg, kh) in enumerate(heads):
            qs_scr[s * nh + h] = _masked_queries(q_ref[s, :, (h // 2) * LANES:(h // 2 + 1) * LANES], h, kh)
        return carry

    lax.fori_loop(0, n_seq, prepare, 0)

    def scores(s, item):
        h = item % nh
        kg = heads[h][0]
        st = _dot_nt(k_ref[s, 0, :, kg * LANES:(kg + 1) * LANES].astype(BF16), qs_scr[s * nh + h])
        s_ring[item % ring] = st
        cmax_scr[item % ring] = jnp.max(st, axis=0, keepdims=True)

    def sequence(s, carry):
        s_next = jnp.minimum(s + 1, n_seq - 1)
        accs = []
        for h, (kg, kh) in enumerate(heads):
            ahead = h + look
            scores(s if ahead < nh else s_next, ahead)
            r = kg * LANES + kh * HEAD_DIM
            vt = vt_ref[s, 0, r:r + HEAD_DIM, :].astype(BF16)
            vt1 = jnp.concatenate([vt, jnp.ones((SUM_ROWS, vt.shape[1]), BF16)], axis=0)
            p = jnp.exp2(s_ring[h % ring] - cmax_scr[h % ring]).astype(BF16)
            accs.append(_dot(vt1, p))
        for t in range(nh // 2):
            halves = [a[0:HEAD_DIM] / a[HEAD_DIM:HEAD_DIM + 1] for a in accs[2 * t:2 * t + 2]]
            o_ref[s, :, t * LANES:(t + 1) * LANES] = jnp.concatenate(halves, axis=0).T.astype(o_ref.dtype)
        return carry

    for item in range(look):
        scores(0, item)
    lax.fori_loop(0, n_seq, sequence, 0)


def _attention_seqs(q, k4, vt4, *, heads, name):
    B, Sq, W = q.shape
    ck, KW = k4.shape[2], k4.shape[3]
    nh = len(heads)
    bb = min(SEQS_PER_STEP, B)
    ring = nh
    est = (2 * bb * (2 * Sq * W * 2 + ck * KW * (k4.dtype.itemsize + vt4.dtype.itemsize))
           + bb * nh * Sq * LANES * 2 + ring * ck * Sq * 4 + 8 * nh * ck * Sq * 4)
    return pl.pallas_call(
        functools.partial(_attn_seqs_kernel, heads=heads),
        grid=(B // bb,),
        in_specs=[pl.BlockSpec((bb, Sq, W), lambda i: (i, 0, 0)),
                  pl.BlockSpec((bb, 1, ck, KW), lambda i: (i, 0, 0, 0)),
                  pl.BlockSpec((bb, 1, KW, ck), lambda i: (i, 0, 0, 0))],
        out_specs=pl.BlockSpec((bb, Sq, W), lambda i: (i, 0, 0)),
        out_shape=jax.ShapeDtypeStruct((B, Sq, W), BF16),
        scratch_shapes=[pltpu.VMEM((bb * nh, Sq, LANES), BF16),
                        pltpu.VMEM((ring, ck, Sq), F32),
                        pltpu.VMEM((ring, 1, Sq), F32)],
        compiler_params=pltpu.CompilerParams(dimension_semantics=("arbitrary",),
                                             vmem_limit_bytes=_vmem_limit(est)),
        name=name,
    )(q, k4, vt4)


def _attention(q, parts, *, heads, name):
    B, Sq, W = q.shape
    ck, KW = parts[0][0].shape[2], parts[0][0].shape[3]
    tq = min(TQ_ATTN, Sq)
    nh = len(heads)
    part_chunks = tuple(k4.shape[1] for k4, _ in parts)
    if part_chunks == (1,) and tq == Sq:
        return _attention_seqs(q, parts[0][0], parts[0][1], heads=heads, name=name)
    in_specs = [pl.BlockSpec((1, tq, W), lambda b, i: (b, i, 0))]
    args = [q]
    kv_bytes = 0
    for k4, vt4 in parts:
        assert k4.shape[2:] == (ck, KW) and vt4.shape[2:] == (KW, ck)
        in_specs += [pl.BlockSpec((1,) + k4.shape[1:], lambda b, i: (b, 0, 0, 0)),
                     pl.BlockSpec((1,) + vt4.shape[1:], lambda b, i: (b, 0, 0, 0))]
        args += [k4, vt4]
        kv_bytes += k4[0].size * k4.dtype.itemsize + vt4[0].size * vt4.dtype.itemsize
    est = (2 * (2 * tq * W * 2 + kv_bytes) + nh * tq * (LANES * 2 + HEAD_DIM * 4 + 64)
           + 8 * nh * ck * tq * 4)
    return pl.pallas_call(
        functools.partial(_attn_kernel, heads=heads, part_chunks=part_chunks),
        grid=(B, Sq // tq),
        in_specs=in_specs,
        out_specs=pl.BlockSpec((1, tq, W), lambda b, i: (b, i, 0)),
        out_shape=jax.ShapeDtypeStruct((B, Sq, W), BF16),
        scratch_shapes=[pltpu.VMEM((nh, tq, LANES), BF16),
                        pltpu.VMEM((nh, 1, tq), F32),
                        pltpu.VMEM((nh, HEAD_DIM + SUM_ROWS, tq), F32),
                        pltpu.VMEM((ATTN_LOOKAHEAD + 1, ck, tq), F32),
                        pltpu.VMEM((ATTN_LOOKAHEAD + 1, 1, tq), F32)],
        compiler_params=pltpu.CompilerParams(dimension_semantics=("arbitrary", "arbitrary"),
                                             vmem_limit_bytes=_vmem_limit(est)),
        name=name,
    )(*args)


_HEADS_GQA = tuple((0, h // (H_ATTN // KV_ATTN)) for h in range(H_ATTN))
_HEADS_MHA = tuple((h // 2, h % 2) for h in range(H_NBHD))


def _nbhd_window(r, rows):
    kr = min(NA_ROWS, rows)
    return min(max(r - kr // 2, 0), rows - kr), kr


def _nbhd_patterns(rows):
    n_groups = rows // NB_GROUP
    pats = []
    for g in (0, 1, n_groups - 1):
        r0 = g * NB_GROUP
        pats.append((r0, min(max(r0 - NA_ROWS // 2, 0), rows - NB_SLAB)))
    return pats


def _nbhd_kernel(rb_ref, q_ref, k_ref, vt_ref, kc_ref, vct_ref, o_ref,
                 bias_scr, qs_scr, m_scr, acc_scr, s_ring, cmax_scr, *, rows):
    hp = pl.program_id(0)
    b = pl.program_id(1)
    gs = pl.program_id(2)
    n_groups = rows // NB_GROUP
    n_dr = 2 * NA_ROWS - 1
    n_dc = 2 * NA_COLS - 1
    tq = NB_GROUP * GRID_W
    ck = tq
    pats = _nbhd_patterns(rows)
    ring = ATTN_LOOKAHEAD + 1
    n_chunks = NB_SLAB * GRID_W // ck + 1
    assert n_chunks % ring == 0 and kc_ref.shape[1] == ck
    blk_rows = NB_KV_BLOCK // GRID_W

    @pl.when((b == 0) & (gs == 0))
    def _build_bias():
        shape = (GRID_W, LANES)
        w = lax.broadcasted_iota(jnp.int32, shape, 0)
        lane = lax.broadcasted_iota(jnp.int32, shape, 1)
        cc = lane % GRID_W
        second = lane >= GRID_W
        cs = jnp.clip(w - NA_COLS // 2, 0, GRID_W - NA_COLS)
        col_ok = (cc >= cs) & (cc < cs + NA_COLS)
        dc = cc - w + (NA_COLS - 1)
        for hh in range(2):
            base = (2 * hp + hh) * (n_dr * n_dc)
            tiles = {}
            for d in range(-1, n_dr):
                acc = jnp.zeros(shape, F32)
                for j in range(n_dc):
                    va = rb_ref[base + d * n_dc + j] * LOG2E if d >= 0 else 0.0
                    vb = rb_ref[base + (d + 1) * n_dc + j] * LOG2E if d + 1 < n_dr else 0.0
                    acc = acc + jnp.where(dc == j, jnp.where(second, vb, va), 0.0)
                tiles[d] = acc
            def query_row_tile(r, kra):
                rs, kr = _nbhd_window(r, rows)
                ok_a = rs <= kra < rs + kr
                ok_b = rs <= kra + 1 < rs + kr
                if not (ok_a or ok_b):
                    return jnp.full(shape, NEG, F32)
                row_ok = (jnp.logical_not(second) if ok_a and not ok_b else
                          second if ok_b and not ok_a else None)
                ok = col_ok if row_ok is None else (col_ok & row_ok)
                return jnp.where(ok, tiles[kra - r + (NA_ROWS - 1)], NEG)

            for pi, (r0, slab0) in enumerate(pats):
                for ip in range(NB_GROUP // 2):
                    for ap in range(NB_SLAB // 2):
                        kra = slab0 + 2 * ap
                        two_rows = jnp.concatenate([query_row_tile(r0 + 2 * ip, kra),
                                                    query_row_tile(r0 + 2 * ip + 1, kra)], axis=0)
                        bias_scr[hh, pi, ap * LANES:(ap + 1) * LANES, ip * LANES:(ip + 1) * LANES] = two_rows.T

    lo = _lane_lo((tq, LANES))
    assert (2 * n_chunks) % ring == 0
    for s in range(NB_STEP_GROUPS):
        q = q_ref[0, s * tq:(s + 1) * tq, :].astype(F32)
        for hh in range(2):
            qs_scr[2 * s + hh] = jnp.where(lo if hh == 0 else jnp.logical_not(lo), q, 0.0).astype(BF16)
    m_scr[...] = jnp.full(m_scr.shape, -jnp.inf, F32)
    acc_scr[...] = jnp.zeros(acc_scr.shape, F32)

    def group_of(s):
        g = gs * NB_STEP_GROUPS + s
        pat = jnp.where(g == 0, 0, jnp.where(g == n_groups - 1, 2, 1))
        slab0 = jnp.clip(g * NB_GROUP - NA_ROWS // 2, 0, rows - NB_SLAB)
        return pat, slab0 // blk_rows

    per_group = 2 * n_chunks
    nb = ck // NB_KV_BLOCK

    def scores(s, item):
        hh, c = (item % per_group) // n_chunks, item % n_chunks
        qm = qs_scr[2 * s + hh]
        if c < n_chunks - 1:
            pat, blk0 = group_of(s)
            kc = k_ref[0, pl.ds(blk0 + c * nb, nb)].reshape(ck, LANES)
            st = _dot_nt(kc, qm) + bias_scr[hh, pat, c * ck:(c + 1) * ck, :]
        else:
            st = _dot_nt(kc_ref[0].astype(BF16), qm)
        s_ring[item % ring] = st
        cmax_scr[item % ring] = jnp.max(st, axis=0, keepdims=True)

    def group(s, carry):
        s_next = jnp.minimum(s + 1, NB_STEP_GROUPS - 1)
        for item in range(per_group):
            ahead = item + ATTN_LOOKAHEAD
            scores(s if ahead < per_group else s_next, ahead)
            hh, c = item // n_chunks, item % n_chunks
            if c < n_chunks - 1:
                _, blk0 = group_of(s)
                vt = jnp.concatenate([vt_ref[0, blk0 + c * nb + i, hh * HEAD_DIM:(hh + 1) * HEAD_DIM, :]
                                      for i in range(nb)], axis=1)
            else:
                vt = vct_ref[0, hh * HEAD_DIM:(hh + 1) * HEAD_DIM, :].astype(BF16)
            _softmax_step(s_ring[item % ring], cmax_scr[item % ring], vt, m_scr, acc_scr, 2 * s + hh)
        o_ref[0, pl.ds(pl.multiple_of(s * tq, tq), tq), :] = (
            _normalised_pair(acc_scr, 2 * s, 2 * s + 1).astype(o_ref.dtype))
        return carry

    for item in range(ATTN_LOOKAHEAD):
        scores(0, item)
    lax.fori_loop(0, NB_STEP_GROUPS, group, 0)


def _nbhd_attention(q, k4, vt4, k_ctx, vct, rel_bias_flat, *, name):
    B, S, W = q.shape
    P = k_ctx.shape[1]
    rows = S // GRID_W
    tq = NB_GROUP * GRID_W
    nk = NB_SLAB * GRID_W
    nblk = S // NB_KV_BLOCK
    tqs = NB_STEP_GROUPS * tq
    nu = 2 * NB_STEP_GROUPS
    ring = ATTN_LOOKAHEAD + 1
    est =(2 * (2 * tqs * LANES * 2 + 2 * S * LANES * 2 + 2 * P * LANES * 4)
           + 2 * 3 * tq * nk * 4 + ring * tq * tq * 4 + 8 * tq * tq * 4 + nu * tq * 1024)
    return pl.pallas_call(
        functools.partial(_nbhd_kernel, rows=rows),
        grid=(W // LANES, B, rows // (NB_GROUP * NB_STEP_GROUPS)),
        in_specs=[pl.BlockSpec(memory_space=pltpu.SMEM),
                  pl.BlockSpec((1, tqs, LANES), lambda p, b, g: (b, g, p)),
                  pl.BlockSpec((1, nblk, NB_KV_BLOCK, LANES), lambda p, b, g: (b, 0, 0, p)),
                  pl.BlockSpec((1, nblk, LANES, NB_KV_BLOCK), lambda p, b, g: (b, 0, p, 0)),
                  pl.BlockSpec((1, P, LANES), lambda p, b, g: (b, 0, p)),
                  pl.BlockSpec((1, LANES, P), lambda p, b, g: (b, p, 0))],
        out_specs=pl.BlockSpec((1, tqs, LANES), lambda p, b, g: (b, g, p)),
        out_shape=jax.ShapeDtypeStruct((B, S, W), BF16),
        scratch_shapes=[pltpu.VMEM((2, 3, nk, tq), F32),
                        pltpu.VMEM((nu, tq, LANES), BF16),
                        pltpu.VMEM((nu, 1, tq), F32),
                        pltpu.VMEM((nu, HEAD_DIM + SUM_ROWS, tq), F32),
                        pltpu.VMEM((ring, tq, tq), F32),
                        pltpu.VMEM((ring, 1, tq), F32)],
        compiler_params=pltpu.CompilerParams(dimension_semantics=("arbitrary",) * 3,
                                             vmem_limit_bytes=_vmem_limit(est)),
        name=name,
    )(rel_bias_flat, q, k4, vt4, k_ctx, vct)


def _split3(x):
    hi = x.astype(BF16)
    r = x - hi.astype(F32)
    mid = r.astype(BF16)
    return hi, mid, (r - mid.astype(F32)).astype(BF16)


def _mlstm_kernel(q_ref, k_ref, v_ref, g_ref, ob_ref, s0_ref, m0_ref, on_ref,
                  out_ref, sf_ref, mf_ref,
                  h_scr, nat_scr, rows_scr, stat_scr, mprev_scr, un_scr, st_scr, *, nc, grp_a, grp):
    d = pl.program_id(1)
    L = L_CHUNK
    bb = q_ref.shape[0]
    NP = H_MLSTM // 2
    row = lax.broadcasted_iota(jnp.int32, (L, L), 0)
    col = lax.broadcasted_iota(jnp.int32, (L, L), 1)
    sign = 1 - 2 * d
    mask = (col - row) * sign <= 0
    maskb = mask.astype(BF16)
    mask3 = jnp.concatenate([maskb, maskb, maskb], axis=1)
    lane = lax.broadcasted_iota(jnp.int32, (L, LANES), 1)
    lo = lane < HEAD_DIM
    top = row < HEAD_DIM
    row2 = lax.broadcasted_iota(jnp.int32, (L, 2 * LANES), 0)
    col2 = lax.broadcasted_iota(jnp.int32, (L, 2 * LANES), 1)
    keep_state = (row2 < HEAD_DIM) == ((col2 % LANES) < HEAD_DIM)
    top2 = row2 < HEAD_DIM
    ones_b = jnp.ones((L, LANES), BF16)
    ones_lo = lo.astype(BF16)
    ones_hi = jnp.logical_not(lo).astype(BF16)

    def chunk_rows(c):
        return pl.ds(pl.multiple_of(c * L, L), L)

    def tokens(c):
        if bb == 1:
            return 0, chunk_rows(c)
        return c // nc, pl.ds(pl.multiple_of((c % nc) * L, L), L)

    def pass_a(it, carry):
        cs = [it * grp_a + u for u in range(grp_a)]
        gts = [g_ref[tokens(c)[0], tokens(c)[1], :] for c in cs]
        bns = [_dot(mask3, jnp.concatenate(_split3(gt), axis=0)) for gt in gts]
        a_all = []
        for c, gt, bn in zip(cs, gts, bns):
            nat = jnp.where(lane < H_MLSTM, gt, bn)
            nat_scr[chunk_rows(c), :] = nat * (-LOG2E)
            nat_t = nat.T
            b_rows = nat_t[H_MLSTM:2 * H_MLSTM]
            c_rows = nat_t[0:H_MLSTM] - b_rows
            rows_scr[c] = jnp.concatenate([c_rows * LOG2E, b_rows], axis=0)
            c_max = jnp.max(c_rows, axis=1, keepdims=True)
            b_tot = jnp.where(d == 0, b_rows[:, L - 1:L], b_rows[:, 0:1])
            stat_scr[c] = jnp.concatenate([jnp.broadcast_to(c_max, (H_MLSTM, LANES)),
                                           jnp.broadcast_to(b_tot, (H_MLSTM, LANES))], axis=0)
            a_all.append(jnp.exp(c_rows - c_max))
        for c, a_rows in zip(cs, a_all):
            sq, rows = tokens(c)
            for p in range(NP):
                lanes = slice(p * LANES, (p + 1) * LANES)
                k_t = k_ref[sq, rows, lanes].astype(F32).T
                a_sel = jnp.where(top, a_rows[2 * p:2 * p + 1], a_rows[2 * p + 1:2 * p + 2])
                vv = jnp.concatenate([v_ref[sq, rows, lanes], ones_b], axis=1)
                un = _dot((k_t * a_sel).astype(BF16), vv)
                un_scr[c, p] = jnp.where(keep_state, un, 0.0)
        return carry

    lax.fori_loop(0, bb * nc // grp_a, pass_a, 0)

    def pass_b(ci, m, sq):
        c = sq * nc + jnp.where(d == 0, ci, nc - 1 - ci)
        st = stat_scr[c]
        c_max, b_tot = st[0:H_MLSTM], st[H_MLSTM:]
        m_new = jnp.maximum(b_tot + m, b_tot + c_max)
        d_old = jnp.exp(b_tot + m - m_new)
        d_new = jnp.exp(b_tot + c_max - m_new)
        mprev_scr[c] = jnp.concatenate([m, m], axis=0) * LOG2E
        for p in range(NP):
            def rows_of(t, p=p):
                even = jnp.concatenate([t[2 * p:2 * p + 1]] * 2, axis=1)
                odd = jnp.concatenate([t[2 * p + 1:2 * p + 2]] * 2, axis=1)
                return jnp.where(top2, even, odd)
            s_prev = st_scr[p]
            st_scr[p] = rows_of(d_old) * s_prev + rows_of(d_new) * un_scr[c, p]
            un_scr[c, p] = s_prev
        return m_new

    def recurrence(sq, carry):
        st_scr[...] = s0_ref[sq, 0]
        m_fin = lax.fori_loop(0, nc, functools.partial(pass_b, sq=sq), m0_ref[sq, 0][0:H_MLSTM])
        sf_ref[sq, 0] = st_scr[...]
        mf_ref[sq, 0] = jnp.concatenate([m_fin, m_fin], axis=0)
        return carry

    lax.fori_loop(0, bb, recurrence, 0)

    def pass_c(it, carry):
        cs = [it * grp + u for u in range(grp)]
        units = [(u, p) for u in range(grp) for p in range(NP)]
        early = {}
        for u, p in units:
            c = cs[u]
            sq, rows = tokens(c)
            lanes = slice(p * LANES, (p + 1) * LANES)
            qp = q_ref[sq, rows, lanes]
            kp = k_ref[sq, rows, lanes]
            s_in = un_scr[c, p].astype(BF16)
            qms = [jnp.where(lo if hh == 0 else jnp.logical_not(lo), qp, jnp.zeros_like(qp)) for hh in range(2)]
            early[u, p] = ([_dot_nt(qm, kp) for qm in qms],
                           _dot(qp, s_in))
        mid = {}
        for u, p in units:
            r_t = rows_scr[cs[u]]
            m_in = mprev_scr[cs[u]]
            for hh in range(2):
                h = 2 * p + hh
                cm = jnp.where(mask, r_t[h:h + 1, :], -jnp.inf)
                m_prev = m_in[h:h + 1, :]
                mu = jnp.maximum(jnp.broadcast_to(jnp.max(cm, axis=1, keepdims=True), (L, LANES)), m_prev)
                w = early[u, p][0][hh] * jnp.exp2(cm - mu)
                mid[u, p, hh] = (w.astype(BF16), mu, m_prev)
        for u, p in units:
            sq, rows = tokens(cs[u])
            lanes = slice(p * LANES, (p + 1) * LANES)
            vp = v_ref[sq, rows, lanes]
            zero = jnp.zeros_like(vp)
            vv = jnp.concatenate([jnp.concatenate([jnp.where(lo, vp, zero), ones_lo], axis=1),
                                  jnp.concatenate([jnp.where(lo, zero, vp), ones_hi], axis=1)], axis=0)
            w2 = jnp.concatenate([mid[u, p, 0][0], mid[u, p, 1][0]], axis=1)
            nd = _dot(w2, vv)
            nat = nat_scr[chunk_rows(cs[u]), :]
            nb = [jnp.broadcast_to(nat[:, H_MLSTM + 2 * p + hh:H_MLSTM + 2 * p + hh + 1], (L, LANES))
                  for hh in range(2)]
            fs = early[u, p][1]
            mu = jnp.where(lo, mid[u, p, 0][1], mid[u, p, 1][1])
            m_prev = jnp.where(lo[0:1], mid[u, p, 0][2], mid[u, p, 1][2])
            w_inter = jnp.exp2(m_prev - mu)
            den = nd[:, LANES:] + w_inter * fs[:, LANES:]
            den = jnp.maximum(jnp.abs(den), jnp.exp2(jnp.where(lo, nb[0], nb[1]) - mu))
            h_scr[d, chunk_rows(cs[u]), lanes] = (nd[:, :LANES] + w_inter * fs[:, :LANES]) / den
        return carry

    lax.fori_loop(0, bb * nc // grp, pass_c, 0)

    @pl.when(d == 1)
    def _finish():
        def rows_block(c, carry):
            sq, rows = tokens(c)
            flat = chunk_rows(c)
            for p in range(NP):
                lanes = slice(p * LANES, (p + 1) * LANES)
                hn = _pair_rms(h_scr[0, flat, lanes] + h_scr[1, flat, lanes], on_ref[:, lanes])
                out_ref[sq, rows, lanes] = (_sigmoid(ob_ref[sq, rows, lanes]) * hn).astype(out_ref.dtype)
            return carry

        lax.fori_loop(0, bb * nc, rows_block, 0)


def _mlstm(q, k, v, gates, ob, s0, m0, out_norm, *, name):
    assert L_CHUNK == LANES
    B, S, W = q.shape
    nc = S // L_CHUNK
    bb = max(1, min(B, MLSTM_STEP_CHUNKS // nc))
    ncb, sb = bb * nc, bb * S
    grp = min(MLSTM_GROUP, ncb)
    grp_a = min(MLSTM_GROUP_A, ncb)
    npair = H_MLSTM // 2
    seq = lambda b, d: (b, 0, 0)
    est = (2 * (3 * sb * W * 2 + sb * LANES * 4 + sb * W * 4 + sb * W * 2) + 2 * sb * W * 4 + sb * LANES * 4
           + ncb * npair * LANES * 2 * LANES * 4 + 12 * 1024 * 1024)
    return pl.pallas_call(
        functools.partial(_mlstm_kernel, nc=nc, grp_a=grp_a, grp=grp),
        grid=(B // bb, 2),
        in_specs=[pl.BlockSpec((bb, S, W), seq), pl.BlockSpec((bb, S, W), seq), pl.BlockSpec((bb, S, W), seq),
                  pl.BlockSpec((bb, S, LANES), lambda b, d: (b, 0, d)),
                  pl.BlockSpec((bb, S, W), seq),
                  pl.BlockSpec((bb, 1, npair, LANES, 2 * LANES), lambda b, d: (b, d, 0, 0, 0)),
                  pl.BlockSpec((bb, 1, 8, LANES), lambda b, d: (b, d, 0, 0)),
                  pl.BlockSpec((1, W), lambda b, d: (0, 0))],
        out_specs=[pl.BlockSpec((bb, S, W), seq),
                   pl.BlockSpec((bb, 1, npair, LANES, 2 * LANES), lambda b, d: (b, d, 0, 0, 0)),
                   pl.BlockSpec((bb, 1, 8, LANES), lambda b, d: (b, d, 0, 0))],
        out_shape=[jax.ShapeDtypeStruct((B, S, W), BF16),
                   jax.ShapeDtypeStruct((B, 2, npair, LANES, 2 * LANES), F32),
                   jax.ShapeDtypeStruct((B, 2, 8, LANES), F32)],
        scratch_shapes=[pltpu.VMEM((2, sb, W), F32),
                        pltpu.VMEM((sb, LANES), F32),
                        pltpu.VMEM((ncb, 8, L_CHUNK), F32),
                        pltpu.VMEM((ncb, 8, LANES), F32),
                        pltpu.VMEM((ncb, 8, LANES), F32),
                        pltpu.VMEM((ncb, npair, LANES, 2 * LANES), F32),
                        pltpu.VMEM((npair, LANES, 2 * LANES), F32)],
        compiler_params=pltpu.CompilerParams(dimension_semantics=("arbitrary", "arbitrary"),
                                             vmem_limit_bytes=_vmem_limit(est)),
        name=name,
    )(q, k, v, gates, ob, s0, m0, out_norm)


def _outproj_kernel(a_ref, b_ref, c_ref, x_ref, mod_ref, w_ref, gpost_ref, gpre_ref, x1_ref, h2_ref):
    mod = mod_ref[0]
    gt1 = mod[:, 2 * D_MODEL:3 * D_MODEL]
    sh2 = mod[:, 3 * D_MODEL:4 * D_MODEL]
    sc2 = mod[:, 4 * D_MODEL:5 * D_MODEL]
    tr = x_ref.shape[0] // ROW_SPLIT
    pieces = [pl.ds(s * tr, tr) for s in range(ROW_SPLIT)]
    mos = [_dot(jnp.concatenate([a_ref[r, :], b_ref[r, :], c_ref[r, :]], axis=1), w_ref[...]) for r in pieces]
    for r, mo in zip(pieces, mos):
        x1 = x_ref[r, :] + gt1 * _rms(mo, gpost_ref[...])
        x1_ref[r, :] = x1
        h2_ref[r, :] = (_rms(x1, gpre_ref[...]) * (1.0 + sc2) + sh2).astype(h2_ref.dtype)


def _outproj(oa, ob, oc, x, mods, w_out, g_post, g_pre, *, rows_per_cond, name):
    T = x.shape[0]
    tm = TM_PROJ
    bpc = rows_per_cond // tm
    row = lambda i: (i, 0)
    const = lambda i: (0, 0)
    est = 2 * (tm * D_MODEL * (2 + 4 + 4 + 2) + D_MODEL * D_MODEL * 2) + 4 * tm * D_MODEL * 4
    return pl.pallas_call(
        _outproj_kernel,
        grid=(T // tm,),
        in_specs=[pl.BlockSpec((tm, W_ATTN), row), pl.BlockSpec((tm, W_MLSTM), row),
                  pl.BlockSpec((tm, W_NBHD), row), pl.BlockSpec((tm, D_MODEL), row),
                  pl.BlockSpec((1, 1, N_MOD * D_MODEL), lambda i: (i // bpc, 0, 0)),
                  pl.BlockSpec((D_MODEL, D_MODEL), const),
                  pl.BlockSpec((1, D_MODEL), const), pl.BlockSpec((1, D_MODEL), const)],
        out_specs=[pl.BlockSpec((tm, D_MODEL), row), pl.BlockSpec((tm, D_MODEL), row)],
        out_shape=[jax.ShapeDtypeStruct((T, D_MODEL), F32), jax.ShapeDtypeStruct((T, D_MODEL), BF16)],
        compiler_params=pltpu.CompilerParams(dimension_semantics=("arbitrary",),
                                             vmem_limit_bytes=_vmem_limit(est)),
        name=name,
    )(oa, ob, oc, x, mods, w_out, g_post, g_pre)


def _ffn_kernel(h_ref, x_ref, mod_ref, wu_ref, wd_ref, g_ref, o_ref, acc_ref):
    j = pl.program_id(1)

    @pl.when(j == 0)
    def _zero():
        acc_ref[...] = jnp.zeros_like(acc_ref)

    u = jnp.maximum(_dot(h_ref[...], wu_ref[...]), 0.0)
    acc_ref[...] += _dot((u * u).astype(BF16), wd_ref[...])

    @pl.when(j == pl.num_programs(1) - 1)
    def _finish():
        gt2 = mod_ref[0][:, 5 * D_MODEL:6 * D_MODEL]
        o_ref[...] = x_ref[...] + gt2 * _rms(acc_ref[...], g_ref[...])


def _ffn(h2, x1, mods, w_up, w_down, g_post, *, rows_per_cond, name):
    T = x1.shape[0]
    tm, tf = TM_FFN, TF_FFN
    tm = min(tm, rows_per_cond)
    bpc = rows_per_cond // tm
    est = 2 * (tm * D_MODEL * (2 + 4 + 4) + 2 * D_MODEL * tf * 2) + tm * D_MODEL * 4 + 3 * tm * tf * 4
    return pl.pallas_call(
        _ffn_kernel,
        grid=(T // tm, D_FF // tf),
        in_specs=[pl.BlockSpec((tm, D_MODEL), lambda i, j: (i, 0)),
                  pl.BlockSpec((tm, D_MODEL), lambda i, j: (i, 0)),
                  pl.BlockSpec((1, 1, N_MOD * D_MODEL), lambda i, j: (i // bpc, 0, 0)),
                  pl.BlockSpec((D_MODEL, tf), lambda i, j: (0, j)),
                  pl.BlockSpec((tf, D_MODEL), lambda i, j: (j, 0)),
                  pl.BlockSpec((1, D_MODEL), lambda i, j: (0, 0))],
        out_specs=pl.BlockSpec((tm, D_MODEL), lambda i, j: (i, 0)),
        out_shape=jax.ShapeDtypeStruct((T, D_MODEL), F32),
        scratch_shapes=[pltpu.VMEM((tm, D_MODEL), F32)],
        compiler_params=pltpu.CompilerParams(dimension_semantics=("arbitrary", "arbitrary"),
                                             vmem_limit_bytes=_vmem_limit(est)),
        name=name,
    )(h2, x1, mods, w_up, w_down, g_post)


def _pad_w_in(w_in_l):
    o = W_ATTN + 2 * W_KV + 4 * W_MLSTM
    pre, gates, post = w_in_l[:, :o], w_in_l[:, o:o + N_GATES], w_in_l[:, o + N_GATES:]
    z = jnp.zeros((D_MODEL, LANES - 2 * H_MLSTM), w_in_l.dtype)
    return jnp.concatenate([pre, gates[:, :2 * H_MLSTM], z, gates[:, 2 * H_MLSTM:], z, post],
                           axis=1).astype(BF16)


def _pad_gate_bias(gb_l):
    z = jnp.zeros((LANES - 2 * H_MLSTM,), gb_l.dtype)
    return jnp.concatenate([gb_l[:2 * H_MLSTM], z, gb_l[2 * H_MLSTM:], z]).reshape(1, 2 * LANES)


def _rope_tables(S):
    quarter = HEAD_DIM // 4
    pos = jnp.arange(S)
    inv_freq = ROPE_THETA ** (-jnp.arange(quarter, dtype=F32) / quarter)

    def tabs(p):
        ang = p.astype(F32)[:, None] * inv_freq[None, :]
        return jnp.cos(ang), jnp.sin(ang)

    cr, sr = tabs(pos // GRID_W)
    cc, sc = tabs(pos % GRID_W)
    cos = jnp.concatenate([cr, cr, cc, cc], axis=1)
    sin = jnp.concatenate([-sr, sr, -sc, sc], axis=1)
    return jnp.tile(cos, (1, 2)), jnp.tile(sin, (1, 2))


def _pack_state(C, n, m):
    B = C.shape[0]
    Cp = C.reshape(B, 2, 2, 2, HEAD_DIM, HEAD_DIM)
    z = jnp.zeros_like(Cp[:, :, :, 0])
    top = jnp.concatenate([Cp[:, :, :, 0], z], axis=-1)
    bot = jnp.concatenate([z, Cp[:, :, :, 1]], axis=-1)
    Cbd = jnp.concatenate([top, bot], axis=-2)
    n_rep = jnp.broadcast_to(n.reshape(B, 2, 2, LANES, 1), (B, 2, 2, LANES, LANES))
    same_head = (jnp.arange(LANES)[:, None] < HEAD_DIM) == (jnp.arange(LANES)[None, :] < HEAD_DIM)
    n_rep = jnp.where(same_head, n_rep, 0.0)
    m_rows = jnp.broadcast_to(m[..., None], m.shape + (LANES,))
    return jnp.concatenate([Cbd, n_rep], axis=-1), jnp.concatenate([m_rows, m_rows], axis=-2)


def _unpack_state(s_p, m_p):
    B = s_p.shape[0]
    c_even = s_p[:, :, :, :HEAD_DIM, :HEAD_DIM]
    c_odd = s_p[:, :, :, HEAD_DIM:, HEAD_DIM:LANES]
    C = jnp.stack([c_even, c_odd], axis=3).reshape(B, 2, H_MLSTM, HEAD_DIM, HEAD_DIM)
    n = jnp.concatenate([s_p[..., :HEAD_DIM, LANES], s_p[..., HEAD_DIM:, LANES + HEAD_DIM]], axis=-1)
    return C, n.reshape(B, 2, H_MLSTM, HEAD_DIM), m_p[:, :, :H_MLSTM, 0]


def _layer(x, mods, lw, *, B, S, cond_rows, rope_tabs, ctx_cache, state, name):
    T = B * S
    kv_dtype = BF16 if ctx_cache is not None else F32
    ck_a = min(CK_ATTN, S)
    pr = _inproj(x, mods, lw["g_pre_mix"], lw["w_in"], lw["q_norm"], lw["k_norm"], lw["gate_bias"],
                 rope_tabs, rows_per_cond=cond_rows, kv_dtype=kv_dtype,
                 vt_blocks=(ck_a, NB_KV_BLOCK), name=name + "_inproj")
    seq = lambda a: a.reshape(B, S, a.shape[-1])
    qa, ka, va = seq(pr["qa"]), seq(pr["ka"]), seq(pr["va"])
    qc, kc, vc = seq(pr["qc"]), seq(pr["kc"]), seq(pr["vc"])
    chunks = lambda a, n: a.reshape(B, S // n, n, a.shape[-1])
    vat = pr["vat"].reshape(B, S // ck_a, W_KV, ck_a)
    vct = pr["vct"].reshape(B, S // NB_KV_BLOCK, W_NBHD, NB_KV_BLOCK)
    if ctx_cache is None:
        out_a = _attention(qa, [(chunks(ka, ck_a), vat)], heads=_HEADS_GQA, name=name + "_attn_a")
        out_c = _attention(qc, [(chunks(kc, NB_KV_BLOCK), vct)], heads=_HEADS_MHA, name=name + "_attn_c")
    else:
        ck_c, cv_c = ctx_cache[2], ctx_cache[3]
        P = ck_c.shape[1]
        ctx_k = ctx_cache[0].reshape(B, P // ck_a, ck_a, W_KV)
        ctx_vt = ctx_cache[1].reshape(B, P // ck_a, ck_a, W_KV).transpose(0, 1, 3, 2)
        out_a = _attention(qa, [(chunks(ka, ck_a), vat), (ctx_k, ctx_vt)], heads=_HEADS_GQA,
                           name=name + "_attn_a")
        out_c = _nbhd_attention(qc, chunks(kc, NB_KV_BLOCK), vct, ck_c, cv_c.transpose(0, 2, 1),
                                lw["rel_bias"], name=name + "_attn_c")
    s0, m0 = state
    out_b, sf, mf = _mlstm(seq(pr["qb"]), seq(pr["kb"]), seq(pr["vb"]), seq(pr["g"]), seq(pr["ob"]),
                           s0, m0, lw["out_norm"], name=name + "_mlstm")
    x1, h2 = _outproj(out_a.reshape(T, W_ATTN), out_b.reshape(T, W_MLSTM), out_c.reshape(T, W_NBHD),
                      x, mods, lw["w_out"], lw["g_post_mix"], lw["g_pre_ffn"],
                      rows_per_cond=cond_rows, name=name + "_outproj")
    x2 = _ffn(h2, x1, mods, lw["w_up"], lw["w_down"], lw["g_post_ffn"],
              rows_per_cond=cond_rows, name=name + "_ffn")
    return x2, (ka, va, kc, vc, sf, mf)


def kernel(x_prompt, x_sample, c, cache_k_attn, cache_v_attn, cache_k_nbhd, cache_v_nbhd, state_mlstm_C, state_mlstm_n, state_mlstm_m, c_ctx, w_ada, b_ada, g_pre_mix, g_post_mix, g_pre_ffn, g_post_ffn, w_in, q_norm_attn, k_norm_attn, mlstm_gate_bias, mlstm_out_norm, nbhd_rel_bias, w_out, w_ffn_up, w_ffn_down):
    Bc, Sc, _ = x_prompt.shape
    Bl, Sl, _ = x_sample.shape
    P = cache_k_attn.shape[2]
    n_cond = 8
    cond = jnp.concatenate([c_ctx[None, :], c, jnp.zeros((n_cond - 1 - Bl, D_MODEL), F32)], axis=0)
    mods_all = _modulation(cond, w_ada, b_ada)

    layers = []
    for l in range(DEPTH):
        layers.append(dict(
            w_in=_pad_w_in(w_in[l]),
            w_out=w_out[l].astype(BF16),
            w_up=w_ffn_up[l].astype(BF16),
            w_down=w_ffn_down[l].astype(BF16),
            g_pre_mix=g_pre_mix[l].reshape(1, D_MODEL), g_post_mix=g_post_mix[l].reshape(1, D_MODEL),
            g_pre_ffn=g_pre_ffn[l].reshape(1, D_MODEL), g_post_ffn=g_post_ffn[l].reshape(1, D_MODEL),
            q_norm=jnp.tile(q_norm_attn[l], 2).reshape(1, LANES),
            k_norm=jnp.tile(k_norm_attn[l], 2).reshape(1, LANES),
            gate_bias=_pad_gate_bias(mlstm_gate_bias[l]),
            out_norm=mlstm_out_norm[l].reshape(1, W_MLSTM),
            rel_bias=nbhd_rel_bias[l].reshape(-1),
        ))

    xp = x_prompt.reshape(Bc * Sc, D_MODEL)
    zero_state = (jnp.zeros((Bc, 2, H_MLSTM // 2, LANES, 2 * LANES), F32), jnp.zeros((Bc, 2, 8, LANES), F32))
    ctx = []
    for l in range(DEPTH):
        mods = mods_all[l, 0:1].reshape(1, 1, N_MOD * D_MODEL)
        xp, extras = _layer(xp, mods, layers[l], B=Bc, S=Sc, cond_rows=Bc * Sc, rope_tabs=None,
                            ctx_cache=None, state=zero_state, name=f"ctx{l}")
        ctx.append(extras)
    new_k_attn = jnp.stack([e[0].reshape(Bc, Sc, KV_ATTN, HEAD_DIM) for e in ctx], axis=1)
    new_v_attn = jnp.stack([e[1].reshape(Bc, Sc, KV_ATTN, HEAD_DIM) for e in ctx], axis=1)
    new_k_nbhd = jnp.stack([e[2].reshape(Bc, Sc, H_NBHD, HEAD_DIM) for e in ctx], axis=1)
    new_v_nbhd = jnp.stack([e[3].reshape(Bc, Sc, H_NBHD, HEAD_DIM) for e in ctx], axis=1)
    states = [_unpack_state(e[4], e[5]) for e in ctx]
    new_C = jnp.stack([s[0] for s in states], axis=1)
    new_n = jnp.stack([s[1] for s in states], axis=1)
    new_m = jnp.stack([s[2] for s in states], axis=1)

    xs = x_sample.reshape(Bl * Sl, D_MODEL)
    rope_tabs = _rope_tables(Sl)
    for l in range(DEPTH):
        mods = mods_all[l, 1:1 + Bl].reshape(Bl, 1, N_MOD * D_MODEL)
        cache = (cache_k_attn[:, l].reshape(Bl, P, W_KV), cache_v_attn[:, l].reshape(Bl, P, W_KV),
                 cache_k_nbhd[:, l].reshape(Bl, P, W_NBHD), cache_v_nbhd[:, l].reshape(Bl, P, W_NBHD))
        state = _pack_state(state_mlstm_C[:, l], state_mlstm_n[:, l], state_mlstm_m[:, l])
        xs, _ = _layer(xs, mods, layers[l], B=Bl, S=Sl, cond_rows=Sl, rope_tabs=rope_tabs,
                       ctx_cache=cache, state=state, name=f"lat{l}")

    return (xp.reshape(Bc, Sc, D_MODEL), xs.reshape(Bl, Sl, D_MODEL),
            new_k_attn, new_v_attn, new_k_nbhd, new_v_nbhd, new_C, new_n, new_m)
```

```python
import functools

import jax
import jax.numpy as jnp
import numpy as np
from jax import lax
from jax.experimental import pallas as pl
from jax.experimental.pallas import tpu as pltpu

F32 = jnp.float32
BF16 = jnp.bfloat16

D_MODEL = 1024
DEPTH = 2
GRID_W = 64
HEAD_DIM = 64
H_ATTN = 6
KV_ATTN = 2
H_MLSTM = 4
H_NBHD = 6
D_FF = 4 * D_MODEL
NA_ROWS = 8
NA_COLS = 16
ROPE_THETA = 10000.0
EPS = 1e-6
N_MOD = 6
W_ATTN = H_ATTN * HEAD_DIM
W_KV = KV_ATTN * HEAD_DIM
W_MLSTM = H_MLSTM * HEAD_DIM
W_NBHD = H_NBHD * HEAD_DIM
N_GATES = 4 * H_MLSTM

LANES = 128
V7X_VMEM_BYTES = 64 * 1024 * 1024
VMEM_CAP_BYTES = 56 * 1024 * 1024

TM_PROJ = 512
PROJ_CHUNK = 512
TM_FFN = 1024
TF_FFN = 1024
ROW_SPLIT = 4
TQ_ATTN = 512
CK_ATTN = 512
ATTN_LOOKAHEAD = 2
SEQS_PER_STEP = 8
L_CHUNK = 128
MLSTM_GROUP = 2
MLSTM_GROUP_A = 4
MLSTM_STEP_CHUNKS = 16
NB_GROUP = 8
NB_SLAB = 16
NB_STEP_GROUPS = 4
NB_KV_BLOCK = 256
NEG = -1e30
LOG2E = 1.4426950408889634

_COLS = {}
_off = 0
for _name, _w in (("qa", W_ATTN), ("ka", W_KV), ("va", W_KV), ("qb", W_MLSTM), ("kb", W_MLSTM),
                  ("vb", W_MLSTM), ("ob", W_MLSTM), ("gf", LANES), ("gb", LANES),
                  ("qc", W_NBHD), ("kc", W_NBHD), ("vc", W_NBHD)):
    _COLS[_name] = (_off, _off + _w)
    _off += _w
IN_PAD = _off


def _vmem_limit(nbytes):
    return int(min(max(nbytes, 16 * 1024 * 1024), VMEM_CAP_BYTES))


def _dot(a, b):
    return jnp.dot(a, b, preferred_element_type=F32)


def _dot_nt(a, b):
    return lax.dot_general(a, b, (((1,), (1,)), ((), ())), preferred_element_type=F32)


def _lane_lo(shape):
    return (lax.broadcasted_iota(jnp.int32, shape, len(shape) - 1) % LANES) < HEAD_DIM


def _rms(x, g):
    ms = jnp.mean(x * x, axis=-1, keepdims=True)
    return (x * lax.rsqrt(ms + EPS)) * g


def _pair_rms(x, g):
    lo = _lane_lo(x.shape)
    x2 = x * x
    s_lo = jnp.sum(jnp.where(lo, x2, 0.0), axis=-1, keepdims=True)
    s_hi = jnp.sum(jnp.where(lo, 0.0, x2), axis=-1, keepdims=True)
    r = jnp.where(lo, lax.rsqrt(s_lo / HEAD_DIM + EPS), lax.rsqrt(s_hi / HEAD_DIM + EPS))
    return (x * r) * g


def _sigmoid(x):
    return 1.0 / (1.0 + jnp.exp(-x))


def _mods_kernel(c_ref, w_ref, b_ref, o_ref):
    c = c_ref[...]
    s = (c * _sigmoid(c)).astype(BF16)
    o_ref[0] = _dot(s, w_ref[0].astype(BF16)) + b_ref[0]


def _modulation(cond, w_ada, b_ada):
    n = cond.shape[0]
    tn = D_MODEL
    return pl.pallas_call(
        _mods_kernel,
        grid=(DEPTH, N_MOD * D_MODEL // tn),
        in_specs=[pl.BlockSpec((n, D_MODEL), lambda l, j: (0, 0)),
                  pl.BlockSpec((1, D_MODEL, tn), lambda l, j: (l, 0, j)),
                  pl.BlockSpec((1, 1, tn), lambda l, j: (l, 0, j))],
        out_specs=pl.BlockSpec((1, n, tn), lambda l, j: (l, 0, j)),
        out_shape=jax.ShapeDtypeStruct((DEPTH, n, N_MOD * D_MODEL), F32),
        compiler_params=pltpu.CompilerParams(
            dimension_semantics=("arbitrary", "arbitrary"),
            vmem_limit_bytes=_vmem_limit(4 * D_MODEL * tn * 4)),
        name="modulation",
    )(cond, w_ada, b_ada.reshape(DEPTH, 1, N_MOD * D_MODEL))


def _inproj_kernel(*refs, rope):
    if rope:
        (x_ref, mod_ref, g_ref, w_ref, qn_ref, kn_ref, gbias_ref, cos_ref, sin_ref,
         qa_ref, ka_ref, va_ref, qb_ref, kb_ref, vb_ref, ob_ref, gate_ref,
         qc_ref, kc_ref, vc_ref, vat_ref, vct_ref) = refs
    else:
        (x_ref, mod_ref, g_ref, w_ref, qn_ref, kn_ref, gbias_ref,
         qa_ref, ka_ref, va_ref, qb_ref, kb_ref, vb_ref, ob_ref, gate_ref,
         qc_ref, kc_ref, vc_ref, vat_ref, vct_ref) = refs
    x = x_ref[...]
    mod = mod_ref[0]
    sh1 = mod[:, 0:D_MODEL]
    sc1 = mod[:, D_MODEL:2 * D_MODEL]
    hb = (_rms(x, g_ref[...]) * (1.0 + sc1) + sh1).astype(BF16)

    z = [_dot(hb, w_ref[:, c0:c0 + PROJ_CHUNK]) for c0 in range(0, IN_PAD, PROJ_CHUNK)]

    def proj(name, j=0, w=None):
        lo, hi = _COLS[name]
        lo = lo + j
        hi = hi if w is None else lo + w
        parts = []
        while lo < hi:
            c, o = divmod(lo, PROJ_CHUNK)
            n = min(hi - lo, PROJ_CHUNK - o)
            parts.append(z[c][:, o:o + n])
            lo += n
        return parts[0] if len(parts) == 1 else jnp.concatenate(parts, axis=1)

    scale = HEAD_DIM ** -0.5
    q_scale = scale * LOG2E

    def rotary(t):
        first = (lax.broadcasted_iota(jnp.int32, t.shape, 1) % 32) < 16
        partner = jnp.where(first, pltpu.roll(t, LANES - 16, 1), pltpu.roll(t, 16, 1))
        return t * cos_ref[...] + partner * sin_ref[...]

    def store_queries(t, j, q_ref, kv_half):
        lo = _lane_lo(t.shape)
        for e in range(2):
            h = 2 * j + e
            src = t if e == kv_half[h] else pltpu.roll(t, HEAD_DIM, 1)
            keep = lo if kv_half[h] == 0 else jnp.logical_not(lo)
            q_ref[:, h * LANES:(h + 1) * LANES] = jnp.where(keep, src, 0.0).astype(q_ref.dtype)

    for j in range(W_ATTN // LANES):
        t = _pair_rms(proj("qa", j * LANES, LANES), qn_ref[...])
        if rope:
            t = rotary(t)
        store_queries(t * q_scale, j, qa_ref, [kh for _, kh in _HEADS_GQA])
    t = _pair_rms(proj("ka"), kn_ref[...])
    if rope:
        t = rotary(t)
    ka_ref[...] = t.astype(ka_ref.dtype)

    def store_v(v, v_ref, vt_ref):
        v_ref[...] = v.astype(v_ref.dtype)
        nblk, _, blk = vt_ref.shape
        for u in range(nblk):
            vt_ref[u] = v[u * blk:(u + 1) * blk].T.astype(vt_ref.dtype)

    store_v(proj("va"), va_ref, vat_ref)
    qb_ref[...] = proj("qb").astype(qb_ref.dtype)
    kb_ref[...] = (proj("kb") * scale).astype(kb_ref.dtype)
    vb_ref[...] = proj("vb").astype(vb_ref.dtype)
    ob_ref[...] = proj("ob").astype(ob_ref.dtype)
    for j, name in enumerate(("gf", "gb")):
        gt = proj(name) + gbias_ref[:, j * LANES:(j + 1) * LANES]
        lane = lax.broadcasted_iota(jnp.int32, gt.shape, 1)
        is_f = (lane >= H_MLSTM) & (lane < 2 * H_MLSTM)
        logsig = jnp.minimum(gt, 0.0) - jnp.log1p(jnp.exp(-jnp.abs(gt)))
        gate_ref[:, j * LANES:(j + 1) * LANES] = jnp.where(is_f, logsig, gt)
    for j in range(W_NBHD // LANES):
        store_queries(proj("qc", j * LANES, LANES) * q_scale, j, qc_ref, [kh for _, kh in _HEADS_MHA])
    kc_ref[...] = proj("kc").astype(kc_ref.dtype)
    store_v(proj("vc"), vc_ref, vct_ref)


def _inproj(x, mods, g_pre, w_in_p, qn, kn, gbias, rope_tabs, *, rows_per_cond, kv_dtype, vt_blocks, name):
    T = x.shape[0]
    tm = TM_PROJ
    bpc = rows_per_cond // tm
    rope = rope_tabs is not None
    row = lambda i: (i, 0)
    const = lambda i: (0, 0)
    in_specs = [pl.BlockSpec((tm, D_MODEL), row),
                pl.BlockSpec((1, 1, N_MOD * D_MODEL), lambda i: (i // bpc, 0, 0)),
                pl.BlockSpec((1, D_MODEL), const),
                pl.BlockSpec((D_MODEL, IN_PAD), const),
                pl.BlockSpec((1, LANES), const),
                pl.BlockSpec((1, LANES), const),
                pl.BlockSpec((1, 2 * LANES), const)]
    args = [x, mods, g_pre, w_in_p, qn, kn, gbias]
    if rope:
        nblk = rope_tabs[0].shape[0] // tm
        in_specs += [pl.BlockSpec((tm, LANES), lambda i: (i % nblk, 0))] * 2
        args += list(rope_tabs)
    widths = [("qa", H_ATTN * LANES, BF16), ("ka", W_KV, kv_dtype), ("va", W_KV, kv_dtype),
              ("qb", W_MLSTM, BF16), ("kb", W_MLSTM, BF16), ("vb", W_MLSTM, BF16),
              ("ob", W_MLSTM, F32), ("g", 2 * LANES, F32),
              ("qc", H_NBHD * LANES, BF16), ("kc", W_NBHD, kv_dtype), ("vc", W_NBHD, kv_dtype)]
    out_specs = [pl.BlockSpec((tm, w), row) for _, w, _ in widths]
    out_shape = [jax.ShapeDtypeStruct((T, w), dt) for _, w, dt in widths]
    for w, blk in zip((W_KV, W_NBHD), vt_blocks):
        out_specs.append(pl.BlockSpec((tm // blk, w, blk), lambda i: (i, 0, 0)))
        out_shape.append(jax.ShapeDtypeStruct((T // blk, w, blk), BF16))
    est = 2 * (tm * D_MODEL * 4 + D_MODEL * IN_PAD * 2 + tm * IN_PAD * 4) + 3 * tm * IN_PAD * 4
    outs = pl.pallas_call(
        functools.partial(_inproj_kernel, rope=rope),
        grid=(T // tm,),
        in_specs=in_specs, out_specs=out_specs, out_shape=out_shape,
        compiler_params=pltpu.CompilerParams(dimension_semantics=("arbitrary",),
                                             vmem_limit_bytes=_vmem_limit(est)),
        name=name,
    )(*args)
    return dict(zip([n for n, _, _ in widths] + ["vat", "vct"], outs))


SUM_ROWS = 16


def _softmax_step(st, col_max, vt, m_scr, acc_scr, u):
    m_old = m_scr[u]
    m_new = jnp.maximum(m_old, col_max)
    p = jnp.exp2(st - m_new).astype(BF16)
    alpha = jnp.exp2(m_old - m_new)
    vt1 = jnp.concatenate([vt, jnp.ones((SUM_ROWS, vt.shape[1]), BF16)], axis=0)
    acc_scr[u] = alpha * acc_scr[u] + _dot(vt1, p)
    m_scr[u] = m_new


def _normalised_pair(acc_scr, u0, u1):
    halves = [acc_scr[u][0:HEAD_DIM] / acc_scr[u][HEAD_DIM:HEAD_DIM + 1] for u in (u0, u1)]
    return jnp.concatenate(halves, axis=0).T
def _attn_kernel(*refs, heads, part_chunks):
    n_parts = len(part_chunks)
    assert n_parts in (1, 2)
    q_ref, kv_refs = refs[0], refs[1:1 + 2 * n_parts]
    o_ref, m_scr, acc_scr, s_ring, cmax_scr = refs[1 + 2 * n_parts:]
    nch = sum(part_chunks)

    def chunk_of(which, j, rows, cols):
        first = kv_refs[which][0, jnp.minimum(j, part_chunks[0] - 1), rows, cols].astype(BF16)
        if n_parts == 1:
            return first
        j2 = jnp.clip(j - part_chunks[0], 0, part_chunks[1] - 1)
        return jnp.where(j < part_chunks[0], first, kv_refs[2 + which][0, j2, rows, cols].astype(BF16))

    nh = len(heads)
    ring = ATTN_LOOKAHEAD + 1
    assert nh % ring == 0
    m_scr[...] = jnp.full(m_scr.shape, -jnp.inf, F32)
    acc_scr[...] = jnp.zeros(acc_scr.shape, F32)

    def scores(j, item):
        h = item % nh
        kg = heads[h][0]
        kj = chunk_of(0, j, slice(None), slice(kg * LANES, (kg + 1) * LANES))
        st = _dot_nt(kj, q_ref[0, :, h * LANES:(h + 1) * LANES])
        s_ring[item % ring] = st
        cmax_scr[item % ring] = jnp.max(st, axis=0, keepdims=True)

    def chunk(j, carry):
        j_next = jnp.minimum(j + 1, nch - 1)
        for h, (kg, kh) in enumerate(heads):
            ahead = h + ATTN_LOOKAHEAD
            scores(j if ahead < nh else j_next, ahead)
            r = kg * LANES + kh * HEAD_DIM
            vt = chunk_of(1, j, slice(r, r + HEAD_DIM), slice(None))
            _softmax_step(s_ring[h % ring], cmax_scr[h % ring], vt, m_scr, acc_scr, h)
        return carry

    for item in range(ATTN_LOOKAHEAD):
        scores(0, item)
    lax.fori_loop(0, nch, chunk, 0)
    for t in range(len(heads) // 2):
        o_ref[0, :, t * LANES:(t + 1) * LANES] = _normalised_pair(acc_scr, 2 * t, 2 * t + 1).astype(o_ref.dtype)


def _attn_seqs_kernel(q_ref, k_ref, vt_ref, o_ref, s_ring, cmax_scr, *, heads):
    nh = len(heads)
    n_seq = q_ref.shape[0]
    ring = nh
    look = nh - 1

    def scores(s, item):
        h = item % nh
        kg = heads[h][0]
        st = _dot_nt(k_ref[s, 0, :, kg * LANES:(kg + 1) * LANES].astype(BF16),
                     q_ref[s, :, h * LANES:(h + 1) * LANES])
        s_ring[item % ring] = st
        cmax_scr[item % ring] = jnp.max(st, axis=0, keepdims=True)

    def sequence(s, carry):
        s_next = jnp.minimum(s + 1, n_seq - 1)
        accs = []
        for h, (kg, kh) in enumerate(heads):
            ahead = h + look
            scores(s if ahead < nh else s_next, ahead)
            r = kg * LANES + kh * HEAD_DIM
            vt = vt_ref[s, 0, r:r + HEAD_DIM, :].astype(BF16)
            vt1 = jnp.concatenate([vt, jnp.ones((SUM_ROWS, vt.shape[1]), BF16)], axis=0)
            p = jnp.exp2(s_ring[h % ring] - cmax_scr[h % ring]).astype(BF16)
            accs.append(_dot(vt1, p))
        for t in range(nh // 2):
            halves = [a[0:HEAD_DIM] / a[HEAD_DIM:HEAD_DIM + 1] for a in accs[2 * t:2 * t + 2]]
            o_ref[s, :, t * LANES:(t + 1) * LANES] = jnp.concatenate(halves, axis=0).T.astype(o_ref.dtype)
        return carry

    for item in range(look):
        scores(0, item)
    lax.fori_loop(0, n_seq, sequence, 0)


def _attention_seqs(q, k4, vt4, *, heads, name):
    B, Sq, WQ = q.shape
    ck, KW = k4.shape[2], k4.shape[3]
    nh = len(heads)
    W = nh * HEAD_DIM
    bb = min(SEQS_PER_STEP, B)
    ring = nh
    est = (2 * bb * (Sq * (WQ + W) * 2 + ck * KW * (k4.dtype.itemsize + vt4.dtype.itemsize))
           + ring * ck * Sq * 4 + 8 * nh * ck * Sq * 4)
    return pl.pallas_call(
        functools.partial(_attn_seqs_kernel, heads=heads),
        grid=(B // bb,),
        in_specs=[pl.BlockSpec((bb, Sq, WQ), lambda i: (i, 0, 0)),
                  pl.BlockSpec((bb, 1, ck, KW), lambda i: (i, 0, 0, 0)),
                  pl.BlockSpec((bb, 1, KW, ck), lambda i: (i, 0, 0, 0))],
        out_specs=pl.BlockSpec((bb, Sq, W), lambda i: (i, 0, 0)),
        out_shape=jax.ShapeDtypeStruct((B, Sq, W), BF16),
        scratch_shapes=[pltpu.VMEM((ring, ck, Sq), F32),
                        pltpu.VMEM((ring, 1, Sq), F32)],
        compiler_params=pltpu.CompilerParams(dimension_semantics=("arbitrary",),
                                             vmem_limit_bytes=_vmem_limit(est)),
        name=name,
    )(q, k4, vt4)


def _attention(q, parts, *, heads, name):
    B, Sq, WQ = q.shape
    ck, KW = parts[0][0].shape[2], parts[0][0].shape[3]
    tq = min(TQ_ATTN, Sq)
    nh = len(heads)
    W = nh * HEAD_DIM
    part_chunks = tuple(k4.shape[1] for k4, _ in parts)
    if part_chunks == (1,) and tq == Sq:
        return _attention_seqs(q, parts[0][0], parts[0][1], heads=heads, name=name)
    in_specs = [pl.BlockSpec((1, tq, WQ), lambda b, i: (b, i, 0))]
    args = [q]
    kv_bytes = 0
    for k4, vt4 in parts:
        assert k4.shape[2:] == (ck, KW) and vt4.shape[2:] == (KW, ck)
        in_specs += [pl.BlockSpec((1,) + k4.shape[1:], lambda b, i: (b, 0, 0, 0)),
                     pl.BlockSpec((1,) + vt4.shape[1:], lambda b, i: (b, 0, 0, 0))]
        args += [k4, vt4]
        kv_bytes += k4[0].size * k4.dtype.itemsize + vt4[0].size * vt4.dtype.itemsize
    est = (2 * (tq * (WQ + W) * 2 + kv_bytes) + nh * tq * (HEAD_DIM * 4 + 64) + 8 * nh * ck * tq * 4)
    return pl.pallas_call(
        functools.partial(_attn_kernel, heads=heads, part_chunks=part_chunks),
        grid=(B, Sq // tq),
        in_specs=in_specs,
        out_specs=pl.BlockSpec((1, tq, W), lambda b, i: (b, i, 0)),
        out_shape=jax.ShapeDtypeStruct((B, Sq, W), BF16),
        scratch_shapes=[pltpu.VMEM((nh, 1, tq), F32),
                        pltpu.VMEM((nh, HEAD_DIM + SUM_ROWS, tq), F32),
                        pltpu.VMEM((ATTN_LOOKAHEAD + 1, ck, tq), F32),
                        pltpu.VMEM((ATTN_LOOKAHEAD + 1, 1, tq), F32)],
        compiler_params=pltpu.CompilerParams(dimension_semantics=("arbitrary", "arbitrary"),
                                             vmem_limit_bytes=_vmem_limit(est)),
        name=name,
    )(*args)


_HEADS_GQA = tuple((0, h // (H_ATTN // KV_ATTN)) for h in range(H_ATTN))
_HEADS_MHA = tuple((h // 2, h % 2) for h in range(H_NBHD))


def _nbhd_window(r, rows):
    kr = min(NA_ROWS, rows)
    return min(max(r - kr // 2, 0), rows - kr), kr


def _nbhd_patterns(rows):
    n_groups = rows // NB_GROUP
    pats = []
    for g in (0, 1, n_groups - 1):
        r0 = g * NB_GROUP
        pats.append((r0, min(max(r0 - NA_ROWS // 2, 0), rows - NB_SLAB)))
    return pats


def _nbhd_kernel(rb_ref, q_ref, k_ref, vt_ref, kc_ref, vct_ref, o_ref,
                 bias_scr, m_scr, acc_scr, s_ring, cmax_scr, *, rows):
    hp = pl.program_id(0)
    b = pl.program_id(1)
    gs = pl.program_id(2)
    n_groups = rows // NB_GROUP
    n_dr = 2 * NA_ROWS - 1
    n_dc = 2 * NA_COLS - 1
    tq = NB_GROUP * GRID_W
    ck = tq
    pats = _nbhd_patterns(rows)
    ring = ATTN_LOOKAHEAD + 1
    n_chunks = NB_SLAB * GRID_W // ck + 1
    assert n_chunks % ring == 0 and kc_ref.shape[1] == ck
    blk_rows = NB_KV_BLOCK // GRID_W

    @pl.when((b == 0) & (gs == 0))
    def _build_bias():
        shape = (GRID_W, LANES)
        w = lax.broadcasted_iota(jnp.int32, shape, 0)
        lane = lax.broadcasted_iota(jnp.int32, shape, 1)
        cc = lane % GRID_W
        second = lane >= GRID_W
        cs = jnp.clip(w - NA_COLS // 2, 0, GRID_W - NA_COLS)
        col_ok = (cc >= cs) & (cc < cs + NA_COLS)
        dc = cc - w + (NA_COLS - 1)
        for hh in range(2):
            base = (2 * hp + hh) * (n_dr * n_dc)
            tiles = {}
            for d in range(-1, n_dr):
                acc = jnp.zeros(shape, F32)
                for j in range(n_dc):
                    va = rb_ref[base + d * n_dc + j] * LOG2E if d >= 0 else 0.0
                    vb = rb_ref[base + (d + 1) * n_dc + j] * LOG2E if d + 1 < n_dr else 0.0
                    acc = acc + jnp.where(dc == j, jnp.where(second, vb, va), 0.0)
                tiles[d] = acc
            def query_row_tile(r, kra):
                rs, kr = _nbhd_window(r, rows)
                ok_a = rs <= kra < rs + kr
                ok_b = rs <= kra + 1 < rs + kr
                if not (ok_a or ok_b):
                    return jnp.full(shape, NEG, F32)
                row_ok = (jnp.logical_not(second) if ok_a and not ok_b else
                          second if ok_b and not ok_a else None)
                ok = col_ok if row_ok is None else (col_ok & row_ok)
                return jnp.where(ok, tiles[kra - r + (NA_ROWS - 1)], NEG)

            for pi, (r0, slab0) in enumerate(pats):
                for ip in range(NB_GROUP // 2):
                    for ap in range(NB_SLAB // 2):
                        kra = slab0 + 2 * ap
                        two_rows = jnp.concatenate([query_row_tile(r0 + 2 * ip, kra),
                                                    query_row_tile(r0 + 2 * ip + 1, kra)], axis=0)
                        bias_scr[hh, pi, ap * LANES:(ap + 1) * LANES, ip * LANES:(ip + 1) * LANES] = two_rows.T

    assert (2 * n_chunks) % ring == 0
    m_scr[...] = jnp.full(m_scr.shape, -jnp.inf, F32)
    acc_scr[...] = jnp.zeros(acc_scr.shape, F32)

    def group_of(s):
        g = gs * NB_STEP_GROUPS + s
        pat = jnp.where(g == 0, 0, jnp.where(g == n_groups - 1, 2, 1))
        slab0 = jnp.clip(g * NB_GROUP - NA_ROWS // 2, 0, rows - NB_SLAB)
        return pat, slab0 // blk_rows

    per_group = 2 * n_chunks
    nb = ck // NB_KV_BLOCK

    def scores(s, item):
        hh, c = (item % per_group) // n_chunks, item % n_chunks
        qm = q_ref[0, pl.ds(pl.multiple_of(s * tq, tq), tq), hh * LANES:(hh + 1) * LANES]
        if c < n_chunks - 1:
            pat, blk0 = group_of(s)
            kc = k_ref[0, pl.ds(blk0 + c * nb, nb)].reshape(ck, LANES)
            st = _dot_nt(kc, qm) + bias_scr[hh, pat, c * ck:(c + 1) * ck, :]
        else:
            st = _dot_nt(kc_ref[0].astype(BF16), qm)
        s_ring[item % ring] = st
        cmax_scr[item % ring] = jnp.max(st, axis=0, keepdims=True)

    def group(s, carry):
        s_next = jnp.minimum(s + 1, NB_STEP_GROUPS - 1)
        for item in range(per_group):
            ahead = item + ATTN_LOOKAHEAD
            scores(s if ahead < per_group else s_next, ahead)
            hh, c = item // n_chunks, item % n_chunks
            if c < n_chunks - 1:
                _, blk0 = group_of(s)
                vt = jnp.concatenate([vt_ref[0, blk0 + c * nb + i, hh * HEAD_DIM:(hh + 1) * HEAD_DIM, :]
                                      for i in range(nb)], axis=1)
            else:
                vt = vct_ref[0, hh * HEAD_DIM:(hh + 1) * HEAD_DIM, :].astype(BF16)
            _softmax_step(s_ring[item % ring], cmax_scr[item % ring], vt, m_scr, acc_scr, 2 * s + hh)
        o_ref[0, pl.ds(pl.multiple_of(s * tq, tq), tq), :] = (
            _normalised_pair(acc_scr, 2 * s, 2 * s + 1).astype(o_ref.dtype))
        return carry

    for item in range(ATTN_LOOKAHEAD):
        scores(0, item)
    lax.fori_loop(0, NB_STEP_GROUPS, group, 0)


def _nbhd_attention(q, k4, vt4, k_ctx, vct, rel_bias_flat, *, name):
    B, S, _ = q.shape
    W = k4.shape[3]
    P = k_ctx.shape[1]
    rows = S // GRID_W
    tq = NB_GROUP * GRID_W
    nk = NB_SLAB * GRID_W
    nblk = S // NB_KV_BLOCK
    tqs = NB_STEP_GROUPS * tq
    nu = 2 * NB_STEP_GROUPS
    ring = ATTN_LOOKAHEAD + 1
    est =(2 * (2 * tqs * LANES * 2 + 2 * S * LANES * 2 + 2 * P * LANES * 4)
           + 2 * 3 * tq * nk * 4 + ring * tq * tq * 4 + 8 * tq * tq * 4 + nu * tq * 1024)
    return pl.pallas_call(
        functools.partial(_nbhd_kernel, rows=rows),
        grid=(W // LANES, B, rows // (NB_GROUP * NB_STEP_GROUPS)),
        in_specs=[pl.BlockSpec(memory_space=pltpu.SMEM),
                  pl.BlockSpec((1, tqs, 2 * LANES), lambda p, b, g: (b, g, p)),
                  pl.BlockSpec((1, nblk, NB_KV_BLOCK, LANES), lambda p, b, g: (b, 0, 0, p)),
                  pl.BlockSpec((1, nblk, LANES, NB_KV_BLOCK), lambda p, b, g: (b, 0, p, 0)),
                  pl.BlockSpec((1, P, LANES), lambda p, b, g: (b, 0, p)),
                  pl.BlockSpec((1, LANES, P), lambda p, b, g: (b, p, 0))],
        out_specs=pl.BlockSpec((1, tqs, LANES), lambda p, b, g: (b, g, p)),
        out_shape=jax.ShapeDtypeStruct((B, S, W), BF16),
        scratch_shapes=[pltpu.VMEM((2, 3, nk, tq), F32),
                        pltpu.VMEM((nu, 1, tq), F32),
                        pltpu.VMEM((nu, HEAD_DIM + SUM_ROWS, tq), F32),
                        pltpu.VMEM((ring, tq, tq), F32),
                        pltpu.VMEM((ring, 1, tq), F32)],
        compiler_params=pltpu.CompilerParams(dimension_semantics=("arbitrary",) * 3,
                                             vmem_limit_bytes=_vmem_limit(est)),
        name=name,
    )(rel_bias_flat, q, k4, vt4, k_ctx, vct)


def _split3(x):
    hi = x.astype(BF16)
    r = x - hi.astype(F32)
    mid = r.astype(BF16)
    return hi, mid, (r - mid.astype(F32)).astype(BF16)


def _mlstm_kernel(q_ref, k_ref, v_ref, g_ref, ob_ref, s0_ref, m0_ref, on_ref,
                  out_ref, sf_ref, mf_ref,
                  h_scr, nat_scr, rows_scr, stat_scr, mprev_scr, un_scr, st_scr, *, nc, grp_a, grp):
    d = pl.program_id(1)
    L = L_CHUNK
    bb = q_ref.shape[0]
    NP = H_MLSTM // 2
    row = lax.broadcasted_iota(jnp.int32, (L, L), 0)
    col = lax.broadcasted_iota(jnp.int32, (L, L), 1)
    sign = 1 - 2 * d
    mask = (col - row) * sign <= 0
    maskb = mask.astype(BF16)
    mask3 = jnp.concatenate([maskb, maskb, maskb], axis=1)
    lane = lax.broadcasted_iota(jnp.int32, (L, LANES), 1)
    lo = lane < HEAD_DIM
    top = row < HEAD_DIM
    row2 = lax.broadcasted_iota(jnp.int32, (L, 2 * LANES), 0)
    col2 = lax.broadcasted_iota(jnp.int32, (L, 2 * LANES), 1)
    keep_state = (row2 < HEAD_DIM) == ((col2 % LANES) < HEAD_DIM)
    top2 = row2 < HEAD_DIM
    ones_b = jnp.ones((L, LANES), BF16)
    ones_lo = lo.astype(BF16)
    ones_hi = jnp.logical_not(lo).astype(BF16)

    def chunk_rows(c):
        return pl.ds(pl.multiple_of(c * L, L), L)

    def tokens(c):
        if bb == 1:
            return 0, chunk_rows(c)
        return c // nc, pl.ds(pl.multiple_of((c % nc) * L, L), L)

    def pass_a(it, carry):
        cs = [it * grp_a + u for u in range(grp_a)]
        gts = [g_ref[tokens(c)[0], tokens(c)[1], :] for c in cs]
        bns = [_dot(mask3, jnp.concatenate(_split3(gt), axis=0)) for gt in gts]
        a_all = []
        for c, gt, bn in zip(cs, gts, bns):
            nat = jnp.where(lane < H_MLSTM, gt, bn)
            nat_scr[chunk_rows(c), :] = nat * (-LOG2E)
            nat_t = nat.T
            b_rows = nat_t[H_MLSTM:2 * H_MLSTM]
            c_rows = nat_t[0:H_MLSTM] - b_rows
            rows_scr[c] = jnp.concatenate([c_rows * LOG2E, b_rows], axis=0)
            c_max = jnp.max(c_rows, axis=1, keepdims=True)
            b_tot = jnp.where(d == 0, b_rows[:, L - 1:L], b_rows[:, 0:1])
            stat_scr[c] = jnp.concatenate([jnp.broadcast_to(c_max, (H_MLSTM, LANES)),
                                           jnp.broadcast_to(b_tot, (H_MLSTM, LANES))], axis=0)
            a_all.append(jnp.exp(c_rows - c_max))
        for c, a_rows in zip(cs, a_all):
            sq, rows = tokens(c)
            for p in range(NP):
                lanes = slice(p * LANES, (p + 1) * LANES)
                k_t = k_ref[sq, rows, lanes].astype(F32).T
                a_sel = jnp.where(top, a_rows[2 * p:2 * p + 1], a_rows[2 * p + 1:2 * p + 2])
                vv = jnp.concatenate([v_ref[sq, rows, lanes], ones_b], axis=1)
                un = _dot((k_t * a_sel).astype(BF16), vv)
                un_scr[c, p] = jnp.where(keep_state, un, 0.0)
        return carry

    lax.fori_loop(0, bb * nc // grp_a, pass_a, 0)

    def pass_b(ci, m, sq):
        c = sq * nc + jnp.where(d == 0, ci, nc - 1 - ci)
        st = stat_scr[c]
        c_max, b_tot = st[0:H_MLSTM], st[H_MLSTM:]
        m_new = jnp.maximum(b_tot + m, b_tot + c_max)
        d_old = jnp.exp(b_tot + m - m_new)
        d_new = jnp.exp(b_tot + c_max - m_new)
        mprev_scr[c] = jnp.concatenate([m, m], axis=0) * LOG2E
        for p in range(NP):
            def rows_of(t, p=p):
                even = jnp.concatenate([t[2 * p:2 * p + 1]] * 2, axis=1)
                odd = jnp.concatenate([t[2 * p + 1:2 * p + 2]] * 2, axis=1)
                return jnp.where(top2, even, odd)
            s_prev = st_scr[p]
            st_scr[p] = rows_of(d_old) * s_prev + rows_of(d_new) * un_scr[c, p]
            un_scr[c, p] = s_prev
        return m_new

    def recurrence(sq, carry):
        st_scr[...] = s0_ref[sq, 0]
        m_fin = lax.fori_loop(0, nc, functools.partial(pass_b, sq=sq), m0_ref[sq, 0][0:H_MLSTM])
        sf_ref[sq, 0] = st_scr[...]
        mf_ref[sq, 0] = jnp.concatenate([m_fin, m_fin], axis=0)
        return carry

    lax.fori_loop(0, bb, recurrence, 0)

    def pass_c(it, carry):
        cs = [it * grp + u for u in range(grp)]
        units = [(u, p) for u in range(grp) for p in range(NP)]
        early = {}
        for u, p in units:
            c = cs[u]
            sq, rows = tokens(c)
            lanes = slice(p * LANES, (p + 1) * LANES)
            qp = q_ref[sq, rows, lanes]
            kp = k_ref[sq, rows, lanes]
            s_in = un_scr[c, p].astype(BF16)
            qms = [jnp.where(lo if hh == 0 else jnp.logical_not(lo), qp, jnp.zeros_like(qp)) for hh in range(2)]
            early[u, p] = ([_dot_nt(qm, kp) for qm in qms],
                           _dot(qp, s_in))
        mid = {}
        for u, p in units:
            r_t = rows_scr[cs[u]]
            m_in = mprev_scr[cs[u]]
            for hh in range(2):
                h = 2 * p + hh
                cm = jnp.where(mask, r_t[h:h + 1, :], -jnp.inf)
                m_prev = m_in[h:h + 1, :]
                mu = jnp.maximum(jnp.broadcast_to(jnp.max(cm, axis=1, keepdims=True), (L, LANES)), m_prev)
                w = early[u, p][0][hh] * jnp.exp2(cm - mu)
                mid[u, p, hh] = (w.astype(BF16), mu, m_prev)
        for u, p in units:
            sq, rows = tokens(cs[u])
            lanes = slice(p * LANES, (p + 1) * LANES)
            vp = v_ref[sq, rows, lanes]
            zero = jnp.zeros_like(vp)
            vv = jnp.concatenate([jnp.concatenate([jnp.where(lo, vp, zero), ones_lo], axis=1),
                                  jnp.concatenate([jnp.where(lo, zero, vp), ones_hi], axis=1)], axis=0)
            w2 = jnp.concatenate([mid[u, p, 0][0], mid[u, p, 1][0]], axis=1)
            nd = _dot(w2, vv)
            nat = nat_scr[chunk_rows(cs[u]), :]
            nb = [jnp.broadcast_to(nat[:, H_MLSTM + 2 * p + hh:H_MLSTM + 2 * p + hh + 1], (L, LANES))
                  for hh in range(2)]
            fs = early[u, p][1]
            mu = jnp.where(lo, mid[u, p, 0][1], mid[u, p, 1][1])
            m_prev = jnp.where(lo[0:1], mid[u, p, 0][2], mid[u, p, 1][2])
            w_inter = jnp.exp2(m_prev - mu)
            den = nd[:, LANES:] + w_inter * fs[:, LANES:]
            den = jnp.maximum(jnp.abs(den), jnp.exp2(jnp.where(lo, nb[0], nb[1]) - mu))
            h_scr[d, chunk_rows(cs[u]), lanes] = (nd[:, :LANES] + w_inter * fs[:, :LANES]) / den
        return carry

    lax.fori_loop(0, bb * nc // grp, pass_c, 0)

    @pl.when(d == 1)
    def _finish():
        def rows_block(c, carry):
            sq, rows = tokens(c)
            flat = chunk_rows(c)
            for p in range(NP):
                lanes = slice(p * LANES, (p + 1) * LANES)
                hn = _pair_rms(h_scr[0, flat, lanes] + h_scr[1, flat, lanes], on_ref[:, lanes])
                out_ref[sq, rows, lanes] = (_sigmoid(ob_ref[sq, rows, lanes]) * hn).astype(out_ref.dtype)
            return carry

        lax.fori_loop(0, bb * nc, rows_block, 0)


def _mlstm(q, k, v, gates, ob, s0, m0, out_norm, *, name):
    assert L_CHUNK == LANES
    B, S, W = q.shape
    nc = S // L_CHUNK
    bb = max(1, min(B, MLSTM_STEP_CHUNKS // nc))
    ncb, sb = bb * nc, bb * S
    grp = min(MLSTM_GROUP, ncb)
    grp_a = min(MLSTM_GROUP_A, ncb)
    npair = H_MLSTM // 2
    seq = lambda b, d: (b, 0, 0)
    est = (2 * (3 * sb * W * 2 + sb * LANES * 4 + sb * W * 4 + sb * W * 2) + 2 * sb * W * 4 + sb * LANES * 4
           + ncb * npair * LANES * 2 * LANES * 4 + 12 * 1024 * 1024)
    return pl.pallas_call(
        functools.partial(_mlstm_kernel, nc=nc, grp_a=grp_a, grp=grp),
        grid=(B // bb, 2),
        in_specs=[pl.BlockSpec((bb, S, W), seq), pl.BlockSpec((bb, S, W), seq), pl.BlockSpec((bb, S, W), seq),
                  pl.BlockSpec((bb, S, LANES), lambda b, d: (b, 0, d)),
                  pl.BlockSpec((bb, S, W), seq),
                  pl.BlockSpec((bb, 1, npair, LANES, 2 * LANES), lambda b, d: (b, d, 0, 0, 0)),
                  pl.BlockSpec((bb, 1, 8, LANES), lambda b, d: (b, d, 0, 0)),
                  pl.BlockSpec((1, W), lambda b, d: (0, 0))],
        out_specs=[pl.BlockSpec((bb, S, W), seq),
                   pl.BlockSpec((bb, 1, npair, LANES, 2 * LANES), lambda b, d: (b, d, 0, 0, 0)),
                   pl.BlockSpec((bb, 1, 8, LANES), lambda b, d: (b, d, 0, 0))],
        out_shape=[jax.ShapeDtypeStruct((B, S, W), BF16),
                   jax.ShapeDtypeStruct((B, 2, npair, LANES, 2 * LANES), F32),
                   jax.ShapeDtypeStruct((B, 2, 8, LANES), F32)],
        scratch_shapes=[pltpu.VMEM((2, sb, W), F32),
                        pltpu.VMEM((sb, LANES), F32),
                        pltpu.VMEM((ncb, 8, L_CHUNK), F32),
                        pltpu.VMEM((ncb, 8, LANES), F32),
                        pltpu.VMEM((ncb, 8, LANES), F32),
                        pltpu.VMEM((ncb, npair, LANES, 2 * LANES), F32),
                        pltpu.VMEM((npair, LANES, 2 * LANES), F32)],
        compiler_params=pltpu.CompilerParams(dimension_semantics=("arbitrary", "arbitrary"),
                                             vmem_limit_bytes=_vmem_limit(est)),
        name=name,
    )(q, k, v, gates, ob, s0, m0, out_norm)


def _outproj_kernel(a_ref, b_ref, c_ref, x_ref, mod_ref, w_ref, gpost_ref, gpre_ref, x1_ref, h2_ref):
    mod = mod_ref[0]
    gt1 = mod[:, 2 * D_MODEL:3 * D_MODEL]
    sh2 = mod[:, 3 * D_MODEL:4 * D_MODEL]
    sc2 = mod[:, 4 * D_MODEL:5 * D_MODEL]
    tr = x_ref.shape[0] // ROW_SPLIT
    pieces = [pl.ds(s * tr, tr) for s in range(ROW_SPLIT)]
    mos = [_dot(jnp.concatenate([a_ref[r, :], b_ref[r, :], c_ref[r, :]], axis=1), w_ref[...]) for r in pieces]
    for r, mo in zip(pieces, mos):
        x1 = x_ref[r, :] + gt1 * _rms(mo, gpost_ref[...])
        x1_ref[r, :] = x1
        h2_ref[r, :] = (_rms(x1, gpre_ref[...]) * (1.0 + sc2) + sh2).astype(h2_ref.dtype)


def _outproj(oa, ob, oc, x, mods, w_out, g_post, g_pre, *, rows_per_cond, name):
    T = x.shape[0]
    tm = TM_PROJ
    bpc = rows_per_cond // tm
    row = lambda i: (i, 0)
    const = lambda i: (0, 0)
    est = 2 * (tm * D_MODEL * (2 + 4 + 4 + 2) + D_MODEL * D_MODEL * 2) + 4 * tm * D_MODEL * 4
    return pl.pallas_call(
        _outproj_kernel,
        grid=(T // tm,),
        in_specs=[pl.BlockSpec((tm, W_ATTN), row), pl.BlockSpec((tm, W_MLSTM), row),
                  pl.BlockSpec((tm, W_NBHD), row), pl.BlockSpec((tm, D_MODEL), row),
                  pl.BlockSpec((1, 1, N_MOD * D_MODEL), lambda i: (i // bpc, 0, 0)),
                  pl.BlockSpec((D_MODEL, D_MODEL), const),
                  pl.BlockSpec((1, D_MODEL), const), pl.BlockSpec((1, D_MODEL), const)],
        out_specs=[pl.BlockSpec((tm, D_MODEL), row), pl.BlockSpec((tm, D_MODEL), row)],
        out_shape=[jax.ShapeDtypeStruct((T, D_MODEL), F32), jax.ShapeDtypeStruct((T, D_MODEL), BF16)],
        compiler_params=pltpu.CompilerParams(dimension_semantics=("arbitrary",),
                                             vmem_limit_bytes=_vmem_limit(est)),
        name=name,
    )(oa, ob, oc, x, mods, w_out, g_post, g_pre)


def _ffn_kernel(h_ref, x_ref, mod_ref, wu_ref, wd_ref, g_ref, o_ref, acc_ref):
    j = pl.program_id(1)

    @pl.when(j == 0)
    def _zero():
        acc_ref[...] = jnp.zeros_like(acc_ref)

    u = jnp.maximum(_dot(h_ref[...], wu_ref[...]), 0.0)
    acc_ref[...] += _dot((u * u).astype(BF16), wd_ref[...])

    @pl.when(j == pl.num_programs(1) - 1)
    def _finish():
        gt2 = mod_ref[0][:, 5 * D_MODEL:6 * D_MODEL]
        o_ref[...] = x_ref[...] + gt2 * _rms(acc_ref[...], g_ref[...])


def _ffn(h2, x1, mods, w_up, w_down, g_post, *, rows_per_cond, name):
    T = x1.shape[0]
    tm, tf = TM_FFN, TF_FFN
    tm = min(tm, rows_per_cond)
    bpc = rows_per_cond // tm
    est = 2 * (tm * D_MODEL * (2 + 4 + 4) + 2 * D_MODEL * tf * 2) + tm * D_MODEL * 4 + 3 * tm * tf * 4
    return pl.pallas_call(
        _ffn_kernel,
        grid=(T // tm, D_FF // tf),
        in_specs=[pl.BlockSpec((tm, D_MODEL), lambda i, j: (i, 0)),
                  pl.BlockSpec((tm, D_MODEL), lambda i, j: (i, 0)),
                  pl.BlockSpec((1, 1, N_MOD * D_MODEL), lambda i, j: (i // bpc, 0, 0)),
                  pl.BlockSpec((D_MODEL, tf), lambda i, j: (0, j)),
                  pl.BlockSpec((tf, D_MODEL), lambda i, j: (j, 0)),
                  pl.BlockSpec((1, D_MODEL), lambda i, j: (0, 0))],
        out_specs=pl.BlockSpec((tm, D_MODEL), lambda i, j: (i, 0)),
        out_shape=jax.ShapeDtypeStruct((T, D_MODEL), F32),
        scratch_shapes=[pltpu.VMEM((tm, D_MODEL), F32)],
        compiler_params=pltpu.CompilerParams(dimension_semantics=("arbitrary", "arbitrary"),
                                             vmem_limit_bytes=_vmem_limit(est)),
        name=name,
    )(h2, x1, mods, w_up, w_down, g_post)


def _pad_w_in(w_in_l):
    o = W_ATTN + 2 * W_KV + 4 * W_MLSTM
    pre, gates, post = w_in_l[:, :o], w_in_l[:, o:o + N_GATES], w_in_l[:, o + N_GATES:]
    z = jnp.zeros((D_MODEL, LANES - 2 * H_MLSTM), w_in_l.dtype)
    return jnp.concatenate([pre, gates[:, :2 * H_MLSTM], z, gates[:, 2 * H_MLSTM:], z, post],
                           axis=1).astype(BF16)


def _pad_gate_bias(gb_l):
    z = jnp.zeros((LANES - 2 * H_MLSTM,), gb_l.dtype)
    return jnp.concatenate([gb_l[:2 * H_MLSTM], z, gb_l[2 * H_MLSTM:], z]).reshape(1, 2 * LANES)


def _rope_tables(S):
    quarter = HEAD_DIM // 4
    pos = np.arange(S)
    inv_freq = np.float32(ROPE_THETA) ** (-np.arange(quarter, dtype=np.float32) / np.float32(quarter))

    def tabs(p):
        ang = p.astype(np.float32)[:, None] * inv_freq[None, :]
        return np.cos(ang), np.sin(ang)

    cr, sr = tabs(pos // GRID_W)
    cc, sc = tabs(pos % GRID_W)
    cos = np.concatenate([cr, cr, cc, cc], axis=1)
    sin = np.concatenate([-sr, sr, -sc, sc], axis=1)
    return jnp.asarray(np.tile(cos, (1, 2)), F32), jnp.asarray(np.tile(sin, (1, 2)), F32)


def _pack_state(C, n, m):
    B = C.shape[0]
    Cp = C.reshape(B, 2, 2, 2, HEAD_DIM, HEAD_DIM)
    z = jnp.zeros_like(Cp[:, :, :, 0])
    top = jnp.concatenate([Cp[:, :, :, 0], z], axis=-1)
    bot = jnp.concatenate([z, Cp[:, :, :, 1]], axis=-1)
    Cbd = jnp.concatenate([top, bot], axis=-2)
    n_rep = jnp.broadcast_to(n.reshape(B, 2, 2, LANES, 1), (B, 2, 2, LANES, LANES))
    same_head = (jnp.arange(LANES)[:, None] < HEAD_DIM) == (jnp.arange(LANES)[None, :] < HEAD_DIM)
    n_rep = jnp.where(same_head, n_rep, 0.0)
    m_rows = jnp.broadcast_to(m[..., None], m.shape + (LANES,))
    return jnp.concatenate([Cbd, n_rep], axis=-1), jnp.concatenate([m_rows, m_rows], axis=-2)


def _unpack_state(s_p, m_p):
    B = s_p.shape[0]
    c_even = s_p[:, :, :, :HEAD_DIM, :HEAD_DIM]
    c_odd = s_p[:, :, :, HEAD_DIM:, HEAD_DIM:LANES]
    C = jnp.stack([c_even, c_odd], axis=3).reshape(B, 2, H_MLSTM, HEAD_DIM, HEAD_DIM)
    n = jnp.concatenate([s_p[..., :HEAD_DIM, LANES], s_p[..., HEAD_DIM:, LANES + HEAD_DIM]], axis=-1)
    return C, n.reshape(B, 2, H_MLSTM, HEAD_DIM), m_p[:, :, :H_MLSTM, 0]


def _layer(x, mods, lw, *, B, S, cond_rows, rope_tabs, ctx_cache, state, name):
    T = B * S
    kv_dtype = BF16 if ctx_cache is not None else F32
    ck_a = min(CK_ATTN, S)
    pr = _inproj(x, mods, lw["g_pre_mix"], lw["w_in"], lw["q_norm"], lw["k_norm"], lw["gate_bias"],
                 rope_tabs, rows_per_cond=cond_rows, kv_dtype=kv_dtype,
                 vt_blocks=(ck_a, NB_KV_BLOCK), name=name + "_inproj")
    seq = lambda a: a.reshape(B, S, a.shape[-1])
    qa, ka, va = seq(pr["qa"]), seq(pr["ka"]), seq(pr["va"])
    qc, kc, vc = seq(pr["qc"]), seq(pr["kc"]), seq(pr["vc"])
    chunks = lambda a, n: a.reshape(B, S // n, n, a.shape[-1])
    vat = pr["vat"].reshape(B, S // ck_a, W_KV, ck_a)
    vct = pr["vct"].reshape(B, S // NB_KV_BLOCK, W_NBHD, NB_KV_BLOCK)
    if ctx_cache is None:
        out_a = _attention(qa, [(chunks(ka, ck_a), vat)], heads=_HEADS_GQA, name=name + "_attn_a")
        out_c = _attention(qc, [(chunks(kc, NB_KV_BLOCK), vct)], heads=_HEADS_MHA, name=name + "_attn_c")
    else:
        ck_c, cv_c = ctx_cache[2], ctx_cache[3]
        P = ck_c.shape[1]
        ctx_k = ctx_cache[0].reshape(B, P // ck_a, ck_a, W_KV)
        ctx_vt = ctx_cache[1].reshape(B, P // ck_a, ck_a, W_KV).transpose(0, 1, 3, 2)
        out_a = _attention(qa, [(chunks(ka, ck_a), vat), (ctx_k, ctx_vt)], heads=_HEADS_GQA,
                           name=name + "_attn_a")
        out_c = _nbhd_attention(qc, chunks(kc, NB_KV_BLOCK), vct, ck_c, cv_c.transpose(0, 2, 1),
                                lw["rel_bias"], name=name + "_attn_c")
    s0, m0 = state
    out_b, sf, mf = _mlstm(seq(pr["qb"]), seq(pr["kb"]), seq(pr["vb"]), seq(pr["g"]), seq(pr["ob"]),
                           s0, m0, lw["out_norm"], name=name + "_mlstm")
    x1, h2 = _outproj(out_a.reshape(T, W_ATTN), out_b.reshape(T, W_MLSTM), out_c.reshape(T, W_NBHD),
                      x, mods, lw["w_out"], lw["g_post_mix"], lw["g_pre_ffn"],
                      rows_per_cond=cond_rows, name=name + "_outproj")
    x2 = _ffn(h2, x1, mods, lw["w_up"], lw["w_down"], lw["g_post_ffn"],
              rows_per_cond=cond_rows, name=name + "_ffn")
    return x2, (ka, va, kc, vc, sf, mf)


def kernel(x_prompt, x_sample, c, cache_k_attn, cache_v_attn, cache_k_nbhd, cache_v_nbhd, state_mlstm_C, state_mlstm_n, state_mlstm_m, c_ctx, w_ada, b_ada, g_pre_mix, g_post_mix, g_pre_ffn, g_post_ffn, w_in, q_norm_attn, k_norm_attn, mlstm_gate_bias, mlstm_out_norm, nbhd_rel_bias, w_out, w_ffn_up, w_ffn_down):
    Bc, Sc, _ = x_prompt.shape
    Bl, Sl, _ = x_sample.shape
    P = cache_k_attn.shape[2]
    n_cond = 8
    cond = jnp.concatenate([c_ctx[None, :], c, jnp.zeros((n_cond - 1 - Bl, D_MODEL), F32)], axis=0)
    mods_all = _modulation(cond, w_ada, b_ada)

    layers = []
    for l in range(DEPTH):
        layers.append(dict(
            w_in=_pad_w_in(w_in[l]),
            w_out=w_out[l].astype(BF16),
            w_up=w_ffn_up[l].astype(BF16),
            w_down=w_ffn_down[l].astype(BF16),
            g_pre_mix=g_pre_mix[l].reshape(1, D_MODEL), g_post_mix=g_post_mix[l].reshape(1, D_MODEL),
            g_pre_ffn=g_pre_ffn[l].reshape(1, D_MODEL), g_post_ffn=g_post_ffn[l].reshape(1, D_MODEL),
            q_norm=jnp.tile(q_norm_attn[l], 2).reshape(1, LANES),
            k_norm=jnp.tile(k_norm_attn[l], 2).reshape(1, LANES),
            gate_bias=_pad_gate_bias(mlstm_gate_bias[l]),
            out_norm=mlstm_out_norm[l].reshape(1, W_MLSTM),
            rel_bias=nbhd_rel_bias[l].reshape(-1),
        ))

    xp = x_prompt.reshape(Bc * Sc, D_MODEL)
    zero_state = (jnp.zeros((Bc, 2, H_MLSTM // 2, LANES, 2 * LANES), F32), jnp.zeros((Bc, 2, 8, LANES), F32))
    ctx = []
    for l in range(DEPTH):
        mods = mods_all[l, 0:1].reshape(1, 1, N_MOD * D_MODEL)
        xp, extras = _layer(xp, mods, layers[l], B=Bc, S=Sc, cond_rows=Bc * Sc, rope_tabs=None,
                            ctx_cache=None, state=zero_state, name=f"ctx{l}")
        ctx.append(extras)
    new_k_attn = jnp.stack([e[0].reshape(Bc, Sc, KV_ATTN, HEAD_DIM) for e in ctx], axis=1)
    new_v_attn = jnp.stack([e[1].reshape(Bc, Sc, KV_ATTN, HEAD_DIM) for e in ctx], axis=1)
    new_k_nbhd = jnp.stack([e[2].reshape(Bc, Sc, H_NBHD, HEAD_DIM) for e in ctx], axis=1)
    new_v_nbhd = jnp.stack([e[3].reshape(Bc, Sc, H_NBHD, HEAD_DIM) for e in ctx], axis=1)
    states = [_unpack_state(e[4], e[5]) for e in ctx]
    new_C = jnp.stack([s[0] for s in states], axis=1)
    new_n = jnp.stack([s[1] for s in states], axis=1)
    new_m = jnp.stack([s[2] for s in states], axis=1)

    xs = x_sample.reshape(Bl * Sl, D_MODEL)
    rope_tabs = _rope_tables(Sl)
    for l in range(DEPTH):
        mods = mods_all[l, 1:1 + Bl].reshape(Bl, 1, N_MOD * D_MODEL)
        cache = (cache_k_attn[:, l].reshape(Bl, P, W_KV), cache_v_attn[:, l].reshape(Bl, P, W_KV),
                 cache_k_nbhd[:, l].reshape(Bl, P, W_NBHD), cache_v_nbhd[:, l].reshape(Bl, P, W_NBHD))
        state = _pack_state(state_mlstm_C[:, l], state_mlstm_n[:, l], state_mlstm_m[:, l])
        xs, _ = _layer(xs, mods, layers[l], B=Bl, S=Sl, cond_rows=Sl, rope_tabs=rope_tabs,
                       ctx_cache=cache, state=state, name=f"lat{l}")

    return (xp.reshape(Bc, Sc, D_MODEL), xs.reshape(Bl, Sl, D_MODEL),
            new_k_attn, new_v_attn, new_k_nbhd, new_v_nbhd, new_C, new_n, new_m)
```

```python
import functools

import jax
import jax.numpy as jnp
import numpy as np
from jax import lax
from jax.experimental import pallas as pl
from jax.experimental.pallas import tpu as pltpu

F32 = jnp.float32
BF16 = jnp.bfloat16

D_MODEL = 1024
DEPTH = 2
GRID_W = 64
HEAD_DIM = 64
H_ATTN = 6
KV_ATTN = 2
H_MLSTM = 4
H_NBHD = 6
D_FF = 4 * D_MODEL
NA_ROWS = 8
NA_COLS = 16
ROPE_THETA = 10000.0
EPS = 1e-6
N_MOD = 6
W_ATTN = H_ATTN * HEAD_DIM
W_KV = KV_ATTN * HEAD_DIM
W_MLSTM = H_MLSTM * HEAD_DIM
W_NBHD = H_NBHD * HEAD_DIM
N_GATES = 4 * H_MLSTM

LANES = 128
V7X_VMEM_BYTES = 64 * 1024 * 1024
VMEM_CAP_BYTES = 56 * 1024 * 1024

TM_PROJ = 512
PROJ_CHUNK = 512
TM_FFN = 1024
TF_FFN = 1024
ROW_SPLIT = 4
TQ_ATTN = 512
CK_ATTN = 512
ATTN_LOOKAHEAD = 2
SEQS_PER_STEP = 8
L_CHUNK = 128
MLSTM_GROUP = 2
MLSTM_GROUP_A = 8
MLSTM_STEP_CHUNKS = 16
NB_GROUP = 8
NB_SLAB = 16
NB_STEP_GROUPS = 4
NB_KV_BLOCK = 256
NEG = -1e30
LOG2E = 1.4426950408889634

_COLS = {}
_off = 0
for _name, _w in (("qa", W_ATTN), ("ka", W_KV), ("va", W_KV), ("qb", W_MLSTM), ("kb", W_MLSTM),
                  ("vb", W_MLSTM), ("ob", W_MLSTM), ("gf", LANES), ("gb", LANES),
                  ("qc", W_NBHD), ("kc", W_NBHD), ("vc", W_NBHD)):
    _COLS[_name] = (_off, _off + _w)
    _off += _w
IN_PAD = _off


def _vmem_limit(nbytes):
    return int(min(max(nbytes, 16 * 1024 * 1024), VMEM_CAP_BYTES))


def _dot(a, b):
    return jnp.dot(a, b, preferred_element_type=F32)


def _dot_nt(a, b):
    return lax.dot_general(a, b, (((1,), (1,)), ((), ())), preferred_element_type=F32)


def _lane_lo(shape):
    return (lax.broadcasted_iota(jnp.int32, shape, len(shape) - 1) % LANES) < HEAD_DIM


def _rms(x, g):
    ms = jnp.mean(x * x, axis=-1, keepdims=True)
    return (x * lax.rsqrt(ms + EPS)) * g


def _pair_rms(x, g):
    lo = _lane_lo(x.shape)
    x2 = x * x
    s_lo = jnp.sum(jnp.where(lo, x2, 0.0), axis=-1, keepdims=True)
    s_hi = jnp.sum(jnp.where(lo, 0.0, x2), axis=-1, keepdims=True)
    r = jnp.where(lo, lax.rsqrt(s_lo / HEAD_DIM + EPS), lax.rsqrt(s_hi / HEAD_DIM + EPS))
    return (x * r) * g


def _sigmoid(x):
    return 1.0 / (1.0 + jnp.exp(-x))


def _mods_kernel(c_ref, w_ref, b_ref, o_ref):
    c = c_ref[...]
    s = (c * _sigmoid(c)).astype(BF16)
    o_ref[0] = _dot(s, w_ref[0].astype(BF16)) + b_ref[0]


def _modulation(cond, w_ada, b_ada):
    n = cond.shape[0]
    tn = D_MODEL
    return pl.pallas_call(
        _mods_kernel,
        grid=(DEPTH, N_MOD * D_MODEL // tn),
        in_specs=[pl.BlockSpec((n, D_MODEL), lambda l, j: (0, 0)),
                  pl.BlockSpec((1, D_MODEL, tn), lambda l, j: (l, 0, j)),
                  pl.BlockSpec((1, 1, tn), lambda l, j: (l, 0, j))],
        out_specs=pl.BlockSpec((1, n, tn), lambda l, j: (l, 0, j)),
        out_shape=jax.ShapeDtypeStruct((DEPTH, n, N_MOD * D_MODEL), F32),
        compiler_params=pltpu.CompilerParams(
            dimension_semantics=("arbitrary", "arbitrary"),
            vmem_limit_bytes=_vmem_limit(4 * D_MODEL * tn * 4)),
        name="modulation",
    )(cond, w_ada, b_ada.reshape(DEPTH, 1, N_MOD * D_MODEL))


def _inproj_kernel(*refs, rope):
    if rope:
        (x_ref, mod_ref, g_ref, w_ref, qn_ref, kn_ref, gbias_ref, cos_ref, sin_ref,
         qa_ref, ka_ref, va_ref, qb_ref, kb_ref, vb_ref, ob_ref, gate_ref,
         qc_ref, kc_ref, vc_ref, vat_ref, vct_ref) = refs
    else:
        (x_ref, mod_ref, g_ref, w_ref, qn_ref, kn_ref, gbias_ref,
         qa_ref, ka_ref, va_ref, qb_ref, kb_ref, vb_ref, ob_ref, gate_ref,
         qc_ref, kc_ref, vc_ref, vat_ref, vct_ref) = refs
    x = x_ref[...]
    mod = mod_ref[0]
    sh1 = mod[:, 0:D_MODEL]
    sc1 = mod[:, D_MODEL:2 * D_MODEL]
    hb = (_rms(x, g_ref[...]) * (1.0 + sc1) + sh1).astype(BF16)

    z = [_dot(hb, w_ref[:, c0:c0 + PROJ_CHUNK]) for c0 in range(0, IN_PAD, PROJ_CHUNK)]

    def proj(name, j=0, w=None):
        lo, hi = _COLS[name]
        lo = lo + j
        hi = hi if w is None else lo + w
        parts = []
        while lo < hi:
            c, o = divmod(lo, PROJ_CHUNK)
            n = min(hi - lo, PROJ_CHUNK - o)
            parts.append(z[c][:, o:o + n])
            lo += n
        return parts[0] if len(parts) == 1 else jnp.concatenate(parts, axis=1)

    scale = HEAD_DIM ** -0.5
    q_scale = scale * LOG2E

    def rotary(t):
        first = (lax.broadcasted_iota(jnp.int32, t.shape, 1) % 32) < 16
        partner = jnp.where(first, pltpu.roll(t, LANES - 16, 1), pltpu.roll(t, 16, 1))
        return t * cos_ref[...] + partner * sin_ref[...]

    def store_queries(t, j, q_ref, kv_half):
        lo = _lane_lo(t.shape)
        for e in range(2):
            h = 2 * j + e
            src = t if e == kv_half[h] else pltpu.roll(t, HEAD_DIM, 1)
            keep = lo if kv_half[h] == 0 else jnp.logical_not(lo)
            q_ref[:, h * LANES:(h + 1) * LANES] = jnp.where(keep, src, 0.0).astype(q_ref.dtype)

    for j in range(W_ATTN // LANES):
        t = _pair_rms(proj("qa", j * LANES, LANES), qn_ref[...])
        if rope:
            t = rotary(t)
        store_queries(t * q_scale, j, qa_ref, [kh for _, kh in _HEADS_GQA])
    t = _pair_rms(proj("ka"), kn_ref[...])
    if rope:
        t = rotary(t)
    ka_ref[...] = t.astype(ka_ref.dtype)

    def store_v(v, v_ref, vt_ref):
        v_ref[...] = v.astype(v_ref.dtype)
        nblk, _, blk = vt_ref.shape
        for u in range(nblk):
            vt_ref[u] = v[u * blk:(u + 1) * blk].T.astype(vt_ref.dtype)

    store_v(proj("va"), va_ref, vat_ref)
    qb_ref[...] = proj("qb").astype(qb_ref.dtype)
    kb_ref[...] = (proj("kb") * scale).astype(kb_ref.dtype)
    vb_ref[...] = proj("vb").astype(vb_ref.dtype)
    ob_ref[...] = proj("ob").astype(ob_ref.dtype)
    for j, name in enumerate(("gf", "gb")):
        gt = proj(name) + gbias_ref[:, j * LANES:(j + 1) * LANES]
        lane = lax.broadcasted_iota(jnp.int32, gt.shape, 1)
        is_f = (lane >= H_MLSTM) & (lane < 2 * H_MLSTM)
        logsig = jnp.minimum(gt, 0.0) - jnp.log1p(jnp.exp(-jnp.abs(gt)))
        gate_ref[:, j * LANES:(j + 1) * LANES] = jnp.where(is_f, logsig, gt)
    for j in range(W_NBHD // LANES):
        store_queries(proj("qc", j * LANES, LANES) * q_scale, j, qc_ref, [kh for _, kh in _HEADS_MHA])
    kc_ref[...] = proj("kc").astype(kc_ref.dtype)
    store_v(proj("vc"), vc_ref, vct_ref)


def _inproj(x, mods, g_pre, w_in_p, qn, kn, gbias, rope_tabs, *, rows_per_cond, kv_dtype, vt_blocks, name):
    T = x.shape[0]
    tm = TM_PROJ
    bpc = rows_per_cond // tm
    rope = rope_tabs is not None
    row = lambda i: (i, 0)
    const = lambda i: (0, 0)
    in_specs = [pl.BlockSpec((tm, D_MODEL), row),
                pl.BlockSpec((1, 1, N_MOD * D_MODEL), lambda i: (i // bpc, 0, 0)),
                pl.BlockSpec((1, D_MODEL), const),
                pl.BlockSpec((D_MODEL, IN_PAD), const),
                pl.BlockSpec((1, LANES), const),
                pl.BlockSpec((1, LANES), const),
                pl.BlockSpec((1, 2 * LANES), const)]
    args = [x, mods, g_pre, w_in_p, qn, kn, gbias]
    if rope:
        nblk = rope_tabs[0].shape[0] // tm
        in_specs += [pl.BlockSpec((tm, LANES), lambda i: (i % nblk, 0))] * 2
        args += list(rope_tabs)
    widths = [("qa", H_ATTN * LANES, BF16), ("ka", W_KV, kv_dtype), ("va", W_KV, kv_dtype),
              ("qb", W_MLSTM, BF16), ("kb", W_MLSTM, BF16), ("vb", W_MLSTM, BF16),
              ("ob", W_MLSTM, F32), ("g", 2 * LANES, F32),
              ("qc", H_NBHD * LANES, BF16), ("kc", W_NBHD, kv_dtype), ("vc", W_NBHD, kv_dtype)]
    out_specs = [pl.BlockSpec((tm, w), row) for _, w, _ in widths]
    out_shape = [jax.ShapeDtypeStruct((T, w), dt) for _, w, dt in widths]
    for w, blk in zip((W_KV, W_NBHD), vt_blocks):
        out_specs.append(pl.BlockSpec((tm // blk, w, blk), lambda i: (i, 0, 0)))
        out_shape.append(jax.ShapeDtypeStruct((T // blk, w, blk), BF16))
    est = 2 * (tm * D_MODEL * 4 + D_MODEL * IN_PAD * 2 + tm * IN_PAD * 4) + 3 * tm * IN_PAD * 4
    outs = pl.pallas_call(
        functools.partial(_inproj_kernel, rope=rope),
        grid=(T // tm,),
        in_specs=in_specs, out_specs=out_specs, out_shape=out_shape,
        compiler_params=pltpu.CompilerParams(dimension_semantics=("arbitrary",),
                                             vmem_limit_bytes=_vmem_limit(est)),
        name=name,
    )(*args)
    return dict(zip([n for n, _, _ in widths] + ["vat", "vct"], outs))


SUM_ROWS = 16


def _softmax_step(st, col_max, vt, m_scr, acc_scr, u):
    m_old = m_scr[u]
    m_new = jnp.maximum(m_old, col_max)
    p = jnp.exp2(st - m_new).astype(BF16)
    alpha = jnp.exp2(m_old - m_new)
    vt1 = jnp.concatenate([vt, jnp.ones((SUM_ROWS, vt.shape[1]), BF16)], axis=0)
    acc_scr[u] = alpha * acc_scr[u] + _dot(vt1, p)
    m_scr[u] = m_new


def _normalised_pair(acc_scr, u0, u1):
    halves = [acc_scr[u][0:HEAD_DIM] / acc_scr[u][HEAD_DIM:HEAD_DIM + 1] for u in (u0, u1)]
    return jnp.concatenate(halves, axis=0).T
def _attn_kernel(*refs, heads, part_chunks):
    n_parts = len(part_chunks)
    assert n_parts in (1, 2)
    q_ref, kv_refs = refs[0], refs[1:1 + 2 * n_parts]
    o_ref, m_scr, acc_scr, s_ring, cmax_scr = refs[1 + 2 * n_parts:]
    nch = sum(part_chunks)

    def chunk_of(which, j, rows, cols):
        first = kv_refs[which][0, jnp.minimum(j, part_chunks[0] - 1), rows, cols].astype(BF16)
        if n_parts == 1:
            return first
        j2 = jnp.clip(j - part_chunks[0], 0, part_chunks[1] - 1)
        return jnp.where(j < part_chunks[0], first, kv_refs[2 + which][0, j2, rows, cols].astype(BF16))

    nh = len(heads)
    ring = ATTN_LOOKAHEAD + 1
    assert nh % ring == 0
    m_scr[...] = jnp.full(m_scr.shape, -jnp.inf, F32)
    acc_scr[...] = jnp.zeros(acc_scr.shape, F32)

    def scores(j, item):
        h = item % nh
        kg = heads[h][0]
        kj = chunk_of(0, j, slice(None), slice(kg * LANES, (kg + 1) * LANES))
        st = _dot_nt(kj, q_ref[0, :, h * LANES:(h + 1) * LANES])
        s_ring[item % ring] = st
        cmax_scr[item % ring] = jnp.max(st, axis=0, keepdims=True)

    def chunk(j, carry):
        j_next = jnp.minimum(j + 1, nch - 1)
        for h, (kg, kh) in enumerate(heads):
            ahead = h + ATTN_LOOKAHEAD
            scores(j if ahead < nh else j_next, ahead)
            r = kg * LANES + kh * HEAD_DIM
            vt = chunk_of(1, j, slice(r, r + HEAD_DIM), slice(None))
            _softmax_step(s_ring[h % ring], cmax_scr[h % ring], vt, m_scr, acc_scr, h)
        return carry

    for item in range(ATTN_LOOKAHEAD):
        scores(0, item)
    lax.fori_loop(0, nch, chunk, 0)
    for t in range(len(heads) // 2):
        o_ref[0, :, t * LANES:(t + 1) * LANES] = _normalised_pair(acc_scr, 2 * t, 2 * t + 1).astype(o_ref.dtype)


def _attn_seqs_kernel(q_ref, k_ref, vt_ref, o_ref, s_ring, cmax_scr, *, heads):
    nh = len(heads)
    n_seq = q_ref.shape[0]
    ring = nh
    look = nh - 1

    def scores(s, item):
        h = item % nh
        kg = heads[h][0]
        st = _dot_nt(k_ref[s, 0, :, kg * LANES:(kg + 1) * LANES].astype(BF16),
                     q_ref[s, :, h * LANES:(h + 1) * LANES])
        s_ring[item % ring] = st
        cmax_scr[item % ring] = jnp.max(st, axis=0, keepdims=True)

    def sequence(s, carry):
        s_next = jnp.minimum(s + 1, n_seq - 1)
        accs = []
        for h, (kg, kh) in enumerate(heads):
            ahead = h + look
            scores(s if ahead < nh else s_next, ahead)
            r = kg * LANES + kh * HEAD_DIM
            vt = vt_ref[s, 0, r:r + HEAD_DIM, :].astype(BF16)
            vt1 = jnp.concatenate([vt, jnp.ones((SUM_ROWS, vt.shape[1]), BF16)], axis=0)
            p = jnp.exp2(s_ring[h % ring] - cmax_scr[h % ring]).astype(BF16)
            accs.append(_dot(vt1, p))
        for t in range(nh // 2):
            halves = [a[0:HEAD_DIM] / a[HEAD_DIM:HEAD_DIM + 1] for a in accs[2 * t:2 * t + 2]]
            o_ref[s, :, t * LANES:(t + 1) * LANES] = jnp.concatenate(halves, axis=0).T.astype(o_ref.dtype)
        return carry

    for item in range(look):
        scores(0, item)
    lax.fori_loop(0, n_seq, sequence, 0)


def _attention_seqs(q, k4, vt4, *, heads, name):
    B, Sq, WQ = q.shape
    ck, KW = k4.shape[2], k4.shape[3]
    nh = len(heads)
    W = nh * HEAD_DIM
    bb = min(SEQS_PER_STEP, B)
    ring = nh
    est = (2 * bb * (Sq * (WQ + W) * 2 + ck * KW * (k4.dtype.itemsize + vt4.dtype.itemsize))
           + ring * ck * Sq * 4 + 8 * nh * ck * Sq * 4)
    return pl.pallas_call(
        functools.partial(_attn_seqs_kernel, heads=heads),
        grid=(B // bb,),
        in_specs=[pl.BlockSpec((bb, Sq, WQ), lambda i: (i, 0, 0)),
                  pl.BlockSpec((bb, 1, ck, KW), lambda i: (i, 0, 0, 0)),
                  pl.BlockSpec((bb, 1, KW, ck), lambda i: (i, 0, 0, 0))],
        out_specs=pl.BlockSpec((bb, Sq, W), lambda i: (i, 0, 0)),
        out_shape=jax.ShapeDtypeStruct((B, Sq, W), BF16),
        scratch_shapes=[pltpu.VMEM((ring, ck, Sq), F32),
                        pltpu.VMEM((ring, 1, Sq), F32)],
        compiler_params=pltpu.CompilerParams(dimension_semantics=("arbitrary",),
                                             vmem_limit_bytes=_vmem_limit(est)),
        name=name,
    )(q, k4, vt4)


def _attention(q, parts, *, heads, name):
    B, Sq, WQ = q.shape
    ck, KW = parts[0][0].shape[2], parts[0][0].shape[3]
    tq = min(TQ_ATTN, Sq)
    nh = len(heads)
    W = nh * HEAD_DIM
    part_chunks = tuple(k4.shape[1] for k4, _ in parts)
    if part_chunks == (1,) and tq == Sq:
        return _attention_seqs(q, parts[0][0], parts[0][1], heads=heads, name=name)
    in_specs = [pl.BlockSpec((1, tq, WQ), lambda b, i: (b, i, 0))]
    args = [q]
    kv_bytes = 0
    for k4, vt4 in parts:
        assert k4.shape[2:] == (ck, KW) and vt4.shape[2:] == (KW, ck)
        in_specs += [pl.BlockSpec((1,) + k4.shape[1:], lambda b, i: (b, 0, 0, 0)),
                     pl.BlockSpec((1,) + vt4.shape[1:], lambda b, i: (b, 0, 0, 0))]
        args += [k4, vt4]
        kv_bytes += k4[0].size * k4.dtype.itemsize + vt4[0].size * vt4.dtype.itemsize
    est = (2 * (tq * (WQ + W) * 2 + kv_bytes) + nh * tq * (HEAD_DIM * 4 + 64) + 8 * nh * ck * tq * 4)
    return pl.pallas_call(
        functools.partial(_attn_kernel, heads=heads, part_chunks=part_chunks),
        grid=(B, Sq // tq),
        in_specs=in_specs,
        out_specs=pl.BlockSpec((1, tq, W), lambda b, i: (b, i, 0)),
        out_shape=jax.ShapeDtypeStruct((B, Sq, W), BF16),
        scratch_shapes=[pltpu.VMEM((nh, 1, tq), F32),
                        pltpu.VMEM((nh, HEAD_DIM + SUM_ROWS, tq), F32),
                        pltpu.VMEM((ATTN_LOOKAHEAD + 1, ck, tq), F32),
                        pltpu.VMEM((ATTN_LOOKAHEAD + 1, 1, tq), F32)],
        compiler_params=pltpu.CompilerParams(dimension_semantics=("arbitrary", "arbitrary"),
                                             vmem_limit_bytes=_vmem_limit(est)),
        name=name,
    )(*args)


_HEADS_GQA = tuple((0, h // (H_ATTN // KV_ATTN)) for h in range(H_ATTN))
_HEADS_MHA = tuple((h // 2, h % 2) for h in range(H_NBHD))


def _nbhd_window(r, rows):
    kr = min(NA_ROWS, rows)
    return min(max(r - kr // 2, 0), rows - kr), kr


def _nbhd_patterns(rows):
    n_groups = rows // NB_GROUP
    pats = []
    for g in (0, 1, n_groups - 1):
        r0 = g * NB_GROUP
        pats.append((r0, min(max(r0 - NA_ROWS // 2, 0), rows - NB_SLAB)))
    return pats


def _nbhd_kernel(rb_ref, q_ref, k_ref, vt_ref, kc_ref, vct_ref, o_ref,
                 bias_scr, m_scr, acc_scr, s_ring, cmax_scr, *, rows):
    hp = pl.program_id(0)
    b = pl.program_id(1)
    gs = pl.program_id(2)
    n_groups = rows // NB_GROUP
    n_dr = 2 * NA_ROWS - 1
    n_dc = 2 * NA_COLS - 1
    tq = NB_GROUP * GRID_W
    ck = tq
    pats = _nbhd_patterns(rows)
    ring = ATTN_LOOKAHEAD + 1
    n_chunks = NB_SLAB * GRID_W // ck + 1
    assert n_chunks % ring == 0 and kc_ref.shape[1] == ck
    blk_rows = NB_KV_BLOCK // GRID_W

    @pl.when((b == 0) & (gs == 0))
    def _build_bias():
        shape = (GRID_W, LANES)
        w = lax.broadcasted_iota(jnp.int32, shape, 0)
        lane = lax.broadcasted_iota(jnp.int32, shape, 1)
        cc = lane % GRID_W
        second = lane >= GRID_W
        cs = jnp.clip(w - NA_COLS // 2, 0, GRID_W - NA_COLS)
        col_ok = (cc >= cs) & (cc < cs + NA_COLS)
        dc = cc - w + (NA_COLS - 1)
        for hh in range(2):
            base = (2 * hp + hh) * (n_dr * n_dc)
            tiles = {}
            for d in range(-1, n_dr):
                acc = jnp.zeros(shape, F32)
                for j in range(n_dc):
                    va = rb_ref[base + d * n_dc + j] * LOG2E if d >= 0 else 0.0
                    vb = rb_ref[base + (d + 1) * n_dc + j] * LOG2E if d + 1 < n_dr else 0.0
                    acc = acc + jnp.where(dc == j, jnp.where(second, vb, va), 0.0)
                tiles[d] = acc
            def query_row_tile(r, kra):
                rs, kr = _nbhd_window(r, rows)
                ok_a = rs <= kra < rs + kr
                ok_b = rs <= kra + 1 < rs + kr
                if not (ok_a or ok_b):
                    return jnp.full(shape, NEG, F32)
                row_ok = (jnp.logical_not(second) if ok_a and not ok_b else
                          second if ok_b and not ok_a else None)
                ok = col_ok if row_ok is None else (col_ok & row_ok)
                return jnp.where(ok, tiles[kra - r + (NA_ROWS - 1)], NEG)

            for pi, (r0, slab0) in enumerate(pats):
                for ip in range(NB_GROUP // 2):
                    for ap in range(NB_SLAB // 2):
                        kra = slab0 + 2 * ap
                        two_rows = jnp.concatenate([query_row_tile(r0 + 2 * ip, kra),
                                                    query_row_tile(r0 + 2 * ip + 1, kra)], axis=0)
                        bias_scr[hh, pi, ap * LANES:(ap + 1) * LANES, ip * LANES:(ip + 1) * LANES] = two_rows.T

    assert (2 * n_chunks) % ring == 0
    m_scr[...] = jnp.full(m_scr.shape, -jnp.inf, F32)
    acc_scr[...] = jnp.zeros(acc_scr.shape, F32)

    def group_of(s):
        g = gs * NB_STEP_GROUPS + s
        pat = jnp.where(g == 0, 0, jnp.where(g == n_groups - 1, 2, 1))
        slab0 = jnp.clip(g * NB_GROUP - NA_ROWS // 2, 0, rows - NB_SLAB)
        return pat, slab0 // blk_rows

    per_group = 2 * n_chunks
    nb = ck // NB_KV_BLOCK

    def scores(s, item):
        hh, c = (item % per_group) // n_chunks, item % n_chunks
        qm = q_ref[0, pl.ds(pl.multiple_of(s * tq, tq), tq), hh * LANES:(hh + 1) * LANES]
        if c < n_chunks - 1:
            pat, blk0 = group_of(s)
            kc = k_ref[0, pl.ds(blk0 + c * nb, nb)].reshape(ck, LANES)
            st = _dot_nt(kc, qm) + bias_scr[hh, pat, c * ck:(c + 1) * ck, :]
        else:
            st = _dot_nt(kc_ref[0].astype(BF16), qm)
        s_ring[item % ring] = st
        cmax_scr[item % ring] = jnp.max(st, axis=0, keepdims=True)

    def group(s, carry):
        s_next = jnp.minimum(s + 1, NB_STEP_GROUPS - 1)
        for item in range(per_group):
            ahead = item + ATTN_LOOKAHEAD
            scores(s if ahead < per_group else s_next, ahead)
            hh, c = item // n_chunks, item % n_chunks
            if c < n_chunks - 1:
                _, blk0 = group_of(s)
                vt = jnp.concatenate([vt_ref[0, blk0 + c * nb + i, hh * HEAD_DIM:(hh + 1) * HEAD_DIM, :]
                                      for i in range(nb)], axis=1)
            else:
                vt = vct_ref[0, hh * HEAD_DIM:(hh + 1) * HEAD_DIM, :].astype(BF16)
            _softmax_step(s_ring[item % ring], cmax_scr[item % ring], vt, m_scr, acc_scr, 2 * s + hh)
        o_ref[0, pl.ds(pl.multiple_of(s * tq, tq), tq), :] = (
            _normalised_pair(acc_scr, 2 * s, 2 * s + 1).astype(o_ref.dtype))
        return carry

    for item in range(ATTN_LOOKAHEAD):
        scores(0, item)
    lax.fori_loop(0, NB_STEP_GROUPS, group, 0)


def _nbhd_attention(q, k4, vt4, k_ctx, vct, rel_bias_flat, *, name):
    B, S, _ = q.shape
    W = k4.shape[3]
    P = k_ctx.shape[1]
    rows = S // GRID_W
    tq = NB_GROUP * GRID_W
    nk = NB_SLAB * GRID_W
    nblk = S // NB_KV_BLOCK
    tqs = NB_STEP_GROUPS * tq
    nu = 2 * NB_STEP_GROUPS
    ring = ATTN_LOOKAHEAD + 1
    est =(2 * (2 * tqs * LANES * 2 + 2 * S * LANES * 2 + 2 * P * LANES * 4)
           + 2 * 3 * tq * nk * 4 + ring * tq * tq * 4 + 8 * tq * tq * 4 + nu * tq * 1024)
    return pl.pallas_call(
        functools.partial(_nbhd_kernel, rows=rows),
        grid=(W // LANES, B, rows // (NB_GROUP * NB_STEP_GROUPS)),
        in_specs=[pl.BlockSpec(memory_space=pltpu.SMEM),
                  pl.BlockSpec((1, tqs, 2 * LANES), lambda p, b, g: (b, g, p)),
                  pl.BlockSpec((1, nblk, NB_KV_BLOCK, LANES), lambda p, b, g: (b, 0, 0, p)),
                  pl.BlockSpec((1, nblk, LANES, NB_KV_BLOCK), lambda p, b, g: (b, 0, p, 0)),
                  pl.BlockSpec((1, P, LANES), lambda p, b, g: (b, 0, p)),
                  pl.BlockSpec((1, LANES, P), lambda p, b, g: (b, p, 0))],
        out_specs=pl.BlockSpec((1, tqs, LANES), lambda p, b, g: (b, g, p)),
        out_shape=jax.ShapeDtypeStruct((B, S, W), BF16),
        scratch_shapes=[pltpu.VMEM((2, 3, nk, tq), F32),
                        pltpu.VMEM((nu, 1, tq), F32),
                        pltpu.VMEM((nu, HEAD_DIM + SUM_ROWS, tq), F32),
                        pltpu.VMEM((ring, tq, tq), F32),
                        pltpu.VMEM((ring, 1, tq), F32)],
        compiler_params=pltpu.CompilerParams(dimension_semantics=("arbitrary",) * 3,
                                             vmem_limit_bytes=_vmem_limit(est)),
        name=name,
    )(rel_bias_flat, q, k4, vt4, k_ctx, vct)


def _split3(x):
    hi = x.astype(BF16)
    r = x - hi.astype(F32)
    mid = r.astype(BF16)
    return hi, mid, (r - mid.astype(F32)).astype(BF16)


def _mlstm_kernel(q_ref, k_ref, v_ref, g_ref, ob_ref, s0_ref, m0_ref, on_ref,
                  out_ref, sf_ref, mf_ref,
                  h_scr, nat_scr, rows_scr, stat_scr, mprev_scr, un_scr, st_scr, *, nc, grp_a, grp):
    d = pl.program_id(1)
    L = L_CHUNK
    bb = q_ref.shape[0]
    NP = H_MLSTM // 2
    row = lax.broadcasted_iota(jnp.int32, (L, L), 0)
    col = lax.broadcasted_iota(jnp.int32, (L, L), 1)
    sign = 1 - 2 * d
    mask = (col - row) * sign <= 0
    maskb = mask.astype(BF16)
    mask3 = jnp.concatenate([maskb, maskb, maskb], axis=1)
    lane = lax.broadcasted_iota(jnp.int32, (L, LANES), 1)
    lo = lane < HEAD_DIM
    top = row < HEAD_DIM
    row2 = lax.broadcasted_iota(jnp.int32, (L, 2 * LANES), 0)
    col2 = lax.broadcasted_iota(jnp.int32, (L, 2 * LANES), 1)
    keep_state = (row2 < HEAD_DIM) == ((col2 % LANES) < HEAD_DIM)
    top2 = row2 < HEAD_DIM
    ones_b = jnp.ones((L, LANES), BF16)
    ones_lo = lo.astype(BF16)
    ones_hi = jnp.logical_not(lo).astype(BF16)

    def chunk_rows(c):
        return pl.ds(pl.multiple_of(c * L, L), L)

    def tokens(c):
        if bb == 1:
            return 0, chunk_rows(c)
        return c // nc, pl.ds(pl.multiple_of((c % nc) * L, L), L)

    def pass_a(it, carry):
        cs = [it * grp_a + u for u in range(grp_a)]
        gts = [g_ref[tokens(c)[0], tokens(c)[1], :] for c in cs]
        bns = [_dot(mask3, jnp.concatenate(_split3(gt), axis=0)) for gt in gts]
        a_all = []
        for c, gt, bn in zip(cs, gts, bns):
            nat = jnp.where(lane < H_MLSTM, gt, bn)
            nat_scr[chunk_rows(c), :] = nat * (-LOG2E)
            nat_t = nat.T
            b_rows = nat_t[H_MLSTM:2 * H_MLSTM]
            c_rows = nat_t[0:H_MLSTM] - b_rows
            rows_scr[c] = jnp.concatenate([c_rows * LOG2E, b_rows], axis=0)
            c_max = jnp.max(c_rows, axis=1, keepdims=True)
            b_tot = jnp.where(d == 0, b_rows[:, L - 1:L], b_rows[:, 0:1])
            stat_scr[c] = jnp.concatenate([jnp.broadcast_to(c_max, (H_MLSTM, LANES)),
                                           jnp.broadcast_to(b_tot, (H_MLSTM, LANES))], axis=0)
            a_all.append(jnp.exp(c_rows - c_max))
        for c, a_rows in zip(cs, a_all):
            sq, rows = tokens(c)
            for p in range(NP):
                lanes = slice(p * LANES, (p + 1) * LANES)
                k_t = k_ref[sq, rows, lanes].astype(F32).T
                a_sel = jnp.where(top, a_rows[2 * p:2 * p + 1], a_rows[2 * p + 1:2 * p + 2])
                vv = jnp.concatenate([v_ref[sq, rows, lanes], ones_b], axis=1)
                un = _dot((k_t * a_sel).astype(BF16), vv)
                un_scr[c, p] = jnp.where(keep_state, un, 0.0)
        return carry

    lax.fori_loop(0, bb * nc // grp_a, pass_a, 0)

    def pass_b(ci, m, sq):
        c = sq * nc + jnp.where(d == 0, ci, nc - 1 - ci)
        st = stat_scr[c]
        c_max, b_tot = st[0:H_MLSTM], st[H_MLSTM:]
        m_new = jnp.maximum(b_tot + m, b_tot + c_max)
        d_old = jnp.exp(b_tot + m - m_new)
        d_new = jnp.exp(b_tot + c_max - m_new)
        mprev_scr[c] = jnp.concatenate([m, m], axis=0) * LOG2E
        for p in range(NP):
            def rows_of(t, p=p):
                even = jnp.concatenate([t[2 * p:2 * p + 1]] * 2, axis=1)
                odd = jnp.concatenate([t[2 * p + 1:2 * p + 2]] * 2, axis=1)
                return jnp.where(top2, even, odd)
            s_prev = st_scr[p]
            st_scr[p] = rows_of(d_old) * s_prev + rows_of(d_new) * un_scr[c, p]
            un_scr[c, p] = s_prev
        return m_new

    def recurrence(sq, carry):
        st_scr[...] = s0_ref[sq, 0]
        m_fin = lax.fori_loop(0, nc, functools.partial(pass_b, sq=sq), m0_ref[sq, 0][0:H_MLSTM])
        sf_ref[sq, 0] = st_scr[...]
        mf_ref[sq, 0] = jnp.concatenate([m_fin, m_fin], axis=0)
        return carry

    lax.fori_loop(0, bb, recurrence, 0)

    def pass_c(it, carry):
        cs = [it * grp + u for u in range(grp)]
        units = [(u, p) for u in range(grp) for p in range(NP)]
        early = {}
        for u, p in units:
            c = cs[u]
            sq, rows = tokens(c)
            lanes = slice(p * LANES, (p + 1) * LANES)
            qp = q_ref[sq, rows, lanes]
            kp = k_ref[sq, rows, lanes]
            s_in = un_scr[c, p].astype(BF16)
            qms = [jnp.where(lo if hh == 0 else jnp.logical_not(lo), qp, jnp.zeros_like(qp)) for hh in range(2)]
            early[u, p] = ([_dot_nt(qm, kp) for qm in qms],
                           _dot(qp, s_in))
        mid = {}
        for u, p in units:
            r_t = rows_scr[cs[u]]
            m_in = mprev_scr[cs[u]]
            for hh in range(2):
                h = 2 * p + hh
                cm = jnp.where(mask, r_t[h:h + 1, :], -jnp.inf)
                m_prev = m_in[h:h + 1, :]
                mu = jnp.maximum(jnp.broadcast_to(jnp.max(cm, axis=1, keepdims=True), (L, LANES)), m_prev)
                w = early[u, p][0][hh] * jnp.exp2(cm - mu)
                mid[u, p, hh] = (w.astype(BF16), mu, m_prev)
        for u, p in units:
            sq, rows = tokens(cs[u])
            lanes = slice(p * LANES, (p + 1) * LANES)
            vp = v_ref[sq, rows, lanes]
            zero = jnp.zeros_like(vp)
            vv = jnp.concatenate([jnp.concatenate([jnp.where(lo, vp, zero), ones_lo], axis=1),
                                  jnp.concatenate([jnp.where(lo, zero, vp), ones_hi], axis=1)], axis=0)
            w2 = jnp.concatenate([mid[u, p, 0][0], mid[u, p, 1][0]], axis=1)
            nd = _dot(w2, vv)
            nat = nat_scr[chunk_rows(cs[u]), :]
            nb = [jnp.broadcast_to(nat[:, H_MLSTM + 2 * p + hh:H_MLSTM + 2 * p + hh + 1], (L, LANES))
                  for hh in range(2)]
            fs = early[u, p][1]
            mu = jnp.where(lo, mid[u, p, 0][1], mid[u, p, 1][1])
            m_prev = jnp.where(lo[0:1], mid[u, p, 0][2], mid[u, p, 1][2])
            w_inter = jnp.exp2(m_prev - mu)
            den = nd[:, LANES:] + w_inter * fs[:, LANES:]
            den = jnp.maximum(jnp.abs(den), jnp.exp2(jnp.where(lo, nb[0], nb[1]) - mu))
            h_scr[d, chunk_rows(cs[u]), lanes] = (nd[:, :LANES] + w_inter * fs[:, :LANES]) / den
        return carry

    lax.fori_loop(0, bb * nc // grp, pass_c, 0)

    @pl.when(d == 1)
    def _finish():
        def rows_block(c, carry):
            sq, rows = tokens(c)
            flat = chunk_rows(c)
            for p in range(NP):
                lanes = slice(p * LANES, (p + 1) * LANES)
                hn = _pair_rms(h_scr[0, flat, lanes] + h_scr[1, flat, lanes], on_ref[:, lanes])
                out_ref[sq, rows, lanes] = (_sigmoid(ob_ref[sq, rows, lanes]) * hn).astype(out_ref.dtype)
            return carry

        lax.fori_loop(0, bb * nc, rows_block, 0, unroll=4)


def _mlstm(q, k, v, gates, ob, s0, m0, out_norm, *, name):
    assert L_CHUNK == LANES
    B, S, W = q.shape
    nc = S // L_CHUNK
    bb = max(1, min(B, MLSTM_STEP_CHUNKS // nc))
    ncb, sb = bb * nc, bb * S
    grp = min(MLSTM_GROUP, ncb)
    grp_a = min(MLSTM_GROUP_A, ncb)
    npair = H_MLSTM // 2
    seq = lambda b, d: (b, 0, 0)
    est = (2 * (3 * sb * W * 2 + sb * LANES * 4 + sb * W * 4 + sb * W * 2) + 2 * sb * W * 4 + sb * LANES * 4
           + ncb * npair * LANES * 2 * LANES * 4 + 12 * 1024 * 1024)
    return pl.pallas_call(
        functools.partial(_mlstm_kernel, nc=nc, grp_a=grp_a, grp=grp),
        grid=(B // bb, 2),
        in_specs=[pl.BlockSpec((bb, S, W), seq), pl.BlockSpec((bb, S, W), seq), pl.BlockSpec((bb, S, W), seq),
                  pl.BlockSpec((bb, S, LANES), lambda b, d: (b, 0, d)),
                  pl.BlockSpec((bb, S, W), seq),
                  pl.BlockSpec((bb, 1, npair, LANES, 2 * LANES), lambda b, d: (b, d, 0, 0, 0)),
                  pl.BlockSpec((bb, 1, 8, LANES), lambda b, d: (b, d, 0, 0)),
                  pl.BlockSpec((1, W), lambda b, d: (0, 0))],
        out_specs=[pl.BlockSpec((bb, S, W), seq),
                   pl.BlockSpec((bb, 1, npair, LANES, 2 * LANES), lambda b, d: (b, d, 0, 0, 0)),
                   pl.BlockSpec((bb, 1, 8, LANES), lambda b, d: (b, d, 0, 0))],
        out_shape=[jax.ShapeDtypeStruct((B, S, W), BF16),
                   jax.ShapeDtypeStruct((B, 2, npair, LANES, 2 * LANES), F32),
                   jax.ShapeDtypeStruct((B, 2, 8, LANES), F32)],
        scratch_shapes=[pltpu.VMEM((2, sb, W), F32),
                        pltpu.VMEM((sb, LANES), F32),
                        pltpu.VMEM((ncb, 8, L_CHUNK), F32),
                        pltpu.VMEM((ncb, 8, LANES), F32),
                        pltpu.VMEM((ncb, 8, LANES), F32),
                        pltpu.VMEM((ncb, npair, LANES, 2 * LANES), F32),
                        pltpu.VMEM((npair, LANES, 2 * LANES), F32)],
        compiler_params=pltpu.CompilerParams(dimension_semantics=("arbitrary", "arbitrary"),
                                             vmem_limit_bytes=_vmem_limit(est)),
        name=name,
    )(q, k, v, gates, ob, s0, m0, out_norm)


def _outproj_kernel(a_ref, b_ref, c_ref, x_ref, mod_ref, w_ref, gpost_ref, gpre_ref, x1_ref, h2_ref):
    mod = mod_ref[0]
    gt1 = mod[:, 2 * D_MODEL:3 * D_MODEL]
    sh2 = mod[:, 3 * D_MODEL:4 * D_MODEL]
    sc2 = mod[:, 4 * D_MODEL:5 * D_MODEL]
    tr = x_ref.shape[0] // ROW_SPLIT
    pieces = [pl.ds(s * tr, tr) for s in range(ROW_SPLIT)]
    mos = [_dot(jnp.concatenate([a_ref[r, :], b_ref[r, :], c_ref[r, :]], axis=1), w_ref[...]) for r in pieces]
    for r, mo in zip(pieces, mos):
        x1 = x_ref[r, :] + gt1 * _rms(mo, gpost_ref[...])
        x1_ref[r, :] = x1
        h2_ref[r, :] = (_rms(x1, gpre_ref[...]) * (1.0 + sc2) + sh2).astype(h2_ref.dtype)


def _outproj(oa, ob, oc, x, mods, w_out, g_post, g_pre, *, rows_per_cond, name):
    T = x.shape[0]
    tm = TM_PROJ
    bpc = rows_per_cond // tm
    row = lambda i: (i, 0)
    const = lambda i: (0, 0)
    est = 2 * (tm * D_MODEL * (2 + 4 + 4 + 2) + D_MODEL * D_MODEL * 2) + 4 * tm * D_MODEL * 4
    return pl.pallas_call(
        _outproj_kernel,
        grid=(T // tm,),
        in_specs=[pl.BlockSpec((tm, W_ATTN), row), pl.BlockSpec((tm, W_MLSTM), row),
                  pl.BlockSpec((tm, W_NBHD), row), pl.BlockSpec((tm, D_MODEL), row),
                  pl.BlockSpec((1, 1, N_MOD * D_MODEL), lambda i: (i // bpc, 0, 0)),
                  pl.BlockSpec((D_MODEL, D_MODEL), const),
                  pl.BlockSpec((1, D_MODEL), const), pl.BlockSpec((1, D_MODEL), const)],
        out_specs=[pl.BlockSpec((tm, D_MODEL), row), pl.BlockSpec((tm, D_MODEL), row)],
        out_shape=[jax.ShapeDtypeStruct((T, D_MODEL), F32), jax.ShapeDtypeStruct((T, D_MODEL), BF16)],
        compiler_params=pltpu.CompilerParams(dimension_semantics=("arbitrary",),
                                             vmem_limit_bytes=_vmem_limit(est)),
        name=name,
    )(oa, ob, oc, x, mods, w_out, g_post, g_pre)


def _ffn_kernel(h_ref, x_ref, mod_ref, wu_ref, wd_ref, g_ref, o_ref, acc_ref):
    j = pl.program_id(1)

    @pl.when(j == 0)
    def _zero():
        acc_ref[...] = jnp.zeros_like(acc_ref)

    u = jnp.maximum(_dot(h_ref[...], wu_ref[...]), 0.0)
    acc_ref[...] += _dot((u * u).astype(BF16), wd_ref[...])

    @pl.when(j == pl.num_programs(1) - 1)
    def _finish():
        gt2 = mod_ref[0][:, 5 * D_MODEL:6 * D_MODEL]
        o_ref[...] = x_ref[...] + gt2 * _rms(acc_ref[...], g_ref[...])


def _ffn(h2, x1, mods, w_up, w_down, g_post, *, rows_per_cond, name):
    T = x1.shape[0]
    tm, tf = TM_FFN, TF_FFN
    tm = min(tm, rows_per_cond)
    bpc = rows_per_cond // tm
    est = 2 * (tm * D_MODEL * (2 + 4 + 4) + 2 * D_MODEL * tf * 2) + tm * D_MODEL * 4 + 3 * tm * tf * 4
    return pl.pallas_call(
        _ffn_kernel,
        grid=(T // tm, D_FF // tf),
        in_specs=[pl.BlockSpec((tm, D_MODEL), lambda i, j: (i, 0)),
                  pl.BlockSpec((tm, D_MODEL), lambda i, j: (i, 0)),
                  pl.BlockSpec((1, 1, N_MOD * D_MODEL), lambda i, j: (i // bpc, 0, 0)),
                  pl.BlockSpec((D_MODEL, tf), lambda i, j: (0, j)),
                  pl.BlockSpec((tf, D_MODEL), lambda i, j: (j, 0)),
                  pl.BlockSpec((1, D_MODEL), lambda i, j: (0, 0))],
        out_specs=pl.BlockSpec((tm, D_MODEL), lambda i, j: (i, 0)),
        out_shape=jax.ShapeDtypeStruct((T, D_MODEL), F32),
        scratch_shapes=[pltpu.VMEM((tm, D_MODEL), F32)],
        compiler_params=pltpu.CompilerParams(dimension_semantics=("arbitrary", "arbitrary"),
                                             vmem_limit_bytes=_vmem_limit(est)),
        name=name,
    )(h2, x1, mods, w_up, w_down, g_post)


def _pad_w_in(w_in_l):
    o = W_ATTN + 2 * W_KV + 4 * W_MLSTM
    pre, gates, post = w_in_l[:, :o], w_in_l[:, o:o + N_GATES], w_in_l[:, o + N_GATES:]
    z = jnp.zeros((D_MODEL, LANES - 2 * H_MLSTM), w_in_l.dtype)
    return jnp.concatenate([pre, gates[:, :2 * H_MLSTM], z, gates[:, 2 * H_MLSTM:], z, post],
                           axis=1).astype(BF16)


def _pad_gate_bias(gb_l):
    z = jnp.zeros((LANES - 2 * H_MLSTM,), gb_l.dtype)
    return jnp.concatenate([gb_l[:2 * H_MLSTM], z, gb_l[2 * H_MLSTM:], z]).reshape(1, 2 * LANES)


def _rope_tables(S):
    quarter = HEAD_DIM // 4
    pos = np.arange(S)
    inv_freq = np.float32(ROPE_THETA) ** (-np.arange(quarter, dtype=np.float32) / np.float32(quarter))

    def tabs(p):
        ang = p.astype(np.float32)[:, None] * inv_freq[None, :]
        return np.cos(ang), np.sin(ang)

    cr, sr = tabs(pos // GRID_W)
    cc, sc = tabs(pos % GRID_W)
    cos = np.concatenate([cr, cr, cc, cc], axis=1)
    sin = np.concatenate([-sr, sr, -sc, sc], axis=1)
    return jnp.asarray(np.tile(cos, (1, 2)), F32), jnp.asarray(np.tile(sin, (1, 2)), F32)


def _pack_state(C, n, m):
    B = C.shape[0]
    Cp = C.reshape(B, 2, 2, 2, HEAD_DIM, HEAD_DIM)
    z = jnp.zeros_like(Cp[:, :, :, 0])
    top = jnp.concatenate([Cp[:, :, :, 0], z], axis=-1)
    bot = jnp.concatenate([z, Cp[:, :, :, 1]], axis=-1)
    Cbd = jnp.concatenate([top, bot], axis=-2)
    n_rep = jnp.broadcast_to(n.reshape(B, 2, 2, LANES, 1), (B, 2, 2, LANES, LANES))
    same_head = (jnp.arange(LANES)[:, None] < HEAD_DIM) == (jnp.arange(LANES)[None, :] < HEAD_DIM)
    n_rep = jnp.where(same_head, n_rep, 0.0)
    m_rows = jnp.broadcast_to(m[..., None], m.shape + (LANES,))
    return jnp.concatenate([Cbd, n_rep], axis=-1), jnp.concatenate([m_rows, m_rows], axis=-2)


def _unpack_state(s_p, m_p):
    B = s_p.shape[0]
    c_even = s_p[:, :, :, :HEAD_DIM, :HEAD_DIM]
    c_odd = s_p[:, :, :, HEAD_DIM:, HEAD_DIM:LANES]
    C = jnp.stack([c_even, c_odd], axis=3).reshape(B, 2, H_MLSTM, HEAD_DIM, HEAD_DIM)
    n = jnp.concatenate([s_p[..., :HEAD_DIM, LANES], s_p[..., HEAD_DIM:, LANES + HEAD_DIM]], axis=-1)
    return C, n.reshape(B, 2, H_MLSTM, HEAD_DIM), m_p[:, :, :H_MLSTM, 0]


def _layer(x, mods, lw, *, B, S, cond_rows, rope_tabs, ctx_cache, state, name):
    T = B * S
    kv_dtype = BF16 if ctx_cache is not None else F32
    ck_a = min(CK_ATTN, S)
    pr = _inproj(x, mods, lw["g_pre_mix"], lw["w_in"], lw["q_norm"], lw["k_norm"], lw["gate_bias"],
                 rope_tabs, rows_per_cond=cond_rows, kv_dtype=kv_dtype,
                 vt_blocks=(ck_a, NB_KV_BLOCK), name=name + "_inproj")
    seq = lambda a: a.reshape(B, S, a.shape[-1])
    qa, ka, va = seq(pr["qa"]), seq(pr["ka"]), seq(pr["va"])
    qc, kc, vc = seq(pr["qc"]), seq(pr["kc"]), seq(pr["vc"])
    chunks = lambda a, n: a.reshape(B, S // n, n, a.shape[-1])
    vat = pr["vat"].reshape(B, S // ck_a, W_KV, ck_a)
    vct = pr["vct"].reshape(B, S // NB_KV_BLOCK, W_NBHD, NB_KV_BLOCK)
    if ctx_cache is None:
        out_a = _attention(qa, [(chunks(ka, ck_a), vat)], heads=_HEADS_GQA, name=name + "_attn_a")
        out_c = _attention(qc, [(chunks(kc, NB_KV_BLOCK), vct)], heads=_HEADS_MHA, name=name + "_attn_c")
    else:
        ck_c, cv_c = ctx_cache[2], ctx_cache[3]
        P = ck_c.shape[1]
        ctx_k = ctx_cache[0].reshape(B, P // ck_a, ck_a, W_KV)
        ctx_vt = ctx_cache[1].reshape(B, P // ck_a, ck_a, W_KV).transpose(0, 1, 3, 2)
        out_a = _attention(qa, [(chunks(ka, ck_a), vat), (ctx_k, ctx_vt)], heads=_HEADS_GQA,
                           name=name + "_attn_a")
        out_c = _nbhd_attention(qc, chunks(kc, NB_KV_BLOCK), vct, ck_c, cv_c.transpose(0, 2, 1),
                                lw["rel_bias"], name=name + "_attn_c")
    s0, m0 = state
    out_b, sf, mf = _mlstm(seq(pr["qb"]), seq(pr["kb"]), seq(pr["vb"]), seq(pr["g"]), seq(pr["ob"]),
                           s0, m0, lw["out_norm"], name=name + "_mlstm")
    x1, h2 = _outproj(out_a.reshape(T, W_ATTN), out_b.reshape(T, W_MLSTM), out_c.reshape(T, W_NBHD),
                      x, mods, lw["w_out"], lw["g_post_mix"], lw["g_pre_ffn"],
                      rows_per_cond=cond_rows, name=name + "_outproj")
    x2 = _ffn(h2, x1, mods, lw["w_up"], lw["w_down"], lw["g_post_ffn"],
              rows_per_cond=cond_rows, name=name + "_ffn")
    return x2, (ka, va, kc, vc, sf, mf)


def kernel(x_prompt, x_sample, c, cache_k_attn, cache_v_attn, cache_k_nbhd, cache_v_nbhd, state_mlstm_C, state_mlstm_n, state_mlstm_m, c_ctx, w_ada, b_ada, g_pre_mix, g_post_mix, g_pre_ffn, g_post_ffn, w_in, q_norm_attn, k_norm_attn, mlstm_gate_bias, mlstm_out_norm, nbhd_rel_bias, w_out, w_ffn_up, w_ffn_down):
    Bc, Sc, _ = x_prompt.shape
    Bl, Sl, _ = x_sample.shape
    P = cache_k_attn.shape[2]
    n_cond = 8
    cond = jnp.concatenate([c_ctx[None, :], c, jnp.zeros((n_cond - 1 - Bl, D_MODEL), F32)], axis=0)
    mods_all = _modulation(cond, w_ada, b_ada)

    layers = []
    for l in range(DEPTH):
        layers.append(dict(
            w_in=_pad_w_in(w_in[l]),
            w_out=w_out[l].astype(BF16),
            w_up=w_ffn_up[l].astype(BF16),
            w_down=w_ffn_down[l].astype(BF16),
            g_pre_mix=g_pre_mix[l].reshape(1, D_MODEL), g_post_mix=g_post_mix[l].reshape(1, D_MODEL),
            g_pre_ffn=g_pre_ffn[l].reshape(1, D_MODEL), g_post_ffn=g_post_ffn[l].reshape(1, D_MODEL),
            q_norm=jnp.tile(q_norm_attn[l], 2).reshape(1, LANES),
            k_norm=jnp.tile(k_norm_attn[l], 2).reshape(1, LANES),
            gate_bias=_pad_gate_bias(mlstm_gate_bias[l]),
            out_norm=mlstm_out_norm[l].reshape(1, W_MLSTM),
            rel_bias=nbhd_rel_bias[l].reshape(-1),
        ))

    xp = x_prompt.reshape(Bc * Sc, D_MODEL)
    zero_state = (jnp.zeros((Bc, 2, H_MLSTM // 2, LANES, 2 * LANES), F32), jnp.zeros((Bc, 2, 8, LANES), F32))
    ctx = []
    for l in range(DEPTH):
        mods = mods_all[l, 0:1].reshape(1, 1, N_MOD * D_MODEL)
        xp, extras = _layer(xp, mods, layers[l], B=Bc, S=Sc, cond_rows=Bc * Sc, rope_tabs=None,
                            ctx_cache=None, state=zero_state, name=f"ctx{l}")
        ctx.append(extras)
    new_k_attn = jnp.stack([e[0].reshape(Bc, Sc, KV_ATTN, HEAD_DIM) for e in ctx], axis=1)
    new_v_attn = jnp.stack([e[1].reshape(Bc, Sc, KV_ATTN, HEAD_DIM) for e in ctx], axis=1)
    new_k_nbhd = jnp.stack([e[2].reshape(Bc, Sc, H_NBHD, HEAD_DIM) for e in ctx], axis=1)
    new_v_nbhd = jnp.stack([e[3].reshape(Bc, Sc, H_NBHD, HEAD_DIM) for e in ctx], axis=1)
    states = [_unpack_state(e[4], e[5]) for e in ctx]
    new_C = jnp.stack([s[0] for s in states], axis=1)
    new_n = jnp.stack([s[1] for s in states], axis=1)
    new_m = jnp.stack([s[2] for s in states], axis=1)

    xs = x_sample.reshape(Bl * Sl, D_MODEL)
    rope_tabs = _rope_tables(Sl)
    for l in range(DEPTH):
        mods = mods_all[l, 1:1 + Bl].reshape(Bl, 1, N_MOD * D_MODEL)
        cache = (cache_k_attn[:, l].reshape(Bl, P, W_KV), cache_v_attn[:, l].reshape(Bl, P, W_KV),
                 cache_k_nbhd[:, l].reshape(Bl, P, W_NBHD), cache_v_nbhd[:, l].reshape(Bl, P, W_NBHD))
        state = _pack_state(state_mlstm_C[:, l], state_mlstm_n[:, l], state_mlstm_m[:, l])
        xs, _ = _layer(xs, mods, layers[l], B=Bl, S=Sl, cond_rows=Sl, rope_tabs=rope_tabs,
                       ctx_cache=cache, state=state, name=f"lat{l}")

    return (xp.reshape(Bc, Sc, D_MODEL), xs.reshape(Bl, Sl, D_MODEL),
            new_k_attn, new_v_attn, new_k_nbhd, new_v_nbhd, new_C, new_n, new_m)
```

```python
import functools

import jax
import jax.numpy as jnp
import numpy as np
from jax import lax
from jax.experimental import pallas as pl
from jax.experimental.pallas import tpu as pltpu

F32 = jnp.float32
BF16 = jnp.bfloat16

D_MODEL = 1024
DEPTH = 2
GRID_W = 64
HEAD_DIM = 64
H_ATTN = 6
KV_ATTN = 2
H_MLSTM = 4
H_NBHD = 6
D_FF = 4 * D_MODEL
NA_ROWS = 8
NA_COLS = 16
ROPE_THETA = 10000.0
EPS = 1e-6
N_MOD = 6
W_ATTN = H_ATTN * HEAD_DIM
W_KV = KV_ATTN * HEAD_DIM
W_MLSTM = H_MLSTM * HEAD_DIM
W_NBHD = H_NBHD * HEAD_DIM
N_GATES = 4 * H_MLSTM

LANES = 128
V7X_VMEM_BYTES = 64 * 1024 * 1024
VMEM_CAP_BYTES = 56 * 1024 * 1024

TM_PROJ = 512
PROJ_CHUNK = 512
TM_FFN = 1024
TF_FFN = 1024
ROW_SPLIT = 4
TQ_ATTN = 512
CK_ATTN = 512
ATTN_LOOKAHEAD = 2
SEQS_PER_STEP = 8
L_CHUNK = 128
MLSTM_GROUP = 2
MLSTM_GROUP_A = 16
MLSTM_STEP_CHUNKS = 16
NB_GROUP = 8
NB_SLAB = 16
NB_STEP_GROUPS = 8
NB_KV_BLOCK = 256
NEG = -1e30
LOG2E = 1.4426950408889634

_COLS = {}
_off = 0
for _name, _w in (("qa", W_ATTN), ("ka", W_KV), ("va", W_KV), ("qb", W_MLSTM), ("kb", W_MLSTM),
                  ("vb", W_MLSTM), ("ob", W_MLSTM), ("gf", LANES), ("gb", LANES),
                  ("qc", W_NBHD), ("kc", W_NBHD), ("vc", W_NBHD)):
    _COLS[_name] = (_off, _off + _w)
    _off += _w
IN_PAD = _off


def _vmem_limit(nbytes):
    return int(min(max(nbytes, 16 * 1024 * 1024), VMEM_CAP_BYTES))


def _dot(a, b):
    return jnp.dot(a, b, preferred_element_type=F32)


def _dot_nt(a, b):
    return lax.dot_general(a, b, (((1,), (1,)), ((), ())), preferred_element_type=F32)


def _lane_lo(shape):
    return (lax.broadcasted_iota(jnp.int32, shape, len(shape) - 1) % LANES) < HEAD_DIM


def _rms(x, g):
    ms = jnp.mean(x * x, axis=-1, keepdims=True)
    return (x * lax.rsqrt(ms + EPS)) * g


def _pair_rms(x, g):
    lo = _lane_lo(x.shape)
    x2 = x * x
    s_lo = jnp.sum(jnp.where(lo, x2, 0.0), axis=-1, keepdims=True)
    s_hi = jnp.sum(jnp.where(lo, 0.0, x2), axis=-1, keepdims=True)
    r = jnp.where(lo, lax.rsqrt(s_lo / HEAD_DIM + EPS), lax.rsqrt(s_hi / HEAD_DIM + EPS))
    return (x * r) * g


def _sigmoid(x):
    return 1.0 / (1.0 + jnp.exp(-x))


def _mods_kernel(c_ref, w_ref, b_ref, o_ref):
    c = c_ref[...]
    s = (c * _sigmoid(c)).astype(BF16)
    o_ref[0] = _dot(s, w_ref[0].astype(BF16)) + b_ref[0]


def _modulation(cond, w_ada, b_ada):
    n = cond.shape[0]
    tn = D_MODEL
    return pl.pallas_call(
        _mods_kernel,
        grid=(DEPTH, N_MOD * D_MODEL // tn),
        in_specs=[pl.BlockSpec((n, D_MODEL), lambda l, j: (0, 0)),
                  pl.BlockSpec((1, D_MODEL, tn), lambda l, j: (l, 0, j)),
                  pl.BlockSpec((1, 1, tn), lambda l, j: (l, 0, j))],
        out_specs=pl.BlockSpec((1, n, tn), lambda l, j: (l, 0, j)),
        out_shape=jax.ShapeDtypeStruct((DEPTH, n, N_MOD * D_MODEL), F32),
        compiler_params=pltpu.CompilerParams(
            dimension_semantics=("arbitrary", "arbitrary"),
            vmem_limit_bytes=_vmem_limit(4 * D_MODEL * tn * 4)),
        name="modulation",
    )(cond, w_ada, b_ada.reshape(DEPTH, 1, N_MOD * D_MODEL))


def _inproj_kernel(*refs, rope):
    if rope:
        (x_ref, mod_ref, g_ref, w_ref, qn_ref, kn_ref, gbias_ref, cos_ref, sin_ref,
         qa_ref, ka_ref, va_ref, qb_ref, kb_ref, vb_ref, ob_ref, gate_ref,
         qc_ref, kc_ref, vc_ref, vat_ref, vct_ref) = refs
    else:
        (x_ref, mod_ref, g_ref, w_ref, qn_ref, kn_ref, gbias_ref,
         qa_ref, ka_ref, va_ref, qb_ref, kb_ref, vb_ref, ob_ref, gate_ref,
         qc_ref, kc_ref, vc_ref, vat_ref, vct_ref) = refs
    x = x_ref[...]
    mod = mod_ref[0]
    sh1 = mod[:, 0:D_MODEL]
    sc1 = mod[:, D_MODEL:2 * D_MODEL]
    hb = (_rms(x, g_ref[...]) * (1.0 + sc1) + sh1).astype(BF16)

    z = [_dot(hb, w_ref[:, c0:c0 + PROJ_CHUNK]) for c0 in range(0, IN_PAD, PROJ_CHUNK)]

    def proj(name, j=0, w=None):
        lo, hi = _COLS[name]
        lo = lo + j
        hi = hi if w is None else lo + w
        parts = []
        while lo < hi:
            c, o = divmod(lo, PROJ_CHUNK)
            n = min(hi - lo, PROJ_CHUNK - o)
            parts.append(z[c][:, o:o + n])
            lo += n
        return parts[0] if len(parts) == 1 else jnp.concatenate(parts, axis=1)

    scale = HEAD_DIM ** -0.5
    q_scale = scale * LOG2E

    def rotary(t):
        first = (lax.broadcasted_iota(jnp.int32, t.shape, 1) % 32) < 16
        partner = jnp.where(first, pltpu.roll(t, LANES - 16, 1), pltpu.roll(t, 16, 1))
        return t * cos_ref[...] + partner * sin_ref[...]

    def store_queries(t, j, q_ref, kv_half):
        lo = _lane_lo(t.shape)
        for e in range(2):
            h = 2 * j + e
            src = t if e == kv_half[h] else pltpu.roll(t, HEAD_DIM, 1)
            keep = lo if kv_half[h] == 0 else jnp.logical_not(lo)
            q_ref[:, h * LANES:(h + 1) * LANES] = jnp.where(keep, src, 0.0).astype(q_ref.dtype)

    for j in range(W_ATTN // LANES):
        t = _pair_rms(proj("qa", j * LANES, LANES), qn_ref[...])
        if rope:
            t = rotary(t)
        store_queries(t * q_scale, j, qa_ref, [kh for _, kh in _HEADS_GQA])
    t = _pair_rms(proj("ka"), kn_ref[...])
    if rope:
        t = rotary(t)
    ka_ref[...] = t.astype(ka_ref.dtype)

    def store_v(v, v_ref, vt_ref):
        v_ref[...] = v.astype(v_ref.dtype)
        nblk, _, blk = vt_ref.shape
        for u in range(nblk):
            vt_ref[u] = v[u * blk:(u + 1) * blk].T.astype(vt_ref.dtype)

    store_v(proj("va"), va_ref, vat_ref)
    qb_ref[...] = proj("qb").astype(qb_ref.dtype)
    kb_ref[...] = (proj("kb") * scale).astype(kb_ref.dtype)
    vb_ref[...] = proj("vb").astype(vb_ref.dtype)
    ob_ref[...] = proj("ob").astype(ob_ref.dtype)
    for j, name in enumerate(("gf", "gb")):
        gt = proj(name) + gbias_ref[:, j * LANES:(j + 1) * LANES]
        lane = lax.broadcasted_iota(jnp.int32, gt.shape, 1)
        is_f = (lane >= H_MLSTM) & (lane < 2 * H_MLSTM)
        logsig = jnp.minimum(gt, 0.0) - jnp.log1p(jnp.exp(-jnp.abs(gt)))
        gate_ref[:, j * LANES:(j + 1) * LANES] = jnp.where(is_f, logsig, gt)
    for j in range(W_NBHD // LANES):
        store_queries(proj("qc", j * LANES, LANES) * q_scale, j, qc_ref, [kh for _, kh in _HEADS_MHA])
    kc_ref[...] = proj("kc").astype(kc_ref.dtype)
    store_v(proj("vc"), vc_ref, vct_ref)


def _inproj(x, mods, g_pre, w_in_p, qn, kn, gbias, rope_tabs, *, rows_per_cond, kv_dtype, vt_blocks, name):
    T = x.shape[0]
    tm = TM_PROJ
    bpc = rows_per_cond // tm
    rope = rope_tabs is not None
    row = lambda i: (i, 0)
    const = lambda i: (0, 0)
    in_specs = [pl.BlockSpec((tm, D_MODEL), row),
                pl.BlockSpec((1, 1, N_MOD * D_MODEL), lambda i: (i // bpc, 0, 0)),
                pl.BlockSpec((1, D_MODEL), const),
                pl.BlockSpec((D_MODEL, IN_PAD), const),
                pl.BlockSpec((1, LANES), const),
                pl.BlockSpec((1, LANES), const),
                pl.BlockSpec((1, 2 * LANES), const)]
    args = [x, mods, g_pre, w_in_p, qn, kn, gbias]
    if rope:
        nblk = rope_tabs[0].shape[0] // tm
        in_specs += [pl.BlockSpec((tm, LANES), lambda i: (i % nblk, 0))] * 2
        args += list(rope_tabs)
    widths = [("qa", H_ATTN * LANES, BF16), ("ka", W_KV, kv_dtype), ("va", W_KV, kv_dtype),
              ("qb", W_MLSTM, BF16), ("kb", W_MLSTM, BF16), ("vb", W_MLSTM, BF16),
              ("ob", W_MLSTM, F32), ("g", 2 * LANES, F32),
              ("qc", H_NBHD * LANES, BF16), ("kc", W_NBHD, kv_dtype), ("vc", W_NBHD, kv_dtype)]
    out_specs = [pl.BlockSpec((tm, w), row) for _, w, _ in widths]
    out_shape = [jax.ShapeDtypeStruct((T, w), dt) for _, w, dt in widths]
    for w, blk in zip((W_KV, W_NBHD), vt_blocks):
        out_specs.append(pl.BlockSpec((tm // blk, w, blk), lambda i: (i, 0, 0)))
        out_shape.append(jax.ShapeDtypeStruct((T // blk, w, blk), BF16))
    est = 2 * (tm * D_MODEL * 4 + D_MODEL * IN_PAD * 2 + tm * IN_PAD * 4) + 3 * tm * IN_PAD * 4
    outs = pl.pallas_call(
        functools.partial(_inproj_kernel, rope=rope),
        grid=(T // tm,),
        in_specs=in_specs, out_specs=out_specs, out_shape=out_shape,
        compiler_params=pltpu.CompilerParams(dimension_semantics=("arbitrary",),
                                             vmem_limit_bytes=_vmem_limit(est)),
        name=name,
    )(*args)
    return dict(zip([n for n, _, _ in widths] + ["vat", "vct"], outs))


SUM_ROWS = 16


def _softmax_step(st, col_max, vt, m_scr, acc_scr, u):
    m_old = m_scr[u]
    m_new = jnp.maximum(m_old, col_max)
    p = jnp.exp2(st - m_new).astype(BF16)
    alpha = jnp.exp2(m_old - m_new)
    vt1 = jnp.concatenate([vt, jnp.ones((SUM_ROWS, vt.shape[1]), BF16)], axis=0)
    acc_scr[u] = alpha * acc_scr[u] + _dot(vt1, p)
    m_scr[u] = m_new


def _normalised_pair(acc_scr, u0, u1):
    halves = [acc_scr[u][0:HEAD_DIM] / acc_scr[u][HEAD_DIM:HEAD_DIM + 1] for u in (u0, u1)]
    return jnp.concatenate(halves, axis=0).T
def _attn_kernel(*refs, heads, part_chunks):
    n_parts = len(part_chunks)
    assert n_parts in (1, 2)
    q_ref, kv_refs = refs[0], refs[1:1 + 2 * n_parts]
    o_ref, m_scr, acc_scr, s_ring, cmax_scr = refs[1 + 2 * n_parts:]
    nch = sum(part_chunks)

    def chunk_of(which, j, rows, cols):
        first = kv_refs[which][0, jnp.minimum(j, part_chunks[0] - 1), rows, cols].astype(BF16)
        if n_parts == 1:
            return first
        j2 = jnp.clip(j - part_chunks[0], 0, part_chunks[1] - 1)
        return jnp.where(j < part_chunks[0], first, kv_refs[2 + which][0, j2, rows, cols].astype(BF16))

    nh = len(heads)
    ring = ATTN_LOOKAHEAD + 1
    assert nh % ring == 0
    m_scr[...] = jnp.full(m_scr.shape, -jnp.inf, F32)
    acc_scr[...] = jnp.zeros(acc_scr.shape, F32)

    def scores(j, item):
        h = item % nh
        kg = heads[h][0]
        kj = chunk_of(0, j, slice(None), slice(kg * LANES, (kg + 1) * LANES))
        st = _dot_nt(kj, q_ref[0, :, h * LANES:(h + 1) * LANES])
        s_ring[item % ring] = st
        cmax_scr[item % ring] = jnp.max(st, axis=0, keepdims=True)

    def chunk(j, carry):
        j_next = jnp.minimum(j + 1, nch - 1)
        for h, (kg, kh) in enumerate(heads):
            ahead = h + ATTN_LOOKAHEAD
            scores(j if ahead < nh else j_next, ahead)
            r = kg * LANES + kh * HEAD_DIM
            vt = chunk_of(1, j, slice(r, r + HEAD_DIM), slice(None))
            _softmax_step(s_ring[h % ring], cmax_scr[h % ring], vt, m_scr, acc_scr, h)
        return carry

    for item in range(ATTN_LOOKAHEAD):
        scores(0, item)
    lax.fori_loop(0, nch, chunk, 0)
    for t in range(len(heads) // 2):
        o_ref[0, :, t * LANES:(t + 1) * LANES] = _normalised_pair(acc_scr, 2 * t, 2 * t + 1).astype(o_ref.dtype)


def _attn_seqs_kernel(q_ref, k_ref, vt_ref, o_ref, s_ring, cmax_scr, *, heads):
    nh = len(heads)
    n_seq = q_ref.shape[0]
    ring = nh
    look = nh - 1

    def scores(s, item):
        h = item % nh
        kg = heads[h][0]
        st = _dot_nt(k_ref[s, 0, :, kg * LANES:(kg + 1) * LANES].astype(BF16),
                     q_ref[s, :, h * LANES:(h + 1) * LANES])
        s_ring[item % ring] = st
        cmax_scr[item % ring] = jnp.max(st, axis=0, keepdims=True)

    def sequence(s, carry):
        s_next = jnp.minimum(s + 1, n_seq - 1)
        accs = []
        for h, (kg, kh) in enumerate(heads):
            ahead = h + look
            scores(s if ahead < nh else s_next, ahead)
            r = kg * LANES + kh * HEAD_DIM
            vt = vt_ref[s, 0, r:r + HEAD_DIM, :].astype(BF16)
            vt1 = jnp.concatenate([vt, jnp.ones((SUM_ROWS, vt.shape[1]), BF16)], axis=0)
            p = jnp.exp2(s_ring[h % ring] - cmax_scr[h % ring]).astype(BF16)
            accs.append(_dot(vt1, p))
        for t in range(nh // 2):
            halves = [a[0:HEAD_DIM] / a[HEAD_DIM:HEAD_DIM + 1] for a in accs[2 * t:2 * t + 2]]
            o_ref[s, :, t * LANES:(t + 1) * LANES] = jnp.concatenate(halves, axis=0).T.astype(o_ref.dtype)
        return carry

    for item in range(look):
        scores(0, item)
    lax.fori_loop(0, n_seq, sequence, 0)


def _attention_seqs(q, k4, vt4, *, heads, name):
    B, Sq, WQ = q.shape
    ck, KW = k4.shape[2], k4.shape[3]
    nh = len(heads)
    W = nh * HEAD_DIM
    bb = min(SEQS_PER_STEP, B)
    ring = nh
    est = (2 * bb * (Sq * (WQ + W) * 2 + ck * KW * (k4.dtype.itemsize + vt4.dtype.itemsize))
           + ring * ck * Sq * 4 + 8 * nh * ck * Sq * 4)
    return pl.pallas_call(
        functools.partial(_attn_seqs_kernel, heads=heads),
        grid=(B // bb,),
        in_specs=[pl.BlockSpec((bb, Sq, WQ), lambda i: (i, 0, 0)),
                  pl.BlockSpec((bb, 1, ck, KW), lambda i: (i, 0, 0, 0)),
                  pl.BlockSpec((bb, 1, KW, ck), lambda i: (i, 0, 0, 0))],
        out_specs=pl.BlockSpec((bb, Sq, W), lambda i: (i, 0, 0)),
        out_shape=jax.ShapeDtypeStruct((B, Sq, W), BF16),
        scratch_shapes=[pltpu.VMEM((ring, ck, Sq), F32),
                        pltpu.VMEM((ring, 1, Sq), F32)],
        compiler_params=pltpu.CompilerParams(dimension_semantics=("arbitrary",),
                                             vmem_limit_bytes=_vmem_limit(est)),
        name=name,
    )(q, k4, vt4)


def _attention(q, parts, *, heads, name):
    B, Sq, WQ = q.shape
    ck, KW = parts[0][0].shape[2], parts[0][0].shape[3]
    tq = min(TQ_ATTN, Sq)
    nh = len(heads)
    W = nh * HEAD_DIM
    part_chunks = tuple(k4.shape[1] for k4, _ in parts)
    if part_chunks == (1,) and tq == Sq:
        return _attention_seqs(q, parts[0][0], parts[0][1], heads=heads, name=name)
    in_specs = [pl.BlockSpec((1, tq, WQ), lambda b, i: (b, i, 0))]
    args = [q]
    kv_bytes = 0
    for k4, vt4 in parts:
        assert k4.shape[2:] == (ck, KW) and vt4.shape[2:] == (KW, ck)
        in_specs += [pl.BlockSpec((1,) + k4.shape[1:], lambda b, i: (b, 0, 0, 0)),
                     pl.BlockSpec((1,) + vt4.shape[1:], lambda b, i: (b, 0, 0, 0))]
        args += [k4, vt4]
        kv_bytes += k4[0].size * k4.dtype.itemsize + vt4[0].size * vt4.dtype.itemsize
    est = (2 * (tq * (WQ + W) * 2 + kv_bytes) + nh * tq * (HEAD_DIM * 4 + 64) + 8 * nh * ck * tq * 4)
    return pl.pallas_call(
        functools.partial(_attn_kernel, heads=heads, part_chunks=part_chunks),
        grid=(B, Sq // tq),
        in_specs=in_specs,
        out_specs=pl.BlockSpec((1, tq, W), lambda b, i: (b, i, 0)),
        out_shape=jax.ShapeDtypeStruct((B, Sq, W), BF16),
        scratch_shapes=[pltpu.VMEM((nh, 1, tq), F32),
                        pltpu.VMEM((nh, HEAD_DIM + SUM_ROWS, tq), F32),
                        pltpu.VMEM((ATTN_LOOKAHEAD + 1, ck, tq), F32),
                        pltpu.VMEM((ATTN_LOOKAHEAD + 1, 1, tq), F32)],
        compiler_params=pltpu.CompilerParams(dimension_semantics=("arbitrary", "arbitrary"),
                                             vmem_limit_bytes=_vmem_limit(est)),
        name=name,
    )(*args)


def _transpose_kernel(a_ref, b_ref, at_ref, bt_ref):
    at_ref[0] = a_ref[0].T.astype(at_ref.dtype)
    bt_ref[0] = b_ref[0].T.astype(bt_ref.dtype)


def _cached_values_t(a, b, *, name):
    B, P, ca = a.shape
    cb = b.shape[2]
    return pl.pallas_call(
        _transpose_kernel,
        grid=(B,),
        in_specs=[pl.BlockSpec((1, P, ca), lambda i: (i, 0, 0)), pl.BlockSpec((1, P, cb), lambda i: (i, 0, 0))],
        out_specs=[pl.BlockSpec((1, ca, P), lambda i: (i, 0, 0)), pl.BlockSpec((1, cb, P), lambda i: (i, 0, 0))],
        out_shape=[jax.ShapeDtypeStruct((B, ca, P), BF16), jax.ShapeDtypeStruct((B, cb, P), BF16)],
        compiler_params=pltpu.CompilerParams(dimension_semantics=("arbitrary",)),
        name=name,
    )(a, b)


_HEADS_GQA = tuple((0, h // (H_ATTN // KV_ATTN)) for h in range(H_ATTN))
_HEADS_MHA = tuple((h // 2, h % 2) for h in range(H_NBHD))


def _nbhd_window(r, rows):
    kr = min(NA_ROWS, rows)
    return min(max(r - kr // 2, 0), rows - kr), kr


def _nbhd_patterns(rows):
    n_groups = rows // NB_GROUP
    pats = []
    for g in (0, 1, n_groups - 1):
        r0 = g * NB_GROUP
        pats.append((r0, min(max(r0 - NA_ROWS // 2, 0), rows - NB_SLAB)))
    return pats


def _nbhd_kernel(rb_ref, q_ref, k_ref, vt_ref, kc_ref, vct_ref, o_ref,
                 bias_scr, m_scr, acc_scr, s_ring, cmax_scr, *, rows):
    hp = pl.program_id(0)
    b = pl.program_id(1)
    gs = pl.program_id(2)
    n_groups = rows // NB_GROUP
    n_dr = 2 * NA_ROWS - 1
    n_dc = 2 * NA_COLS - 1
    tq = NB_GROUP * GRID_W
    ck = tq
    pats = _nbhd_patterns(rows)
    ring = ATTN_LOOKAHEAD + 1
    n_chunks = NB_SLAB * GRID_W // ck + 1
    assert n_chunks % ring == 0 and kc_ref.shape[1] == ck
    blk_rows = NB_KV_BLOCK // GRID_W

    @pl.when((b == 0) & (gs == 0))
    def _build_bias():
        shape = (GRID_W, LANES)
        w = lax.broadcasted_iota(jnp.int32, shape, 0)
        lane = lax.broadcasted_iota(jnp.int32, shape, 1)
        cc = lane % GRID_W
        second = lane >= GRID_W
        cs = jnp.clip(w - NA_COLS // 2, 0, GRID_W - NA_COLS)
        col_ok = (cc >= cs) & (cc < cs + NA_COLS)
        dc = cc - w + (NA_COLS - 1)
        for hh in range(2):
            base = (2 * hp + hh) * (n_dr * n_dc)
            tiles = {}
            for d in range(-1, n_dr):
                acc = jnp.zeros(shape, F32)
                for j in range(n_dc):
                    va = rb_ref[base + d * n_dc + j] * LOG2E if d >= 0 else 0.0
                    vb = rb_ref[base + (d + 1) * n_dc + j] * LOG2E if d + 1 < n_dr else 0.0
                    acc = acc + jnp.where(dc == j, jnp.where(second, vb, va), 0.0)
                tiles[d] = acc
            def query_row_tile(r, kra):
                rs, kr = _nbhd_window(r, rows)
                ok_a = rs <= kra < rs + kr
                ok_b = rs <= kra + 1 < rs + kr
                if not (ok_a or ok_b):
                    return jnp.full(shape, NEG, F32)
                row_ok = (jnp.logical_not(second) if ok_a and not ok_b else
                          second if ok_b and not ok_a else None)
                ok = col_ok if row_ok is None else (col_ok & row_ok)
                return jnp.where(ok, tiles[kra - r + (NA_ROWS - 1)], NEG)

            for pi, (r0, slab0) in enumerate(pats):
                for ip in range(NB_GROUP // 2):
                    for ap in range(NB_SLAB // 2):
                        kra = slab0 + 2 * ap
                        two_rows = jnp.concatenate([query_row_tile(r0 + 2 * ip, kra),
                                                    query_row_tile(r0 + 2 * ip + 1, kra)], axis=0)
                        bias_scr[hh, pi, ap * LANES:(ap + 1) * LANES, ip * LANES:(ip + 1) * LANES] = two_rows.T

    assert (2 * n_chunks) % ring == 0
    m_scr[...] = jnp.full(m_scr.shape, -jnp.inf, F32)
    acc_scr[...] = jnp.zeros(acc_scr.shape, F32)

    def group_of(s):
        g = gs * NB_STEP_GROUPS + s
        pat = jnp.where(g == 0, 0, jnp.where(g == n_groups - 1, 2, 1))
        slab0 = jnp.clip(g * NB_GROUP - NA_ROWS // 2, 0, rows - NB_SLAB)
        return pat, slab0 // blk_rows

    per_group = 2 * n_chunks
    nb = ck // NB_KV_BLOCK

    def scores(s, item):
        hh, c = (item % per_group) // n_chunks, item % n_chunks
        qm = q_ref[0, pl.ds(pl.multiple_of(s * tq, tq), tq), hh * LANES:(hh + 1) * LANES]
        if c < n_chunks - 1:
            pat, blk0 = group_of(s)
            kc = k_ref[0, pl.ds(blk0 + c * nb, nb)].reshape(ck, LANES)
            st = _dot_nt(kc, qm) + bias_scr[hh, pat, c * ck:(c + 1) * ck, :]
        else:
            st = _dot_nt(kc_ref[0].astype(BF16), qm)
        s_ring[item % ring] = st
        cmax_scr[item % ring] = jnp.max(st, axis=0, keepdims=True)

    def group(s, carry):
        s_next = jnp.minimum(s + 1, NB_STEP_GROUPS - 1)
        for item in range(per_group):
            ahead = item + ATTN_LOOKAHEAD
            scores(s if ahead < per_group else s_next, ahead)
            hh, c = item // n_chunks, item % n_chunks
            if c < n_chunks - 1:
                _, blk0 = group_of(s)
                vt = jnp.concatenate([vt_ref[0, blk0 + c * nb + i, hh * HEAD_DIM:(hh + 1) * HEAD_DIM, :]
                                      for i in range(nb)], axis=1)
            else:
                vt = vct_ref[0, hh * HEAD_DIM:(hh + 1) * HEAD_DIM, :].astype(BF16)
            _softmax_step(s_ring[item % ring], cmax_scr[item % ring], vt, m_scr, acc_scr, 2 * s + hh)
        o_ref[0, pl.ds(pl.multiple_of(s * tq, tq), tq), :] = (
            _normalised_pair(acc_scr, 2 * s, 2 * s + 1).astype(o_ref.dtype))
        return carry

    for item in range(ATTN_LOOKAHEAD):
        scores(0, item)
    lax.fori_loop(0, NB_STEP_GROUPS, group, 0)


def _nbhd_attention(q, k4, vt4, k_ctx, vct, rel_bias_flat, *, name):
    B, S, _ = q.shape
    W = k4.shape[3]
    P = k_ctx.shape[1]
    rows = S // GRID_W
    tq = NB_GROUP * GRID_W
    nk = NB_SLAB * GRID_W
    nblk = S // NB_KV_BLOCK
    tqs = NB_STEP_GROUPS * tq
    nu = 2 * NB_STEP_GROUPS
    ring = ATTN_LOOKAHEAD + 1
    est =(2 * (2 * tqs * LANES * 2 + 2 * S * LANES * 2 + 2 * P * LANES * 4)
           + 2 * 3 * tq * nk * 4 + ring * tq * tq * 4 + 8 * tq * tq * 4 + nu * tq * 1024)
    return pl.pallas_call(
        functools.partial(_nbhd_kernel, rows=rows),
        grid=(W // LANES, B, rows // (NB_GROUP * NB_STEP_GROUPS)),
        in_specs=[pl.BlockSpec(memory_space=pltpu.SMEM),
                  pl.BlockSpec((1, tqs, 2 * LANES), lambda p, b, g: (b, g, p)),
                  pl.BlockSpec((1, nblk, NB_KV_BLOCK, LANES), lambda p, b, g: (b, 0, 0, p)),
                  pl.BlockSpec((1, nblk, LANES, NB_KV_BLOCK), lambda p, b, g: (b, 0, p, 0)),
                  pl.BlockSpec((1, P, LANES), lambda p, b, g: (b, 0, p)),
                  pl.BlockSpec((1, LANES, P), lambda p, b, g: (b, p, 0))],
        out_specs=pl.BlockSpec((1, tqs, LANES), lambda p, b, g: (b, g, p)),
        out_shape=jax.ShapeDtypeStruct((B, S, W), BF16),
        scratch_shapes=[pltpu.VMEM((2, 3, nk, tq), F32),
                        pltpu.VMEM((nu, 1, tq), F32),
                        pltpu.VMEM((nu, HEAD_DIM + SUM_ROWS, tq), F32),
                        pltpu.VMEM((ring, tq, tq), F32),
                        pltpu.VMEM((ring, 1, tq), F32)],
        compiler_params=pltpu.CompilerParams(dimension_semantics=("arbitrary",) * 3,
                                             vmem_limit_bytes=_vmem_limit(est)),
        name=name,
    )(rel_bias_flat, q, k4, vt4, k_ctx, vct)


def _split3(x):
    hi = x.astype(BF16)
    r = x - hi.astype(F32)
    mid = r.astype(BF16)
    return hi, mid, (r - mid.astype(F32)).astype(BF16)


def _mlstm_kernel(*refs, nc, grp_a, grp, has_state):
    q_ref, k_ref, v_ref, g_ref, ob_ref = refs[:5]
    s0_ref, m0_ref = refs[5:7] if has_state else (None, None)
    (on_ref, out_ref, sf_ref, mf_ref,
     h_scr, nat_scr, rows_scr, stat_scr, mprev_scr, un_scr, st_scr) = refs[7 if has_state else 5:]
    d = pl.program_id(1)
    L = L_CHUNK
    bb = q_ref.shape[0]
    NP = H_MLSTM // 2
    row = lax.broadcasted_iota(jnp.int32, (L, L), 0)
    col = lax.broadcasted_iota(jnp.int32, (L, L), 1)
    sign = 1 - 2 * d
    mask = (col - row) * sign <= 0
    maskb = mask.astype(BF16)
    mask3 = jnp.concatenate([maskb, maskb, maskb], axis=1)
    lane = lax.broadcasted_iota(jnp.int32, (L, LANES), 1)
    lo = lane < HEAD_DIM
    top = row < HEAD_DIM
    row2 = lax.broadcasted_iota(jnp.int32, (L, 2 * LANES), 0)
    col2 = lax.broadcasted_iota(jnp.int32, (L, 2 * LANES), 1)
    keep_state = (row2 < HEAD_DIM) == ((col2 % LANES) < HEAD_DIM)
    top2 = row2 < HEAD_DIM
    ones_b = jnp.ones((L, LANES), BF16)
    ones_lo = lo.astype(BF16)
    ones_hi = jnp.logical_not(lo).astype(BF16)

    def chunk_rows(c):
        return pl.ds(pl.multiple_of(c * L, L), L)

    def tokens(c):
        if bb == 1:
            return 0, chunk_rows(c)
        return c // nc, pl.ds(pl.multiple_of((c % nc) * L, L), L)

    def pass_a(it, carry):
        cs = [it * grp_a + u for u in range(grp_a)]
        gts = [g_ref[tokens(c)[0], tokens(c)[1], :] for c in cs]
        bns = [_dot(mask3, jnp.concatenate(_split3(gt), axis=0)) for gt in gts]
        a_all = []
        for c, gt, bn in zip(cs, gts, bns):
            nat = jnp.where(lane < H_MLSTM, gt, bn)
            nat_scr[chunk_rows(c), :] = nat * (-LOG2E)
            nat_t = nat.T
            b_rows = nat_t[H_MLSTM:2 * H_MLSTM]
            c_rows = nat_t[0:H_MLSTM] - b_rows
            rows_scr[c] = jnp.concatenate([c_rows * LOG2E, b_rows], axis=0)
            c_max = jnp.max(c_rows, axis=1, keepdims=True)
            b_tot = jnp.where(d == 0, b_rows[:, L - 1:L], b_rows[:, 0:1])
            stat_scr[c] = jnp.concatenate([jnp.broadcast_to(c_max, (H_MLSTM, LANES)),
                                           jnp.broadcast_to(b_tot, (H_MLSTM, LANES))], axis=0)
            a_all.append(jnp.exp(c_rows - c_max))
        for c, a_rows in zip(cs, a_all):
            sq, rows = tokens(c)
            for p in range(NP):
                lanes = slice(p * LANES, (p + 1) * LANES)
                k_t = k_ref[sq, rows, lanes].astype(F32).T
                a_sel = jnp.where(top, a_rows[2 * p:2 * p + 1], a_rows[2 * p + 1:2 * p + 2])
                vv = jnp.concatenate([v_ref[sq, rows, lanes], ones_b], axis=1)
                un = _dot((k_t * a_sel).astype(BF16), vv)
                un_scr[c, p] = jnp.where(keep_state, un, 0.0)
        return carry

    lax.fori_loop(0, bb * nc // grp_a, pass_a, 0)

    def pass_b(ci, m, sq):
        c = sq * nc + jnp.where(d == 0, ci, nc - 1 - ci)
        st = stat_scr[c]
        c_max, b_tot = st[0:H_MLSTM], st[H_MLSTM:]
        m_new = jnp.maximum(b_tot + m, b_tot + c_max)
        d_old = jnp.exp(b_tot + m - m_new)
        d_new = jnp.exp(b_tot + c_max - m_new)
        mprev_scr[c] = jnp.concatenate([m, m], axis=0) * LOG2E
        for p in range(NP):
            def rows_of(t, p=p):
                even = jnp.concatenate([t[2 * p:2 * p + 1]] * 2, axis=1)
                odd = jnp.concatenate([t[2 * p + 1:2 * p + 2]] * 2, axis=1)
                return jnp.where(top2, even, odd)
            s_prev = st_scr[p]
            st_scr[p] = rows_of(d_old) * s_prev + rows_of(d_new) * un_scr[c, p]
            un_scr[c, p] = s_prev
        return m_new

    def recurrence(sq, carry):
        if has_state:
            st_scr[...] = s0_ref[sq, 0]
            m_start = m0_ref[sq, 0][0:H_MLSTM]
        else:
            st_scr[...] = jnp.zeros(st_scr.shape, F32)
            m_start = jnp.zeros((H_MLSTM, LANES), F32)
        m_fin = lax.fori_loop(0, nc, functools.partial(pass_b, sq=sq), m_start)
        sf_ref[sq, 0] = st_scr[...]
        mf_ref[sq, 0] = jnp.concatenate([m_fin, m_fin], axis=0)
        return carry

    lax.fori_loop(0, bb, recurrence, 0)

    def pass_c(it, carry):
        cs = [it * grp + u for u in range(grp)]
        units = [(u, p) for u in range(grp) for p in range(NP)]
        early = {}
        for u, p in units:
            c = cs[u]
            sq, rows = tokens(c)
            lanes = slice(p * LANES, (p + 1) * LANES)
            qp = q_ref[sq, rows, lanes]
            kp = k_ref[sq, rows, lanes]
            s_in = un_scr[c, p].astype(BF16)
            qms = [jnp.where(lo if hh == 0 else jnp.logical_not(lo), qp, jnp.zeros_like(qp)) for hh in range(2)]
            early[u, p] = ([_dot_nt(qm, kp) for qm in qms],
                           _dot(qp, s_in))
        mid = {}
        for u, p in units:
            r_t = rows_scr[cs[u]]
            m_in = mprev_scr[cs[u]]
            for hh in range(2):
                h = 2 * p + hh
                cm = jnp.where(mask, r_t[h:h + 1, :], -jnp.inf)
                m_prev = m_in[h:h + 1, :]
                mu = jnp.maximum(jnp.broadcast_to(jnp.max(cm, axis=1, keepdims=True), (L, LANES)), m_prev)
                w = early[u, p][0][hh] * jnp.exp2(cm - mu)
                mid[u, p, hh] = (w.astype(BF16), mu, m_prev)
        for u, p in units:
            sq, rows = tokens(cs[u])
            lanes = slice(p * LANES, (p + 1) * LANES)
            vp = v_ref[sq, rows, lanes]
            zero = jnp.zeros_like(vp)
            vv = jnp.concatenate([jnp.concatenate([jnp.where(lo, vp, zero), ones_lo], axis=1),
                                  jnp.concatenate([jnp.where(lo, zero, vp), ones_hi], axis=1)], axis=0)
            w2 = jnp.concatenate([mid[u, p, 0][0], mid[u, p, 1][0]], axis=1)
            nd = _dot(w2, vv)
            nat = nat_scr[chunk_rows(cs[u]), :]
            nb = [jnp.broadcast_to(nat[:, H_MLSTM + 2 * p + hh:H_MLSTM + 2 * p + hh + 1], (L, LANES))
                  for hh in range(2)]
            fs = early[u, p][1]
            mu = jnp.where(lo, mid[u, p, 0][1], mid[u, p, 1][1])
            m_prev = jnp.where(lo[0:1], mid[u, p, 0][2], mid[u, p, 1][2])
            w_inter = jnp.exp2(m_prev - mu)
            den = nd[:, LANES:] + w_inter * fs[:, LANES:]
            den = jnp.maximum(jnp.abs(den), jnp.exp2(jnp.where(lo, nb[0], nb[1]) - mu))
            h_scr[d, chunk_rows(cs[u]), lanes] = (nd[:, :LANES] + w_inter * fs[:, :LANES]) / den
        return carry

    lax.fori_loop(0, bb * nc // grp, pass_c, 0)

    @pl.when(d == 1)
    def _finish():
        def rows_block(c, carry):
            sq, rows = tokens(c)
            flat = chunk_rows(c)
            for p in range(NP):
                lanes = slice(p * LANES, (p + 1) * LANES)
                hn = _pair_rms(h_scr[0, flat, lanes] + h_scr[1, flat, lanes], on_ref[:, lanes])
                out_ref[sq, rows, lanes] = (_sigmoid(ob_ref[sq, rows, lanes]) * hn).astype(out_ref.dtype)
            return carry

        lax.fori_loop(0, bb * nc, rows_block, 0, unroll=4)


def _mlstm(q, k, v, gates, ob, state, out_norm, *, name):
    assert L_CHUNK == LANES
    B, S, W = q.shape
    nc = S // L_CHUNK
    bb = max(1, min(B, MLSTM_STEP_CHUNKS // nc))
    ncb, sb = bb * nc, bb * S
    grp = min(MLSTM_GROUP, ncb)
    grp_a = min(MLSTM_GROUP_A, ncb)
    npair = H_MLSTM // 2
    seq = lambda b, d: (b, 0, 0)
    est = (2 * (3 * sb * W * 2 + sb * LANES * 4 + sb * W * 4 + sb * W * 2) + 2 * sb * W * 4 + sb * LANES * 4
           + ncb * npair * LANES * 2 * LANES * 4 + 12 * 1024 * 1024)
    state_specs = [pl.BlockSpec((bb, 1, npair, LANES, 2 * LANES), lambda b, d: (b, d, 0, 0, 0)),
                   pl.BlockSpec((bb, 1, 8, LANES), lambda b, d: (b, d, 0, 0))]
    has_state = state is not None
    return pl.pallas_call(
        functools.partial(_mlstm_kernel, nc=nc, grp_a=grp_a, grp=grp, has_state=has_state),
        grid=(B // bb, 2),
        in_specs=[pl.BlockSpec((bb, S, W), seq), pl.BlockSpec((bb, S, W), seq), pl.BlockSpec((bb, S, W), seq),
                  pl.BlockSpec((bb, S, LANES), lambda b, d: (b, 0, d)),
                  pl.BlockSpec((bb, S, W), seq)] + (state_specs if has_state else [])
                 + [pl.BlockSpec((1, W), lambda b, d: (0, 0))],
        out_specs=[pl.BlockSpec((bb, S, W), seq),
                   pl.BlockSpec((bb, 1, npair, LANES, 2 * LANES), lambda b, d: (b, d, 0, 0, 0)),
                   pl.BlockSpec((bb, 1, 8, LANES), lambda b, d: (b, d, 0, 0))],
        out_shape=[jax.ShapeDtypeStruct((B, S, W), BF16),
                   jax.ShapeDtypeStruct((B, 2, npair, LANES, 2 * LANES), F32),
                   jax.ShapeDtypeStruct((B, 2, 8, LANES), F32)],
        scratch_shapes=[pltpu.VMEM((2, sb, W), F32),
                        pltpu.VMEM((sb, LANES), F32),
                        pltpu.VMEM((ncb, 8, L_CHUNK), F32),
                        pltpu.VMEM((ncb, 8, LANES), F32),
                        pltpu.VMEM((ncb, 8, LANES), F32),
                        pltpu.VMEM((ncb, npair, LANES, 2 * LANES), F32),
                        pltpu.VMEM((npair, LANES, 2 * LANES), F32)],
        compiler_params=pltpu.CompilerParams(dimension_semantics=("arbitrary", "arbitrary"),
                                             vmem_limit_bytes=_vmem_limit(est)),
        name=name,
    )(q, k, v, gates, ob, *(state if has_state else ()), out_norm)


def _outproj_kernel(a_ref, b_ref, c_ref, x_ref, mod_ref, w_ref, gpost_ref, gpre_ref, x1_ref, h2_ref):
    mod = mod_ref[0]
    gt1 = mod[:, 2 * D_MODEL:3 * D_MODEL]
    sh2 = mod[:, 3 * D_MODEL:4 * D_MODEL]
    sc2 = mod[:, 4 * D_MODEL:5 * D_MODEL]
    tr = x_ref.shape[0] // ROW_SPLIT
    pieces = [pl.ds(s * tr, tr) for s in range(ROW_SPLIT)]
    mos = [_dot(jnp.concatenate([a_ref[r, :], b_ref[r, :], c_ref[r, :]], axis=1), w_ref[...]) for r in pieces]
    for r, mo in zip(pieces, mos):
        x1 = x_ref[r, :] + gt1 * _rms(mo, gpost_ref[...])
        x1_ref[r, :] = x1
        h2_ref[r, :] = (_rms(x1, gpre_ref[...]) * (1.0 + sc2) + sh2).astype(h2_ref.dtype)


def _outproj(oa, ob, oc, x, mods, w_out, g_post, g_pre, *, rows_per_cond, name):
    T = x.shape[0]
    tm = TM_PROJ
    bpc = rows_per_cond // tm
    row = lambda i: (i, 0)
    const = lambda i: (0, 0)
    est = 2 * (tm * D_MODEL * (2 + 4 + 4 + 2) + D_MODEL * D_MODEL * 2) + 4 * tm * D_MODEL * 4
    return pl.pallas_call(
        _outproj_kernel,
        grid=(T // tm,),
        in_specs=[pl.BlockSpec((tm, W_ATTN), row), pl.BlockSpec((tm, W_MLSTM), row),
                  pl.BlockSpec((tm, W_NBHD), row), pl.BlockSpec((tm, D_MODEL), row),
                  pl.BlockSpec((1, 1, N_MOD * D_MODEL), lambda i: (i // bpc, 0, 0)),
                  pl.BlockSpec((D_MODEL, D_MODEL), const),
                  pl.BlockSpec((1, D_MODEL), const), pl.BlockSpec((1, D_MODEL), const)],
        out_specs=[pl.BlockSpec((tm, D_MODEL), row), pl.BlockSpec((tm, D_MODEL), row)],
        out_shape=[jax.ShapeDtypeStruct((T, D_MODEL), F32), jax.ShapeDtypeStruct((T, D_MODEL), BF16)],
        compiler_params=pltpu.CompilerParams(dimension_semantics=("arbitrary",),
                                             vmem_limit_bytes=_vmem_limit(est)),
        name=name,
    )(oa, ob, oc, x, mods, w_out, g_post, g_pre)


def _ffn_kernel(h_ref, x_ref, mod_ref, wu_ref, wd_ref, g_ref, o_ref, acc_ref):
    j = pl.program_id(1)

    @pl.when(j == 0)
    def _zero():
        acc_ref[...] = jnp.zeros_like(acc_ref)

    u = jnp.maximum(_dot(h_ref[...], wu_ref[...]), 0.0)
    acc_ref[...] += _dot((u * u).astype(BF16), wd_ref[...])

    @pl.when(j == pl.num_programs(1) - 1)
    def _finish():
        gt2 = mod_ref[0][:, 5 * D_MODEL:6 * D_MODEL]
        o_ref[...] = x_ref[...] + gt2 * _rms(acc_ref[...], g_ref[...])


def _ffn(h2, x1, mods, w_up, w_down, g_post, *, rows_per_cond, name):
    T = x1.shape[0]
    tm, tf = TM_FFN, TF_FFN
    tm = min(tm, rows_per_cond)
    bpc = rows_per_cond // tm
    est = 2 * (tm * D_MODEL * (2 + 4 + 4) + 2 * D_MODEL * tf * 2) + tm * D_MODEL * 4 + 3 * tm * tf * 4
    return pl.pallas_call(
        _ffn_kernel,
        grid=(T // tm, D_FF // tf),
        in_specs=[pl.BlockSpec((tm, D_MODEL), lambda i, j: (i, 0)),
                  pl.BlockSpec((tm, D_MODEL), lambda i, j: (i, 0)),
                  pl.BlockSpec((1, 1, N_MOD * D_MODEL), lambda i, j: (i // bpc, 0, 0)),
                  pl.BlockSpec((D_MODEL, tf), lambda i, j: (0, j)),
                  pl.BlockSpec((tf, D_MODEL), lambda i, j: (j, 0)),
                  pl.BlockSpec((1, D_MODEL), lambda i, j: (0, 0))],
        out_specs=pl.BlockSpec((tm, D_MODEL), lambda i, j: (i, 0)),
        out_shape=jax.ShapeDtypeStruct((T, D_MODEL), F32),
        scratch_shapes=[pltpu.VMEM((tm, D_MODEL), F32)],
        compiler_params=pltpu.CompilerParams(dimension_semantics=("arbitrary", "arbitrary"),
                                             vmem_limit_bytes=_vmem_limit(est)),
        name=name,
    )(h2, x1, mods, w_up, w_down, g_post)


def _pad_w_in(w_in_l):
    o = W_ATTN + 2 * W_KV + 4 * W_MLSTM
    pre, gates, post = w_in_l[:, :o], w_in_l[:, o:o + N_GATES], w_in_l[:, o + N_GATES:]
    z = jnp.zeros((D_MODEL, LANES - 2 * H_MLSTM), w_in_l.dtype)
    return jnp.concatenate([pre, gates[:, :2 * H_MLSTM], z, gates[:, 2 * H_MLSTM:], z, post],
                           axis=1).astype(BF16)


def _pad_gate_bias(gb_l):
    z = jnp.zeros((LANES - 2 * H_MLSTM,), gb_l.dtype)
    return jnp.concatenate([gb_l[:2 * H_MLSTM], z, gb_l[2 * H_MLSTM:], z]).reshape(1, 2 * LANES)


def _rope_tables(S):
    quarter = HEAD_DIM // 4
    pos = np.arange(S)
    inv_freq = np.float32(ROPE_THETA) ** (-np.arange(quarter, dtype=np.float32) / np.float32(quarter))

    def tabs(p):
        ang = p.astype(np.float32)[:, None] * inv_freq[None, :]
        return np.cos(ang), np.sin(ang)

    cr, sr = tabs(pos // GRID_W)
    cc, sc = tabs(pos % GRID_W)
    cos = np.concatenate([cr, cr, cc, cc], axis=1)
    sin = np.concatenate([-sr, sr, -sc, sc], axis=1)
    return jnp.asarray(np.tile(cos, (1, 2)), F32), jnp.asarray(np.tile(sin, (1, 2)), F32)


def _pack_state(C, n, m):
    B = C.shape[0]
    Cp = C.reshape(B, 2, 2, 2, HEAD_DIM, HEAD_DIM)
    z = jnp.zeros_like(Cp[:, :, :, 0])
    top = jnp.concatenate([Cp[:, :, :, 0], z], axis=-1)
    bot = jnp.concatenate([z, Cp[:, :, :, 1]], axis=-1)
    Cbd = jnp.concatenate([top, bot], axis=-2)
    n_rep = jnp.broadcast_to(n.reshape(B, 2, 2, LANES, 1), (B, 2, 2, LANES, LANES))
    same_head = (jnp.arange(LANES)[:, None] < HEAD_DIM) == (jnp.arange(LANES)[None, :] < HEAD_DIM)
    n_rep = jnp.where(same_head, n_rep, 0.0)
    m_rows = jnp.broadcast_to(m[..., None], m.shape + (LANES,))
    return jnp.concatenate([Cbd, n_rep], axis=-1), jnp.concatenate([m_rows, m_rows], axis=-2)


def _unpack_state(s_p, m_p):
    B = s_p.shape[0]
    c_even = s_p[:, :, :, :HEAD_DIM, :HEAD_DIM]
    c_odd = s_p[:, :, :, HEAD_DIM:, HEAD_DIM:LANES]
    C = jnp.stack([c_even, c_odd], axis=3).reshape(B, 2, H_MLSTM, HEAD_DIM, HEAD_DIM)
    n = jnp.concatenate([s_p[..., :HEAD_DIM, LANES], s_p[..., HEAD_DIM:, LANES + HEAD_DIM]], axis=-1)
    return C, n.reshape(B, 2, H_MLSTM, HEAD_DIM), m_p[:, :, :H_MLSTM, 0]


def _layer(x, mods, lw, *, B, S, cond_rows, rope_tabs, ctx_cache, state, name):
    T = B * S
    kv_dtype = BF16 if ctx_cache is not None else F32
    ck_a = min(CK_ATTN, S)
    pr = _inproj(x, mods, lw["g_pre_mix"], lw["w_in"], lw["q_norm"], lw["k_norm"], lw["gate_bias"],
                 rope_tabs, rows_per_cond=cond_rows, kv_dtype=kv_dtype,
                 vt_blocks=(ck_a, NB_KV_BLOCK), name=name + "_inproj")
    seq = lambda a: a.reshape(B, S, a.shape[-1])
    qa, ka, va = seq(pr["qa"]), seq(pr["ka"]), seq(pr["va"])
    qc, kc, vc = seq(pr["qc"]), seq(pr["kc"]), seq(pr["vc"])
    chunks = lambda a, n: a.reshape(B, S // n, n, a.shape[-1])
    vat = pr["vat"].reshape(B, S // ck_a, W_KV, ck_a)
    vct = pr["vct"].reshape(B, S // NB_KV_BLOCK, W_NBHD, NB_KV_BLOCK)
    if ctx_cache is None:
        out_a = _attention(qa, [(chunks(ka, ck_a), vat)], heads=_HEADS_GQA, name=name + "_attn_a")
        out_c = _attention(qc, [(chunks(kc, NB_KV_BLOCK), vct)], heads=_HEADS_MHA, name=name + "_attn_c")
    else:
        ck_c, cv_c = ctx_cache[2], ctx_cache[3]
        P = ck_c.shape[1]
        assert P == ck_a
        cva_t, cvc_t = _cached_values_t(ctx_cache[1], cv_c, name=name + "_cache_t")
        out_a = _attention(qa, [(chunks(ka, ck_a), vat),
                                (ctx_cache[0].reshape(B, 1, P, W_KV), cva_t.reshape(B, 1, W_KV, P))],
                           heads=_HEADS_GQA, name=name + "_attn_a")
        out_c = _nbhd_attention(qc, chunks(kc, NB_KV_BLOCK), vct, ck_c, cvc_t,
                                lw["rel_bias"], name=name + "_attn_c")
    out_b, sf, mf = _mlstm(seq(pr["qb"]), seq(pr["kb"]), seq(pr["vb"]), seq(pr["g"]), seq(pr["ob"]),
                           state, lw["out_norm"], name=name + "_mlstm")
    x1, h2 = _outproj(out_a.reshape(T, W_ATTN), out_b.reshape(T, W_MLSTM), out_c.reshape(T, W_NBHD),
                      x, mods, lw["w_out"], lw["g_post_mix"], lw["g_pre_ffn"],
                      rows_per_cond=cond_rows, name=name + "_outproj")
    x2 = _ffn(h2, x1, mods, lw["w_up"], lw["w_down"], lw["g_post_ffn"],
              rows_per_cond=cond_rows, name=name + "_ffn")
    return x2, (ka, va, kc, vc, sf, mf)


def kernel(x_prompt, x_sample, c, cache_k_attn, cache_v_attn, cache_k_nbhd, cache_v_nbhd, state_mlstm_C, state_mlstm_n, state_mlstm_m, c_ctx, w_ada, b_ada, g_pre_mix, g_post_mix, g_pre_ffn, g_post_ffn, w_in, q_norm_attn, k_norm_attn, mlstm_gate_bias, mlstm_out_norm, nbhd_rel_bias, w_out, w_ffn_up, w_ffn_down):
    Bc, Sc, _ = x_prompt.shape
    Bl, Sl, _ = x_sample.shape
    P = cache_k_attn.shape[2]
    n_cond = 8
    cond = jnp.concatenate([c_ctx[None, :], c, jnp.zeros((n_cond - 1 - Bl, D_MODEL), F32)], axis=0)
    mods_all = _modulation(cond, w_ada, b_ada)

    layers = []
    for l in range(DEPTH):
        layers.append(dict(
            w_in=_pad_w_in(w_in[l]),
            w_out=w_out[l].astype(BF16),
            w_up=w_ffn_up[l].astype(BF16),
            w_down=w_ffn_down[l].astype(BF16),
            g_pre_mix=g_pre_mix[l].reshape(1, D_MODEL), g_post_mix=g_post_mix[l].reshape(1, D_MODEL),
            g_pre_ffn=g_pre_ffn[l].reshape(1, D_MODEL), g_post_ffn=g_post_ffn[l].reshape(1, D_MODEL),
            q_norm=jnp.tile(q_norm_attn[l], 2).reshape(1, LANES),
            k_norm=jnp.tile(k_norm_attn[l], 2).reshape(1, LANES),
            gate_bias=_pad_gate_bias(mlstm_gate_bias[l]),
            out_norm=mlstm_out_norm[l].reshape(1, W_MLSTM),
            rel_bias=nbhd_rel_bias[l].reshape(-1),
        ))

    xp = x_prompt.reshape(Bc * Sc, D_MODEL)
    ctx = []
    for l in range(DEPTH):
        mods = mods_all[l, 0:1].reshape(1, 1, N_MOD * D_MODEL)
        xp, extras = _layer(xp, mods, layers[l], B=Bc, S=Sc, cond_rows=Bc * Sc, rope_tabs=None,
                            ctx_cache=None, state=None, name=f"ctx{l}")
        ctx.append(extras)
    new_k_attn = jnp.stack([e[0].reshape(Bc, Sc, KV_ATTN, HEAD_DIM) for e in ctx], axis=1)
    new_v_attn = jnp.stack([e[1].reshape(Bc, Sc, KV_ATTN, HEAD_DIM) for e in ctx], axis=1)
    new_k_nbhd = jnp.stack([e[2].reshape(Bc, Sc, H_NBHD, HEAD_DIM) for e in ctx], axis=1)
    new_v_nbhd = jnp.stack([e[3].reshape(Bc, Sc, H_NBHD, HEAD_DIM) for e in ctx], axis=1)
    states = [_unpack_state(e[4], e[5]) for e in ctx]
    new_C = jnp.stack([s[0] for s in states], axis=1)
    new_n = jnp.stack([s[1] for s in states], axis=1)
    new_m = jnp.stack([s[2] for s in states], axis=1)

    xs = x_sample.reshape(Bl * Sl, D_MODEL)
    rope_tabs = _rope_tables(Sl)
    for l in range(DEPTH):
        mods = mods_all[l, 1:1 + Bl].reshape(Bl, 1, N_MOD * D_MODEL)
        cache = (cache_k_attn[:, l].reshape(Bl, P, W_KV), cache_v_attn[:, l].reshape(Bl, P, W_KV),
                 cache_k_nbhd[:, l].reshape(Bl, P, W_NBHD), cache_v_nbhd[:, l].reshape(Bl, P, W_NBHD))
        state = _pack_state(state_mlstm_C[:, l], state_mlstm_n[:, l], state_mlstm_m[:, l])
        xs, _ = _layer(xs, mods, layers[l], B=Bl, S=Sl, cond_rows=Sl, rope_tabs=rope_tabs,
                       ctx_cache=cache, state=state, name=f"lat{l}")

    return (xp.reshape(Bc, Sc, D_MODEL), xs.reshape(Bl, Sl, D_MODEL),
            new_k_attn, new_v_attn, new_k_nbhd, new_v_nbhd, new_C, new_n, new_m)
```

```python
import functools

import jax
import jax.numpy as jnp
import numpy as np
from jax import lax
from jax.experimental import pallas as pl
from jax.experimental.pallas import tpu as pltpu

F32 = jnp.float32
BF16 = jnp.bfloat16

D_MODEL = 1024
DEPTH = 2
GRID_W = 64
HEAD_DIM = 64
H_ATTN = 6
KV_ATTN = 2
H_MLSTM = 4
H_NBHD = 6
D_FF = 4 * D_MODEL
NA_ROWS = 8
NA_COLS = 16
ROPE_THETA = 10000.0
EPS = 1e-6
N_MOD = 6
W_ATTN = H_ATTN * HEAD_DIM
W_KV = KV_ATTN * HEAD_DIM
W_MLSTM = H_MLSTM * HEAD_DIM
W_NBHD = H_NBHD * HEAD_DIM
N_GATES = 4 * H_MLSTM

LANES = 128
V7X_VMEM_BYTES = 64 * 1024 * 1024
VMEM_CAP_BYTES = 56 * 1024 * 1024

TM_PROJ = 512
PROJ_CHUNK = 512
TM_FFN = 1024
TF_FFN = 1024
ROW_SPLIT = 4
TQ_ATTN = 512
CK_ATTN = 512
ATTN_LOOKAHEAD = 2
SEQS_PER_STEP = 8
L_CHUNK = 128
MLSTM_GROUP = 2
MLSTM_GROUP_A = 16
MLSTM_STEP_CHUNKS = 16
NB_GROUP = 8
NB_SLAB = 16
NB_STEP_GROUPS = 8
NB_KV_BLOCK = 256
NEG = -1e30
LOG2E = 1.4426950408889634

_COLS = {}
_off = 0
for _name, _w in (("qa", W_ATTN), ("ka", W_KV), ("va", W_KV), ("qb", W_MLSTM), ("kb", W_MLSTM),
                  ("vb", W_MLSTM), ("ob", W_MLSTM), ("gf", LANES), ("gb", LANES),
                  ("qc", W_NBHD), ("kc", W_NBHD), ("vc", W_NBHD)):
    _COLS[_name] = (_off, _off + _w)
    _off += _w
IN_PAD = _off


def _vmem_limit(nbytes):
    return int(min(max(nbytes, 16 * 1024 * 1024), VMEM_CAP_BYTES))


def _dot(a, b):
    return jnp.dot(a, b, preferred_element_type=F32)


def _dot_nt(a, b):
    return lax.dot_general(a, b, (((1,), (1,)), ((), ())), preferred_element_type=F32)


def _lane_lo(shape):
    return (lax.broadcasted_iota(jnp.int32, shape, len(shape) - 1) % LANES) < HEAD_DIM


def _rms(x, g):
    ms = jnp.mean(x * x, axis=-1, keepdims=True)
    return (x * lax.rsqrt(ms + EPS)) * g


def _pair_rms(x, g):
    lo = _lane_lo(x.shape)
    x2 = x * x
    s_lo = jnp.sum(jnp.where(lo, x2, 0.0), axis=-1, keepdims=True)
    s_hi = jnp.sum(jnp.where(lo, 0.0, x2), axis=-1, keepdims=True)
    r = jnp.where(lo, lax.rsqrt(s_lo / HEAD_DIM + EPS), lax.rsqrt(s_hi / HEAD_DIM + EPS))
    return (x * r) * g


def _sigmoid(x):
    return 1.0 / (1.0 + jnp.exp(-x))


def _mods_kernel(c_ref, w_ref, b_ref, o_ref):
    c = c_ref[...]
    s = (c * _sigmoid(c)).astype(BF16)
    o_ref[0] = _dot(s, w_ref[0].astype(BF16)) + b_ref[0]


def _modulation(cond, w_ada, b_ada):
    n = cond.shape[0]
    tn = D_MODEL
    return pl.pallas_call(
        _mods_kernel,
        grid=(DEPTH, N_MOD * D_MODEL // tn),
        in_specs=[pl.BlockSpec((n, D_MODEL), lambda l, j: (0, 0)),
                  pl.BlockSpec((1, D_MODEL, tn), lambda l, j: (l, 0, j)),
                  pl.BlockSpec((1, 1, tn), lambda l, j: (l, 0, j))],
        out_specs=pl.BlockSpec((1, n, tn), lambda l, j: (l, 0, j)),
        out_shape=jax.ShapeDtypeStruct((DEPTH, n, N_MOD * D_MODEL), F32),
        compiler_params=pltpu.CompilerParams(
            dimension_semantics=("arbitrary", "arbitrary"),
            vmem_limit_bytes=_vmem_limit(4 * D_MODEL * tn * 4)),
        name="modulation",
    )(cond, w_ada, b_ada.reshape(DEPTH, 1, N_MOD * D_MODEL))


def _inproj_kernel(*refs, rope):
    if rope:
        (x_ref, mod_ref, g_ref, w_ref, qn_ref, kn_ref, gbias_ref, cos_ref, sin_ref,
         qa_ref, ka_ref, va_ref, qb_ref, kb_ref, vb_ref, ob_ref, gate_ref,
         qc_ref, kc_ref, vc_ref, vat_ref, vct_ref) = refs
    else:
        (x_ref, mod_ref, g_ref, w_ref, qn_ref, kn_ref, gbias_ref,
         qa_ref, ka_ref, va_ref, qb_ref, kb_ref, vb_ref, ob_ref, gate_ref,
         qc_ref, kc_ref, vc_ref, vat_ref, vct_ref) = refs
    x = x_ref[...]
    mod = mod_ref[0]
    sh1 = mod[:, 0:D_MODEL]
    sc1 = mod[:, D_MODEL:2 * D_MODEL]
    hb = (_rms(x, g_ref[...]) * (1.0 + sc1) + sh1).astype(BF16)

    z = [_dot(hb, w_ref[:, c0:c0 + PROJ_CHUNK]) for c0 in range(0, IN_PAD, PROJ_CHUNK)]

    def proj(name, j=0, w=None):
        lo, hi = _COLS[name]
        lo = lo + j
        hi = hi if w is None else lo + w
        parts = []
        while lo < hi:
            c, o = divmod(lo, PROJ_CHUNK)
            n = min(hi - lo, PROJ_CHUNK - o)
            parts.append(z[c][:, o:o + n])
            lo += n
        return parts[0] if len(parts) == 1 else jnp.concatenate(parts, axis=1)

    scale = HEAD_DIM ** -0.5
    q_scale = scale * LOG2E

    def rotary(t):
        first = (lax.broadcasted_iota(jnp.int32, t.shape, 1) % 32) < 16
        partner = jnp.where(first, pltpu.roll(t, LANES - 16, 1), pltpu.roll(t, 16, 1))
        return t * cos_ref[...] + partner * sin_ref[...]

    def store_queries(t, j, q_ref, kv_half):
        lo = _lane_lo(t.shape)
        for e in range(2):
            h = 2 * j + e
            src = t if e == kv_half[h] else pltpu.roll(t, HEAD_DIM, 1)
            keep = lo if kv_half[h] == 0 else jnp.logical_not(lo)
            q_ref[:, h * LANES:(h + 1) * LANES] = jnp.where(keep, src, 0.0).astype(q_ref.dtype)

    for j in range(W_ATTN // LANES):
        t = _pair_rms(proj("qa", j * LANES, LANES), qn_ref[...])
        if rope:
            t = rotary(t)
        store_queries(t * q_scale, j, qa_ref, [kh for _, kh in _HEADS_GQA])
    t = _pair_rms(proj("ka"), kn_ref[...])
    if rope:
        t = rotary(t)
    ka_ref[...] = t.astype(ka_ref.dtype)

    def store_v(v, v_ref, vt_ref):
        v_ref[...] = v.astype(v_ref.dtype)
        nblk, _, blk = vt_ref.shape
        for u in range(nblk):
            vt_ref[u] = v[u * blk:(u + 1) * blk].T.astype(vt_ref.dtype)

    store_v(proj("va"), va_ref, vat_ref)
    qb_ref[...] = proj("qb").astype(qb_ref.dtype)
    kb_ref[...] = (proj("kb") * scale).astype(kb_ref.dtype)
    vb_ref[...] = proj("vb").astype(vb_ref.dtype)
    ob_ref[...] = proj("ob").astype(ob_ref.dtype)
    for j, name in enumerate(("gf", "gb")):
        gt = proj(name) + gbias_ref[:, j * LANES:(j + 1) * LANES]
        lane = lax.broadcasted_iota(jnp.int32, gt.shape, 1)
        is_f = (lane >= H_MLSTM) & (lane < 2 * H_MLSTM)
        logsig = jnp.minimum(gt, 0.0) - jnp.log1p(jnp.exp(-jnp.abs(gt)))
        gate_ref[:, j * LANES:(j + 1) * LANES] = jnp.where(is_f, logsig, gt)
    for j in range(W_NBHD // LANES):
        store_queries(proj("qc", j * LANES, LANES) * q_scale, j, qc_ref, [kh for _, kh in _HEADS_MHA])
    kc_ref[...] = proj("kc").astype(kc_ref.dtype)
    store_v(proj("vc"), vc_ref, vct_ref)


def _inproj(x, mods, g_pre, w_in_p, qn, kn, gbias, rope_tabs, *, rows_per_cond, kv_dtype, vt_blocks, name):
    T = x.shape[0]
    tm = TM_PROJ
    bpc = rows_per_cond // tm
    rope = rope_tabs is not None
    row = lambda i: (i, 0)
    const = lambda i: (0, 0)
    in_specs = [pl.BlockSpec((tm, D_MODEL), row),
                pl.BlockSpec((1, 1, N_MOD * D_MODEL), lambda i: (i // bpc, 0, 0)),
                pl.BlockSpec((1, D_MODEL), const),
                pl.BlockSpec((D_MODEL, IN_PAD), const),
                pl.BlockSpec((1, LANES), const),
                pl.BlockSpec((1, LANES), const),
                pl.BlockSpec((1, 2 * LANES), const)]
    args = [x, mods, g_pre, w_in_p, qn, kn, gbias]
    if rope:
        nblk = rope_tabs[0].shape[0] // tm
        in_specs += [pl.BlockSpec((tm, LANES), lambda i: (i % nblk, 0))] * 2
        args += list(rope_tabs)
    widths = [("qa", H_ATTN * LANES, BF16), ("ka", W_KV, kv_dtype), ("va", W_KV, kv_dtype),
              ("qb", W_MLSTM, BF16), ("kb", W_MLSTM, BF16), ("vb", W_MLSTM, BF16),
              ("ob", W_MLSTM, F32), ("g", 2 * LANES, F32),
              ("qc", H_NBHD * LANES, BF16), ("kc", W_NBHD, kv_dtype), ("vc", W_NBHD, kv_dtype)]
    out_specs = [pl.BlockSpec((tm, w), row) for _, w, _ in widths]
    out_shape = [jax.ShapeDtypeStruct((T, w), dt) for _, w, dt in widths]
    for w, blk in zip((W_KV, W_NBHD), vt_blocks):
        out_specs.append(pl.BlockSpec((tm // blk, w, blk), lambda i: (i, 0, 0)))
        out_shape.append(jax.ShapeDtypeStruct((T // blk, w, blk), BF16))
    est = 2 * (tm * D_MODEL * 4 + D_MODEL * IN_PAD * 2 + tm * IN_PAD * 4) + 3 * tm * IN_PAD * 4
    outs = pl.pallas_call(
        functools.partial(_inproj_kernel, rope=rope),
        grid=(T // tm,),
        in_specs=in_specs, out_specs=out_specs, out_shape=out_shape,
        compiler_params=pltpu.CompilerParams(dimension_semantics=("arbitrary",),
                                             vmem_limit_bytes=_vmem_limit(est)),
        name=name,
    )(*args)
    return dict(zip([n for n, _, _ in widths] + ["vat", "vct"], outs))


SUM_ROWS = 16


def _softmax_step(st, col_max, vt, m_scr, acc_scr, u):
    m_old = m_scr[u]
    m_new = jnp.maximum(m_old, col_max)
    p = jnp.exp2(st - m_new).astype(BF16)
    alpha = jnp.exp2(m_old - m_new)
    vt1 = jnp.concatenate([vt, jnp.ones((SUM_ROWS, vt.shape[1]), BF16)], axis=0)
    acc_scr[u] = alpha * acc_scr[u] + _dot(vt1, p)
    m_scr[u] = m_new


def _normalised_pair(acc_scr, u0, u1):
    halves = [acc_scr[u][0:HEAD_DIM] / acc_scr[u][HEAD_DIM:HEAD_DIM + 1] for u in (u0, u1)]
    return jnp.concatenate(halves, axis=0).T
def _attn_kernel(*refs, heads, part_chunks):
    n_parts = len(part_chunks)
    assert n_parts in (1, 2)
    q_ref, kv_refs = refs[0], refs[1:1 + 2 * n_parts]
    o_ref, m_scr, acc_scr, s_ring, cmax_scr = refs[1 + 2 * n_parts:]
    nch = sum(part_chunks)

    def chunk_of(which, j, rows, cols):
        first = kv_refs[which][0, jnp.minimum(j, part_chunks[0] - 1), rows, cols].astype(BF16)
        if n_parts == 1:
            return first
        j2 = jnp.clip(j - part_chunks[0], 0, part_chunks[1] - 1)
        return jnp.where(j < part_chunks[0], first, kv_refs[2 + which][0, j2, rows, cols].astype(BF16))

    nh = len(heads)
    ring = ATTN_LOOKAHEAD + 1
    assert nh % ring == 0
    m_scr[...] = jnp.full(m_scr.shape, -jnp.inf, F32)
    acc_scr[...] = jnp.zeros(acc_scr.shape, F32)

    def scores(j, item):
        h = item % nh
        kg = heads[h][0]
        kj = chunk_of(0, j, slice(None), slice(kg * LANES, (kg + 1) * LANES))
        st = _dot_nt(kj, q_ref[0, :, h * LANES:(h + 1) * LANES])
        s_ring[item % ring] = st
        cmax_scr[item % ring] = jnp.max(st, axis=0, keepdims=True)

    def chunk(j, carry, last=False):
        for h, (kg, kh) in enumerate(heads):
            ahead = h + ATTN_LOOKAHEAD
            if ahead < nh:
                scores(j, ahead)
            elif not last:
                scores(j + 1, ahead)
            r = kg * LANES + kh * HEAD_DIM
            vt = chunk_of(1, j, slice(r, r + HEAD_DIM), slice(None))
            _softmax_step(s_ring[h % ring], cmax_scr[h % ring], vt, m_scr, acc_scr, h)
        return carry

    for item in range(ATTN_LOOKAHEAD):
        scores(0, item)
    lax.fori_loop(0, nch - 1, chunk, 0)
    chunk(nch - 1, 0, last=True)
    for t in range(len(heads) // 2):
        o_ref[0, :, t * LANES:(t + 1) * LANES] = _normalised_pair(acc_scr, 2 * t, 2 * t + 1).astype(o_ref.dtype)


def _attn_seqs_kernel(q_ref, k_ref, vt_ref, o_ref, s_ring, cmax_scr, *, heads):
    nh = len(heads)
    n_seq = q_ref.shape[0]
    ring = nh
    look = nh - 1

    def scores(s, item):
        h = item % nh
        kg = heads[h][0]
        st = _dot_nt(k_ref[s, 0, :, kg * LANES:(kg + 1) * LANES].astype(BF16),
                     q_ref[s, :, h * LANES:(h + 1) * LANES])
        s_ring[item % ring] = st
        cmax_scr[item % ring] = jnp.max(st, axis=0, keepdims=True)

    def sequence(s, carry, last=False):
        accs = []
        for h, (kg, kh) in enumerate(heads):
            ahead = h + look
            if ahead < nh:
                scores(s, ahead)
            elif not last:
                scores(s + 1, ahead)
            r = kg * LANES + kh * HEAD_DIM
            vt = vt_ref[s, 0, r:r + HEAD_DIM, :].astype(BF16)
            vt1 = jnp.concatenate([vt, jnp.ones((SUM_ROWS, vt.shape[1]), BF16)], axis=0)
            p = jnp.exp2(s_ring[h % ring] - cmax_scr[h % ring]).astype(BF16)
            accs.append(_dot(vt1, p))
        for t in range(nh // 2):
            halves = [a[0:HEAD_DIM] / a[HEAD_DIM:HEAD_DIM + 1] for a in accs[2 * t:2 * t + 2]]
            o_ref[s, :, t * LANES:(t + 1) * LANES] = jnp.concatenate(halves, axis=0).T.astype(o_ref.dtype)
        return carry

    for item in range(look):
        scores(0, item)
    lax.fori_loop(0, n_seq - 1, sequence, 0)
    sequence(n_seq - 1, 0, last=True)


def _attention_seqs(q, k4, vt4, *, heads, name):
    B, Sq, WQ = q.shape
    ck, KW = k4.shape[2], k4.shape[3]
    nh = len(heads)
    W = nh * HEAD_DIM
    bb = min(SEQS_PER_STEP, B)
    ring = nh
    est = (2 * bb * (Sq * (WQ + W) * 2 + ck * KW * (k4.dtype.itemsize + vt4.dtype.itemsize))
           + ring * ck * Sq * 4 + 8 * nh * ck * Sq * 4)
    return pl.pallas_call(
        functools.partial(_attn_seqs_kernel, heads=heads),
        grid=(B // bb,),
        in_specs=[pl.BlockSpec((bb, Sq, WQ), lambda i: (i, 0, 0)),
                  pl.BlockSpec((bb, 1, ck, KW), lambda i: (i, 0, 0, 0)),
                  pl.BlockSpec((bb, 1, KW, ck), lambda i: (i, 0, 0, 0))],
        out_specs=pl.BlockSpec((bb, Sq, W), lambda i: (i, 0, 0)),
        out_shape=jax.ShapeDtypeStruct((B, Sq, W), BF16),
        scratch_shapes=[pltpu.VMEM((ring, ck, Sq), F32),
                        pltpu.VMEM((ring, 1, Sq), F32)],
        compiler_params=pltpu.CompilerParams(dimension_semantics=("arbitrary",),
                                             vmem_limit_bytes=_vmem_limit(est)),
        name=name,
    )(q, k4, vt4)


def _attention(q, parts, *, heads, name):
    B, Sq, WQ = q.shape
    ck, KW = parts[0][0].shape[2], parts[0][0].shape[3]
    tq = min(TQ_ATTN, Sq)
    nh = len(heads)
    W = nh * HEAD_DIM
    part_chunks = tuple(k4.shape[1] for k4, _ in parts)
    if part_chunks == (1,) and tq == Sq:
        return _attention_seqs(q, parts[0][0], parts[0][1], heads=heads, name=name)
    in_specs = [pl.BlockSpec((1, tq, WQ), lambda b, i: (b, i, 0))]
    args = [q]
    kv_bytes = 0
    for k4, vt4 in parts:
        assert k4.shape[2:] == (ck, KW) and vt4.shape[2:] == (KW, ck)
        in_specs += [pl.BlockSpec((1,) + k4.shape[1:], lambda b, i: (b, 0, 0, 0)),
                     pl.BlockSpec((1,) + vt4.shape[1:], lambda b, i: (b, 0, 0, 0))]
        args += [k4, vt4]
        kv_bytes += k4[0].size * k4.dtype.itemsize + vt4[0].size * vt4.dtype.itemsize
    est = (2 * (tq * (WQ + W) * 2 + kv_bytes) + nh * tq * (HEAD_DIM * 4 + 64) + 8 * nh * ck * tq * 4)
    return pl.pallas_call(
        functools.partial(_attn_kernel, heads=heads, part_chunks=part_chunks),
        grid=(B, Sq // tq),
        in_specs=in_specs,
        out_specs=pl.BlockSpec((1, tq, W), lambda b, i: (b, i, 0)),
        out_shape=jax.ShapeDtypeStruct((B, Sq, W), BF16),
        scratch_shapes=[pltpu.VMEM((nh, 1, tq), F32),
                        pltpu.VMEM((nh, HEAD_DIM + SUM_ROWS, tq), F32),
                        pltpu.VMEM((ATTN_LOOKAHEAD + 1, ck, tq), F32),
                        pltpu.VMEM((ATTN_LOOKAHEAD + 1, 1, tq), F32)],
        compiler_params=pltpu.CompilerParams(dimension_semantics=("arbitrary", "arbitrary"),
                                             vmem_limit_bytes=_vmem_limit(est)),
        name=name,
    )(*args)


def _transpose_kernel(a_ref, b_ref, at_ref, bt_ref):
    at_ref[0] = a_ref[0].T.astype(at_ref.dtype)
    bt_ref[0] = b_ref[0].T.astype(bt_ref.dtype)


def _cached_values_t(a, b, *, name):
    B, P, ca = a.shape
    cb = b.shape[2]
    return pl.pallas_call(
        _transpose_kernel,
        grid=(B,),
        in_specs=[pl.BlockSpec((1, P, ca), lambda i: (i, 0, 0)), pl.BlockSpec((1, P, cb), lambda i: (i, 0, 0))],
        out_specs=[pl.BlockSpec((1, ca, P), lambda i: (i, 0, 0)), pl.BlockSpec((1, cb, P), lambda i: (i, 0, 0))],
        out_shape=[jax.ShapeDtypeStruct((B, ca, P), BF16), jax.ShapeDtypeStruct((B, cb, P), BF16)],
        compiler_params=pltpu.CompilerParams(dimension_semantics=("arbitrary",)),
        name=name,
    )(a, b)


_HEADS_GQA = tuple((0, h // (H_ATTN // KV_ATTN)) for h in range(H_ATTN))
_HEADS_MHA = tuple((h // 2, h % 2) for h in range(H_NBHD))


def _nbhd_window(r, rows):
    kr = min(NA_ROWS, rows)
    return min(max(r - kr // 2, 0), rows - kr), kr


def _nbhd_patterns(rows):
    n_groups = rows // NB_GROUP
    pats = []
    for g in (0, 1, n_groups - 1):
        r0 = g * NB_GROUP
        pats.append((r0, min(max(r0 - NA_ROWS // 2, 0), rows - NB_SLAB)))
    return pats


def _nbhd_kernel(rb_ref, q_ref, k_ref, vt_ref, kc_ref, vct_ref, o_ref,
                 bias_scr, m_scr, acc_scr, s_ring, cmax_scr, *, rows):
    hp = pl.program_id(0)
    b = pl.program_id(1)
    gs = pl.program_id(2)
    n_groups = rows // NB_GROUP
    n_dr = 2 * NA_ROWS - 1
    n_dc = 2 * NA_COLS - 1
    tq = NB_GROUP * GRID_W
    ck = tq
    pats = _nbhd_patterns(rows)
    ring = ATTN_LOOKAHEAD + 1
    n_chunks = NB_SLAB * GRID_W // ck + 1
    assert n_chunks % ring == 0 and kc_ref.shape[1] == ck
    blk_rows = NB_KV_BLOCK // GRID_W

    @pl.when((b == 0) & (gs == 0))
    def _build_bias():
        shape = (GRID_W, LANES)
        w = lax.broadcasted_iota(jnp.int32, shape, 0)
        lane = lax.broadcasted_iota(jnp.int32, shape, 1)
        cc = lane % GRID_W
        second = lane >= GRID_W
        cs = jnp.clip(w - NA_COLS // 2, 0, GRID_W - NA_COLS)
        col_ok = (cc >= cs) & (cc < cs + NA_COLS)
        dc = cc - w + (NA_COLS - 1)
        for hh in range(2):
            base = (2 * hp + hh) * (n_dr * n_dc)
            tiles = {}
            for d in range(-1, n_dr):
                acc = jnp.zeros(shape, F32)
                for j in range(n_dc):
                    va = rb_ref[base + d * n_dc + j] * LOG2E if d >= 0 else 0.0
                    vb = rb_ref[base + (d + 1) * n_dc + j] * LOG2E if d + 1 < n_dr else 0.0
                    acc = acc + jnp.where(dc == j, jnp.where(second, vb, va), 0.0)
                tiles[d] = acc
            def query_row_tile(r, kra):
                rs, kr = _nbhd_window(r, rows)
                ok_a = rs <= kra < rs + kr
                ok_b = rs <= kra + 1 < rs + kr
                if not (ok_a or ok_b):
                    return jnp.full(shape, NEG, F32)
                row_ok = (jnp.logical_not(second) if ok_a and not ok_b else
                          second if ok_b and not ok_a else None)
                ok = col_ok if row_ok is None else (col_ok & row_ok)
                return jnp.where(ok, tiles[kra - r + (NA_ROWS - 1)], NEG)

            for pi, (r0, slab0) in enumerate(pats):
                for ip in range(NB_GROUP // 2):
                    for ap in range(NB_SLAB // 2):
                        kra = slab0 + 2 * ap
                        two_rows = jnp.concatenate([query_row_tile(r0 + 2 * ip, kra),
                                                    query_row_tile(r0 + 2 * ip + 1, kra)], axis=0)
                        bias_scr[hh, pi, ap * LANES:(ap + 1) * LANES, ip * LANES:(ip + 1) * LANES] = two_rows.T

    assert (2 * n_chunks) % ring == 0
    m_scr[...] = jnp.full(m_scr.shape, -jnp.inf, F32)
    acc_scr[...] = jnp.zeros(acc_scr.shape, F32)

    def group_of(s):
        g = gs * NB_STEP_GROUPS + s
        pat = jnp.where(g == 0, 0, jnp.where(g == n_groups - 1, 2, 1))
        slab0 = jnp.clip(g * NB_GROUP - NA_ROWS // 2, 0, rows - NB_SLAB)
        return pat, slab0 // blk_rows

    per_group = 2 * n_chunks
    nb = ck // NB_KV_BLOCK

    def scores(s, item):
        hh, c = (item % per_group) // n_chunks, item % n_chunks
        qm = q_ref[0, pl.ds(pl.multiple_of(s * tq, tq), tq), hh * LANES:(hh + 1) * LANES]
        if c < n_chunks - 1:
            pat, blk0 = group_of(s)
            kc = k_ref[0, pl.ds(blk0 + c * nb, nb)].reshape(ck, LANES)
            st = _dot_nt(kc, qm) + bias_scr[hh, pat, c * ck:(c + 1) * ck, :]
        else:
            st = _dot_nt(kc_ref[0].astype(BF16), qm)
        s_ring[item % ring] = st
        cmax_scr[item % ring] = jnp.max(st, axis=0, keepdims=True)

    def group(s, carry, last=False):
        for item in range(per_group):
            ahead = item + ATTN_LOOKAHEAD
            if ahead < per_group:
                scores(s, ahead)
            elif not last:
                scores(s + 1, ahead)
            hh, c = item // n_chunks, item % n_chunks
            if c < n_chunks - 1:
                _, blk0 = group_of(s)
                vt = jnp.concatenate([vt_ref[0, blk0 + c * nb + i, hh * HEAD_DIM:(hh + 1) * HEAD_DIM, :]
                                      for i in range(nb)], axis=1)
            else:
                vt = vct_ref[0, hh * HEAD_DIM:(hh + 1) * HEAD_DIM, :].astype(BF16)
            _softmax_step(s_ring[item % ring], cmax_scr[item % ring], vt, m_scr, acc_scr, 2 * s + hh)
        o_ref[0, pl.ds(pl.multiple_of(s * tq, tq), tq), :] = (
            _normalised_pair(acc_scr, 2 * s, 2 * s + 1).astype(o_ref.dtype))
        return carry

    for item in range(ATTN_LOOKAHEAD):
        scores(0, item)
    lax.fori_loop(0, NB_STEP_GROUPS - 1, group, 0)
    group(jnp.int32(NB_STEP_GROUPS - 1), 0, last=True)


def _nbhd_attention(q, k4, vt4, k_ctx, vct, rel_bias_flat, *, name):
    B, S, _ = q.shape
    W = k4.shape[3]
    P = k_ctx.shape[1]
    rows = S // GRID_W
    tq = NB_GROUP * GRID_W
    nk = NB_SLAB * GRID_W
    nblk = S // NB_KV_BLOCK
    tqs = NB_STEP_GROUPS * tq
    nu = 2 * NB_STEP_GROUPS
    ring = ATTN_LOOKAHEAD + 1
    est =(2 * (2 * tqs * LANES * 2 + 2 * S * LANES * 2 + 2 * P * LANES * 4)
           + 2 * 3 * tq * nk * 4 + ring * tq * tq * 4 + 8 * tq * tq * 4 + nu * tq * 1024)
    return pl.pallas_call(
        functools.partial(_nbhd_kernel, rows=rows),
        grid=(W // LANES, B, rows // (NB_GROUP * NB_STEP_GROUPS)),
        in_specs=[pl.BlockSpec(memory_space=pltpu.SMEM),
                  pl.BlockSpec((1, tqs, 2 * LANES), lambda p, b, g: (b, g, p)),
                  pl.BlockSpec((1, nblk, NB_KV_BLOCK, LANES), lambda p, b, g: (b, 0, 0, p)),
                  pl.BlockSpec((1, nblk, LANES, NB_KV_BLOCK), lambda p, b, g: (b, 0, p, 0)),
                  pl.BlockSpec((1, P, LANES), lambda p, b, g: (b, 0, p)),
                  pl.BlockSpec((1, LANES, P), lambda p, b, g: (b, p, 0))],
        out_specs=pl.BlockSpec((1, tqs, LANES), lambda p, b, g: (b, g, p)),
        out_shape=jax.ShapeDtypeStruct((B, S, W), BF16),
        scratch_shapes=[pltpu.VMEM((2, 3, nk, tq), F32),
                        pltpu.VMEM((nu, 1, tq), F32),
                        pltpu.VMEM((nu, HEAD_DIM + SUM_ROWS, tq), F32),
                        pltpu.VMEM((ring, tq, tq), F32),
                        pltpu.VMEM((ring, 1, tq), F32)],
        compiler_params=pltpu.CompilerParams(dimension_semantics=("arbitrary",) * 3,
                                             vmem_limit_bytes=_vmem_limit(est)),
        name=name,
    )(rel_bias_flat, q, k4, vt4, k_ctx, vct)


def _split3(x):
    hi = x.astype(BF16)
    r = x - hi.astype(F32)
    mid = r.astype(BF16)
    return hi, mid, (r - mid.astype(F32)).astype(BF16)


def _mlstm_kernel(*refs, nc, grp_a, grp, has_state):
    q_ref, k_ref, v_ref, g_ref, ob_ref = refs[:5]
    s0_ref, m0_ref = refs[5:7] if has_state else (None, None)
    (on_ref, out_ref, sf_ref, mf_ref,
     h_scr, nat_scr, rows_scr, stat_scr, mprev_scr, un_scr, st_scr) = refs[7 if has_state else 5:]
    d = pl.program_id(1)
    L = L_CHUNK
    bb = q_ref.shape[0]
    NP = H_MLSTM // 2
    row = lax.broadcasted_iota(jnp.int32, (L, L), 0)
    col = lax.broadcasted_iota(jnp.int32, (L, L), 1)
    sign = 1 - 2 * d
    mask = (col - row) * sign <= 0
    maskb = mask.astype(BF16)
    mask3 = jnp.concatenate([maskb, maskb, maskb], axis=1)
    lane = lax.broadcasted_iota(jnp.int32, (L, LANES), 1)
    lo = lane < HEAD_DIM
    top = row < HEAD_DIM
    row2 = lax.broadcasted_iota(jnp.int32, (L, 2 * LANES), 0)
    col2 = lax.broadcasted_iota(jnp.int32, (L, 2 * LANES), 1)
    keep_state = (row2 < HEAD_DIM) == ((col2 % LANES) < HEAD_DIM)
    top2 = row2 < HEAD_DIM
    ones_b = jnp.ones((L, LANES), BF16)
    ones_lo = lo.astype(BF16)
    ones_hi = jnp.logical_not(lo).astype(BF16)

    def chunk_rows(c):
        return pl.ds(pl.multiple_of(c * L, L), L)

    def tokens(c):
        if bb == 1:
            return 0, chunk_rows(c)
        return c // nc, pl.ds(pl.multiple_of((c % nc) * L, L), L)

    def pass_a(it, carry):
        cs = [it * grp_a + u for u in range(grp_a)]
        gts = [g_ref[tokens(c)[0], tokens(c)[1], :] for c in cs]
        bns = [_dot(mask3, jnp.concatenate(_split3(gt), axis=0)) for gt in gts]
        a_all = []
        for c, gt, bn in zip(cs, gts, bns):
            nat = jnp.where(lane < H_MLSTM, gt, bn)
            nat_scr[chunk_rows(c), :] = nat * (-LOG2E)
            nat_t = nat.T
            b_rows = nat_t[H_MLSTM:2 * H_MLSTM]
            c_rows = nat_t[0:H_MLSTM] - b_rows
            rows_scr[c] = jnp.concatenate([c_rows * LOG2E, b_rows], axis=0)
            c_max = jnp.max(c_rows, axis=1, keepdims=True)
            b_tot = jnp.where(d == 0, b_rows[:, L - 1:L], b_rows[:, 0:1])
            stat_scr[c] = jnp.concatenate([jnp.broadcast_to(c_max, (H_MLSTM, LANES)),
                                           jnp.broadcast_to(b_tot, (H_MLSTM, LANES))], axis=0)
            a_all.append(jnp.exp(c_rows - c_max))
        for c, a_rows in zip(cs, a_all):
            sq, rows = tokens(c)
            for p in range(NP):
                lanes = slice(p * LANES, (p + 1) * LANES)
                k_t = k_ref[sq, rows, lanes].astype(F32).T
                a_sel = jnp.where(top, a_rows[2 * p:2 * p + 1], a_rows[2 * p + 1:2 * p + 2])
                vv = jnp.concatenate([v_ref[sq, rows, lanes], ones_b], axis=1)
                un = _dot((k_t * a_sel).astype(BF16), vv)
                un_scr[c, p] = jnp.where(keep_state, un, 0.0)
        return carry

    lax.fori_loop(0, bb * nc // grp_a, pass_a, 0)

    def pass_b(ci, m, sq):
        c = sq * nc + jnp.where(d == 0, ci, nc - 1 - ci)
        st = stat_scr[c]
        c_max, b_tot = st[0:H_MLSTM], st[H_MLSTM:]
        m_new = jnp.maximum(b_tot + m, b_tot + c_max)
        d_old = jnp.exp(b_tot + m - m_new)
        d_new = jnp.exp(b_tot + c_max - m_new)
        mprev_scr[c] = jnp.concatenate([m, m], axis=0) * LOG2E
        for p in range(NP):
            def rows_of(t, p=p):
                even = jnp.concatenate([t[2 * p:2 * p + 1]] * 2, axis=1)
                odd = jnp.concatenate([t[2 * p + 1:2 * p + 2]] * 2, axis=1)
                return jnp.where(top2, even, odd)
            s_prev = st_scr[p]
            st_scr[p] = rows_of(d_old) * s_prev + rows_of(d_new) * un_scr[c, p]
            un_scr[c, p] = s_prev
        return m_new

    def recurrence(sq, carry):
        if has_state:
            st_scr[...] = s0_ref[sq, 0]
            m_start = m0_ref[sq, 0][0:H_MLSTM]
        else:
            st_scr[...] = jnp.zeros(st_scr.shape, F32)
            m_start = jnp.zeros((H_MLSTM, LANES), F32)
        m_fin = lax.fori_loop(0, nc, functools.partial(pass_b, sq=sq), m_start)
        sf_ref[sq, 0] = st_scr[...]
        mf_ref[sq, 0] = jnp.concatenate([m_fin, m_fin], axis=0)
        return carry

    lax.fori_loop(0, bb, recurrence, 0)

    def pass_c(it, carry):
        cs = [it * grp + u for u in range(grp)]
        units = [(u, p) for u in range(grp) for p in range(NP)]
        early = {}
        for u, p in units:
            c = cs[u]
            sq, rows = tokens(c)
            lanes = slice(p * LANES, (p + 1) * LANES)
            qp = q_ref[sq, rows, lanes]
            kp = k_ref[sq, rows, lanes]
            s_in = un_scr[c, p].astype(BF16)
            qms = [jnp.where(lo if hh == 0 else jnp.logical_not(lo), qp, jnp.zeros_like(qp)) for hh in range(2)]
            early[u, p] = ([_dot_nt(qm, kp) for qm in qms],
                           _dot(qp, s_in))
        mid = {}
        for u, p in units:
            r_t = rows_scr[cs[u]]
            m_in = mprev_scr[cs[u]]
            for hh in range(2):
                h = 2 * p + hh
                cm = jnp.where(mask, r_t[h:h + 1, :], -jnp.inf)
                m_prev = m_in[h:h + 1, :]
                mu = jnp.maximum(jnp.broadcast_to(jnp.max(cm, axis=1, keepdims=True), (L, LANES)), m_prev)
                w = early[u, p][0][hh] * jnp.exp2(cm - mu)
                mid[u, p, hh] = (w.astype(BF16), mu, m_prev)
        for u, p in units:
            sq, rows = tokens(cs[u])
            lanes = slice(p * LANES, (p + 1) * LANES)
            vp = v_ref[sq, rows, lanes]
            zero = jnp.zeros_like(vp)
            vv = jnp.concatenate([jnp.concatenate([jnp.where(lo, vp, zero), ones_lo], axis=1),
                                  jnp.concatenate([jnp.where(lo, zero, vp), ones_hi], axis=1)], axis=0)
            w2 = jnp.concatenate([mid[u, p, 0][0], mid[u, p, 1][0]], axis=1)
            nd = _dot(w2, vv)
            nat = nat_scr[chunk_rows(cs[u]), :]
            nb = [jnp.broadcast_to(nat[:, H_MLSTM + 2 * p + hh:H_MLSTM + 2 * p + hh + 1], (L, LANES))
                  for hh in range(2)]
            fs = early[u, p][1]
            mu = jnp.where(lo, mid[u, p, 0][1], mid[u, p, 1][1])
            m_prev = jnp.where(lo[0:1], mid[u, p, 0][2], mid[u, p, 1][2])
            w_inter = jnp.exp2(m_prev - mu)
            den = nd[:, LANES:] + w_inter * fs[:, LANES:]
            den = jnp.maximum(jnp.abs(den), jnp.exp2(jnp.where(lo, nb[0], nb[1]) - mu))
            h_scr[d, chunk_rows(cs[u]), lanes] = (nd[:, :LANES] + w_inter * fs[:, :LANES]) / den
        return carry

    lax.fori_loop(0, bb * nc // grp, pass_c, 0)

    @pl.when(d == 1)
    def _finish():
        def rows_block(c, carry):
            sq, rows = tokens(c)
            flat = chunk_rows(c)
            for p in range(NP):
                lanes = slice(p * LANES, (p + 1) * LANES)
                hn = _pair_rms(h_scr[0, flat, lanes] + h_scr[1, flat, lanes], on_ref[:, lanes])
                out_ref[sq, rows, lanes] = (_sigmoid(ob_ref[sq, rows, lanes]) * hn).astype(out_ref.dtype)
            return carry

        lax.fori_loop(0, bb * nc, rows_block, 0, unroll=4)


def _mlstm(q, k, v, gates, ob, state, out_norm, *, name):
    assert L_CHUNK == LANES
    B, S, W = q.shape
    nc = S // L_CHUNK
    bb = max(1, min(B, MLSTM_STEP_CHUNKS // nc))
    ncb, sb = bb * nc, bb * S
    grp = min(MLSTM_GROUP, ncb)
    grp_a = min(MLSTM_GROUP_A, ncb)
    npair = H_MLSTM // 2
    seq = lambda b, d: (b, 0, 0)
    est = (2 * (3 * sb * W * 2 + sb * LANES * 4 + sb * W * 4 + sb * W * 2) + 2 * sb * W * 4 + sb * LANES * 4
           + ncb * npair * LANES * 2 * LANES * 4 + 12 * 1024 * 1024)
    state_specs = [pl.BlockSpec((bb, 1, npair, LANES, 2 * LANES), lambda b, d: (b, d, 0, 0, 0)),
                   pl.BlockSpec((bb, 1, 8, LANES), lambda b, d: (b, d, 0, 0))]
    has_state = state is not None
    return pl.pallas_call(
        functools.partial(_mlstm_kernel, nc=nc, grp_a=grp_a, grp=grp, has_state=has_state),
        grid=(B // bb, 2),
        in_specs=[pl.BlockSpec((bb, S, W), seq), pl.BlockSpec((bb, S, W), seq), pl.BlockSpec((bb, S, W), seq),
                  pl.BlockSpec((bb, S, LANES), lambda b, d: (b, 0, d)),
                  pl.BlockSpec((bb, S, W), seq)] + (state_specs if has_state else [])
                 + [pl.BlockSpec((1, W), lambda b, d: (0, 0))],
        out_specs=[pl.BlockSpec((bb, S, W), seq),
                   pl.BlockSpec((bb, 1, npair, LANES, 2 * LANES), lambda b, d: (b, d, 0, 0, 0)),
                   pl.BlockSpec((bb, 1, 8, LANES), lambda b, d: (b, d, 0, 0))],
        out_shape=[jax.ShapeDtypeStruct((B, S, W), BF16),
                   jax.ShapeDtypeStruct((B, 2, npair, LANES, 2 * LANES), F32),
                   jax.ShapeDtypeStruct((B, 2, 8, LANES), F32)],
        scratch_shapes=[pltpu.VMEM((2, sb, W), F32),
                        pltpu.VMEM((sb, LANES), F32),
                        pltpu.VMEM((ncb, 8, L_CHUNK), F32),
                        pltpu.VMEM((ncb, 8, LANES), F32),
                        pltpu.VMEM((ncb, 8, LANES), F32),
                        pltpu.VMEM((ncb, npair, LANES, 2 * LANES), F32),
                        pltpu.VMEM((npair, LANES, 2 * LANES), F32)],
        compiler_params=pltpu.CompilerParams(dimension_semantics=("arbitrary", "arbitrary"),
                                             vmem_limit_bytes=_vmem_limit(est)),
        name=name,
    )(q, k, v, gates, ob, *(state if has_state else ()), out_norm)


def _outproj_kernel(a_ref, b_ref, c_ref, x_ref, mod_ref, w_ref, gpost_ref, gpre_ref, x1_ref, h2_ref):
    mod = mod_ref[0]
    gt1 = mod[:, 2 * D_MODEL:3 * D_MODEL]
    sh2 = mod[:, 3 * D_MODEL:4 * D_MODEL]
    sc2 = mod[:, 4 * D_MODEL:5 * D_MODEL]
    tr = x_ref.shape[0] // ROW_SPLIT
    pieces = [pl.ds(s * tr, tr) for s in range(ROW_SPLIT)]
    mos = [_dot(jnp.concatenate([a_ref[r, :], b_ref[r, :], c_ref[r, :]], axis=1), w_ref[...]) for r in pieces]
    for r, mo in zip(pieces, mos):
        x1 = x_ref[r, :] + gt1 * _rms(mo, gpost_ref[...])
        x1_ref[r, :] = x1
        h2_ref[r, :] = (_rms(x1, gpre_ref[...]) * (1.0 + sc2) + sh2).astype(h2_ref.dtype)


def _outproj(oa, ob, oc, x, mods, w_out, g_post, g_pre, *, rows_per_cond, name):
    T = x.shape[0]
    tm = TM_PROJ
    bpc = rows_per_cond // tm
    row = lambda i: (i, 0)
    const = lambda i: (0, 0)
    est = 2 * (tm * D_MODEL * (2 + 4 + 4 + 2) + D_MODEL * D_MODEL * 2) + 4 * tm * D_MODEL * 4
    return pl.pallas_call(
        _outproj_kernel,
        grid=(T // tm,),
        in_specs=[pl.BlockSpec((tm, W_ATTN), row), pl.BlockSpec((tm, W_MLSTM), row),
                  pl.BlockSpec((tm, W_NBHD), row), pl.BlockSpec((tm, D_MODEL), row),
                  pl.BlockSpec((1, 1, N_MOD * D_MODEL), lambda i: (i // bpc, 0, 0)),
                  pl.BlockSpec((D_MODEL, D_MODEL), const),
                  pl.BlockSpec((1, D_MODEL), const), pl.BlockSpec((1, D_MODEL), const)],
        out_specs=[pl.BlockSpec((tm, D_MODEL), row), pl.BlockSpec((tm, D_MODEL), row)],
        out_shape=[jax.ShapeDtypeStruct((T, D_MODEL), F32), jax.ShapeDtypeStruct((T, D_MODEL), BF16)],
        compiler_params=pltpu.CompilerParams(dimension_semantics=("arbitrary",),
                                             vmem_limit_bytes=_vmem_limit(est)),
        name=name,
    )(oa, ob, oc, x, mods, w_out, g_post, g_pre)


def _ffn_kernel(h_ref, x_ref, mod_ref, wu_ref, wd_ref, g_ref, o_ref, acc_ref):
    j = pl.program_id(1)

    @pl.when(j == 0)
    def _zero():
        acc_ref[...] = jnp.zeros_like(acc_ref)

    u = jnp.maximum(_dot(h_ref[...], wu_ref[...]), 0.0)
    acc_ref[...] += _dot((u * u).astype(BF16), wd_ref[...])

    @pl.when(j == pl.num_programs(1) - 1)
    def _finish():
        gt2 = mod_ref[0][:, 5 * D_MODEL:6 * D_MODEL]
        o_ref[...] = x_ref[...] + gt2 * _rms(acc_ref[...], g_ref[...])


def _ffn(h2, x1, mods, w_up, w_down, g_post, *, rows_per_cond, name):
    T = x1.shape[0]
    tm, tf = TM_FFN, TF_FFN
    tm = min(tm, rows_per_cond)
    bpc = rows_per_cond // tm
    est = 2 * (tm * D_MODEL * (2 + 4 + 4) + 2 * D_MODEL * tf * 2) + tm * D_MODEL * 4 + 3 * tm * tf * 4
    return pl.pallas_call(
        _ffn_kernel,
        grid=(T // tm, D_FF // tf),
        in_specs=[pl.BlockSpec((tm, D_MODEL), lambda i, j: (i, 0)),
                  pl.BlockSpec((tm, D_MODEL), lambda i, j: (i, 0)),
                  pl.BlockSpec((1, 1, N_MOD * D_MODEL), lambda i, j: (i // bpc, 0, 0)),
                  pl.BlockSpec((D_MODEL, tf), lambda i, j: (0, j)),
                  pl.BlockSpec((tf, D_MODEL), lambda i, j: (j, 0)),
                  pl.BlockSpec((1, D_MODEL), lambda i, j: (0, 0))],
        out_specs=pl.BlockSpec((tm, D_MODEL), lambda i, j: (i, 0)),
        out_shape=jax.ShapeDtypeStruct((T, D_MODEL), F32),
        scratch_shapes=[pltpu.VMEM((tm, D_MODEL), F32)],
        compiler_params=pltpu.CompilerParams(dimension_semantics=("arbitrary", "arbitrary"),
                                             vmem_limit_bytes=_vmem_limit(est)),
        name=name,
    )(h2, x1, mods, w_up, w_down, g_post)


def _pad_w_in(w_in_l):
    o = W_ATTN + 2 * W_KV + 4 * W_MLSTM
    pre, gates, post = w_in_l[:, :o], w_in_l[:, o:o + N_GATES], w_in_l[:, o + N_GATES:]
    z = jnp.zeros((D_MODEL, LANES - 2 * H_MLSTM), w_in_l.dtype)
    return jnp.concatenate([pre, gates[:, :2 * H_MLSTM], z, gates[:, 2 * H_MLSTM:], z, post],
                           axis=1).astype(BF16)


def _pad_gate_bias(gb_l):
    z = jnp.zeros((LANES - 2 * H_MLSTM,), gb_l.dtype)
    return jnp.concatenate([gb_l[:2 * H_MLSTM], z, gb_l[2 * H_MLSTM:], z]).reshape(1, 2 * LANES)


def _rope_tables(S):
    quarter = HEAD_DIM // 4
    pos = np.arange(S)
    inv_freq = np.float32(ROPE_THETA) ** (-np.arange(quarter, dtype=np.float32) / np.float32(quarter))

    def tabs(p):
        ang = p.astype(np.float32)[:, None] * inv_freq[None, :]
        return np.cos(ang), np.sin(ang)

    cr, sr = tabs(pos // GRID_W)
    cc, sc = tabs(pos % GRID_W)
    cos = np.concatenate([cr, cr, cc, cc], axis=1)
    sin = np.concatenate([-sr, sr, -sc, sc], axis=1)
    return jnp.asarray(np.tile(cos, (1, 2)), F32), jnp.asarray(np.tile(sin, (1, 2)), F32)


def _pack_state(C, n, m):
    B = C.shape[0]
    Cp = C.reshape(B, 2, 2, 2, HEAD_DIM, HEAD_DIM)
    z = jnp.zeros_like(Cp[:, :, :, 0])
    top = jnp.concatenate([Cp[:, :, :, 0], z], axis=-1)
    bot = jnp.concatenate([z, Cp[:, :, :, 1]], axis=-1)
    Cbd = jnp.concatenate([top, bot], axis=-2)
    n_rep = jnp.broadcast_to(n.reshape(B, 2, 2, LANES, 1), (B, 2, 2, LANES, LANES))
    same_head = (jnp.arange(LANES)[:, None] < HEAD_DIM) == (jnp.arange(LANES)[None, :] < HEAD_DIM)
    n_rep = jnp.where(same_head, n_rep, 0.0)
    m_rows = jnp.broadcast_to(m[..., None], m.shape + (LANES,))
    return jnp.concatenate([Cbd, n_rep], axis=-1), jnp.concatenate([m_rows, m_rows], axis=-2)


def _unpack_state(s_p, m_p):
    B = s_p.shape[0]
    c_even = s_p[:, :, :, :HEAD_DIM, :HEAD_DIM]
    c_odd = s_p[:, :, :, HEAD_DIM:, HEAD_DIM:LANES]
    C = jnp.stack([c_even, c_odd], axis=3).reshape(B, 2, H_MLSTM, HEAD_DIM, HEAD_DIM)
    n = jnp.concatenate([s_p[..., :HEAD_DIM, LANES], s_p[..., HEAD_DIM:, LANES + HEAD_DIM]], axis=-1)
    return C, n.reshape(B, 2, H_MLSTM, HEAD_DIM), m_p[:, :, :H_MLSTM, 0]


def _layer(x, mods, lw, *, B, S, cond_rows, rope_tabs, ctx_cache, state, name):
    T = B * S
    kv_dtype = BF16 if ctx_cache is not None else F32
    ck_a = min(CK_ATTN, S)
    pr = _inproj(x, mods, lw["g_pre_mix"], lw["w_in"], lw["q_norm"], lw["k_norm"], lw["gate_bias"],
                 rope_tabs, rows_per_cond=cond_rows, kv_dtype=kv_dtype,
                 vt_blocks=(ck_a, NB_KV_BLOCK), name=name + "_inproj")
    seq = lambda a: a.reshape(B, S, a.shape[-1])
    qa, ka, va = seq(pr["qa"]), seq(pr["ka"]), seq(pr["va"])
    qc, kc, vc = seq(pr["qc"]), seq(pr["kc"]), seq(pr["vc"])
    chunks = lambda a, n: a.reshape(B, S // n, n, a.shape[-1])
    vat = pr["vat"].reshape(B, S // ck_a, W_KV, ck_a)
    vct = pr["vct"].reshape(B, S // NB_KV_BLOCK, W_NBHD, NB_KV_BLOCK)
    if ctx_cache is None:
        out_a = _attention(qa, [(chunks(ka, ck_a), vat)], heads=_HEADS_GQA, name=name + "_attn_a")
        out_c = _attention(qc, [(chunks(kc, NB_KV_BLOCK), vct)], heads=_HEADS_MHA, name=name + "_attn_c")
    else:
        ck_c, cv_c = ctx_cache[2], ctx_cache[3]
        P = ck_c.shape[1]
        assert P == ck_a
        cva_t, cvc_t = _cached_values_t(ctx_cache[1], cv_c, name=name + "_cache_t")
        out_a = _attention(qa, [(chunks(ka, ck_a), vat),
                                (ctx_cache[0].reshape(B, 1, P, W_KV), cva_t.reshape(B, 1, W_KV, P))],
                           heads=_HEADS_GQA, name=name + "_attn_a")
        out_c = _nbhd_attention(qc, chunks(kc, NB_KV_BLOCK), vct, ck_c, cvc_t,
                                lw["rel_bias"], name=name + "_attn_c")
    out_b, sf, mf = _mlstm(seq(pr["qb"]), seq(pr["kb"]), seq(pr["vb"]), seq(pr["g"]), seq(pr["ob"]),
                           state, lw["out_norm"], name=name + "_mlstm")
    x1, h2 = _outproj(out_a.reshape(T, W_ATTN), out_b.reshape(T, W_MLSTM), out_c.reshape(T, W_NBHD),
                      x, mods, lw["w_out"], lw["g_post_mix"], lw["g_pre_ffn"],
                      rows_per_cond=cond_rows, name=name + "_outproj")
    x2 = _ffn(h2, x1, mods, lw["w_up"], lw["w_down"], lw["g_post_ffn"],
              rows_per_cond=cond_rows, name=name + "_ffn")
    return x2, (ka, va, kc, vc, sf, mf)


def kernel(x_prompt, x_sample, c, cache_k_attn, cache_v_attn, cache_k_nbhd, cache_v_nbhd, state_mlstm_C, state_mlstm_n, state_mlstm_m, c_ctx, w_ada, b_ada, g_pre_mix, g_post_mix, g_pre_ffn, g_post_ffn, w_in, q_norm_attn, k_norm_attn, mlstm_gate_bias, mlstm_out_norm, nbhd_rel_bias, w_out, w_ffn_up, w_ffn_down):
    Bc, Sc, _ = x_prompt.shape
    Bl, Sl, _ = x_sample.shape
    P = cache_k_attn.shape[2]
    n_cond = 8
    cond = jnp.concatenate([c_ctx[None, :], c, jnp.zeros((n_cond - 1 - Bl, D_MODEL), F32)], axis=0)
    mods_all = _modulation(cond, w_ada, b_ada)

    layers = []
    for l in range(DEPTH):
        layers.append(dict(
            w_in=_pad_w_in(w_in[l]),
            w_out=w_out[l].astype(BF16),
            w_up=w_ffn_up[l].astype(BF16),
            w_down=w_ffn_down[l].astype(BF16),
            g_pre_mix=g_pre_mix[l].reshape(1, D_MODEL), g_post_mix=g_post_mix[l].reshape(1, D_MODEL),
            g_pre_ffn=g_pre_ffn[l].reshape(1, D_MODEL), g_post_ffn=g_post_ffn[l].reshape(1, D_MODEL),
            q_norm=jnp.tile(q_norm_attn[l], 2).reshape(1, LANES),
            k_norm=jnp.tile(k_norm_attn[l], 2).reshape(1, LANES),
            gate_bias=_pad_gate_bias(mlstm_gate_bias[l]),
            out_norm=mlstm_out_norm[l].reshape(1, W_MLSTM),
            rel_bias=nbhd_rel_bias[l].reshape(-1),
        ))

    xp = x_prompt.reshape(Bc * Sc, D_MODEL)
    ctx = []
    for l in range(DEPTH):
        mods = mods_all[l, 0:1].reshape(1, 1, N_MOD * D_MODEL)
        xp, extras = _layer(xp, mods, layers[l], B=Bc, S=Sc, cond_rows=Bc * Sc, rope_tabs=None,
                            ctx_cache=None, state=None, name=f"ctx{l}")
        ctx.append(extras)
    new_k_attn = jnp.stack([e[0].reshape(Bc, Sc, KV_ATTN, HEAD_DIM) for e in ctx], axis=1)
    new_v_attn = jnp.stack([e[1].reshape(Bc, Sc, KV_ATTN, HEAD_DIM) for e in ctx], axis=1)
    new_k_nbhd = jnp.stack([e[2].reshape(Bc, Sc, H_NBHD, HEAD_DIM) for e in ctx], axis=1)
    new_v_nbhd = jnp.stack([e[3].reshape(Bc, Sc, H_NBHD, HEAD_DIM) for e in ctx], axis=1)
    states = [_unpack_state(e[4], e[5]) for e in ctx]
    new_C = jnp.stack([s[0] for s in states], axis=1)
    new_n = jnp.stack([s[1] for s in states], axis=1)
    new_m = jnp.stack([s[2] for s in states], axis=1)

    xs = x_sample.reshape(Bl * Sl, D_MODEL)
    rope_tabs = _rope_tables(Sl)
    for l in range(DEPTH):
        mods = mods_all[l, 1:1 + Bl].reshape(Bl, 1, N_MOD * D_MODEL)
        cache = (cache_k_attn[:, l].reshape(Bl, P, W_KV), cache_v_attn[:, l].reshape(Bl, P, W_KV),
                 cache_k_nbhd[:, l].reshape(Bl, P, W_NBHD), cache_v_nbhd[:, l].reshape(Bl, P, W_NBHD))
        state = _pack_state(state_mlstm_C[:, l], state_mlstm_n[:, l], state_mlstm_m[:, l])
        xs, _ = _layer(xs, mods, layers[l], B=Bl, S=Sl, cond_rows=Sl, rope_tabs=rope_tabs,
                       ctx_cache=cache, state=state, name=f"lat{l}")

    return (xp.reshape(Bc, Sc, D_MODEL), xs.reshape(Bl, Sl, D_MODEL),
            new_k_attn, new_v_attn, new_k_nbhd, new_v_nbhd, new_C, new_n, new_m)
```

```python
import functools

import jax
import jax.numpy as jnp
import numpy as np
from jax import lax
from jax.experimental import pallas as pl
from jax.experimental.pallas import tpu as pltpu

F32 = jnp.float32
BF16 = jnp.bfloat16

D_MODEL = 1024
DEPTH = 2
GRID_W = 64
HEAD_DIM = 64
H_ATTN = 6
KV_ATTN = 2
H_MLSTM = 4
H_NBHD = 6
D_FF = 4 * D_MODEL
NA_ROWS = 8
NA_COLS = 16
ROPE_THETA = 10000.0
EPS = 1e-6
N_MOD = 6
W_ATTN = H_ATTN * HEAD_DIM
W_KV = KV_ATTN * HEAD_DIM
W_MLSTM = H_MLSTM * HEAD_DIM
W_NBHD = H_NBHD * HEAD_DIM
N_GATES = 4 * H_MLSTM

LANES = 128
V7X_VMEM_BYTES = 64 * 1024 * 1024
VMEM_CAP_BYTES = 56 * 1024 * 1024

TM_PROJ = 512
PROJ_CHUNK = 512
TM_FFN = 1024
TF_FFN = 1024
ROW_SPLIT = 4
TQ_ATTN = 512
CK_ATTN = 512
ATTN_LOOKAHEAD = 2
SEQS_PER_STEP = 16
L_CHUNK = 128
MLSTM_GROUP = 2
MLSTM_GROUP_A = 16
MLSTM_STEP_CHUNKS = 32
NB_GROUP = 8
NB_SLAB = 16
NB_STEP_GROUPS = 8
NB_KV_BLOCK = 256
NEG = -1e30
LOG2E = 1.4426950408889634

_COLS = {}
_off = 0
for _name, _w in (("qa", W_ATTN), ("ka", W_KV), ("va", W_KV), ("qb", W_MLSTM), ("kb", W_MLSTM),
                  ("vb", W_MLSTM), ("ob", W_MLSTM), ("gf", LANES), ("gb", LANES),
                  ("qc", W_NBHD), ("kc", W_NBHD), ("vc", W_NBHD)):
    _COLS[_name] = (_off, _off + _w)
    _off += _w
IN_PAD = _off


def _vmem_limit(nbytes):
    return int(min(max(nbytes, 16 * 1024 * 1024), VMEM_CAP_BYTES))


def _dot(a, b):
    return jnp.dot(a, b, preferred_element_type=F32)


def _dot_nt(a, b):
    return lax.dot_general(a, b, (((1,), (1,)), ((), ())), preferred_element_type=F32)


def _lane_lo(shape):
    return (lax.broadcasted_iota(jnp.int32, shape, len(shape) - 1) % LANES) < HEAD_DIM


def _rms(x, g):
    ms = jnp.mean(x * x, axis=-1, keepdims=True)
    return (x * lax.rsqrt(ms + EPS)) * g


def _pair_rms(x, g):
    lo = _lane_lo(x.shape)
    x2 = x * x
    s_lo = jnp.sum(jnp.where(lo, x2, 0.0), axis=-1, keepdims=True)
    s_hi = jnp.sum(jnp.where(lo, 0.0, x2), axis=-1, keepdims=True)
    r = jnp.where(lo, lax.rsqrt(s_lo / HEAD_DIM + EPS), lax.rsqrt(s_hi / HEAD_DIM + EPS))
    return (x * r) * g


def _sigmoid(x):
    return 1.0 / (1.0 + jnp.exp(-x))


def _mods_kernel(c_ref, w_ref, b_ref, o_ref):
    c = c_ref[...]
    s = (c * _sigmoid(c)).astype(BF16)
    o_ref[0] = _dot(s, w_ref[0].astype(BF16)) + b_ref[0]


def _modulation(cond, w_ada, b_ada):
    n = cond.shape[0]
    tn = D_MODEL
    return pl.pallas_call(
        _mods_kernel,
        grid=(DEPTH, N_MOD * D_MODEL // tn),
        in_specs=[pl.BlockSpec((n, D_MODEL), lambda l, j: (0, 0)),
                  pl.BlockSpec((1, D_MODEL, tn), lambda l, j: (l, 0, j)),
                  pl.BlockSpec((1, 1, tn), lambda l, j: (l, 0, j))],
        out_specs=pl.BlockSpec((1, n, tn), lambda l, j: (l, 0, j)),
        out_shape=jax.ShapeDtypeStruct((DEPTH, n, N_MOD * D_MODEL), F32),
        compiler_params=pltpu.CompilerParams(
            dimension_semantics=("arbitrary", "arbitrary"),
            vmem_limit_bytes=_vmem_limit(4 * D_MODEL * tn * 4)),
        name="modulation",
    )(cond, w_ada, b_ada.reshape(DEPTH, 1, N_MOD * D_MODEL))


def _inproj_kernel(*refs, rope):
    if rope:
        (x_ref, mod_ref, g_ref, w_ref, qn_ref, kn_ref, gbias_ref, cos_ref, sin_ref,
         qa_ref, ka_ref, va_ref, qb_ref, kb_ref, vb_ref, ob_ref, gate_ref,
         qc_ref, kc_ref, vc_ref, vat_ref, vct_ref) = refs
    else:
        (x_ref, mod_ref, g_ref, w_ref, qn_ref, kn_ref, gbias_ref,
         qa_ref, ka_ref, va_ref, qb_ref, kb_ref, vb_ref, ob_ref, gate_ref,
         qc_ref, kc_ref, vc_ref, vat_ref, vct_ref) = refs
    x = x_ref[...]
    mod = mod_ref[0]
    sh1 = mod[:, 0:D_MODEL]
    sc1 = mod[:, D_MODEL:2 * D_MODEL]
    hb = (_rms(x, g_ref[...]) * (1.0 + sc1) + sh1).astype(BF16)

    z = [_dot(hb, w_ref[:, c0:c0 + PROJ_CHUNK]) for c0 in range(0, IN_PAD, PROJ_CHUNK)]

    def proj(name, j=0, w=None):
        lo, hi = _COLS[name]
        lo = lo + j
        hi = hi if w is None else lo + w
        parts = []
        while lo < hi:
            c, o = divmod(lo, PROJ_CHUNK)
            n = min(hi - lo, PROJ_CHUNK - o)
            parts.append(z[c][:, o:o + n])
            lo += n
        return parts[0] if len(parts) == 1 else jnp.concatenate(parts, axis=1)

    scale = HEAD_DIM ** -0.5
    q_scale = scale * LOG2E

    def rotary(t):
        first = (lax.broadcasted_iota(jnp.int32, t.shape, 1) % 32) < 16
        partner = jnp.where(first, pltpu.roll(t, LANES - 16, 1), pltpu.roll(t, 16, 1))
        return t * cos_ref[...] + partner * sin_ref[...]

    def store_queries(t, j, q_ref, kv_half):
        lo = _lane_lo(t.shape)
        for e in range(2):
            h = 2 * j + e
            src = t if e == kv_half[h] else pltpu.roll(t, HEAD_DIM, 1)
            keep = lo if kv_half[h] == 0 else jnp.logical_not(lo)
            q_ref[:, h * LANES:(h + 1) * LANES] = jnp.where(keep, src, 0.0).astype(q_ref.dtype)

    for j in range(W_ATTN // LANES):
        t = _pair_rms(proj("qa", j * LANES, LANES), qn_ref[...])
        if rope:
            t = rotary(t)
        store_queries(t * q_scale, j, qa_ref, [kh for _, kh in _HEADS_GQA])
    t = _pair_rms(proj("ka"), kn_ref[...])
    if rope:
        t = rotary(t)
    ka_ref[...] = t.astype(ka_ref.dtype)

    def store_v(v, v_ref, vt_ref):
        v_ref[...] = v.astype(v_ref.dtype)
        nblk, _, blk = vt_ref.shape
        for u in range(nblk):
            vt_ref[u] = v[u * blk:(u + 1) * blk].T.astype(vt_ref.dtype)

    store_v(proj("va"), va_ref, vat_ref)
    qb_ref[...] = proj("qb").astype(qb_ref.dtype)
    kb_ref[...] = (proj("kb") * scale).astype(kb_ref.dtype)
    vb_ref[...] = proj("vb").astype(vb_ref.dtype)
    ob_ref[...] = proj("ob").astype(ob_ref.dtype)
    for j, name in enumerate(("gf", "gb")):
        gt = proj(name) + gbias_ref[:, j * LANES:(j + 1) * LANES]
        lane = lax.broadcasted_iota(jnp.int32, gt.shape, 1)
        is_f = (lane >= H_MLSTM) & (lane < 2 * H_MLSTM)
        logsig = jnp.minimum(gt, 0.0) - jnp.log1p(jnp.exp(-jnp.abs(gt)))
        gate_ref[:, j * LANES:(j + 1) * LANES] = jnp.where(is_f, logsig, gt)
    for j in range(W_NBHD // LANES):
        store_queries(proj("qc", j * LANES, LANES) * q_scale, j, qc_ref, [kh for _, kh in _HEADS_MHA])
    kc_ref[...] = proj("kc").astype(kc_ref.dtype)
    store_v(proj("vc"), vc_ref, vct_ref)


def _inproj(x, mods, g_pre, w_in_p, qn, kn, gbias, rope_tabs, *, rows_per_cond, kv_dtype, vt_blocks, name):
    T = x.shape[0]
    tm = TM_PROJ
    bpc = rows_per_cond // tm
    rope = rope_tabs is not None
    row = lambda i: (i, 0)
    const = lambda i: (0, 0)
    in_specs = [pl.BlockSpec((tm, D_MODEL), row),
                pl.BlockSpec((1, 1, N_MOD * D_MODEL), lambda i: (i // bpc, 0, 0)),
                pl.BlockSpec((1, D_MODEL), const),
                pl.BlockSpec((D_MODEL, IN_PAD), const),
                pl.BlockSpec((1, LANES), const),
                pl.BlockSpec((1, LANES), const),
                pl.BlockSpec((1, 2 * LANES), const)]
    args = [x, mods, g_pre, w_in_p, qn, kn, gbias]
    if rope:
        nblk = rope_tabs[0].shape[0] // tm
        in_specs += [pl.BlockSpec((tm, LANES), lambda i: (i % nblk, 0))] * 2
        args += list(rope_tabs)
    widths = [("qa", H_ATTN * LANES, BF16), ("ka", W_KV, kv_dtype), ("va", W_KV, kv_dtype),
              ("qb", W_MLSTM, BF16), ("kb", W_MLSTM, BF16), ("vb", W_MLSTM, BF16),
              ("ob", W_MLSTM, F32), ("g", 2 * LANES, F32),
              ("qc", H_NBHD * LANES, BF16), ("kc", W_NBHD, kv_dtype), ("vc", W_NBHD, kv_dtype)]
    out_specs = [pl.BlockSpec((tm, w), row) for _, w, _ in widths]
    out_shape = [jax.ShapeDtypeStruct((T, w), dt) for _, w, dt in widths]
    for w, blk in zip((W_KV, W_NBHD), vt_blocks):
        out_specs.append(pl.BlockSpec((tm // blk, w, blk), lambda i: (i, 0, 0)))
        out_shape.append(jax.ShapeDtypeStruct((T // blk, w, blk), BF16))
    est = 2 * (tm * D_MODEL * 4 + D_MODEL * IN_PAD * 2 + tm * IN_PAD * 4) + 3 * tm * IN_PAD * 4
    outs = pl.pallas_call(
        functools.partial(_inproj_kernel, rope=rope),
        grid=(T // tm,),
        in_specs=in_specs, out_specs=out_specs, out_shape=out_shape,
        compiler_params=pltpu.CompilerParams(dimension_semantics=("arbitrary",),
                                             vmem_limit_bytes=_vmem_limit(est)),
        name=name,
    )(*args)
    return dict(zip([n for n, _, _ in widths] + ["vat", "vct"], outs))


SUM_ROWS = 16


def _softmax_step(st, col_max, vt, m_scr, acc_scr, u):
    m_old = m_scr[u]
    m_new = jnp.maximum(m_old, col_max)
    p = jnp.exp2(st - m_new).astype(BF16)
    alpha = jnp.exp2(m_old - m_new)
    vt1 = jnp.concatenate([vt, jnp.ones((SUM_ROWS, vt.shape[1]), BF16)], axis=0)
    acc_scr[u] = alpha * acc_scr[u] + _dot(vt1, p)
    m_scr[u] = m_new


def _normalised_pair(acc_scr, u0, u1):
    halves = [acc_scr[u][0:HEAD_DIM] / acc_scr[u][HEAD_DIM:HEAD_DIM + 1] for u in (u0, u1)]
    return jnp.concatenate(halves, axis=0).T
def _attn_kernel(*refs, heads, part_chunks):
    n_parts = len(part_chunks)
    assert n_parts in (1, 2)
    q_ref, kv_refs = refs[0], refs[1:1 + 2 * n_parts]
    o_ref, m_scr, acc_scr, s_ring, cmax_scr = refs[1 + 2 * n_parts:]
    nch = sum(part_chunks)

    def chunk_of(which, j, rows, cols):
        first = kv_refs[which][0, jnp.minimum(j, part_chunks[0] - 1), rows, cols].astype(BF16)
        if n_parts == 1:
            return first
        j2 = jnp.clip(j - part_chunks[0], 0, part_chunks[1] - 1)
        return jnp.where(j < part_chunks[0], first, kv_refs[2 + which][0, j2, rows, cols].astype(BF16))

    nh = len(heads)
    ring = ATTN_LOOKAHEAD + 1
    assert nh % ring == 0
    m_scr[...] = jnp.full(m_scr.shape, -jnp.inf, F32)
    acc_scr[...] = jnp.zeros(acc_scr.shape, F32)

    def scores(j, item):
        h = item % nh
        kg = heads[h][0]
        kj = chunk_of(0, j, slice(None), slice(kg * LANES, (kg + 1) * LANES))
        st = _dot_nt(kj, q_ref[0, :, h * LANES:(h + 1) * LANES])
        s_ring[item % ring] = st
        cmax_scr[item % ring] = jnp.max(st, axis=0, keepdims=True)

    def chunk(j, carry, last=False):
        for h, (kg, kh) in enumerate(heads):
            ahead = h + ATTN_LOOKAHEAD
            if ahead < nh:
                scores(j, ahead)
            elif not last:
                scores(j + 1, ahead)
            r = kg * LANES + kh * HEAD_DIM
            vt = chunk_of(1, j, slice(r, r + HEAD_DIM), slice(None))
            _softmax_step(s_ring[h % ring], cmax_scr[h % ring], vt, m_scr, acc_scr, h)
        return carry

    for item in range(ATTN_LOOKAHEAD):
        scores(0, item)
    lax.fori_loop(0, nch - 1, chunk, 0)
    chunk(nch - 1, 0, last=True)
    for t in range(len(heads) // 2):
        o_ref[0, :, t * LANES:(t + 1) * LANES] = _normalised_pair(acc_scr, 2 * t, 2 * t + 1).astype(o_ref.dtype)


def _attn_seqs_kernel(q_ref, k_ref, vt_ref, o_ref, s_ring, cmax_scr, *, heads):
    nh = len(heads)
    n_seq = q_ref.shape[0]
    ring = nh
    look = nh - 1

    def scores(s, item):
        h = item % nh
        kg = heads[h][0]
        st = _dot_nt(k_ref[s, 0, :, kg * LANES:(kg + 1) * LANES].astype(BF16),
                     q_ref[s, :, h * LANES:(h + 1) * LANES])
        s_ring[item % ring] = st
        cmax_scr[item % ring] = jnp.max(st, axis=0, keepdims=True)

    def sequence(s, carry, last=False):
        accs = []
        for h, (kg, kh) in enumerate(heads):
            ahead = h + look
            if ahead < nh:
                scores(s, ahead)
            elif not last:
                scores(s + 1, ahead)
            r = kg * LANES + kh * HEAD_DIM
            vt = vt_ref[s, 0, r:r + HEAD_DIM, :].astype(BF16)
            vt1 = jnp.concatenate([vt, jnp.ones((SUM_ROWS, vt.shape[1]), BF16)], axis=0)
            p = jnp.exp2(s_ring[h % ring] - cmax_scr[h % ring]).astype(BF16)
            accs.append(_dot(vt1, p))
        for t in range(nh // 2):
            halves = [a[0:HEAD_DIM] / a[HEAD_DIM:HEAD_DIM + 1] for a in accs[2 * t:2 * t + 2]]
            o_ref[s, :, t * LANES:(t + 1) * LANES] = jnp.concatenate(halves, axis=0).T.astype(o_ref.dtype)
        return carry

    for item in range(look):
        scores(0, item)
    lax.fori_loop(0, n_seq - 1, sequence, 0)
    sequence(n_seq - 1, 0, last=True)


def _attention_seqs(q, k4, vt4, *, heads, name):
    B, Sq, WQ = q.shape
    ck, KW = k4.shape[2], k4.shape[3]
    nh = len(heads)
    W = nh * HEAD_DIM
    bb = min(SEQS_PER_STEP, B)
    ring = nh
    est = (2 * bb * (Sq * (WQ + W) * 2 + ck * KW * (k4.dtype.itemsize + vt4.dtype.itemsize))
           + ring * ck * Sq * 4 + 8 * nh * ck * Sq * 4)
    return pl.pallas_call(
        functools.partial(_attn_seqs_kernel, heads=heads),
        grid=(B // bb,),
        in_specs=[pl.BlockSpec((bb, Sq, WQ), lambda i: (i, 0, 0)),
                  pl.BlockSpec((bb, 1, ck, KW), lambda i: (i, 0, 0, 0)),
                  pl.BlockSpec((bb, 1, KW, ck), lambda i: (i, 0, 0, 0))],
        out_specs=pl.BlockSpec((bb, Sq, W), lambda i: (i, 0, 0)),
        out_shape=jax.ShapeDtypeStruct((B, Sq, W), BF16),
        scratch_shapes=[pltpu.VMEM((ring, ck, Sq), F32),
                        pltpu.VMEM((ring, 1, Sq), F32)],
        compiler_params=pltpu.CompilerParams(dimension_semantics=("arbitrary",),
                                             vmem_limit_bytes=_vmem_limit(est)),
        name=name,
    )(q, k4, vt4)


def _attention(q, parts, *, heads, name):
    B, Sq, WQ = q.shape
    ck, KW = parts[0][0].shape[2], parts[0][0].shape[3]
    tq = min(TQ_ATTN, Sq)
    nh = len(heads)
    W = nh * HEAD_DIM
    part_chunks = tuple(k4.shape[1] for k4, _ in parts)
    if part_chunks == (1,) and tq == Sq:
        return _attention_seqs(q, parts[0][0], parts[0][1], heads=heads, name=name)
    in_specs = [pl.BlockSpec((1, tq, WQ), lambda b, i: (b, i, 0))]
    args = [q]
    kv_bytes = 0
    for k4, vt4 in parts:
        assert k4.shape[2:] == (ck, KW) and vt4.shape[2:] == (KW, ck)
        in_specs += [pl.BlockSpec((1,) + k4.shape[1:], lambda b, i: (b, 0, 0, 0)),
                     pl.BlockSpec((1,) + vt4.shape[1:], lambda b, i: (b, 0, 0, 0))]
        args += [k4, vt4]
        kv_bytes += k4[0].size * k4.dtype.itemsize + vt4[0].size * vt4.dtype.itemsize
    est = (2 * (tq * (WQ + W) * 2 + kv_bytes) + nh * tq * (HEAD_DIM * 4 + 64) + 8 * nh * ck * tq * 4)
    return pl.pallas_call(
        functools.partial(_attn_kernel, heads=heads, part_chunks=part_chunks),
        grid=(B, Sq // tq),
        in_specs=in_specs,
        out_specs=pl.BlockSpec((1, tq, W), lambda b, i: (b, i, 0)),
        out_shape=jax.ShapeDtypeStruct((B, Sq, W), BF16),
        scratch_shapes=[pltpu.VMEM((nh, 1, tq), F32),
                        pltpu.VMEM((nh, HEAD_DIM + SUM_ROWS, tq), F32),
                        pltpu.VMEM((ATTN_LOOKAHEAD + 1, ck, tq), F32),
                        pltpu.VMEM((ATTN_LOOKAHEAD + 1, 1, tq), F32)],
        compiler_params=pltpu.CompilerParams(dimension_semantics=("arbitrary", "arbitrary"),
                                             vmem_limit_bytes=_vmem_limit(est)),
        name=name,
    )(*args)


def _transpose_kernel(a_ref, b_ref, at_ref, bt_ref):
    at_ref[0] = a_ref[0].T.astype(at_ref.dtype)
    bt_ref[0] = b_ref[0].T.astype(bt_ref.dtype)


def _cached_values_t(a, b, *, name):
    B, P, ca = a.shape
    cb = b.shape[2]
    return pl.pallas_call(
        _transpose_kernel,
        grid=(B,),
        in_specs=[pl.BlockSpec((1, P, ca), lambda i: (i, 0, 0)), pl.BlockSpec((1, P, cb), lambda i: (i, 0, 0))],
        out_specs=[pl.BlockSpec((1, ca, P), lambda i: (i, 0, 0)), pl.BlockSpec((1, cb, P), lambda i: (i, 0, 0))],
        out_shape=[jax.ShapeDtypeStruct((B, ca, P), BF16), jax.ShapeDtypeStruct((B, cb, P), BF16)],
        compiler_params=pltpu.CompilerParams(dimension_semantics=("arbitrary",)),
        name=name,
    )(a, b)


_HEADS_GQA = tuple((0, h // (H_ATTN // KV_ATTN)) for h in range(H_ATTN))
_HEADS_MHA = tuple((h // 2, h % 2) for h in range(H_NBHD))


def _nbhd_window(r, rows):
    kr = min(NA_ROWS, rows)
    return min(max(r - kr // 2, 0), rows - kr), kr


def _nbhd_patterns(rows):
    n_groups = rows // NB_GROUP
    pats = []
    for g in (0, 1, n_groups - 1):
        r0 = g * NB_GROUP
        pats.append((r0, min(max(r0 - NA_ROWS // 2, 0), rows - NB_SLAB)))
    return pats


def _nbhd_kernel(rb_ref, q_ref, k_ref, vt_ref, kc_ref, vct_ref, o_ref,
                 bias_scr, m_scr, acc_scr, s_ring, cmax_scr, *, rows):
    hp = pl.program_id(0)
    b = pl.program_id(1)
    gs = pl.program_id(2)
    n_groups = rows // NB_GROUP
    n_dr = 2 * NA_ROWS - 1
    n_dc = 2 * NA_COLS - 1
    tq = NB_GROUP * GRID_W
    ck = tq
    pats = _nbhd_patterns(rows)
    ring = ATTN_LOOKAHEAD + 1
    n_chunks = NB_SLAB * GRID_W // ck + 1
    assert n_chunks % ring == 0 and kc_ref.shape[1] == ck
    blk_rows = NB_KV_BLOCK // GRID_W

    @pl.when((b == 0) & (gs == 0))
    def _build_bias():
        shape = (GRID_W, LANES)
        w = lax.broadcasted_iota(jnp.int32, shape, 0)
        lane = lax.broadcasted_iota(jnp.int32, shape, 1)
        cc = lane % GRID_W
        second = lane >= GRID_W
        cs = jnp.clip(w - NA_COLS // 2, 0, GRID_W - NA_COLS)
        col_ok = (cc >= cs) & (cc < cs + NA_COLS)
        dc = cc - w + (NA_COLS - 1)
        for hh in range(2):
            base = (2 * hp + hh) * (n_dr * n_dc)
            tiles = {}
            for d in range(-1, n_dr):
                acc = jnp.zeros(shape, F32)
                for j in range(n_dc):
                    va = rb_ref[base + d * n_dc + j] * LOG2E if d >= 0 else 0.0
                    vb = rb_ref[base + (d + 1) * n_dc + j] * LOG2E if d + 1 < n_dr else 0.0
                    acc = acc + jnp.where(dc == j, jnp.where(second, vb, va), 0.0)
                tiles[d] = acc
            def query_row_tile(r, kra):
                rs, kr = _nbhd_window(r, rows)
                ok_a = rs <= kra < rs + kr
                ok_b = rs <= kra + 1 < rs + kr
                if not (ok_a or ok_b):
                    return jnp.full(shape, NEG, F32)
                row_ok = (jnp.logical_not(second) if ok_a and not ok_b else
                          second if ok_b and not ok_a else None)
                ok = col_ok if row_ok is None else (col_ok & row_ok)
                return jnp.where(ok, tiles[kra - r + (NA_ROWS - 1)], NEG)

            for pi, (r0, slab0) in enumerate(pats):
                for ip in range(NB_GROUP // 2):
                    for ap in range(NB_SLAB // 2):
                        kra = slab0 + 2 * ap
                        two_rows = jnp.concatenate([query_row_tile(r0 + 2 * ip, kra),
                                                    query_row_tile(r0 + 2 * ip + 1, kra)], axis=0)
                        bias_scr[hh, pi, ap * LANES:(ap + 1) * LANES, ip * LANES:(ip + 1) * LANES] = two_rows.T

    assert (2 * n_chunks) % ring == 0
    m_scr[...] = jnp.full(m_scr.shape, -jnp.inf, F32)
    acc_scr[...] = jnp.zeros(acc_scr.shape, F32)

    def group_of(s):
        g = gs * NB_STEP_GROUPS + s
        pat = jnp.where(g == 0, 0, jnp.where(g == n_groups - 1, 2, 1))
        slab0 = jnp.clip(g * NB_GROUP - NA_ROWS // 2, 0, rows - NB_SLAB)
        return pat, slab0 // blk_rows

    per_group = 2 * n_chunks
    nb = ck // NB_KV_BLOCK

    def scores(s, item):
        hh, c = (item % per_group) // n_chunks, item % n_chunks
        qm = q_ref[0, pl.ds(pl.multiple_of(s * tq, tq), tq), hh * LANES:(hh + 1) * LANES]
        if c < n_chunks - 1:
            pat, blk0 = group_of(s)
            kc = k_ref[0, pl.ds(blk0 + c * nb, nb)].reshape(ck, LANES)
            st = _dot_nt(kc, qm) + bias_scr[hh, pat, c * ck:(c + 1) * ck, :]
        else:
            st = _dot_nt(kc_ref[0].astype(BF16), qm)
        s_ring[item % ring] = st
        cmax_scr[item % ring] = jnp.max(st, axis=0, keepdims=True)

    def group(s, carry, last=False):
        for item in range(per_group):
            ahead = item + ATTN_LOOKAHEAD
            if ahead < per_group:
                scores(s, ahead)
            elif not last:
                scores(s + 1, ahead)
            hh, c = item // n_chunks, item % n_chunks
            if c < n_chunks - 1:
                _, blk0 = group_of(s)
                vt = jnp.concatenate([vt_ref[0, blk0 + c * nb + i, hh * HEAD_DIM:(hh + 1) * HEAD_DIM, :]
                                      for i in range(nb)], axis=1)
            else:
                vt = vct_ref[0, hh * HEAD_DIM:(hh + 1) * HEAD_DIM, :].astype(BF16)
            _softmax_step(s_ring[item % ring], cmax_scr[item % ring], vt, m_scr, acc_scr, 2 * s + hh)
        o_ref[0, pl.ds(pl.multiple_of(s * tq, tq), tq), :] = (
            _normalised_pair(acc_scr, 2 * s, 2 * s + 1).astype(o_ref.dtype))
        return carry

    for item in range(ATTN_LOOKAHEAD):
        scores(0, item)
    lax.fori_loop(0, NB_STEP_GROUPS - 1, group, 0)
    group(jnp.int32(NB_STEP_GROUPS - 1), 0, last=True)


def _nbhd_attention(q, k4, vt4, k_ctx, vct, rel_bias_flat, *, name):
    B, S, _ = q.shape
    W = k4.shape[3]
    P = k_ctx.shape[1]
    rows = S // GRID_W
    tq = NB_GROUP * GRID_W
    nk = NB_SLAB * GRID_W
    nblk = S // NB_KV_BLOCK
    tqs = NB_STEP_GROUPS * tq
    nu = 2 * NB_STEP_GROUPS
    ring = ATTN_LOOKAHEAD + 1
    est =(2 * (2 * tqs * LANES * 2 + 2 * S * LANES * 2 + 2 * P * LANES * 4)
           + 2 * 3 * tq * nk * 4 + ring * tq * tq * 4 + 8 * tq * tq * 4 + nu * tq * 1024)
    return pl.pallas_call(
        functools.partial(_nbhd_kernel, rows=rows),
        grid=(W // LANES, B, rows // (NB_GROUP * NB_STEP_GROUPS)),
        in_specs=[pl.BlockSpec(memory_space=pltpu.SMEM),
                  pl.BlockSpec((1, tqs, 2 * LANES), lambda p, b, g: (b, g, p)),
                  pl.BlockSpec((1, nblk, NB_KV_BLOCK, LANES), lambda p, b, g: (b, 0, 0, p)),
                  pl.BlockSpec((1, nblk, LANES, NB_KV_BLOCK), lambda p, b, g: (b, 0, p, 0)),
                  pl.BlockSpec((1, P, LANES), lambda p, b, g: (b, 0, p)),
                  pl.BlockSpec((1, LANES, P), lambda p, b, g: (b, p, 0))],
        out_specs=pl.BlockSpec((1, tqs, LANES), lambda p, b, g: (b, g, p)),
        out_shape=jax.ShapeDtypeStruct((B, S, W), BF16),
        scratch_shapes=[pltpu.VMEM((2, 3, nk, tq), F32),
                        pltpu.VMEM((nu, 1, tq), F32),
                        pltpu.VMEM((nu, HEAD_DIM + SUM_ROWS, tq), F32),
                        pltpu.VMEM((ring, tq, tq), F32),
                        pltpu.VMEM((ring, 1, tq), F32)],
        compiler_params=pltpu.CompilerParams(dimension_semantics=("arbitrary",) * 3,
                                             vmem_limit_bytes=_vmem_limit(est)),
        name=name,
    )(rel_bias_flat, q, k4, vt4, k_ctx, vct)


def _split3(x):
    hi = x.astype(BF16)
    r = x - hi.astype(F32)
    mid = r.astype(BF16)
    return hi, mid, (r - mid.astype(F32)).astype(BF16)


def _mlstm_kernel(*refs, nc, grp_a, grp, has_state):
    q_ref, k_ref, v_ref, g_ref, ob_ref = refs[:5]
    s0_ref, m0_ref = refs[5:7] if has_state else (None, None)
    (on_ref, out_ref, sf_ref, mf_ref,
     h_scr, nat_scr, rows_scr, stat_scr, mprev_scr, un_scr, st_scr) = refs[7 if has_state else 5:]
    d = pl.program_id(1)
    L = L_CHUNK
    bb = q_ref.shape[0]
    NP = H_MLSTM // 2
    row = lax.broadcasted_iota(jnp.int32, (L, L), 0)
    col = lax.broadcasted_iota(jnp.int32, (L, L), 1)
    sign = 1 - 2 * d
    mask = (col - row) * sign <= 0
    maskb = mask.astype(BF16)
    mask3 = jnp.concatenate([maskb, maskb, maskb], axis=1)
    lane = lax.broadcasted_iota(jnp.int32, (L, LANES), 1)
    lo = lane < HEAD_DIM
    top = row < HEAD_DIM
    row2 = lax.broadcasted_iota(jnp.int32, (L, 2 * LANES), 0)
    col2 = lax.broadcasted_iota(jnp.int32, (L, 2 * LANES), 1)
    keep_state = (row2 < HEAD_DIM) == ((col2 % LANES) < HEAD_DIM)
    top2 = row2 < HEAD_DIM
    ones_b = jnp.ones((L, LANES), BF16)
    ones_lo = lo.astype(BF16)
    ones_hi = jnp.logical_not(lo).astype(BF16)

    def chunk_rows(c):
        return pl.ds(pl.multiple_of(c * L, L), L)

    def tokens(c):
        if bb == 1:
            return 0, chunk_rows(c)
        return c // nc, pl.ds(pl.multiple_of((c % nc) * L, L), L)

    def pass_a(it, carry):
        cs = [it * grp_a + u for u in range(grp_a)]
        gts = [g_ref[tokens(c)[0], tokens(c)[1], :] for c in cs]
        bns = [_dot(mask3, jnp.concatenate(_split3(gt), axis=0)) for gt in gts]
        a_all = []
        for c, gt, bn in zip(cs, gts, bns):
            nat = jnp.where(lane < H_MLSTM, gt, bn)
            nat_scr[chunk_rows(c), :] = nat * (-LOG2E)
            nat_t = nat.T
            b_rows = nat_t[H_MLSTM:2 * H_MLSTM]
            c_rows = nat_t[0:H_MLSTM] - b_rows
            rows_scr[c] = jnp.concatenate([c_rows * LOG2E, b_rows], axis=0)
            c_max = jnp.max(c_rows, axis=1, keepdims=True)
            b_tot = jnp.where(d == 0, b_rows[:, L - 1:L], b_rows[:, 0:1])
            stat_scr[c] = jnp.concatenate([jnp.broadcast_to(c_max, (H_MLSTM, LANES)),
                                           jnp.broadcast_to(b_tot, (H_MLSTM, LANES))], axis=0)
            a_all.append(jnp.exp(c_rows - c_max))
        for c, a_rows in zip(cs, a_all):
            sq, rows = tokens(c)
            for p in range(NP):
                lanes = slice(p * LANES, (p + 1) * LANES)
                k_t = k_ref[sq, rows, lanes].astype(F32).T
                a_sel = jnp.where(top, a_rows[2 * p:2 * p + 1], a_rows[2 * p + 1:2 * p + 2])
                vv = jnp.concatenate([v_ref[sq, rows, lanes], ones_b], axis=1)
                un = _dot((k_t * a_sel).astype(BF16), vv)
                un_scr[c, p] = jnp.where(keep_state, un, 0.0)
        return carry

    lax.fori_loop(0, bb * nc // grp_a, pass_a, 0)

    def pass_b(ci, m, sq):
        c = sq * nc + jnp.where(d == 0, ci, nc - 1 - ci)
        st = stat_scr[c]
        c_max, b_tot = st[0:H_MLSTM], st[H_MLSTM:]
        m_new = jnp.maximum(b_tot + m, b_tot + c_max)
        d_old = jnp.exp(b_tot + m - m_new)
        d_new = jnp.exp(b_tot + c_max - m_new)
        mprev_scr[c] = jnp.concatenate([m, m], axis=0) * LOG2E
        for p in range(NP):
            def rows_of(t, p=p):
                even = jnp.concatenate([t[2 * p:2 * p + 1]] * 2, axis=1)
                odd = jnp.concatenate([t[2 * p + 1:2 * p + 2]] * 2, axis=1)
                return jnp.where(top2, even, odd)
            s_prev = st_scr[p]
            st_scr[p] = rows_of(d_old) * s_prev + rows_of(d_new) * un_scr[c, p]
            un_scr[c, p] = s_prev
        return m_new

    def recurrence(sq, carry):
        if has_state:
            st_scr[...] = s0_ref[sq, 0]
            m_start = m0_ref[sq, 0][0:H_MLSTM]
        else:
            st_scr[...] = jnp.zeros(st_scr.shape, F32)
            m_start = jnp.zeros((H_MLSTM, LANES), F32)
        m_fin = lax.fori_loop(0, nc, functools.partial(pass_b, sq=sq), m_start)
        sf_ref[sq, 0] = st_scr[...]
        mf_ref[sq, 0] = jnp.concatenate([m_fin, m_fin], axis=0)
        return carry

    lax.fori_loop(0, bb, recurrence, 0)

    def pass_c(it, carry):
        cs = [it * grp + u for u in range(grp)]
        units = [(u, p) for u in range(grp) for p in range(NP)]
        early = {}
        for u, p in units:
            c = cs[u]
            sq, rows = tokens(c)
            lanes = slice(p * LANES, (p + 1) * LANES)
            qp = q_ref[sq, rows, lanes]
            kp = k_ref[sq, rows, lanes]
            s_in = un_scr[c, p].astype(BF16)
            qms = [jnp.where(lo if hh == 0 else jnp.logical_not(lo), qp, jnp.zeros_like(qp)) for hh in range(2)]
            early[u, p] = ([_dot_nt(qm, kp) for qm in qms],
                           _dot(qp, s_in))
        mid = {}
        for u, p in units:
            r_t = rows_scr[cs[u]]
            m_in = mprev_scr[cs[u]]
            for hh in range(2):
                h = 2 * p + hh
                cm = jnp.where(mask, r_t[h:h + 1, :], -jnp.inf)
                m_prev = m_in[h:h + 1, :]
                mu = jnp.maximum(jnp.broadcast_to(jnp.max(cm, axis=1, keepdims=True), (L, LANES)), m_prev)
                w = early[u, p][0][hh] * jnp.exp2(cm - mu)
                mid[u, p, hh] = (w.astype(BF16), mu, m_prev)
        for u, p in units:
            sq, rows = tokens(cs[u])
            lanes = slice(p * LANES, (p + 1) * LANES)
            vp = v_ref[sq, rows, lanes]
            zero = jnp.zeros_like(vp)
            vv = jnp.concatenate([jnp.concatenate([jnp.where(lo, vp, zero), ones_lo], axis=1),
                                  jnp.concatenate([jnp.where(lo, zero, vp), ones_hi], axis=1)], axis=0)
            w2 = jnp.concatenate([mid[u, p, 0][0], mid[u, p, 1][0]], axis=1)
            nd = _dot(w2, vv)
            nat = nat_scr[chunk_rows(cs[u]), :]
            nb = [jnp.broadcast_to(nat[:, H_MLSTM + 2 * p + hh:H_MLSTM + 2 * p + hh + 1], (L, LANES))
                  for hh in range(2)]
            fs = early[u, p][1]
            mu = jnp.where(lo, mid[u, p, 0][1], mid[u, p, 1][1])
            m_prev = jnp.where(lo[0:1], mid[u, p, 0][2], mid[u, p, 1][2])
            w_inter = jnp.exp2(m_prev - mu)
            den = nd[:, LANES:] + w_inter * fs[:, LANES:]
            den = jnp.maximum(jnp.abs(den), jnp.exp2(jnp.where(lo, nb[0], nb[1]) - mu))
            h_scr[d, chunk_rows(cs[u]), lanes] = (nd[:, :LANES] + w_inter * fs[:, :LANES]) / den
        return carry

    lax.fori_loop(0, bb * nc // grp, pass_c, 0)

    @pl.when(d == 1)
    def _finish():
        def rows_block(c, carry):
            sq, rows = tokens(c)
            flat = chunk_rows(c)
            for p in range(NP):
                lanes = slice(p * LANES, (p + 1) * LANES)
                hn = _pair_rms(h_scr[0, flat, lanes] + h_scr[1, flat, lanes], on_ref[:, lanes])
                out_ref[sq, rows, lanes] = (_sigmoid(ob_ref[sq, rows, lanes]) * hn).astype(out_ref.dtype)
            return carry

        lax.fori_loop(0, bb * nc, rows_block, 0, unroll=4)


def _mlstm(q, k, v, gates, ob, state, out_norm, *, name):
    assert L_CHUNK == LANES
    B, S, W = q.shape
    nc = S // L_CHUNK
    bb = max(1, min(B, MLSTM_STEP_CHUNKS // nc))
    ncb, sb = bb * nc, bb * S
    grp = min(MLSTM_GROUP, ncb)
    grp_a = min(MLSTM_GROUP_A, ncb)
    npair = H_MLSTM // 2
    seq = lambda b, d: (b, 0, 0)
    est = (2 * (3 * sb * W * 2 + sb * LANES * 4 + sb * W * 4 + sb * W * 2) + 2 * sb * W * 4 + sb * LANES * 4
           + ncb * npair * LANES * 2 * LANES * 4 + 12 * 1024 * 1024)
    state_specs = [pl.BlockSpec((bb, 1, npair, LANES, 2 * LANES), lambda b, d: (b, d, 0, 0, 0)),
                   pl.BlockSpec((bb, 1, 8, LANES), lambda b, d: (b, d, 0, 0))]
    has_state = state is not None
    return pl.pallas_call(
        functools.partial(_mlstm_kernel, nc=nc, grp_a=grp_a, grp=grp, has_state=has_state),
        grid=(B // bb, 2),
        in_specs=[pl.BlockSpec((bb, S, W), seq), pl.BlockSpec((bb, S, W), seq), pl.BlockSpec((bb, S, W), seq),
                  pl.BlockSpec((bb, S, LANES), lambda b, d: (b, 0, d)),
                  pl.BlockSpec((bb, S, W), seq)] + (state_specs if has_state else [])
                 + [pl.BlockSpec((1, W), lambda b, d: (0, 0))],
        out_specs=[pl.BlockSpec((bb, S, W), seq),
                   pl.BlockSpec((bb, 1, npair, LANES, 2 * LANES), lambda b, d: (b, d, 0, 0, 0)),
                   pl.BlockSpec((bb, 1, 8, LANES), lambda b, d: (b, d, 0, 0))],
        out_shape=[jax.ShapeDtypeStruct((B, S, W), BF16),
                   jax.ShapeDtypeStruct((B, 2, npair, LANES, 2 * LANES), F32),
                   jax.ShapeDtypeStruct((B, 2, 8, LANES), F32)],
        scratch_shapes=[pltpu.VMEM((2, sb, W), F32),
                        pltpu.VMEM((sb, LANES), F32),
                        pltpu.VMEM((ncb, 8, L_CHUNK), F32),
                        pltpu.VMEM((ncb, 8, LANES), F32),
                        pltpu.VMEM((ncb, 8, LANES), F32),
                        pltpu.VMEM((ncb, npair, LANES, 2 * LANES), F32),
                        pltpu.VMEM((npair, LANES, 2 * LANES), F32)],
        compiler_params=pltpu.CompilerParams(dimension_semantics=("arbitrary", "arbitrary"),
                                             vmem_limit_bytes=_vmem_limit(est)),
        name=name,
    )(q, k, v, gates, ob, *(state if has_state else ()), out_norm)


def _outproj_kernel(a_ref, b_ref, c_ref, x_ref, mod_ref, w_ref, gpost_ref, gpre_ref, x1_ref, h2_ref):
    mod = mod_ref[0]
    gt1 = mod[:, 2 * D_MODEL:3 * D_MODEL]
    sh2 = mod[:, 3 * D_MODEL:4 * D_MODEL]
    sc2 = mod[:, 4 * D_MODEL:5 * D_MODEL]
    tr = x_ref.shape[0] // ROW_SPLIT
    pieces = [pl.ds(s * tr, tr) for s in range(ROW_SPLIT)]
    mos = [_dot(jnp.concatenate([a_ref[r, :], b_ref[r, :], c_ref[r, :]], axis=1), w_ref[...]) for r in pieces]
    for r, mo in zip(pieces, mos):
        x1 = x_ref[r, :] + gt1 * _rms(mo, gpost_ref[...])
        x1_ref[r, :] = x1
        h2_ref[r, :] = (_rms(x1, gpre_ref[...]) * (1.0 + sc2) + sh2).astype(h2_ref.dtype)


def _outproj(oa, ob, oc, x, mods, w_out, g_post, g_pre, *, rows_per_cond, name):
    T = x.shape[0]
    tm = TM_PROJ
    bpc = rows_per_cond // tm
    row = lambda i: (i, 0)
    const = lambda i: (0, 0)
    est = 2 * (tm * D_MODEL * (2 + 4 + 4 + 2) + D_MODEL * D_MODEL * 2) + 4 * tm * D_MODEL * 4
    return pl.pallas_call(
        _outproj_kernel,
        grid=(T // tm,),
        in_specs=[pl.BlockSpec((tm, W_ATTN), row), pl.BlockSpec((tm, W_MLSTM), row),
                  pl.BlockSpec((tm, W_NBHD), row), pl.BlockSpec((tm, D_MODEL), row),
                  pl.BlockSpec((1, 1, N_MOD * D_MODEL), lambda i: (i // bpc, 0, 0)),
                  pl.BlockSpec((D_MODEL, D_MODEL), const),
                  pl.BlockSpec((1, D_MODEL), const), pl.BlockSpec((1, D_MODEL), const)],
        out_specs=[pl.BlockSpec((tm, D_MODEL), row), pl.BlockSpec((tm, D_MODEL), row)],
        out_shape=[jax.ShapeDtypeStruct((T, D_MODEL), F32), jax.ShapeDtypeStruct((T, D_MODEL), BF16)],
        compiler_params=pltpu.CompilerParams(dimension_semantics=("arbitrary",),
                                             vmem_limit_bytes=_vmem_limit(est)),
        name=name,
    )(oa, ob, oc, x, mods, w_out, g_post, g_pre)


def _ffn_kernel(h_ref, x_ref, mod_ref, wu_ref, wd_ref, g_ref, o_ref, acc_ref):
    j = pl.program_id(1)

    @pl.when(j == 0)
    def _zero():
        acc_ref[...] = jnp.zeros_like(acc_ref)

    u = jnp.maximum(_dot(h_ref[...], wu_ref[...]), 0.0)
    acc_ref[...] += _dot((u * u).astype(BF16), wd_ref[...])

    @pl.when(j == pl.num_programs(1) - 1)
    def _finish():
        gt2 = mod_ref[0][:, 5 * D_MODEL:6 * D_MODEL]
        o_ref[...] = x_ref[...] + gt2 * _rms(acc_ref[...], g_ref[...])


def _ffn(h2, x1, mods, w_up, w_down, g_post, *, rows_per_cond, name):
    T = x1.shape[0]
    tm, tf = TM_FFN, TF_FFN
    tm = min(tm, rows_per_cond)
    bpc = rows_per_cond // tm
    est = 2 * (tm * D_MODEL * (2 + 4 + 4) + 2 * D_MODEL * tf * 2) + tm * D_MODEL * 4 + 3 * tm * tf * 4
    return pl.pallas_call(
        _ffn_kernel,
        grid=(T // tm, D_FF // tf),
        in_specs=[pl.BlockSpec((tm, D_MODEL), lambda i, j: (i, 0)),
                  pl.BlockSpec((tm, D_MODEL), lambda i, j: (i, 0)),
                  pl.BlockSpec((1, 1, N_MOD * D_MODEL), lambda i, j: (i // bpc, 0, 0)),
                  pl.BlockSpec((D_MODEL, tf), lambda i, j: (0, j)),
                  pl.BlockSpec((tf, D_MODEL), lambda i, j: (j, 0)),
                  pl.BlockSpec((1, D_MODEL), lambda i, j: (0, 0))],
        out_specs=pl.BlockSpec((tm, D_MODEL), lambda i, j: (i, 0)),
        out_shape=jax.ShapeDtypeStruct((T, D_MODEL), F32),
        scratch_shapes=[pltpu.VMEM((tm, D_MODEL), F32)],
        compiler_params=pltpu.CompilerParams(dimension_semantics=("arbitrary", "arbitrary"),
                                             vmem_limit_bytes=_vmem_limit(est)),
        name=name,
    )(h2, x1, mods, w_up, w_down, g_post)


def _pad_w_in(w_in_l):
    o = W_ATTN + 2 * W_KV + 4 * W_MLSTM
    pre, gates, post = w_in_l[:, :o], w_in_l[:, o:o + N_GATES], w_in_l[:, o + N_GATES:]
    z = jnp.zeros((D_MODEL, LANES - 2 * H_MLSTM), w_in_l.dtype)
    return jnp.concatenate([pre, gates[:, :2 * H_MLSTM], z, gates[:, 2 * H_MLSTM:], z, post],
                           axis=1).astype(BF16)


def _pad_gate_bias(gb_l):
    z = jnp.zeros((LANES - 2 * H_MLSTM,), gb_l.dtype)
    return jnp.concatenate([gb_l[:2 * H_MLSTM], z, gb_l[2 * H_MLSTM:], z]).reshape(1, 2 * LANES)


def _rope_tables(S):
    quarter = HEAD_DIM // 4
    pos = np.arange(S)
    inv_freq = np.float32(ROPE_THETA) ** (-np.arange(quarter, dtype=np.float32) / np.float32(quarter))

    def tabs(p):
        ang = p.astype(np.float32)[:, None] * inv_freq[None, :]
        return np.cos(ang), np.sin(ang)

    cr, sr = tabs(pos // GRID_W)
    cc, sc = tabs(pos % GRID_W)
    cos = np.concatenate([cr, cr, cc, cc], axis=1)
    sin = np.concatenate([-sr, sr, -sc, sc], axis=1)
    return jnp.asarray(np.tile(cos, (1, 2)), F32), jnp.asarray(np.tile(sin, (1, 2)), F32)


def _pack_state(C, n, m):
    B = C.shape[0]
    Cp = C.reshape(B, 2, 2, 2, HEAD_DIM, HEAD_DIM)
    z = jnp.zeros_like(Cp[:, :, :, 0])
    top = jnp.concatenate([Cp[:, :, :, 0], z], axis=-1)
    bot = jnp.concatenate([z, Cp[:, :, :, 1]], axis=-1)
    Cbd = jnp.concatenate([top, bot], axis=-2)
    n_rep = jnp.broadcast_to(n.reshape(B, 2, 2, LANES, 1), (B, 2, 2, LANES, LANES))
    same_head = (jnp.arange(LANES)[:, None] < HEAD_DIM) == (jnp.arange(LANES)[None, :] < HEAD_DIM)
    n_rep = jnp.where(same_head, n_rep, 0.0)
    m_rows = jnp.broadcast_to(m[..., None], m.shape + (LANES,))
    return jnp.concatenate([Cbd, n_rep], axis=-1), jnp.concatenate([m_rows, m_rows], axis=-2)


def _unpack_state(s_p, m_p):
    B = s_p.shape[0]
    c_even = s_p[:, :, :, :HEAD_DIM, :HEAD_DIM]
    c_odd = s_p[:, :, :, HEAD_DIM:, HEAD_DIM:LANES]
    C = jnp.stack([c_even, c_odd], axis=3).reshape(B, 2, H_MLSTM, HEAD_DIM, HEAD_DIM)
    n = jnp.concatenate([s_p[..., :HEAD_DIM, LANES], s_p[..., HEAD_DIM:, LANES + HEAD_DIM]], axis=-1)
    return C, n.reshape(B, 2, H_MLSTM, HEAD_DIM), m_p[:, :, :H_MLSTM, 0]


def _layer(x, mods, lw, *, B, S, cond_rows, rope_tabs, ctx_cache, state, name):
    T = B * S
    kv_dtype = BF16 if ctx_cache is not None else F32
    ck_a = min(CK_ATTN, S)
    pr = _inproj(x, mods, lw["g_pre_mix"], lw["w_in"], lw["q_norm"], lw["k_norm"], lw["gate_bias"],
                 rope_tabs, rows_per_cond=cond_rows, kv_dtype=kv_dtype,
                 vt_blocks=(ck_a, NB_KV_BLOCK), name=name + "_inproj")
    seq = lambda a: a.reshape(B, S, a.shape[-1])
    qa, ka, va = seq(pr["qa"]), seq(pr["ka"]), seq(pr["va"])
    qc, kc, vc = seq(pr["qc"]), seq(pr["kc"]), seq(pr["vc"])
    chunks = lambda a, n: a.reshape(B, S // n, n, a.shape[-1])
    vat = pr["vat"].reshape(B, S // ck_a, W_KV, ck_a)
    vct = pr["vct"].reshape(B, S // NB_KV_BLOCK, W_NBHD, NB_KV_BLOCK)
    if ctx_cache is None:
        out_a = _attention(qa, [(chunks(ka, ck_a), vat)], heads=_HEADS_GQA, name=name + "_attn_a")
        out_c = _attention(qc, [(chunks(kc, NB_KV_BLOCK), vct)], heads=_HEADS_MHA, name=name + "_attn_c")
    else:
        ck_c, cv_c = ctx_cache[2], ctx_cache[3]
        P = ck_c.shape[1]
        assert P == ck_a
        cva_t, cvc_t = _cached_values_t(ctx_cache[1], cv_c, name=name + "_cache_t")
        out_a = _attention(qa, [(chunks(ka, ck_a), vat),
                                (ctx_cache[0].reshape(B, 1, P, W_KV), cva_t.reshape(B, 1, W_KV, P))],
                           heads=_HEADS_GQA, name=name + "_attn_a")
        out_c = _nbhd_attention(qc, chunks(kc, NB_KV_BLOCK), vct, ck_c, cvc_t,
                                lw["rel_bias"], name=name + "_attn_c")
    out_b, sf, mf = _mlstm(seq(pr["qb"]), seq(pr["kb"]), seq(pr["vb"]), seq(pr["g"]), seq(pr["ob"]),
                           state, lw["out_norm"], name=name + "_mlstm")
    x1, h2 = _outproj(out_a.reshape(T, W_ATTN), out_b.reshape(T, W_MLSTM), out_c.reshape(T, W_NBHD),
                      x, mods, lw["w_out"], lw["g_post_mix"], lw["g_pre_ffn"],
                      rows_per_cond=cond_rows, name=name + "_outproj")
    x2 = _ffn(h2, x1, mods, lw["w_up"], lw["w_down"], lw["g_post_ffn"],
              rows_per_cond=cond_rows, name=name + "_ffn")
    return x2, (ka, va, kc, vc, sf, mf)


def kernel(x_prompt, x_sample, c, cache_k_attn, cache_v_attn, cache_k_nbhd, cache_v_nbhd, state_mlstm_C, state_mlstm_n, state_mlstm_m, c_ctx, w_ada, b_ada, g_pre_mix, g_post_mix, g_pre_ffn, g_post_ffn, w_in, q_norm_attn, k_norm_attn, mlstm_gate_bias, mlstm_out_norm, nbhd_rel_bias, w_out, w_ffn_up, w_ffn_down):
    Bc, Sc, _ = x_prompt.shape
    Bl, Sl, _ = x_sample.shape
    P = cache_k_attn.shape[2]
    n_cond = 8
    cond = jnp.concatenate([c_ctx[None, :], c, jnp.zeros((n_cond - 1 - Bl, D_MODEL), F32)], axis=0)
    mods_all = _modulation(cond, w_ada, b_ada)

    layers = []
    for l in range(DEPTH):
        layers.append(dict(
            w_in=_pad_w_in(w_in[l]),
            w_out=w_out[l].astype(BF16),
            w_up=w_ffn_up[l].astype(BF16),
            w_down=w_ffn_down[l].astype(BF16),
            g_pre_mix=g_pre_mix[l].reshape(1, D_MODEL), g_post_mix=g_post_mix[l].reshape(1, D_MODEL),
            g_pre_ffn=g_pre_ffn[l].reshape(1, D_MODEL), g_post_ffn=g_post_ffn[l].reshape(1, D_MODEL),
            q_norm=jnp.tile(q_norm_attn[l], 2).reshape(1, LANES),
            k_norm=jnp.tile(k_norm_attn[l], 2).reshape(1, LANES),
            gate_bias=_pad_gate_bias(mlstm_gate_bias[l]),
            out_norm=mlstm_out_norm[l].reshape(1, W_MLSTM),
            rel_bias=nbhd_rel_bias[l].reshape(-1),
        ))

    xp = x_prompt.reshape(Bc * Sc, D_MODEL)
    ctx = []
    for l in range(DEPTH):
        mods = mods_all[l, 0:1].reshape(1, 1, N_MOD * D_MODEL)
        xp, extras = _layer(xp, mods, layers[l], B=Bc, S=Sc, cond_rows=Bc * Sc, rope_tabs=None,
                            ctx_cache=None, state=None, name=f"ctx{l}")
        ctx.append(extras)
    new_k_attn = jnp.stack([e[0].reshape(Bc, Sc, KV_ATTN, HEAD_DIM) for e in ctx], axis=1)
    new_v_attn = jnp.stack([e[1].reshape(Bc, Sc, KV_ATTN, HEAD_DIM) for e in ctx], axis=1)
    new_k_nbhd = jnp.stack([e[2].reshape(Bc, Sc, H_NBHD, HEAD_DIM) for e in ctx], axis=1)
    new_v_nbhd = jnp.stack([e[3].reshape(Bc, Sc, H_NBHD, HEAD_DIM) for e in ctx], axis=1)
    states = [_unpack_state(e[4], e[5]) for e in ctx]
    new_C = jnp.stack([s[0] for s in states], axis=1)
    new_n = jnp.stack([s[1] for s in states], axis=1)
    new_m = jnp.stack([s[2] for s in states], axis=1)

    xs = x_sample.reshape(Bl * Sl, D_MODEL)
    rope_tabs = _rope_tables(Sl)
    for l in range(DEPTH):
        mods = mods_all[l, 1:1 + Bl].reshape(Bl, 1, N_MOD * D_MODEL)
        cache = (cache_k_attn[:, l].reshape(Bl, P, W_KV), cache_v_attn[:, l].reshape(Bl, P, W_KV),
                 cache_k_nbhd[:, l].reshape(Bl, P, W_NBHD), cache_v_nbhd[:, l].reshape(Bl, P, W_NBHD))
        state = _pack_state(state_mlstm_C[:, l], state_mlstm_n[:, l], state_mlstm_m[:, l])
        xs, _ = _layer(xs, mods, layers[l], B=Bl, S=Sl, cond_rows=Sl, rope_tabs=rope_tabs,
                       ctx_cache=cache, state=state, name=f"lat{l}")

    return (xp.reshape(Bc, Sc, D_MODEL), xs.reshape(Bl, Sl, D_MODEL),
            new_k_attn, new_v_attn, new_k_nbhd, new_v_nbhd, new_C, new_n, new_m)
```

```python
import functools

import jax
import jax.numpy as jnp
import numpy as np
from jax import lax
from jax.experimental import pallas as pl
from jax.experimental.pallas import tpu as pltpu

F32 = jnp.float32
BF16 = jnp.bfloat16

D_MODEL = 1024
DEPTH = 2
GRID_W = 64
HEAD_DIM = 64
H_ATTN = 6
KV_ATTN = 2
H_MLSTM = 4
H_NBHD = 6
D_FF = 4 * D_MODEL
NA_ROWS = 8
NA_COLS = 16
ROPE_THETA = 10000.0
EPS = 1e-6
N_MOD = 6
W_ATTN = H_ATTN * HEAD_DIM
W_KV = KV_ATTN * HEAD_DIM
W_MLSTM = H_MLSTM * HEAD_DIM
W_NBHD = H_NBHD * HEAD_DIM
N_GATES = 4 * H_MLSTM

LANES = 128
V7X_VMEM_BYTES = 64 * 1024 * 1024
VMEM_CAP_BYTES = 56 * 1024 * 1024

TM_PROJ = 512
PROJ_CHUNK = 512
TM_FFN = 1024
TF_FFN = 1024
ROW_SPLIT = 4
TQ_ATTN = 512
CK_ATTN = 512
ATTN_LOOKAHEAD = 2
SEQS_PER_STEP = 16
L_CHUNK = 128
MLSTM_GROUP = 2
MLSTM_GROUP_A = 16
MLSTM_STEP_CHUNKS = 32
NB_GROUP = 8
NB_SLAB = 16
NB_STEP_GROUPS = 8
NB_KV_BLOCK = 256
NEG = -1e30
LOG2E = 1.4426950408889634

_COLS = {}
_off = 0
for _name, _w in (("qa", W_ATTN), ("ka", W_KV), ("va", W_KV), ("qb", W_MLSTM), ("kb", W_MLSTM),
                  ("vb", W_MLSTM), ("ob", W_MLSTM), ("gf", LANES), ("gb", LANES),
                  ("qc", W_NBHD), ("kc", W_NBHD), ("vc", W_NBHD)):
    _COLS[_name] = (_off, _off + _w)
    _off += _w
IN_PAD = _off


def _vmem_limit(nbytes):
    return int(min(max(nbytes, 16 * 1024 * 1024), VMEM_CAP_BYTES))


def _dot(a, b):
    return jnp.dot(a, b, preferred_element_type=F32)


def _dot_nt(a, b):
    return lax.dot_general(a, b, (((1,), (1,)), ((), ())), preferred_element_type=F32)


def _lane_lo(shape):
    return (lax.broadcasted_iota(jnp.int32, shape, len(shape) - 1) % LANES) < HEAD_DIM


def _rms(x, g):
    ms = jnp.mean(x * x, axis=-1, keepdims=True)
    return (x * lax.rsqrt(ms + EPS)) * g


def _pair_rms(x, g):
    lo = _lane_lo(x.shape)
    x2 = x * x
    s_lo = jnp.sum(jnp.where(lo, x2, 0.0), axis=-1, keepdims=True)
    s_hi = jnp.sum(jnp.where(lo, 0.0, x2), axis=-1, keepdims=True)
    r = jnp.where(lo, lax.rsqrt(s_lo / HEAD_DIM + EPS), lax.rsqrt(s_hi / HEAD_DIM + EPS))
    return (x * r) * g


def _sigmoid(x):
    return 1.0 / (1.0 + jnp.exp(-x))


def _mods_kernel(c_ref, w_ref, b_ref, o_ref):
    c = c_ref[...]
    s = (c * _sigmoid(c)).astype(BF16)
    o_ref[0] = _dot(s, w_ref[0].astype(BF16)) + b_ref[0]


def _modulation(cond, w_ada, b_ada):
    n = cond.shape[0]
    tn = D_MODEL
    return pl.pallas_call(
        _mods_kernel,
        grid=(DEPTH, N_MOD * D_MODEL // tn),
        in_specs=[pl.BlockSpec((n, D_MODEL), lambda l, j: (0, 0)),
                  pl.BlockSpec((1, D_MODEL, tn), lambda l, j: (l, 0, j)),
                  pl.BlockSpec((1, 1, tn), lambda l, j: (l, 0, j))],
        out_specs=pl.BlockSpec((1, n, tn), lambda l, j: (l, 0, j)),
        out_shape=jax.ShapeDtypeStruct((DEPTH, n, N_MOD * D_MODEL), F32),
        compiler_params=pltpu.CompilerParams(
            dimension_semantics=("arbitrary", "arbitrary"),
            vmem_limit_bytes=_vmem_limit(4 * D_MODEL * tn * 4)),
        name="modulation",
    )(cond, w_ada, b_ada.reshape(DEPTH, 1, N_MOD * D_MODEL))


def _inproj_kernel(*refs, rope):
    if rope:
        (x_ref, mod_ref, g_ref, w_ref, qn_ref, kn_ref, gbias_ref, cos_ref, sin_ref,
         qa_ref, ka_ref, va_ref, qb_ref, kb_ref, vb_ref, ob_ref, gate_ref,
         qc_ref, kc_ref, vc_ref, vat_ref, vct_ref) = refs
    else:
        (x_ref, mod_ref, g_ref, w_ref, qn_ref, kn_ref, gbias_ref,
         qa_ref, ka_ref, va_ref, qb_ref, kb_ref, vb_ref, ob_ref, gate_ref,
         qc_ref, kc_ref, vc_ref, vat_ref, vct_ref) = refs
    x = x_ref[...]
    mod = mod_ref[0]
    sh1 = mod[:, 0:D_MODEL]
    sc1 = mod[:, D_MODEL:2 * D_MODEL]
    hb = (_rms(x, g_ref[...]) * (1.0 + sc1) + sh1).astype(BF16)

    z = [_dot(hb, w_ref[:, c0:c0 + PROJ_CHUNK]) for c0 in range(0, IN_PAD, PROJ_CHUNK)]

    def proj(name, j=0, w=None):
        lo, hi = _COLS[name]
        lo = lo + j
        hi = hi if w is None else lo + w
        parts = []
        while lo < hi:
            c, o = divmod(lo, PROJ_CHUNK)
            n = min(hi - lo, PROJ_CHUNK - o)
            parts.append(z[c][:, o:o + n])
            lo += n
        return parts[0] if len(parts) == 1 else jnp.concatenate(parts, axis=1)

    scale = HEAD_DIM ** -0.5
    q_scale = scale * LOG2E

    def rotary(t):
        first = (lax.broadcasted_iota(jnp.int32, t.shape, 1) % 32) < 16
        partner = jnp.where(first, pltpu.roll(t, LANES - 16, 1), pltpu.roll(t, 16, 1))
        return t * cos_ref[...] + partner * sin_ref[...]

    def store_queries(t, j, q_ref, kv_half):
        lo = _lane_lo(t.shape)
        for e in range(2):
            h = 2 * j + e
            src = t if e == kv_half[h] else pltpu.roll(t, HEAD_DIM, 1)
            keep = lo if kv_half[h] == 0 else jnp.logical_not(lo)
            q_ref[:, h * LANES:(h + 1) * LANES] = jnp.where(keep, src, 0.0).astype(q_ref.dtype)

    for j in range(W_ATTN // LANES):
        t = _pair_rms(proj("qa", j * LANES, LANES), qn_ref[...])
        if rope:
            t = rotary(t)
        store_queries(t * q_scale, j, qa_ref, [kh for _, kh in _HEADS_GQA])
    t = _pair_rms(proj("ka"), kn_ref[...])
    if rope:
        t = rotary(t)
    ka_ref[...] = t.astype(ka_ref.dtype)

    def store_v(v, v_ref, vt_ref):
        v_ref[...] = v.astype(v_ref.dtype)
        nblk, _, blk = vt_ref.shape
        for u in range(nblk):
            vt_ref[u] = v[u * blk:(u + 1) * blk].T.astype(vt_ref.dtype)

    store_v(proj("va"), va_ref, vat_ref)
    qb_ref[...] = proj("qb").astype(qb_ref.dtype)
    kb_ref[...] = (proj("kb") * scale).astype(kb_ref.dtype)
    vb_ref[...] = proj("vb").astype(vb_ref.dtype)
    ob_ref[...] = proj("ob").astype(ob_ref.dtype)
    for j, name in enumerate(("gf", "gb")):
        gt = proj(name) + gbias_ref[:, j * LANES:(j + 1) * LANES]
        lane = lax.broadcasted_iota(jnp.int32, gt.shape, 1)
        is_f = (lane >= H_MLSTM) & (lane < 2 * H_MLSTM)
        logsig = jnp.minimum(gt, 0.0) - jnp.log1p(jnp.exp(-jnp.abs(gt)))
        gate_ref[:, j * LANES:(j + 1) * LANES] = jnp.where(is_f, logsig, gt)
    for j in range(W_NBHD // LANES):
        store_queries(proj("qc", j * LANES, LANES) * q_scale, j, qc_ref, [kh for _, kh in _HEADS_MHA])
    kc_ref[...] = proj("kc").astype(kc_ref.dtype)
    store_v(proj("vc"), vc_ref, vct_ref)


def _inproj(x, mods, g_pre, w_in_p, qn, kn, gbias, rope_tabs, *, rows_per_cond, kv_dtype, vt_blocks, name):
    T = x.shape[0]
    tm = TM_PROJ
    bpc = rows_per_cond // tm
    rope = rope_tabs is not None
    row = lambda i: (i, 0)
    const = lambda i: (0, 0)
    in_specs = [pl.BlockSpec((tm, D_MODEL), row),
                pl.BlockSpec((1, 1, N_MOD * D_MODEL), lambda i: (i // bpc, 0, 0)),
                pl.BlockSpec((1, D_MODEL), const),
                pl.BlockSpec((D_MODEL, IN_PAD), const),
                pl.BlockSpec((1, LANES), const),
                pl.BlockSpec((1, LANES), const),
                pl.BlockSpec((1, 2 * LANES), const)]
    args = [x, mods, g_pre, w_in_p, qn, kn, gbias]
    if rope:
        nblk = rope_tabs[0].shape[0] // tm
        in_specs += [pl.BlockSpec((tm, LANES), lambda i: (i % nblk, 0))] * 2
        args += list(rope_tabs)
    widths = [("qa", H_ATTN * LANES, BF16), ("ka", W_KV, kv_dtype), ("va", W_KV, kv_dtype),
              ("qb", W_MLSTM, BF16), ("kb", W_MLSTM, BF16), ("vb", W_MLSTM, BF16),
              ("ob", W_MLSTM, F32), ("g", 2 * LANES, F32),
              ("qc", H_NBHD * LANES, BF16), ("kc", W_NBHD, kv_dtype), ("vc", W_NBHD, kv_dtype)]
    out_specs = [pl.BlockSpec((tm, w), row) for _, w, _ in widths]
    out_shape = [jax.ShapeDtypeStruct((T, w), dt) for _, w, dt in widths]
    for w, blk in zip((W_KV, W_NBHD), vt_blocks):
        out_specs.append(pl.BlockSpec((tm // blk, w, blk), lambda i: (i, 0, 0)))
        out_shape.append(jax.ShapeDtypeStruct((T // blk, w, blk), BF16))
    est = 2 * (tm * D_MODEL * 4 + D_MODEL * IN_PAD * 2 + tm * IN_PAD * 4) + 3 * tm * IN_PAD * 4
    outs = pl.pallas_call(
        functools.partial(_inproj_kernel, rope=rope),
        grid=(T // tm,),
        in_specs=in_specs, out_specs=out_specs, out_shape=out_shape,
        compiler_params=pltpu.CompilerParams(dimension_semantics=("arbitrary",),
                                             vmem_limit_bytes=_vmem_limit(est)),
        name=name,
    )(*args)
    return dict(zip([n for n, _, _ in widths] + ["vat", "vct"], outs))


SUM_ROWS = 16


def _softmax_step(st, col_max, vt, m_scr, acc_scr, u):
    m_old = m_scr[u]
    m_new = jnp.maximum(m_old, col_max)
    p = jnp.exp2(st - m_new).astype(BF16)
    alpha = jnp.exp2(m_old - m_new)
    vt1 = jnp.concatenate([vt, jnp.ones((SUM_ROWS, vt.shape[1]), BF16)], axis=0)
    acc_scr[u] = alpha * acc_scr[u] + _dot(vt1, p)
    m_scr[u] = m_new


def _normalised_pair(acc_scr, u0, u1):
    halves = [acc_scr[u][0:HEAD_DIM] / acc_scr[u][HEAD_DIM:HEAD_DIM + 1] for u in (u0, u1)]
    return jnp.concatenate(halves, axis=0).T
def _attn_kernel(*refs, heads, part_chunks):
    n_parts = len(part_chunks)
    assert n_parts in (1, 2)
    q_ref, kv_refs = refs[0], refs[1:1 + 2 * n_parts]
    o_ref, m_scr, acc_scr, s_ring, cmax_scr = refs[1 + 2 * n_parts:]
    nch = sum(part_chunks)

    def chunk_of(which, j, rows, cols):
        first = kv_refs[which][0, jnp.minimum(j, part_chunks[0] - 1), rows, cols].astype(BF16)
        if n_parts == 1:
            return first
        j2 = jnp.clip(j - part_chunks[0], 0, part_chunks[1] - 1)
        return jnp.where(j < part_chunks[0], first, kv_refs[2 + which][0, j2, rows, cols].astype(BF16))

    nh = len(heads)
    ring = ATTN_LOOKAHEAD + 1
    assert nh % ring == 0
    m_scr[...] = jnp.full(m_scr.shape, -jnp.inf, F32)
    acc_scr[...] = jnp.zeros(acc_scr.shape, F32)

    def scores(j, item):
        h = item % nh
        kg = heads[h][0]
        kj = chunk_of(0, j, slice(None), slice(kg * LANES, (kg + 1) * LANES))
        st = _dot_nt(kj, q_ref[0, :, h * LANES:(h + 1) * LANES])
        s_ring[item % ring] = st
        cmax_scr[item % ring] = jnp.max(st, axis=0, keepdims=True)

    def chunk(j, carry, last=False):
        for h, (kg, kh) in enumerate(heads):
            ahead = h + ATTN_LOOKAHEAD
            if ahead < nh:
                scores(j, ahead)
            elif not last:
                scores(j + 1, ahead)
            r = kg * LANES + kh * HEAD_DIM
            vt = chunk_of(1, j, slice(r, r + HEAD_DIM), slice(None))
            _softmax_step(s_ring[h % ring], cmax_scr[h % ring], vt, m_scr, acc_scr, h)
        return carry

    for item in range(ATTN_LOOKAHEAD):
        scores(0, item)
    lax.fori_loop(0, nch - 1, chunk, 0)
    chunk(nch - 1, 0, last=True)
    for t in range(len(heads) // 2):
        o_ref[0, :, t * LANES:(t + 1) * LANES] = _normalised_pair(acc_scr, 2 * t, 2 * t + 1).astype(o_ref.dtype)


def _attn_seqs_kernel(q_ref, k_ref, vt_ref, o_ref, s_ring, cmax_scr, *, heads):
    nh = len(heads)
    n_seq = q_ref.shape[0]
    ring = nh
    look = nh - 1

    def scores(s, item):
        h = item % nh
        kg = heads[h][0]
        st = _dot_nt(k_ref[s, 0, :, kg * LANES:(kg + 1) * LANES].astype(BF16),
                     q_ref[s, :, h * LANES:(h + 1) * LANES])
        s_ring[item % ring] = st
        cmax_scr[item % ring] = jnp.max(st, axis=0, keepdims=True)

    def sequence(s, carry, last=False):
        accs = []
        for h, (kg, kh) in enumerate(heads):
            ahead = h + look
            if ahead < nh:
                scores(s, ahead)
            elif not last:
                scores(s + 1, ahead)
            r = kg * LANES + kh * HEAD_DIM
            vt = vt_ref[s, 0, r:r + HEAD_DIM, :].astype(BF16)
            vt1 = jnp.concatenate([vt, jnp.ones((SUM_ROWS, vt.shape[1]), BF16)], axis=0)
            p = jnp.exp2(s_ring[h % ring] - cmax_scr[h % ring]).astype(BF16)
            accs.append(_dot(vt1, p))
        for t in range(nh // 2):
            halves = [a[0:HEAD_DIM] / a[HEAD_DIM:HEAD_DIM + 1] for a in accs[2 * t:2 * t + 2]]
            o_ref[s, :, t * LANES:(t + 1) * LANES] = jnp.concatenate(halves, axis=0).T.astype(o_ref.dtype)
        return carry

    for item in range(look):
        scores(0, item)
    lax.fori_loop(0, n_seq - 1, sequence, 0)
    sequence(n_seq - 1, 0, last=True)


def _attention_seqs(q, k4, vt4, *, heads, name):
    B, Sq, WQ = q.shape
    ck, KW = k4.shape[2], k4.shape[3]
    nh = len(heads)
    W = nh * HEAD_DIM
    bb = min(SEQS_PER_STEP, B)
    ring = nh
    est = (2 * bb * (Sq * (WQ + W) * 2 + ck * KW * (k4.dtype.itemsize + vt4.dtype.itemsize))
           + ring * ck * Sq * 4 + 8 * nh * ck * Sq * 4)
    return pl.pallas_call(
        functools.partial(_attn_seqs_kernel, heads=heads),
        grid=(B // bb,),
        in_specs=[pl.BlockSpec((bb, Sq, WQ), lambda i: (i, 0, 0)),
                  pl.BlockSpec((bb, 1, ck, KW), lambda i: (i, 0, 0, 0)),
                  pl.BlockSpec((bb, 1, KW, ck), lambda i: (i, 0, 0, 0))],
        out_specs=pl.BlockSpec((bb, Sq, W), lambda i: (i, 0, 0)),
        out_shape=jax.ShapeDtypeStruct((B, Sq, W), BF16),
        scratch_shapes=[pltpu.VMEM((ring, ck, Sq), F32),
                        pltpu.VMEM((ring, 1, Sq), F32)],
        compiler_params=pltpu.CompilerParams(dimension_semantics=("arbitrary",),
                                             vmem_limit_bytes=_vmem_limit(est)),
        name=name,
    )(q, k4, vt4)


def _attention(q, parts, *, heads, name):
    B, Sq, WQ = q.shape
    ck, KW = parts[0][0].shape[2], parts[0][0].shape[3]
    tq = min(TQ_ATTN, Sq)
    nh = len(heads)
    W = nh * HEAD_DIM
    part_chunks = tuple(k4.shape[1] for k4, _ in parts)
    if part_chunks == (1,) and tq == Sq:
        return _attention_seqs(q, parts[0][0], parts[0][1], heads=heads, name=name)
    in_specs = [pl.BlockSpec((1, tq, WQ), lambda b, i: (b, i, 0))]
    args = [q]
    kv_bytes = 0
    for k4, vt4 in parts:
        assert k4.shape[2:] == (ck, KW) and vt4.shape[2:] == (KW, ck)
        in_specs += [pl.BlockSpec((1,) + k4.shape[1:], lambda b, i: (b, 0, 0, 0)),
                     pl.BlockSpec((1,) + vt4.shape[1:], lambda b, i: (b, 0, 0, 0))]
        args += [k4, vt4]
        kv_bytes += k4[0].size * k4.dtype.itemsize + vt4[0].size * vt4.dtype.itemsize
    est = (2 * (tq * (WQ + W) * 2 + kv_bytes) + nh * tq * (HEAD_DIM * 4 + 64) + 8 * nh * ck * tq * 4)
    return pl.pallas_call(
        functools.partial(_attn_kernel, heads=heads, part_chunks=part_chunks),
        grid=(B, Sq // tq),
        in_specs=in_specs,
        out_specs=pl.BlockSpec((1, tq, W), lambda b, i: (b, i, 0)),
        out_shape=jax.ShapeDtypeStruct((B, Sq, W), BF16),
        scratch_shapes=[pltpu.VMEM((nh, 1, tq), F32),
                        pltpu.VMEM((nh, HEAD_DIM + SUM_ROWS, tq), F32),
                        pltpu.VMEM((ATTN_LOOKAHEAD + 1, ck, tq), F32),
                        pltpu.VMEM((ATTN_LOOKAHEAD + 1, 1, tq), F32)],
        compiler_params=pltpu.CompilerParams(dimension_semantics=("arbitrary", "arbitrary"),
                                             vmem_limit_bytes=_vmem_limit(est)),
        name=name,
    )(*args)


def _transpose_kernel(a_ref, b_ref, at_ref, bt_ref):
    at_ref[0] = a_ref[0].T.astype(at_ref.dtype)
    bt_ref[0] = b_ref[0].T.astype(bt_ref.dtype)


def _cached_values_t(a, b, *, name):
    B, P, ca = a.shape
    cb = b.shape[2]
    return pl.pallas_call(
        _transpose_kernel,
        grid=(B,),
        in_specs=[pl.BlockSpec((1, P, ca), lambda i: (i, 0, 0)), pl.BlockSpec((1, P, cb), lambda i: (i, 0, 0))],
        out_specs=[pl.BlockSpec((1, ca, P), lambda i: (i, 0, 0)), pl.BlockSpec((1, cb, P), lambda i: (i, 0, 0))],
        out_shape=[jax.ShapeDtypeStruct((B, ca, P), BF16), jax.ShapeDtypeStruct((B, cb, P), BF16)],
        compiler_params=pltpu.CompilerParams(dimension_semantics=("arbitrary",)),
        name=name,
    )(a, b)


_HEADS_GQA = tuple((0, h // (H_ATTN // KV_ATTN)) for h in range(H_ATTN))
_HEADS_MHA = tuple((h // 2, h % 2) for h in range(H_NBHD))


def _nbhd_window(r, rows):
    kr = min(NA_ROWS, rows)
    return min(max(r - kr // 2, 0), rows - kr), kr


def _nbhd_patterns(rows):
    n_groups = rows // NB_GROUP
    pats = []
    for g in (0, 1, n_groups - 1):
        r0 = g * NB_GROUP
        pats.append((r0, min(max(r0 - NA_ROWS // 2, 0), rows - NB_SLAB)))
    return pats


def _nbhd_kernel(rb_ref, q_ref, k_ref, vt_ref, kc_ref, vct_ref, o_ref,
                 bias_scr, m_scr, acc_scr, s_ring, cmax_scr, *, rows):
    hp = pl.program_id(0)
    b = pl.program_id(1)
    gs = pl.program_id(2)
    n_groups = rows // NB_GROUP
    n_dr = 2 * NA_ROWS - 1
    n_dc = 2 * NA_COLS - 1
    tq = NB_GROUP * GRID_W
    ck = tq
    pats = _nbhd_patterns(rows)
    ring = ATTN_LOOKAHEAD + 1
    n_chunks = NB_SLAB * GRID_W // ck + 1
    assert n_chunks % ring == 0 and kc_ref.shape[1] == ck
    blk_rows = NB_KV_BLOCK // GRID_W

    @pl.when((b == 0) & (gs == 0))
    def _build_bias():
        shape = (GRID_W, LANES)
        w = lax.broadcasted_iota(jnp.int32, shape, 0)
        lane = lax.broadcasted_iota(jnp.int32, shape, 1)
        cc = lane % GRID_W
        second = lane >= GRID_W
        cs = jnp.clip(w - NA_COLS // 2, 0, GRID_W - NA_COLS)
        col_ok = (cc >= cs) & (cc < cs + NA_COLS)
        dc = cc - w + (NA_COLS - 1)
        for hh in range(2):
            base = (2 * hp + hh) * (n_dr * n_dc)
            tiles = {}
            for d in range(-1, n_dr):
                acc = jnp.zeros(shape, F32)
                for j in range(n_dc):
                    va = rb_ref[base + d * n_dc + j] * LOG2E if d >= 0 else 0.0
                    vb = rb_ref[base + (d + 1) * n_dc + j] * LOG2E if d + 1 < n_dr else 0.0
                    acc = acc + jnp.where(dc == j, jnp.where(second, vb, va), 0.0)
                tiles[d] = acc
            def query_row_tile(r, kra):
                rs, kr = _nbhd_window(r, rows)
                ok_a = rs <= kra < rs + kr
                ok_b = rs <= kra + 1 < rs + kr
                if not (ok_a or ok_b):
                    return jnp.full(shape, NEG, F32)
                row_ok = (jnp.logical_not(second) if ok_a and not ok_b else
                          second if ok_b and not ok_a else None)
                ok = col_ok if row_ok is None else (col_ok & row_ok)
                return jnp.where(ok, tiles[kra - r + (NA_ROWS - 1)], NEG)

            for pi, (r0, slab0) in enumerate(pats):
                for ip in range(NB_GROUP // 2):
                    for ap in range(NB_SLAB // 2):
                        kra = slab0 + 2 * ap
                        two_rows = jnp.concatenate([query_row_tile(r0 + 2 * ip, kra),
                                                    query_row_tile(r0 + 2 * ip + 1, kra)], axis=0)
                        bias_scr[hh, pi, ap * LANES:(ap + 1) * LANES, ip * LANES:(ip + 1) * LANES] = two_rows.T

    assert (2 * n_chunks) % ring == 0
    m_scr[...] = jnp.full(m_scr.shape, -jnp.inf, F32)
    acc_scr[...] = jnp.zeros(acc_scr.shape, F32)

    def group_of(s):
        g = gs * NB_STEP_GROUPS + s
        pat = jnp.where(g == 0, 0, jnp.where(g == n_groups - 1, 2, 1))
        slab0 = jnp.clip(g * NB_GROUP - NA_ROWS // 2, 0, rows - NB_SLAB)
        return pat, slab0 // blk_rows

    per_group = 2 * n_chunks
    nb = ck // NB_KV_BLOCK

    def scores(s, item):
        hh, c = (item % per_group) // n_chunks, item % n_chunks
        qm = q_ref[0, pl.ds(pl.multiple_of(s * tq, tq), tq), hh * LANES:(hh + 1) * LANES]
        if c < n_chunks - 1:
            pat, blk0 = group_of(s)
            kc = k_ref[0, pl.ds(blk0 + c * nb, nb)].reshape(ck, LANES)
            st = _dot_nt(kc, qm) + bias_scr[hh, pat, c * ck:(c + 1) * ck, :]
        else:
            st = _dot_nt(kc_ref[0].astype(BF16), qm)
        s_ring[item % ring] = st
        cmax_scr[item % ring] = jnp.max(st, axis=0, keepdims=True)

    def group(s, carry, last=False):
        for item in range(per_group):
            ahead = item + ATTN_LOOKAHEAD
            if ahead < per_group:
                scores(s, ahead)
            elif not last:
                scores(s + 1, ahead)
            hh, c = item // n_chunks, item % n_chunks
            if c < n_chunks - 1:
                _, blk0 = group_of(s)
                vt = jnp.concatenate([vt_ref[0, blk0 + c * nb + i, hh * HEAD_DIM:(hh + 1) * HEAD_DIM, :]
                                      for i in range(nb)], axis=1)
            else:
                vt = vct_ref[0, hh * HEAD_DIM:(hh + 1) * HEAD_DIM, :].astype(BF16)
            _softmax_step(s_ring[item % ring], cmax_scr[item % ring], vt, m_scr, acc_scr, 2 * s + hh)
        o_ref[0, pl.ds(pl.multiple_of(s * tq, tq), tq), :] = (
            _normalised_pair(acc_scr, 2 * s, 2 * s + 1).astype(o_ref.dtype))
        return carry

    for item in range(ATTN_LOOKAHEAD):
        scores(0, item)
    lax.fori_loop(0, NB_STEP_GROUPS - 1, group, 0)
    group(jnp.int32(NB_STEP_GROUPS - 1), 0, last=True)


def _nbhd_attention(q, k4, vt4, k_ctx, vct, rel_bias_flat, *, name):
    B, S, _ = q.shape
    W = k4.shape[3]
    P = k_ctx.shape[1]
    rows = S // GRID_W
    tq = NB_GROUP * GRID_W
    nk = NB_SLAB * GRID_W
    nblk = S // NB_KV_BLOCK
    tqs = NB_STEP_GROUPS * tq
    nu = 2 * NB_STEP_GROUPS
    ring = ATTN_LOOKAHEAD + 1
    est =(2 * (2 * tqs * LANES * 2 + 2 * S * LANES * 2 + 2 * P * LANES * 4)
           + 2 * 3 * tq * nk * 4 + ring * tq * tq * 4 + 8 * tq * tq * 4 + nu * tq * 1024)
    return pl.pallas_call(
        functools.partial(_nbhd_kernel, rows=rows),
        grid=(W // LANES, B, rows // (NB_GROUP * NB_STEP_GROUPS)),
        in_specs=[pl.BlockSpec(memory_space=pltpu.SMEM),
                  pl.BlockSpec((1, tqs, 2 * LANES), lambda p, b, g: (b, g, p)),
                  pl.BlockSpec((1, nblk, NB_KV_BLOCK, LANES), lambda p, b, g: (b, 0, 0, p)),
                  pl.BlockSpec((1, nblk, LANES, NB_KV_BLOCK), lambda p, b, g: (b, 0, p, 0)),
                  pl.BlockSpec((1, P, LANES), lambda p, b, g: (b, 0, p)),
                  pl.BlockSpec((1, LANES, P), lambda p, b, g: (b, p, 0))],
        out_specs=pl.BlockSpec((1, tqs, LANES), lambda p, b, g: (b, g, p)),
        out_shape=jax.ShapeDtypeStruct((B, S, W), BF16),
        scratch_shapes=[pltpu.VMEM((2, 3, nk, tq), F32),
                        pltpu.VMEM((nu, 1, tq), F32),
                        pltpu.VMEM((nu, HEAD_DIM + SUM_ROWS, tq), F32),
                        pltpu.VMEM((ring, tq, tq), F32),
                        pltpu.VMEM((ring, 1, tq), F32)],
        compiler_params=pltpu.CompilerParams(dimension_semantics=("arbitrary",) * 3,
                                             vmem_limit_bytes=_vmem_limit(est)),
        name=name,
    )(rel_bias_flat, q, k4, vt4, k_ctx, vct)


def _split3(x):
    hi = x.astype(BF16)
    r = x - hi.astype(F32)
    mid = r.astype(BF16)
    return hi, mid, (r - mid.astype(F32)).astype(BF16)


def _mlstm_kernel(*refs, nc, grp_a, grp, has_state):
    q_ref, k_ref, v_ref, g_ref, ob_ref = refs[:5]
    s0_ref, m0_ref = refs[5:7] if has_state else (None, None)
    (on_ref, out_ref, sf_ref, mf_ref,
     h_scr, nat_scr, rows_scr, stat_scr, mprev_scr, un_scr, st_scr) = refs[7 if has_state else 5:]
    d = pl.program_id(1)
    L = L_CHUNK
    bb = q_ref.shape[0]
    NP = H_MLSTM // 2
    row = lax.broadcasted_iota(jnp.int32, (L, L), 0)
    col = lax.broadcasted_iota(jnp.int32, (L, L), 1)
    sign = 1 - 2 * d
    mask = (col - row) * sign <= 0
    maskb = mask.astype(BF16)
    mask3 = jnp.concatenate([maskb, maskb, maskb], axis=1)
    lane = lax.broadcasted_iota(jnp.int32, (L, LANES), 1)
    lo = lane < HEAD_DIM
    top = row < HEAD_DIM
    row2 = lax.broadcasted_iota(jnp.int32, (L, 2 * LANES), 0)
    col2 = lax.broadcasted_iota(jnp.int32, (L, 2 * LANES), 1)
    keep_state = (row2 < HEAD_DIM) == ((col2 % LANES) < HEAD_DIM)
    top2 = row2 < HEAD_DIM
    ones_b = jnp.ones((L, LANES), BF16)
    ones_lo = lo.astype(BF16)
    ones_hi = jnp.logical_not(lo).astype(BF16)

    def chunk_rows(c):
        return pl.ds(pl.multiple_of(c * L, L), L)

    def tokens(c):
        if bb == 1:
            return 0, chunk_rows(c)
        return c // nc, pl.ds(pl.multiple_of((c % nc) * L, L), L)

    def pass_a(it, carry):
        cs = [it * grp_a + u for u in range(grp_a)]
        gts = [g_ref[tokens(c)[0], tokens(c)[1], :] for c in cs]
        bns = [_dot(mask3, jnp.concatenate(_split3(gt), axis=0)) for gt in gts]
        a_all = []
        for c, gt, bn in zip(cs, gts, bns):
            nat = jnp.where(lane < H_MLSTM, gt, bn)
            nat_scr[chunk_rows(c), :] = nat * (-LOG2E)
            nat_t = nat.T
            b_rows = nat_t[H_MLSTM:2 * H_MLSTM]
            c_rows = nat_t[0:H_MLSTM] - b_rows
            rows_scr[c] = jnp.concatenate([c_rows * LOG2E, b_rows], axis=0)
            c_max = jnp.max(c_rows, axis=1, keepdims=True)
            b_tot = jnp.where(d == 0, b_rows[:, L - 1:L], b_rows[:, 0:1])
            stat_scr[c] = jnp.concatenate([jnp.broadcast_to(c_max, (H_MLSTM, LANES)),
                                           jnp.broadcast_to(b_tot, (H_MLSTM, LANES))], axis=0)
            a_all.append(jnp.exp(c_rows - c_max))
        for c, a_rows in zip(cs, a_all):
            sq, rows = tokens(c)
            for p in range(NP):
                lanes = slice(p * LANES, (p + 1) * LANES)
                k_t = k_ref[sq, rows, lanes].astype(F32).T
                a_sel = jnp.where(top, a_rows[2 * p:2 * p + 1], a_rows[2 * p + 1:2 * p + 2])
                vv = jnp.concatenate([v_ref[sq, rows, lanes], ones_b], axis=1)
                un = _dot((k_t * a_sel).astype(BF16), vv)
                un_scr[c, p] = jnp.where(keep_state, un, 0.0)
        return carry

    lax.fori_loop(0, bb * nc // grp_a, pass_a, 0)

    def pass_b(ci, m, sq):
        c = sq * nc + jnp.where(d == 0, ci, nc - 1 - ci)
        st = stat_scr[c]
        c_max, b_tot = st[0:H_MLSTM], st[H_MLSTM:]
        m_new = jnp.maximum(b_tot + m, b_tot + c_max)
        d_old = jnp.exp(b_tot + m - m_new)
        d_new = jnp.exp(b_tot + c_max - m_new)
        mprev_scr[c] = jnp.concatenate([m, m], axis=0) * LOG2E
        for p in range(NP):
            def rows_of(t, p=p):
                even = jnp.concatenate([t[2 * p:2 * p + 1]] * 2, axis=1)
                odd = jnp.concatenate([t[2 * p + 1:2 * p + 2]] * 2, axis=1)
                return jnp.where(top2, even, odd)
            s_prev = st_scr[p]
            st_scr[p] = rows_of(d_old) * s_prev + rows_of(d_new) * un_scr[c, p]
            un_scr[c, p] = s_prev
        return m_new

    def recurrence(sq, carry):
        if has_state:
            st_scr[...] = s0_ref[sq, 0]
            m_start = m0_ref[sq, 0][0:H_MLSTM]
        else:
            st_scr[...] = jnp.zeros(st_scr.shape, F32)
            m_start = jnp.zeros((H_MLSTM, LANES), F32)
        m_fin = lax.fori_loop(0, nc, functools.partial(pass_b, sq=sq), m_start)
        sf_ref[sq, 0] = st_scr[...]
        mf_ref[sq, 0] = jnp.concatenate([m_fin, m_fin], axis=0)
        return carry

    lax.fori_loop(0, bb, recurrence, 0)

    def pass_c(it, carry):
        cs = [it * grp + u for u in range(grp)]
        units = [(u, p) for u in range(grp) for p in range(NP)]
        early = {}
        for u, p in units:
            c = cs[u]
            sq, rows = tokens(c)
            lanes = slice(p * LANES, (p + 1) * LANES)
            qp = q_ref[sq, rows, lanes]
            kp = k_ref[sq, rows, lanes]
            s_in = un_scr[c, p].astype(BF16)
            qms = [jnp.where(lo if hh == 0 else jnp.logical_not(lo), qp, jnp.zeros_like(qp)) for hh in range(2)]
            early[u, p] = ([_dot_nt(qm, kp) for qm in qms],
                           _dot(qp, s_in))
        mid = {}
        for u, p in units:
            r_t = rows_scr[cs[u]]
            m_in = mprev_scr[cs[u]]
            for hh in range(2):
                h = 2 * p + hh
                cm = jnp.where(mask, r_t[h:h + 1, :], -jnp.inf)
                m_prev = m_in[h:h + 1, :]
                mu = jnp.maximum(jnp.broadcast_to(jnp.max(cm, axis=1, keepdims=True), (L, LANES)), m_prev)
                w = early[u, p][0][hh] * jnp.exp2(cm - mu)
                mid[u, p, hh] = (w.astype(BF16), mu, m_prev)
        for u, p in units:
            sq, rows = tokens(cs[u])
            lanes = slice(p * LANES, (p + 1) * LANES)
            vp = v_ref[sq, rows, lanes]
            zero = jnp.zeros_like(vp)
            vv = jnp.concatenate([jnp.concatenate([jnp.where(lo, vp, zero), ones_lo], axis=1),
                                  jnp.concatenate([jnp.where(lo, zero, vp), ones_hi], axis=1)], axis=0)
            w2 = jnp.concatenate([mid[u, p, 0][0], mid[u, p, 1][0]], axis=1)
            nd = _dot(w2, vv)
            nat = nat_scr[chunk_rows(cs[u]), :]
            nb = [jnp.broadcast_to(nat[:, H_MLSTM + 2 * p + hh:H_MLSTM + 2 * p + hh + 1], (L, LANES))
                  for hh in range(2)]
            fs = early[u, p][1]
            mu = jnp.where(lo, mid[u, p, 0][1], mid[u, p, 1][1])
            m_prev = jnp.where(lo[0:1], mid[u, p, 0][2], mid[u, p, 1][2])
            w_inter = jnp.exp2(m_prev - mu)
            den = nd[:, LANES:] + w_inter * fs[:, LANES:]
            den = jnp.maximum(jnp.abs(den), jnp.exp2(jnp.where(lo, nb[0], nb[1]) - mu))
            h_scr[d, chunk_rows(cs[u]), lanes] = (nd[:, :LANES] + w_inter * fs[:, :LANES]) / den
        return carry

    lax.fori_loop(0, bb * nc // grp, pass_c, 0)

    @pl.when(d == 1)
    def _finish():
        def rows_block(c, carry):
            sq, rows = tokens(c)
            flat = chunk_rows(c)
            for p in range(NP):
                lanes = slice(p * LANES, (p + 1) * LANES)
                hn = _pair_rms(h_scr[0, flat, lanes] + h_scr[1, flat, lanes], on_ref[:, lanes])
                out_ref[sq, rows, lanes] = (_sigmoid(ob_ref[sq, rows, lanes]) * hn).astype(out_ref.dtype)
            return carry

        lax.fori_loop(0, bb * nc, rows_block, 0, unroll=4)


def _mlstm(q, k, v, gates, ob, state, out_norm, *, name):
    assert L_CHUNK == LANES
    B, S, W = q.shape
    nc = S // L_CHUNK
    bb = max(1, min(B, MLSTM_STEP_CHUNKS // nc))
    ncb, sb = bb * nc, bb * S
    grp = min(MLSTM_GROUP, ncb)
    grp_a = min(MLSTM_GROUP_A, ncb)
    npair = H_MLSTM // 2
    seq = lambda b, d: (b, 0, 0)
    est = (2 * (3 * sb * W * 2 + sb * LANES * 4 + sb * W * 4 + sb * W * 2) + 2 * sb * W * 4 + sb * LANES * 4
           + ncb * npair * LANES * 2 * LANES * 4 + 12 * 1024 * 1024)
    state_specs = [pl.BlockSpec((bb, 1, npair, LANES, 2 * LANES), lambda b, d: (b, d, 0, 0, 0)),
                   pl.BlockSpec((bb, 1, 8, LANES), lambda b, d: (b, d, 0, 0))]
    has_state = state is not None
    return pl.pallas_call(
        functools.partial(_mlstm_kernel, nc=nc, grp_a=grp_a, grp=grp, has_state=has_state),
        grid=(B // bb, 2),
        in_specs=[pl.BlockSpec((bb, S, W), seq), pl.BlockSpec((bb, S, W), seq), pl.BlockSpec((bb, S, W), seq),
                  pl.BlockSpec((bb, S, LANES), lambda b, d: (b, 0, d)),
                  pl.BlockSpec((bb, S, W), seq)] + (state_specs if has_state else [])
                 + [pl.BlockSpec((1, W), lambda b, d: (0, 0))],
        out_specs=[pl.BlockSpec((bb, S, W), seq),
                   pl.BlockSpec((bb, 1, npair, LANES, 2 * LANES), lambda b, d: (b, d, 0, 0, 0)),
                   pl.BlockSpec((bb, 1, 8, LANES), lambda b, d: (b, d, 0, 0))],
        out_shape=[jax.ShapeDtypeStruct((B, S, W), BF16),
                   jax.ShapeDtypeStruct((B, 2, npair, LANES, 2 * LANES), F32),
                   jax.ShapeDtypeStruct((B, 2, 8, LANES), F32)],
        scratch_shapes=[pltpu.VMEM((2, sb, W), F32),
                        pltpu.VMEM((sb, LANES), F32),
                        pltpu.VMEM((ncb, 8, L_CHUNK), F32),
                        pltpu.VMEM((ncb, 8, LANES), F32),
                        pltpu.VMEM((ncb, 8, LANES), F32),
                        pltpu.VMEM((ncb, npair, LANES, 2 * LANES), F32),
                        pltpu.VMEM((npair, LANES, 2 * LANES), F32)],
        compiler_params=pltpu.CompilerParams(dimension_semantics=("arbitrary", "arbitrary"),
                                             vmem_limit_bytes=_vmem_limit(est)),
        name=name,
    )(q, k, v, gates, ob, *(state if has_state else ()), out_norm)


def _outproj_kernel(a_ref, b_ref, c_ref, x_ref, mod_ref, w_ref, gpost_ref, gpre_ref, x1_ref, h2_ref):
    mod = mod_ref[0]
    gt1 = mod[:, 2 * D_MODEL:3 * D_MODEL]
    sh2 = mod[:, 3 * D_MODEL:4 * D_MODEL]
    sc2 = mod[:, 4 * D_MODEL:5 * D_MODEL]
    tr = x_ref.shape[0] // ROW_SPLIT
    pieces = [pl.ds(s * tr, tr) for s in range(ROW_SPLIT)]
    mos = [_dot(jnp.concatenate([a_ref[r, :], b_ref[r, :], c_ref[r, :]], axis=1), w_ref[...]) for r in pieces]
    for r, mo in zip(pieces, mos):
        x1 = x_ref[r, :] + gt1 * _rms(mo, gpost_ref[...])
        x1_ref[r, :] = x1
        h2_ref[r, :] = (_rms(x1, gpre_ref[...]) * (1.0 + sc2) + sh2).astype(h2_ref.dtype)


def _outproj(oa, ob, oc, x, mods, w_out, g_post, g_pre, *, rows_per_cond, name):
    T = x.shape[0]
    tm = TM_PROJ
    bpc = rows_per_cond // tm
    row = lambda i: (i, 0)
    const = lambda i: (0, 0)
    est = 2 * (tm * D_MODEL * (2 + 4 + 4 + 2) + D_MODEL * D_MODEL * 2) + 4 * tm * D_MODEL * 4
    return pl.pallas_call(
        _outproj_kernel,
        grid=(T // tm,),
        in_specs=[pl.BlockSpec((tm, W_ATTN), row), pl.BlockSpec((tm, W_MLSTM), row),
                  pl.BlockSpec((tm, W_NBHD), row), pl.BlockSpec((tm, D_MODEL), row),
                  pl.BlockSpec((1, 1, N_MOD * D_MODEL), lambda i: (i // bpc, 0, 0)),
                  pl.BlockSpec((D_MODEL, D_MODEL), const),
                  pl.BlockSpec((1, D_MODEL), const), pl.BlockSpec((1, D_MODEL), const)],
        out_specs=[pl.BlockSpec((tm, D_MODEL), row), pl.BlockSpec((tm, D_MODEL), row)],
        out_shape=[jax.ShapeDtypeStruct((T, D_MODEL), F32), jax.ShapeDtypeStruct((T, D_MODEL), BF16)],
        compiler_params=pltpu.CompilerParams(dimension_semantics=("arbitrary",),
                                             vmem_limit_bytes=_vmem_limit(est)),
        name=name,
    )(oa, ob, oc, x, mods, w_out, g_post, g_pre)


def _ffn_kernel(h_ref, x_ref, mod_ref, wu_ref, wd_ref, g_ref, o_ref, acc_ref):
    j = pl.program_id(1)

    @pl.when(j == 0)
    def _zero():
        acc_ref[...] = jnp.zeros_like(acc_ref)

    u = jnp.maximum(_dot(h_ref[...], wu_ref[...]), 0.0)
    acc_ref[...] += _dot((u * u).astype(BF16), wd_ref[...])

    @pl.when(j == pl.num_programs(1) - 1)
    def _finish():
        gt2 = mod_ref[0][:, 5 * D_MODEL:6 * D_MODEL]
        o_ref[...] = x_ref[...] + gt2 * _rms(acc_ref[...], g_ref[...])


def _ffn(h2, x1, mods, w_up, w_down, g_post, *, rows_per_cond, name):
    T = x1.shape[0]
    tm, tf = TM_FFN, TF_FFN
    tm = min(tm, rows_per_cond)
    bpc = rows_per_cond // tm
    est = 2 * (tm * D_MODEL * (2 + 4 + 4) + 2 * D_MODEL * tf * 2) + tm * D_MODEL * 4 + 3 * tm * tf * 4
    return pl.pallas_call(
        _ffn_kernel,
        grid=(T // tm, D_FF // tf),
        in_specs=[pl.BlockSpec((tm, D_MODEL), lambda i, j: (i, 0)),
                  pl.BlockSpec((tm, D_MODEL), lambda i, j: (i, 0)),
                  pl.BlockSpec((1, 1, N_MOD * D_MODEL), lambda i, j: (i // bpc, 0, 0)),
                  pl.BlockSpec((D_MODEL, tf), lambda i, j: (0, j)),
                  pl.BlockSpec((tf, D_MODEL), lambda i, j: (j, 0)),
                  pl.BlockSpec((1, D_MODEL), lambda i, j: (0, 0))],
        out_specs=pl.BlockSpec((tm, D_MODEL), lambda i, j: (i, 0)),
        out_shape=jax.ShapeDtypeStruct((T, D_MODEL), F32),
        scratch_shapes=[pltpu.VMEM((tm, D_MODEL), F32)],
        compiler_params=pltpu.CompilerParams(dimension_semantics=("arbitrary", "arbitrary"),
                                             vmem_limit_bytes=_vmem_limit(est)),
        name=name,
    )(h2, x1, mods, w_up, w_down, g_post)


def _pad_w_in(w_in_l):
    o = W_ATTN + 2 * W_KV + 4 * W_MLSTM
    pre, gates, post = w_in_l[:, :o], w_in_l[:, o:o + N_GATES], w_in_l[:, o + N_GATES:]
    z = jnp.zeros((D_MODEL, LANES - 2 * H_MLSTM), w_in_l.dtype)
    return jnp.concatenate([pre, gates[:, :2 * H_MLSTM], z, gates[:, 2 * H_MLSTM:], z, post],
                           axis=1).astype(BF16)


def _pad_gate_bias(gb_l):
    z = jnp.zeros((LANES - 2 * H_MLSTM,), gb_l.dtype)
    return jnp.concatenate([gb_l[:2 * H_MLSTM], z, gb_l[2 * H_MLSTM:], z]).reshape(1, 2 * LANES)


def _rope_tables(S):
    quarter = HEAD_DIM // 4
    pos = np.arange(S)
    inv_freq = np.float32(ROPE_THETA) ** (-np.arange(quarter, dtype=np.float32) / np.float32(quarter))

    def tabs(p):
        ang = p.astype(np.float32)[:, None] * inv_freq[None, :]
        return np.cos(ang), np.sin(ang)

    cr, sr = tabs(pos // GRID_W)
    cc, sc = tabs(pos % GRID_W)
    cos = np.concatenate([cr, cr, cc, cc], axis=1)
    sin = np.concatenate([-sr, sr, -sc, sc], axis=1)
    return jnp.asarray(np.tile(cos, (1, 2)), F32), jnp.asarray(np.tile(sin, (1, 2)), F32)


def _pack_state(C, n, m):
    B = C.shape[0]
    Cp = C.reshape(B, 2, 2, 2, HEAD_DIM, HEAD_DIM)
    z = jnp.zeros_like(Cp[:, :, :, 0])
    top = jnp.concatenate([Cp[:, :, :, 0], z], axis=-1)
    bot = jnp.concatenate([z, Cp[:, :, :, 1]], axis=-1)
    Cbd = jnp.concatenate([top, bot], axis=-2)
    n_rep = jnp.broadcast_to(n.reshape(B, 2, 2, LANES, 1), (B, 2, 2, LANES, LANES))
    same_head = (jnp.arange(LANES)[:, None] < HEAD_DIM) == (jnp.arange(LANES)[None, :] < HEAD_DIM)
    n_rep = jnp.where(same_head, n_rep, 0.0)
    m_rows = jnp.broadcast_to(m[..., None], m.shape + (LANES,))
    return jnp.concatenate([Cbd, n_rep], axis=-1), jnp.concatenate([m_rows, m_rows], axis=-2)


def _unpack_state(s_p, m_p):
    B = s_p.shape[0]
    c_even = s_p[:, :, :, :HEAD_DIM, :HEAD_DIM]
    c_odd = s_p[:, :, :, HEAD_DIM:, HEAD_DIM:LANES]
    C = jnp.stack([c_even, c_odd], axis=3).reshape(B, 2, H_MLSTM, HEAD_DIM, HEAD_DIM)
    n = jnp.concatenate([s_p[..., :HEAD_DIM, LANES], s_p[..., HEAD_DIM:, LANES + HEAD_DIM]], axis=-1)
    return C, n.reshape(B, 2, H_MLSTM, HEAD_DIM), m_p[:, :, :H_MLSTM, 0]


def _layer(x, mods, lw, *, B, S, cond_rows, rope_tabs, ctx_cache, state, name):
    T = B * S
    kv_dtype = BF16 if ctx_cache is not None else F32
    ck_a = min(CK_ATTN, S)
    pr = _inproj(x, mods, lw["g_pre_mix"], lw["w_in"], lw["q_norm"], lw["k_norm"], lw["gate_bias"],
                 rope_tabs, rows_per_cond=cond_rows, kv_dtype=kv_dtype,
                 vt_blocks=(ck_a, NB_KV_BLOCK), name=name + "_inproj")
    seq = lambda a: a.reshape(B, S, a.shape[-1])
    qa, ka, va = seq(pr["qa"]), seq(pr["ka"]), seq(pr["va"])
    qc, kc, vc = seq(pr["qc"]), seq(pr["kc"]), seq(pr["vc"])
    chunks = lambda a, n: a.reshape(B, S // n, n, a.shape[-1])
    vat = pr["vat"].reshape(B, S // ck_a, W_KV, ck_a)
    vct = pr["vct"].reshape(B, S // NB_KV_BLOCK, W_NBHD, NB_KV_BLOCK)
    if ctx_cache is None:
        out_a = _attention(qa, [(chunks(ka, ck_a), vat)], heads=_HEADS_GQA, name=name + "_attn_a")
        out_c = _attention(qc, [(chunks(kc, NB_KV_BLOCK), vct)], heads=_HEADS_MHA, name=name + "_attn_c")
    else:
        ck_c, cv_c = ctx_cache[2], ctx_cache[3]
        P = ck_c.shape[1]
        assert P == ck_a
        cva_t, cvc_t = _cached_values_t(ctx_cache[1], cv_c, name=name + "_cache_t")
        out_a = _attention(qa, [(chunks(ka, ck_a), vat),
                                (ctx_cache[0].reshape(B, 1, P, W_KV), cva_t.reshape(B, 1, W_KV, P))],
                           heads=_HEADS_GQA, name=name + "_attn_a")
        out_c = _nbhd_attention(qc, chunks(kc, NB_KV_BLOCK), vct, ck_c, cvc_t,
                                lw["rel_bias"], name=name + "_attn_c")
    out_b, sf, mf = _mlstm(seq(pr["qb"]), seq(pr["kb"]), seq(pr["vb"]), seq(pr["g"]), seq(pr["ob"]),
                           state, lw["out_norm"], name=name + "_mlstm")
    x1, h2 = _outproj(out_a.reshape(T, W_ATTN), out_b.reshape(T, W_MLSTM), out_c.reshape(T, W_NBHD),
                      x, mods, lw["w_out"], lw["g_post_mix"], lw["g_pre_ffn"],
                      rows_per_cond=cond_rows, name=name + "_outproj")
    x2 = _ffn(h2, x1, mods, lw["w_up"], lw["w_down"], lw["g_post_ffn"],
              rows_per_cond=cond_rows, name=name + "_ffn")
    return x2, (ka, va, kc, vc, sf, mf)


def kernel(x_prompt, x_sample, c, cache_k_attn, cache_v_attn, cache_k_nbhd, cache_v_nbhd, state_mlstm_C, state_mlstm_n, state_mlstm_m, c_ctx, w_ada, b_ada, g_pre_mix, g_post_mix, g_pre_ffn, g_post_ffn, w_in, q_norm_attn, k_norm_attn, mlstm_gate_bias, mlstm_out_norm, nbhd_rel_bias, w_out, w_ffn_up, w_ffn_down):
    Bc, Sc, _ = x_prompt.shape
    Bl, Sl, _ = x_sample.shape
    P = cache_k_attn.shape[2]
    n_cond = 8
    cond = jnp.concatenate([c_ctx[None, :], c, jnp.zeros((n_cond - 1 - Bl, D_MODEL), F32)], axis=0)
    mods_all = _modulation(cond, w_ada, b_ada)

    layers = []
    for l in range(DEPTH):
        layers.append(dict(
            w_in=_pad_w_in(w_in[l]),
            w_out=w_out[l].astype(BF16),
            w_up=w_ffn_up[l].astype(BF16),
            w_down=w_ffn_down[l].astype(BF16),
            g_pre_mix=g_pre_mix[l].reshape(1, D_MODEL), g_post_mix=g_post_mix[l].reshape(1, D_MODEL),
            g_pre_ffn=g_pre_ffn[l].reshape(1, D_MODEL), g_post_ffn=g_post_ffn[l].reshape(1, D_MODEL),
            q_norm=jnp.tile(q_norm_attn[l], 2).reshape(1, LANES),
            k_norm=jnp.tile(k_norm_attn[l], 2).reshape(1, LANES),
            gate_bias=_pad_gate_bias(mlstm_gate_bias[l]),
            out_norm=mlstm_out_norm[l].reshape(1, W_MLSTM),
            rel_bias=nbhd_rel_bias[l].reshape(-1),
        ))

    xp = x_prompt.reshape(Bc * Sc, D_MODEL)
    ctx = []
    for l in range(DEPTH):
        mods = mods_all[l, 0:1].reshape(1, 1, N_MOD * D_MODEL)
        xp, extras = _layer(xp, mods, layers[l], B=Bc, S=Sc, cond_rows=Bc * Sc, rope_tabs=None,
                            ctx_cache=None, state=None, name=f"ctx{l}")
        ctx.append(extras)
    stacked = lambda i, heads: jnp.stack([e[i] for e in ctx], axis=1).reshape(Bc, DEPTH, Sc, heads, HEAD_DIM)
    new_k_attn, new_v_attn = stacked(0, KV_ATTN), stacked(1, KV_ATTN)
    new_k_nbhd, new_v_nbhd = stacked(2, H_NBHD), stacked(3, H_NBHD)
    states = [_unpack_state(e[4], e[5]) for e in ctx]
    new_C = jnp.stack([s[0] for s in states], axis=1)
    new_n = jnp.stack([s[1] for s in states], axis=1)
    new_m = jnp.stack([s[2] for s in states], axis=1)

    xs = x_sample.reshape(Bl * Sl, D_MODEL)
    rope_tabs = _rope_tables(Sl)
    for l in range(DEPTH):
        mods = mods_all[l, 1:1 + Bl].reshape(Bl, 1, N_MOD * D_MODEL)
        cache = (cache_k_attn[:, l].reshape(Bl, P, W_KV), cache_v_attn[:, l].reshape(Bl, P, W_KV),
                 cache_k_nbhd[:, l].reshape(Bl, P, W_NBHD), cache_v_nbhd[:, l].reshape(Bl, P, W_NBHD))
        state = _pack_state(state_mlstm_C[:, l], state_mlstm_n[:, l], state_mlstm_m[:, l])
        xs, _ = _layer(xs, mods, layers[l], B=Bl, S=Sl, cond_rows=Sl, rope_tabs=rope_tabs,
                       ctx_cache=cache, state=state, name=f"lat{l}")

    return (xp.reshape(Bc, Sc, D_MODEL), xs.reshape(Bl, Sl, D_MODEL),
            new_k_attn, new_v_attn, new_k_nbhd, new_v_nbhd, new_C, new_n, new_m)
```

```python
import functools

import jax
import jax.numpy as jnp
import numpy as np
from jax import lax
from jax.experimental import pallas as pl
from jax.experimental.pallas import tpu as pltpu

F32 = jnp.float32
BF16 = jnp.bfloat16

D_MODEL = 1024
DEPTH = 2
GRID_W = 64
HEAD_DIM = 64
H_ATTN = 6
KV_ATTN = 2
H_MLSTM = 4
H_NBHD = 6
D_FF = 4 * D_MODEL
NA_ROWS = 8
NA_COLS = 16
ROPE_THETA = 10000.0
EPS = 1e-6
N_MOD = 6
W_ATTN = H_ATTN * HEAD_DIM
W_KV = KV_ATTN * HEAD_DIM
W_MLSTM = H_MLSTM * HEAD_DIM
W_NBHD = H_NBHD * HEAD_DIM
N_GATES = 4 * H_MLSTM

LANES = 128
V7X_VMEM_BYTES = 64 * 1024 * 1024
VMEM_CAP_BYTES = 56 * 1024 * 1024

TM_PROJ = 512
PROJ_CHUNK = 512
TM_FFN = 1024
TF_FFN = 2048
ROW_SPLIT = 4
TQ_ATTN = 512
CK_ATTN = 512
ATTN_LOOKAHEAD = 2
SEQS_PER_STEP = 16
L_CHUNK = 128
MLSTM_GROUP = 2
MLSTM_GROUP_A = 16
MLSTM_STEP_CHUNKS = 32
NB_GROUP = 8
NB_SLAB = 16
NB_STEP_GROUPS = 8
NB_KV_BLOCK = 256
NEG = -1e30
LOG2E = 1.4426950408889634

_COLS = {}
_off = 0
for _name, _w in (("qa", W_ATTN), ("ka", W_KV), ("va", W_KV), ("qb", W_MLSTM), ("kb", W_MLSTM),
                  ("vb", W_MLSTM), ("ob", W_MLSTM), ("gf", LANES), ("gb", LANES),
                  ("qc", W_NBHD), ("kc", W_NBHD), ("vc", W_NBHD)):
    _COLS[_name] = (_off, _off + _w)
    _off += _w
IN_PAD = _off


def _vmem_limit(nbytes):
    return int(min(max(nbytes, 16 * 1024 * 1024), VMEM_CAP_BYTES))


def _dot(a, b):
    return jnp.dot(a, b, preferred_element_type=F32)


def _dot_nt(a, b):
    return lax.dot_general(a, b, (((1,), (1,)), ((), ())), preferred_element_type=F32)


def _lane_lo(shape):
    return (lax.broadcasted_iota(jnp.int32, shape, len(shape) - 1) % LANES) < HEAD_DIM


def _rms(x, g):
    ms = jnp.mean(x * x, axis=-1, keepdims=True)
    return (x * lax.rsqrt(ms + EPS)) * g


def _pair_rms(x, g):
    lo = _lane_lo(x.shape)
    x2 = x * x
    s_lo = jnp.sum(jnp.where(lo, x2, 0.0), axis=-1, keepdims=True)
    s_hi = jnp.sum(jnp.where(lo, 0.0, x2), axis=-1, keepdims=True)
    r = jnp.where(lo, lax.rsqrt(s_lo / HEAD_DIM + EPS), lax.rsqrt(s_hi / HEAD_DIM + EPS))
    return (x * r) * g


def _sigmoid(x):
    return 1.0 / (1.0 + jnp.exp(-x))


def _mods_kernel(c_ref, w_ref, b_ref, o_ref):
    c = c_ref[...]
    s = (c * _sigmoid(c)).astype(BF16)
    o_ref[0] = _dot(s, w_ref[0].astype(BF16)) + b_ref[0]


def _modulation(cond, w_ada, b_ada):
    n = cond.shape[0]
    tn = D_MODEL
    return pl.pallas_call(
        _mods_kernel,
        grid=(DEPTH, N_MOD * D_MODEL // tn),
        in_specs=[pl.BlockSpec((n, D_MODEL), lambda l, j: (0, 0)),
                  pl.BlockSpec((1, D_MODEL, tn), lambda l, j: (l, 0, j)),
                  pl.BlockSpec((1, 1, tn), lambda l, j: (l, 0, j))],
        out_specs=pl.BlockSpec((1, n, tn), lambda l, j: (l, 0, j)),
        out_shape=jax.ShapeDtypeStruct((DEPTH, n, N_MOD * D_MODEL), F32),
        compiler_params=pltpu.CompilerParams(
            dimension_semantics=("arbitrary", "arbitrary"),
            vmem_limit_bytes=_vmem_limit(4 * D_MODEL * tn * 4)),
        name="modulation",
    )(cond, w_ada, b_ada.reshape(DEPTH, 1, N_MOD * D_MODEL))


def _inproj_kernel(*refs, rope):
    if rope:
        (x_ref, mod_ref, g_ref, w_ref, qn_ref, kn_ref, gbias_ref, cos_ref, sin_ref,
         qa_ref, ka_ref, va_ref, qb_ref, kb_ref, vb_ref, ob_ref, gate_ref,
         qc_ref, kc_ref, vc_ref, vat_ref, vct_ref) = refs
    else:
        (x_ref, mod_ref, g_ref, w_ref, qn_ref, kn_ref, gbias_ref,
         qa_ref, ka_ref, va_ref, qb_ref, kb_ref, vb_ref, ob_ref, gate_ref,
         qc_ref, kc_ref, vc_ref, vat_ref, vct_ref) = refs
    x = x_ref[...]
    mod = mod_ref[0]
    sh1 = mod[:, 0:D_MODEL]
    sc1 = mod[:, D_MODEL:2 * D_MODEL]
    hb = (_rms(x, g_ref[...]) * (1.0 + sc1) + sh1).astype(BF16)

    z = [_dot(hb, w_ref[:, c0:c0 + PROJ_CHUNK]) for c0 in range(0, IN_PAD, PROJ_CHUNK)]

    def proj(name, j=0, w=None):
        lo, hi = _COLS[name]
        lo = lo + j
        hi = hi if w is None else lo + w
        parts = []
        while lo < hi:
            c, o = divmod(lo, PROJ_CHUNK)
            n = min(hi - lo, PROJ_CHUNK - o)
            parts.append(z[c][:, o:o + n])
            lo += n
        return parts[0] if len(parts) == 1 else jnp.concatenate(parts, axis=1)

    scale = HEAD_DIM ** -0.5
    q_scale = scale * LOG2E

    def rotary(t):
        first = (lax.broadcasted_iota(jnp.int32, t.shape, 1) % 32) < 16
        partner = jnp.where(first, pltpu.roll(t, LANES - 16, 1), pltpu.roll(t, 16, 1))
        return t * cos_ref[...] + partner * sin_ref[...]

    def store_queries(t, j, q_ref, kv_half):
        lo = _lane_lo(t.shape)
        for e in range(2):
            h = 2 * j + e
            src = t if e == kv_half[h] else pltpu.roll(t, HEAD_DIM, 1)
            keep = lo if kv_half[h] == 0 else jnp.logical_not(lo)
            q_ref[:, h * LANES:(h + 1) * LANES] = jnp.where(keep, src, 0.0).astype(q_ref.dtype)

    for j in range(W_ATTN // LANES):
        t = _pair_rms(proj("qa", j * LANES, LANES), qn_ref[...])
        if rope:
            t = rotary(t)
        store_queries(t * q_scale, j, qa_ref, [kh for _, kh in _HEADS_GQA])
    t = _pair_rms(proj("ka"), kn_ref[...])
    if rope:
        t = rotary(t)
    ka_ref[...] = t.astype(ka_ref.dtype)

    def store_v(v, v_ref, vt_ref):
        v_ref[...] = v.astype(v_ref.dtype)
        nblk, _, blk = vt_ref.shape
        for u in range(nblk):
            vt_ref[u] = v[u * blk:(u + 1) * blk].T.astype(vt_ref.dtype)

    store_v(proj("va"), va_ref, vat_ref)
    qb_ref[...] = proj("qb").astype(qb_ref.dtype)
    kb_ref[...] = (proj("kb") * scale).astype(kb_ref.dtype)
    vb_ref[...] = proj("vb").astype(vb_ref.dtype)
    ob_ref[...] = proj("ob").astype(ob_ref.dtype)
    for j, name in enumerate(("gf", "gb")):
        gt = proj(name) + gbias_ref[:, j * LANES:(j + 1) * LANES]
        lane = lax.broadcasted_iota(jnp.int32, gt.shape, 1)
        is_f = (lane >= H_MLSTM) & (lane < 2 * H_MLSTM)
        logsig = jnp.minimum(gt, 0.0) - jnp.log1p(jnp.exp(-jnp.abs(gt)))
        gate_ref[:, j * LANES:(j + 1) * LANES] = jnp.where(is_f, logsig, gt)
    for j in range(W_NBHD // LANES):
        store_queries(proj("qc", j * LANES, LANES) * q_scale, j, qc_ref, [kh for _, kh in _HEADS_MHA])
    kc_ref[...] = proj("kc").astype(kc_ref.dtype)
    store_v(proj("vc"), vc_ref, vct_ref)


def _inproj(x, mods, g_pre, w_in_p, qn, kn, gbias, rope_tabs, *, rows_per_cond, kv_dtype, vt_blocks, name):
    T = x.shape[0]
    tm = TM_PROJ
    bpc = rows_per_cond // tm
    rope = rope_tabs is not None
    row = lambda i: (i, 0)
    const = lambda i: (0, 0)
    in_specs = [pl.BlockSpec((tm, D_MODEL), row),
                pl.BlockSpec((1, 1, N_MOD * D_MODEL), lambda i: (i // bpc, 0, 0)),
                pl.BlockSpec((1, D_MODEL), const),
                pl.BlockSpec((D_MODEL, IN_PAD), const),
                pl.BlockSpec((1, LANES), const),
                pl.BlockSpec((1, LANES), const),
                pl.BlockSpec((1, 2 * LANES), const)]
    args = [x, mods, g_pre, w_in_p, qn, kn, gbias]
    if rope:
        nblk = rope_tabs[0].shape[0] // tm
        in_specs += [pl.BlockSpec((tm, LANES), lambda i: (i % nblk, 0))] * 2
        args += list(rope_tabs)
    widths = [("qa", H_ATTN * LANES, BF16), ("ka", W_KV, kv_dtype), ("va", W_KV, kv_dtype),
              ("qb", W_MLSTM, BF16), ("kb", W_MLSTM, BF16), ("vb", W_MLSTM, BF16),
              ("ob", W_MLSTM, F32), ("g", 2 * LANES, F32),
              ("qc", H_NBHD * LANES, BF16), ("kc", W_NBHD, kv_dtype), ("vc", W_NBHD, kv_dtype)]
    out_specs = [pl.BlockSpec((tm, w), row) for _, w, _ in widths]
    out_shape = [jax.ShapeDtypeStruct((T, w), dt) for _, w, dt in widths]
    for w, blk in zip((W_KV, W_NBHD), vt_blocks):
        out_specs.append(pl.BlockSpec((tm // blk, w, blk), lambda i: (i, 0, 0)))
        out_shape.append(jax.ShapeDtypeStruct((T // blk, w, blk), BF16))
    est = 2 * (tm * D_MODEL * 4 + D_MODEL * IN_PAD * 2 + tm * IN_PAD * 4) + 3 * tm * IN_PAD * 4
    outs = pl.pallas_call(
        functools.partial(_inproj_kernel, rope=rope),
        grid=(T // tm,),
        in_specs=in_specs, out_specs=out_specs, out_shape=out_shape,
        compiler_params=pltpu.CompilerParams(dimension_semantics=("arbitrary",),
                                             vmem_limit_bytes=_vmem_limit(est)),
        name=name,
    )(*args)
    return dict(zip([n for n, _, _ in widths] + ["vat", "vct"], outs))


SUM_ROWS = 16


def _softmax_step(st, col_max, vt, m_scr, acc_scr, u):
    m_old = m_scr[u]
    m_new = jnp.maximum(m_old, col_max)
    p = jnp.exp2(st - m_new).astype(BF16)
    alpha = jnp.exp2(m_old - m_new)
    vt1 = jnp.concatenate([vt, jnp.ones((SUM_ROWS, vt.shape[1]), BF16)], axis=0)
    acc_scr[u] = alpha * acc_scr[u] + _dot(vt1, p)
    m_scr[u] = m_new


def _normalised_pair(acc_scr, u0, u1):
    halves = [acc_scr[u][0:HEAD_DIM] / acc_scr[u][HEAD_DIM:HEAD_DIM + 1] for u in (u0, u1)]
    return jnp.concatenate(halves, axis=0).T
def _attn_kernel(*refs, heads, part_chunks):
    n_parts = len(part_chunks)
    assert n_parts in (1, 2)
    q_ref, kv_refs = refs[0], refs[1:1 + 2 * n_parts]
    o_ref, m_scr, acc_scr, s_ring, cmax_scr = refs[1 + 2 * n_parts:]
    nch = sum(part_chunks)

    def chunk_of(which, j, rows, cols):
        first = kv_refs[which][0, jnp.minimum(j, part_chunks[0] - 1), rows, cols].astype(BF16)
        if n_parts == 1:
            return first
        j2 = jnp.clip(j - part_chunks[0], 0, part_chunks[1] - 1)
        return jnp.where(j < part_chunks[0], first, kv_refs[2 + which][0, j2, rows, cols].astype(BF16))

    nh = len(heads)
    ring = ATTN_LOOKAHEAD + 1
    assert nh % ring == 0
    m_scr[...] = jnp.full(m_scr.shape, -jnp.inf, F32)
    acc_scr[...] = jnp.zeros(acc_scr.shape, F32)

    def scores(j, item):
        h = item % nh
        kg = heads[h][0]
        kj = chunk_of(0, j, slice(None), slice(kg * LANES, (kg + 1) * LANES))
        st = _dot_nt(kj, q_ref[0, :, h * LANES:(h + 1) * LANES])
        s_ring[item % ring] = st
        cmax_scr[item % ring] = jnp.max(st, axis=0, keepdims=True)

    def chunk(j, carry, last=False):
        for h, (kg, kh) in enumerate(heads):
            ahead = h + ATTN_LOOKAHEAD
            if ahead < nh:
                scores(j, ahead)
            elif not last:
                scores(j + 1, ahead)
            r = kg * LANES + kh * HEAD_DIM
            vt = chunk_of(1, j, slice(r, r + HEAD_DIM), slice(None))
            _softmax_step(s_ring[h % ring], cmax_scr[h % ring], vt, m_scr, acc_scr, h)
        return carry

    for item in range(ATTN_LOOKAHEAD):
        scores(0, item)
    lax.fori_loop(0, nch - 1, chunk, 0)
    chunk(nch - 1, 0, last=True)
    for t in range(len(heads) // 2):
        o_ref[0, :, t * LANES:(t + 1) * LANES] = _normalised_pair(acc_scr, 2 * t, 2 * t + 1).astype(o_ref.dtype)


def _attn_seqs_kernel(q_ref, k_ref, vt_ref, o_ref, s_ring, cmax_scr, *, heads):
    nh = len(heads)
    n_seq = q_ref.shape[0]
    ring = nh
    look = nh - 1

    def scores(s, item):
        h = item % nh
        kg = heads[h][0]
        st = _dot_nt(k_ref[s, 0, :, kg * LANES:(kg + 1) * LANES].astype(BF16),
                     q_ref[s, :, h * LANES:(h + 1) * LANES])
        s_ring[item % ring] = st
        cmax_scr[item % ring] = jnp.max(st, axis=0, keepdims=True)

    def sequence(s, carry, last=False):
        accs = []
        for h, (kg, kh) in enumerate(heads):
            ahead = h + look
            if ahead < nh:
                scores(s, ahead)
            elif not last:
                scores(s + 1, ahead)
            r = kg * LANES + kh * HEAD_DIM
            vt = vt_ref[s, 0, r:r + HEAD_DIM, :].astype(BF16)
            vt1 = jnp.concatenate([vt, jnp.ones((SUM_ROWS, vt.shape[1]), BF16)], axis=0)
            p = jnp.exp2(s_ring[h % ring] - cmax_scr[h % ring]).astype(BF16)
            accs.append(_dot(vt1, p))
        for t in range(nh // 2):
            halves = [a[0:HEAD_DIM] / a[HEAD_DIM:HEAD_DIM + 1] for a in accs[2 * t:2 * t + 2]]
            o_ref[s, :, t * LANES:(t + 1) * LANES] = jnp.concatenate(halves, axis=0).T.astype(o_ref.dtype)
        return carry

    for item in range(look):
        scores(0, item)
    lax.fori_loop(0, n_seq - 1, sequence, 0)
    sequence(n_seq - 1, 0, last=True)


def _attention_seqs(q, k4, vt4, *, heads, name):
    B, Sq, WQ = q.shape
    ck, KW = k4.shape[2], k4.shape[3]
    nh = len(heads)
    W = nh * HEAD_DIM
    bb = min(SEQS_PER_STEP, B)
    ring = nh
    est = (2 * bb * (Sq * (WQ + W) * 2 + ck * KW * (k4.dtype.itemsize + vt4.dtype.itemsize))
           + ring * ck * Sq * 4 + 8 * nh * ck * Sq * 4)
    return pl.pallas_call(
        functools.partial(_attn_seqs_kernel, heads=heads),
        grid=(B // bb,),
        in_specs=[pl.BlockSpec((bb, Sq, WQ), lambda i: (i, 0, 0)),
                  pl.BlockSpec((bb, 1, ck, KW), lambda i: (i, 0, 0, 0)),
                  pl.BlockSpec((bb, 1, KW, ck), lambda i: (i, 0, 0, 0))],
        out_specs=pl.BlockSpec((bb, Sq, W), lambda i: (i, 0, 0)),
        out_shape=jax.ShapeDtypeStruct((B, Sq, W), BF16),
        scratch_shapes=[pltpu.VMEM((ring, ck, Sq), F32),
                        pltpu.VMEM((ring, 1, Sq), F32)],
        compiler_params=pltpu.CompilerParams(dimension_semantics=("arbitrary",),
                                             vmem_limit_bytes=_vmem_limit(est)),
        name=name,
    )(q, k4, vt4)


def _attention(q, parts, *, heads, name):
    B, Sq, WQ = q.shape
    ck, KW = parts[0][0].shape[2], parts[0][0].shape[3]
    tq = min(TQ_ATTN, Sq)
    nh = len(heads)
    W = nh * HEAD_DIM
    part_chunks = tuple(k4.shape[1] for k4, _ in parts)
    if part_chunks == (1,) and tq == Sq:
        return _attention_seqs(q, parts[0][0], parts[0][1], heads=heads, name=name)
    in_specs = [pl.BlockSpec((1, tq, WQ), lambda b, i: (b, i, 0))]
    args = [q]
    kv_bytes = 0
    for k4, vt4 in parts:
        assert k4.shape[2:] == (ck, KW) and vt4.shape[2:] == (KW, ck)
        in_specs += [pl.BlockSpec((1,) + k4.shape[1:], lambda b, i: (b, 0, 0, 0)),
                     pl.BlockSpec((1,) + vt4.shape[1:], lambda b, i: (b, 0, 0, 0))]
        args += [k4, vt4]
        kv_bytes += k4[0].size * k4.dtype.itemsize + vt4[0].size * vt4.dtype.itemsize
    est = (2 * (tq * (WQ + W) * 2 + kv_bytes) + nh * tq * (HEAD_DIM * 4 + 64) + 8 * nh * ck * tq * 4)
    return pl.pallas_call(
        functools.partial(_attn_kernel, heads=heads, part_chunks=part_chunks),
        grid=(B, Sq // tq),
        in_specs=in_specs,
        out_specs=pl.BlockSpec((1, tq, W), lambda b, i: (b, i, 0)),
        out_shape=jax.ShapeDtypeStruct((B, Sq, W), BF16),
        scratch_shapes=[pltpu.VMEM((nh, 1, tq), F32),
                        pltpu.VMEM((nh, HEAD_DIM + SUM_ROWS, tq), F32),
                        pltpu.VMEM((ATTN_LOOKAHEAD + 1, ck, tq), F32),
                        pltpu.VMEM((ATTN_LOOKAHEAD + 1, 1, tq), F32)],
        compiler_params=pltpu.CompilerParams(dimension_semantics=("arbitrary", "arbitrary"),
                                             vmem_limit_bytes=_vmem_limit(est)),
        name=name,
    )(*args)


def _transpose_kernel(a_ref, b_ref, at_ref, bt_ref):
    at_ref[0] = a_ref[0].T.astype(at_ref.dtype)
    bt_ref[0] = b_ref[0].T.astype(bt_ref.dtype)


def _cached_values_t(a, b, *, name):
    B, P, ca = a.shape
    cb = b.shape[2]
    return pl.pallas_call(
        _transpose_kernel,
        grid=(B,),
        in_specs=[pl.BlockSpec((1, P, ca), lambda i: (i, 0, 0)), pl.BlockSpec((1, P, cb), lambda i: (i, 0, 0))],
        out_specs=[pl.BlockSpec((1, ca, P), lambda i: (i, 0, 0)), pl.BlockSpec((1, cb, P), lambda i: (i, 0, 0))],
        out_shape=[jax.ShapeDtypeStruct((B, ca, P), BF16), jax.ShapeDtypeStruct((B, cb, P), BF16)],
        compiler_params=pltpu.CompilerParams(dimension_semantics=("arbitrary",)),
        name=name,
    )(a, b)


_HEADS_GQA = tuple((0, h // (H_ATTN // KV_ATTN)) for h in range(H_ATTN))
_HEADS_MHA = tuple((h // 2, h % 2) for h in range(H_NBHD))


def _nbhd_window(r, rows):
    kr = min(NA_ROWS, rows)
    return min(max(r - kr // 2, 0), rows - kr), kr


def _nbhd_patterns(rows):
    n_groups = rows // NB_GROUP
    pats = []
    for g in (0, 1, n_groups - 1):
        r0 = g * NB_GROUP
        pats.append((r0, min(max(r0 - NA_ROWS // 2, 0), rows - NB_SLAB)))
    return pats


def _nbhd_kernel(rb_ref, q_ref, k_ref, vt_ref, kc_ref, vct_ref, o_ref,
                 bias_scr, m_scr, acc_scr, s_ring, cmax_scr, *, rows):
    hp = pl.program_id(0)
    b = pl.program_id(1)
    gs = pl.program_id(2)
    n_groups = rows // NB_GROUP
    n_dr = 2 * NA_ROWS - 1
    n_dc = 2 * NA_COLS - 1
    tq = NB_GROUP * GRID_W
    ck = tq
    pats = _nbhd_patterns(rows)
    ring = ATTN_LOOKAHEAD + 1
    n_chunks = NB_SLAB * GRID_W // ck + 1
    assert n_chunks % ring == 0 and kc_ref.shape[1] == ck
    blk_rows = NB_KV_BLOCK // GRID_W

    @pl.when((b == 0) & (gs == 0))
    def _build_bias():
        shape = (GRID_W, LANES)
        w = lax.broadcasted_iota(jnp.int32, shape, 0)
        lane = lax.broadcasted_iota(jnp.int32, shape, 1)
        cc = lane % GRID_W
        second = lane >= GRID_W
        cs = jnp.clip(w - NA_COLS // 2, 0, GRID_W - NA_COLS)
        col_ok = (cc >= cs) & (cc < cs + NA_COLS)
        dc = cc - w + (NA_COLS - 1)
        for hh in range(2):
            base = (2 * hp + hh) * (n_dr * n_dc)
            tiles = {}
            for d in range(-1, n_dr):
                acc = jnp.zeros(shape, F32)
                for j in range(n_dc):
                    va = rb_ref[base + d * n_dc + j] * LOG2E if d >= 0 else 0.0
                    vb = rb_ref[base + (d + 1) * n_dc + j] * LOG2E if d + 1 < n_dr else 0.0
                    acc = acc + jnp.where(dc == j, jnp.where(second, vb, va), 0.0)
                tiles[d] = acc
            def query_row_tile(r, kra):
                rs, kr = _nbhd_window(r, rows)
                ok_a = rs <= kra < rs + kr
                ok_b = rs <= kra + 1 < rs + kr
                if not (ok_a or ok_b):
                    return jnp.full(shape, NEG, F32)
                row_ok = (jnp.logical_not(second) if ok_a and not ok_b else
                          second if ok_b and not ok_a else None)
                ok = col_ok if row_ok is None else (col_ok & row_ok)
                return jnp.where(ok, tiles[kra - r + (NA_ROWS - 1)], NEG)

            for pi, (r0, slab0) in enumerate(pats):
                for ip in range(NB_GROUP // 2):
                    for ap in range(NB_SLAB // 2):
                        kra = slab0 + 2 * ap
                        two_rows = jnp.concatenate([query_row_tile(r0 + 2 * ip, kra),
                                                    query_row_tile(r0 + 2 * ip + 1, kra)], axis=0)
                        bias_scr[hh, pi, ap * LANES:(ap + 1) * LANES, ip * LANES:(ip + 1) * LANES] = two_rows.T

    assert (2 * n_chunks) % ring == 0
    m_scr[...] = jnp.full(m_scr.shape, -jnp.inf, F32)
    acc_scr[...] = jnp.zeros(acc_scr.shape, F32)

    def group_of(s):
        g = gs * NB_STEP_GROUPS + s
        pat = jnp.where(g == 0, 0, jnp.where(g == n_groups - 1, 2, 1))
        slab0 = jnp.clip(g * NB_GROUP - NA_ROWS // 2, 0, rows - NB_SLAB)
        return pat, slab0 // blk_rows

    per_group = 2 * n_chunks
    nb = ck // NB_KV_BLOCK

    def scores(s, item):
        hh, c = (item % per_group) // n_chunks, item % n_chunks
        qm = q_ref[0, pl.ds(pl.multiple_of(s * tq, tq), tq), hh * LANES:(hh + 1) * LANES]
        if c < n_chunks - 1:
            pat, blk0 = group_of(s)
            kc = k_ref[0, pl.ds(blk0 + c * nb, nb)].reshape(ck, LANES)
            st = _dot_nt(kc, qm) + bias_scr[hh, pat, c * ck:(c + 1) * ck, :]
        else:
            st = _dot_nt(kc_ref[0].astype(BF16), qm)
        s_ring[item % ring] = st
        cmax_scr[item % ring] = jnp.max(st, axis=0, keepdims=True)

    def group(s, carry, last=False):
        for item in range(per_group):
            ahead = item + ATTN_LOOKAHEAD
            if ahead < per_group:
                scores(s, ahead)
            elif not last:
                scores(s + 1, ahead)
            hh, c = item // n_chunks, item % n_chunks
            if c < n_chunks - 1:
                _, blk0 = group_of(s)
                vt = jnp.concatenate([vt_ref[0, blk0 + c * nb + i, hh * HEAD_DIM:(hh + 1) * HEAD_DIM, :]
                                      for i in range(nb)], axis=1)
            else:
                vt = vct_ref[0, hh * HEAD_DIM:(hh + 1) * HEAD_DIM, :].astype(BF16)
            _softmax_step(s_ring[item % ring], cmax_scr[item % ring], vt, m_scr, acc_scr, 2 * s + hh)
        o_ref[0, pl.ds(pl.multiple_of(s * tq, tq), tq), :] = (
            _normalised_pair(acc_scr, 2 * s, 2 * s + 1).astype(o_ref.dtype))
        return carry

    for item in range(ATTN_LOOKAHEAD):
        scores(0, item)
    lax.fori_loop(0, NB_STEP_GROUPS - 1, group, 0)
    group(jnp.int32(NB_STEP_GROUPS - 1), 0, last=True)


def _nbhd_attention(q, k4, vt4, k_ctx, vct, rel_bias_flat, *, name):
    B, S, _ = q.shape
    W = k4.shape[3]
    P = k_ctx.shape[1]
    rows = S // GRID_W
    tq = NB_GROUP * GRID_W
    nk = NB_SLAB * GRID_W
    nblk = S // NB_KV_BLOCK
    tqs = NB_STEP_GROUPS * tq
    nu = 2 * NB_STEP_GROUPS
    ring = ATTN_LOOKAHEAD + 1
    est =(2 * (2 * tqs * LANES * 2 + 2 * S * LANES * 2 + 2 * P * LANES * 4)
           + 2 * 3 * tq * nk * 4 + ring * tq * tq * 4 + 8 * tq * tq * 4 + nu * tq * 1024)
    return pl.pallas_call(
        functools.partial(_nbhd_kernel, rows=rows),
        grid=(W // LANES, B, rows // (NB_GROUP * NB_STEP_GROUPS)),
        in_specs=[pl.BlockSpec(memory_space=pltpu.SMEM),
                  pl.BlockSpec((1, tqs, 2 * LANES), lambda p, b, g: (b, g, p)),
                  pl.BlockSpec((1, nblk, NB_KV_BLOCK, LANES), lambda p, b, g: (b, 0, 0, p)),
                  pl.BlockSpec((1, nblk, LANES, NB_KV_BLOCK), lambda p, b, g: (b, 0, p, 0)),
                  pl.BlockSpec((1, P, LANES), lambda p, b, g: (b, 0, p)),
                  pl.BlockSpec((1, LANES, P), lambda p, b, g: (b, p, 0))],
        out_specs=pl.BlockSpec((1, tqs, LANES), lambda p, b, g: (b, g, p)),
        out_shape=jax.ShapeDtypeStruct((B, S, W), BF16),
        scratch_shapes=[pltpu.VMEM((2, 3, nk, tq), F32),
                        pltpu.VMEM((nu, 1, tq), F32),
                        pltpu.VMEM((nu, HEAD_DIM + SUM_ROWS, tq), F32),
                        pltpu.VMEM((ring, tq, tq), F32),
                        pltpu.VMEM((ring, 1, tq), F32)],
        compiler_params=pltpu.CompilerParams(dimension_semantics=("arbitrary",) * 3,
                                             vmem_limit_bytes=_vmem_limit(est)),
        name=name,
    )(rel_bias_flat, q, k4, vt4, k_ctx, vct)


def _split3(x):
    hi = x.astype(BF16)
    r = x - hi.astype(F32)
    mid = r.astype(BF16)
    return hi, mid, (r - mid.astype(F32)).astype(BF16)


def _mlstm_kernel(*refs, nc, grp_a, grp, has_state):
    q_ref, k_ref, v_ref, g_ref, ob_ref = refs[:5]
    s0_ref, m0_ref = refs[5:7] if has_state else (None, None)
    (on_ref, out_ref, sf_ref, mf_ref,
     h_scr, nat_scr, rows_scr, stat_scr, mprev_scr, un_scr, st_scr) = refs[7 if has_state else 5:]
    d = pl.program_id(1)
    L = L_CHUNK
    bb = q_ref.shape[0]
    NP = H_MLSTM // 2
    row = lax.broadcasted_iota(jnp.int32, (L, L), 0)
    col = lax.broadcasted_iota(jnp.int32, (L, L), 1)
    sign = 1 - 2 * d
    mask = (col - row) * sign <= 0
    maskb = mask.astype(BF16)
    mask3 = jnp.concatenate([maskb, maskb, maskb], axis=1)
    lane = lax.broadcasted_iota(jnp.int32, (L, LANES), 1)
    lo = lane < HEAD_DIM
    top = row < HEAD_DIM
    row2 = lax.broadcasted_iota(jnp.int32, (L, 2 * LANES), 0)
    col2 = lax.broadcasted_iota(jnp.int32, (L, 2 * LANES), 1)
    keep_state = (row2 < HEAD_DIM) == ((col2 % LANES) < HEAD_DIM)
    top2 = row2 < HEAD_DIM
    ones_b = jnp.ones((L, LANES), BF16)
    ones_lo = lo.astype(BF16)
    ones_hi = jnp.logical_not(lo).astype(BF16)

    def chunk_rows(c):
        return pl.ds(pl.multiple_of(c * L, L), L)

    def tokens(c):
        if bb == 1:
            return 0, chunk_rows(c)
        return c // nc, pl.ds(pl.multiple_of((c % nc) * L, L), L)

    def pass_a(it, carry):
        cs = [it * grp_a + u for u in range(grp_a)]
        gts = [g_ref[tokens(c)[0], tokens(c)[1], :] for c in cs]
        bns = [_dot(mask3, jnp.concatenate(_split3(gt), axis=0)) for gt in gts]
        a_all = []
        for c, gt, bn in zip(cs, gts, bns):
            nat = jnp.where(lane < H_MLSTM, gt, bn)
            nat_scr[chunk_rows(c), :] = nat * (-LOG2E)
            nat_t = nat.T
            b_rows = nat_t[H_MLSTM:2 * H_MLSTM]
            c_rows = nat_t[0:H_MLSTM] - b_rows
            rows_scr[c] = jnp.concatenate([c_rows * LOG2E, b_rows], axis=0)
            c_max = jnp.max(c_rows, axis=1, keepdims=True)
            b_tot = jnp.where(d == 0, b_rows[:, L - 1:L], b_rows[:, 0:1])
            stat_scr[c] = jnp.concatenate([jnp.broadcast_to(c_max, (H_MLSTM, LANES)),
                                           jnp.broadcast_to(b_tot, (H_MLSTM, LANES))], axis=0)
            a_all.append(jnp.exp(c_rows - c_max))
        for c, a_rows in zip(cs, a_all):
            sq, rows = tokens(c)
            for p in range(NP):
                lanes = slice(p * LANES, (p + 1) * LANES)
                k_t = k_ref[sq, rows, lanes].astype(F32).T
                a_sel = jnp.where(top, a_rows[2 * p:2 * p + 1], a_rows[2 * p + 1:2 * p + 2])
                vv = jnp.concatenate([v_ref[sq, rows, lanes], ones_b], axis=1)
                un = _dot((k_t * a_sel).astype(BF16), vv)
                un_scr[c, p] = jnp.where(keep_state, un, 0.0)
        return carry

    lax.fori_loop(0, bb * nc // grp_a, pass_a, 0)

    def pass_b(ci, m, sq):
        c = sq * nc + jnp.where(d == 0, ci, nc - 1 - ci)
        st = stat_scr[c]
        c_max, b_tot = st[0:H_MLSTM], st[H_MLSTM:]
        m_new = jnp.maximum(b_tot + m, b_tot + c_max)
        d_old = jnp.exp(b_tot + m - m_new)
        d_new = jnp.exp(b_tot + c_max - m_new)
        mprev_scr[c] = jnp.concatenate([m, m], axis=0) * LOG2E
        for p in range(NP):
            def rows_of(t, p=p):
                even = jnp.concatenate([t[2 * p:2 * p + 1]] * 2, axis=1)
                odd = jnp.concatenate([t[2 * p + 1:2 * p + 2]] * 2, axis=1)
                return jnp.where(top2, even, odd)
            s_prev = st_scr[p]
            st_scr[p] = rows_of(d_old) * s_prev + rows_of(d_new) * un_scr[c, p]
            un_scr[c, p] = s_prev
        return m_new

    def recurrence(sq, carry):
        if has_state:
            st_scr[...] = s0_ref[sq, 0]
            m_start = m0_ref[sq, 0][0:H_MLSTM]
        else:
            st_scr[...] = jnp.zeros(st_scr.shape, F32)
            m_start = jnp.zeros((H_MLSTM, LANES), F32)
        m_fin = lax.fori_loop(0, nc, functools.partial(pass_b, sq=sq), m_start)
        sf_ref[sq, 0] = st_scr[...]
        mf_ref[sq, 0] = jnp.concatenate([m_fin, m_fin], axis=0)
        return carry

    lax.fori_loop(0, bb, recurrence, 0)

    def pass_c(it, carry):
        cs = [it * grp + u for u in range(grp)]
        units = [(u, p) for u in range(grp) for p in range(NP)]
        early = {}
        for u, p in units:
            c = cs[u]
            sq, rows = tokens(c)
            lanes = slice(p * LANES, (p + 1) * LANES)
            qp = q_ref[sq, rows, lanes]
            kp = k_ref[sq, rows, lanes]
            s_in = un_scr[c, p].astype(BF16)
            qms = [jnp.where(lo if hh == 0 else jnp.logical_not(lo), qp, jnp.zeros_like(qp)) for hh in range(2)]
            early[u, p] = ([_dot_nt(qm, kp) for qm in qms],
                           _dot(qp, s_in))
        mid = {}
        for u, p in units:
            r_t = rows_scr[cs[u]]
            m_in = mprev_scr[cs[u]]
            for hh in range(2):
                h = 2 * p + hh
                cm = jnp.where(mask, r_t[h:h + 1, :], -jnp.inf)
                m_prev = m_in[h:h + 1, :]
                mu = jnp.maximum(jnp.broadcast_to(jnp.max(cm, axis=1, keepdims=True), (L, LANES)), m_prev)
                w = early[u, p][0][hh] * jnp.exp2(cm - mu)
                mid[u, p, hh] = (w.astype(BF16), mu, m_prev)
        for u, p in units:
            sq, rows = tokens(cs[u])
            lanes = slice(p * LANES, (p + 1) * LANES)
            vp = v_ref[sq, rows, lanes]
            zero = jnp.zeros_like(vp)
            vv = jnp.concatenate([jnp.concatenate([jnp.where(lo, vp, zero), ones_lo], axis=1),
                                  jnp.concatenate([jnp.where(lo, zero, vp), ones_hi], axis=1)], axis=0)
            w2 = jnp.concatenate([mid[u, p, 0][0], mid[u, p, 1][0]], axis=1)
            nd = _dot(w2, vv)
            nat = nat_scr[chunk_rows(cs[u]), :]
            nb = [jnp.broadcast_to(nat[:, H_MLSTM + 2 * p + hh:H_MLSTM + 2 * p + hh + 1], (L, LANES))
                  for hh in range(2)]
            fs = early[u, p][1]
            mu = jnp.where(lo, mid[u, p, 0][1], mid[u, p, 1][1])
            m_prev = jnp.where(lo[0:1], mid[u, p, 0][2], mid[u, p, 1][2])
            w_inter = jnp.exp2(m_prev - mu)
            den = nd[:, LANES:] + w_inter * fs[:, LANES:]
            den = jnp.maximum(jnp.abs(den), jnp.exp2(jnp.where(lo, nb[0], nb[1]) - mu))
            h_scr[d, chunk_rows(cs[u]), lanes] = (nd[:, :LANES] + w_inter * fs[:, :LANES]) / den
        return carry

    lax.fori_loop(0, bb * nc // grp, pass_c, 0)

    @pl.when(d == 1)
    def _finish():
        def rows_block(c, carry):
            sq, rows = tokens(c)
            flat = chunk_rows(c)
            for p in range(NP):
                lanes = slice(p * LANES, (p + 1) * LANES)
                hn = _pair_rms(h_scr[0, flat, lanes] + h_scr[1, flat, lanes], on_ref[:, lanes])
                out_ref[sq, rows, lanes] = (_sigmoid(ob_ref[sq, rows, lanes]) * hn).astype(out_ref.dtype)
            return carry

        lax.fori_loop(0, bb * nc, rows_block, 0, unroll=4)


def _mlstm(q, k, v, gates, ob, state, out_norm, *, name):
    assert L_CHUNK == LANES
    B, S, W = q.shape
    nc = S // L_CHUNK
    bb = max(1, min(B, MLSTM_STEP_CHUNKS // nc))
    ncb, sb = bb * nc, bb * S
    grp = min(MLSTM_GROUP, ncb)
    grp_a = min(MLSTM_GROUP_A, ncb)
    npair = H_MLSTM // 2
    seq = lambda b, d: (b, 0, 0)
    est = (2 * (3 * sb * W * 2 + sb * LANES * 4 + sb * W * 4 + sb * W * 2) + 2 * sb * W * 4 + sb * LANES * 4
           + ncb * npair * LANES * 2 * LANES * 4 + 12 * 1024 * 1024)
    state_specs = [pl.BlockSpec((bb, 1, npair, LANES, 2 * LANES), lambda b, d: (b, d, 0, 0, 0)),
                   pl.BlockSpec((bb, 1, 8, LANES), lambda b, d: (b, d, 0, 0))]
    has_state = state is not None
    return pl.pallas_call(
        functools.partial(_mlstm_kernel, nc=nc, grp_a=grp_a, grp=grp, has_state=has_state),
        grid=(B // bb, 2),
        in_specs=[pl.BlockSpec((bb, S, W), seq), pl.BlockSpec((bb, S, W), seq), pl.BlockSpec((bb, S, W), seq),
                  pl.BlockSpec((bb, S, LANES), lambda b, d: (b, 0, d)),
                  pl.BlockSpec((bb, S, W), seq)] + (state_specs if has_state else [])
                 + [pl.BlockSpec((1, W), lambda b, d: (0, 0))],
        out_specs=[pl.BlockSpec((bb, S, W), seq),
                   pl.BlockSpec((bb, 1, npair, LANES, 2 * LANES), lambda b, d: (b, d, 0, 0, 0)),
                   pl.BlockSpec((bb, 1, 8, LANES), lambda b, d: (b, d, 0, 0))],
        out_shape=[jax.ShapeDtypeStruct((B, S, W), BF16),
                   jax.ShapeDtypeStruct((B, 2, npair, LANES, 2 * LANES), F32),
                   jax.ShapeDtypeStruct((B, 2, 8, LANES), F32)],
        scratch_shapes=[pltpu.VMEM((2, sb, W), F32),
                        pltpu.VMEM((sb, LANES), F32),
                        pltpu.VMEM((ncb, 8, L_CHUNK), F32),
                        pltpu.VMEM((ncb, 8, LANES), F32),
                        pltpu.VMEM((ncb, 8, LANES), F32),
                        pltpu.VMEM((ncb, npair, LANES, 2 * LANES), F32),
                        pltpu.VMEM((npair, LANES, 2 * LANES), F32)],
        compiler_params=pltpu.CompilerParams(dimension_semantics=("arbitrary", "arbitrary"),
                                             vmem_limit_bytes=_vmem_limit(est)),
        name=name,
    )(q, k, v, gates, ob, *(state if has_state else ()), out_norm)


def _outproj_kernel(a_ref, b_ref, c_ref, x_ref, mod_ref, w_ref, gpost_ref, gpre_ref, x1_ref, h2_ref):
    mod = mod_ref[0]
    gt1 = mod[:, 2 * D_MODEL:3 * D_MODEL]
    sh2 = mod[:, 3 * D_MODEL:4 * D_MODEL]
    sc2 = mod[:, 4 * D_MODEL:5 * D_MODEL]
    tr = x_ref.shape[0] // ROW_SPLIT
    pieces = [pl.ds(s * tr, tr) for s in range(ROW_SPLIT)]
    mos = [_dot(jnp.concatenate([a_ref[r, :], b_ref[r, :], c_ref[r, :]], axis=1), w_ref[...]) for r in pieces]
    for r, mo in zip(pieces, mos):
        x1 = x_ref[r, :] + gt1 * _rms(mo, gpost_ref[...])
        x1_ref[r, :] = x1
        h2_ref[r, :] = (_rms(x1, gpre_ref[...]) * (1.0 + sc2) + sh2).astype(h2_ref.dtype)


def _outproj(oa, ob, oc, x, mods, w_out, g_post, g_pre, *, rows_per_cond, name):
    T = x.shape[0]
    tm = TM_PROJ
    bpc = rows_per_cond // tm
    row = lambda i: (i, 0)
    const = lambda i: (0, 0)
    est = 2 * (tm * D_MODEL * (2 + 4 + 4 + 2) + D_MODEL * D_MODEL * 2) + 4 * tm * D_MODEL * 4
    return pl.pallas_call(
        _outproj_kernel,
        grid=(T // tm,),
        in_specs=[pl.BlockSpec((tm, W_ATTN), row), pl.BlockSpec((tm, W_MLSTM), row),
                  pl.BlockSpec((tm, W_NBHD), row), pl.BlockSpec((tm, D_MODEL), row),
                  pl.BlockSpec((1, 1, N_MOD * D_MODEL), lambda i: (i // bpc, 0, 0)),
                  pl.BlockSpec((D_MODEL, D_MODEL), const),
                  pl.BlockSpec((1, D_MODEL), const), pl.BlockSpec((1, D_MODEL), const)],
        out_specs=[pl.BlockSpec((tm, D_MODEL), row), pl.BlockSpec((tm, D_MODEL), row)],
        out_shape=[jax.ShapeDtypeStruct((T, D_MODEL), F32), jax.ShapeDtypeStruct((T, D_MODEL), BF16)],
        compiler_params=pltpu.CompilerParams(dimension_semantics=("arbitrary",),
                                             vmem_limit_bytes=_vmem_limit(est)),
        name=name,
    )(oa, ob, oc, x, mods, w_out, g_post, g_pre)


def _ffn_kernel(h_ref, x_ref, mod_ref, wu_ref, wd_ref, g_ref, o_ref, acc_ref):
    j = pl.program_id(1)

    @pl.when(j == 0)
    def _zero():
        acc_ref[...] = jnp.zeros_like(acc_ref)

    u = jnp.maximum(_dot(h_ref[...], wu_ref[...]), 0.0)
    acc_ref[...] += _dot((u * u).astype(BF16), wd_ref[...])

    @pl.when(j == pl.num_programs(1) - 1)
    def _finish():
        gt2 = mod_ref[0][:, 5 * D_MODEL:6 * D_MODEL]
        o_ref[...] = x_ref[...] + gt2 * _rms(acc_ref[...], g_ref[...])


def _ffn(h2, x1, mods, w_up, w_down, g_post, *, rows_per_cond, name):
    T = x1.shape[0]
    tm, tf = TM_FFN, TF_FFN
    tm = min(tm, rows_per_cond)
    bpc = rows_per_cond // tm
    est = 2 * (tm * D_MODEL * (2 + 4 + 4) + 2 * D_MODEL * tf * 2) + tm * D_MODEL * 4 + 3 * tm * tf * 4
    return pl.pallas_call(
        _ffn_kernel,
        grid=(T // tm, D_FF // tf),
        in_specs=[pl.BlockSpec((tm, D_MODEL), lambda i, j: (i, 0)),
                  pl.BlockSpec((tm, D_MODEL), lambda i, j: (i, 0)),
                  pl.BlockSpec((1, 1, N_MOD * D_MODEL), lambda i, j: (i // bpc, 0, 0)),
                  pl.BlockSpec((D_MODEL, tf), lambda i, j: (0, j)),
                  pl.BlockSpec((tf, D_MODEL), lambda i, j: (j, 0)),
                  pl.BlockSpec((1, D_MODEL), lambda i, j: (0, 0))],
        out_specs=pl.BlockSpec((tm, D_MODEL), lambda i, j: (i, 0)),
        out_shape=jax.ShapeDtypeStruct((T, D_MODEL), F32),
        scratch_shapes=[pltpu.VMEM((tm, D_MODEL), F32)],
        compiler_params=pltpu.CompilerParams(dimension_semantics=("arbitrary", "arbitrary"),
                                             vmem_limit_bytes=_vmem_limit(est)),
        name=name,
    )(h2, x1, mods, w_up, w_down, g_post)


def _pad_w_in(w_in_l):
    o = W_ATTN + 2 * W_KV + 4 * W_MLSTM
    pre, gates, post = w_in_l[:, :o], w_in_l[:, o:o + N_GATES], w_in_l[:, o + N_GATES:]
    z = jnp.zeros((D_MODEL, LANES - 2 * H_MLSTM), w_in_l.dtype)
    return jnp.concatenate([pre, gates[:, :2 * H_MLSTM], z, gates[:, 2 * H_MLSTM:], z, post],
                           axis=1).astype(BF16)


def _pad_gate_bias(gb_l):
    z = jnp.zeros((LANES - 2 * H_MLSTM,), gb_l.dtype)
    return jnp.concatenate([gb_l[:2 * H_MLSTM], z, gb_l[2 * H_MLSTM:], z]).reshape(1, 2 * LANES)


def _rope_tables(S):
    quarter = HEAD_DIM // 4
    pos = np.arange(S)
    inv_freq = np.float32(ROPE_THETA) ** (-np.arange(quarter, dtype=np.float32) / np.float32(quarter))

    def tabs(p):
        ang = p.astype(np.float32)[:, None] * inv_freq[None, :]
        return np.cos(ang), np.sin(ang)

    cr, sr = tabs(pos // GRID_W)
    cc, sc = tabs(pos % GRID_W)
    cos = np.concatenate([cr, cr, cc, cc], axis=1)
    sin = np.concatenate([-sr, sr, -sc, sc], axis=1)
    return jnp.asarray(np.tile(cos, (1, 2)), F32), jnp.asarray(np.tile(sin, (1, 2)), F32)


def _pack_state(C, n, m):
    B = C.shape[0]
    Cp = C.reshape(B, 2, 2, 2, HEAD_DIM, HEAD_DIM)
    z = jnp.zeros_like(Cp[:, :, :, 0])
    top = jnp.concatenate([Cp[:, :, :, 0], z], axis=-1)
    bot = jnp.concatenate([z, Cp[:, :, :, 1]], axis=-1)
    Cbd = jnp.concatenate([top, bot], axis=-2)
    n_rep = jnp.broadcast_to(n.reshape(B, 2, 2, LANES, 1), (B, 2, 2, LANES, LANES))
    same_head = (jnp.arange(LANES)[:, None] < HEAD_DIM) == (jnp.arange(LANES)[None, :] < HEAD_DIM)
    n_rep = jnp.where(same_head, n_rep, 0.0)
    m_rows = jnp.broadcast_to(m[..., None], m.shape + (LANES,))
    return jnp.concatenate([Cbd, n_rep], axis=-1), jnp.concatenate([m_rows, m_rows], axis=-2)


def _unpack_state(s_p, m_p):
    B = s_p.shape[0]
    c_even = s_p[:, :, :, :HEAD_DIM, :HEAD_DIM]
    c_odd = s_p[:, :, :, HEAD_DIM:, HEAD_DIM:LANES]
    C = jnp.stack([c_even, c_odd], axis=3).reshape(B, 2, H_MLSTM, HEAD_DIM, HEAD_DIM)
    n = jnp.concatenate([s_p[..., :HEAD_DIM, LANES], s_p[..., HEAD_DIM:, LANES + HEAD_DIM]], axis=-1)
    return C, n.reshape(B, 2, H_MLSTM, HEAD_DIM), m_p[:, :, :H_MLSTM, 0]


def _layer(x, mods, lw, *, B, S, cond_rows, rope_tabs, ctx_cache, state, name):
    T = B * S
    kv_dtype = BF16 if ctx_cache is not None else F32
    ck_a = min(CK_ATTN, S)
    pr = _inproj(x, mods, lw["g_pre_mix"], lw["w_in"], lw["q_norm"], lw["k_norm"], lw["gate_bias"],
                 rope_tabs, rows_per_cond=cond_rows, kv_dtype=kv_dtype,
                 vt_blocks=(ck_a, NB_KV_BLOCK), name=name + "_inproj")
    seq = lambda a: a.reshape(B, S, a.shape[-1])
    qa, ka, va = seq(pr["qa"]), seq(pr["ka"]), seq(pr["va"])
    qc, kc, vc = seq(pr["qc"]), seq(pr["kc"]), seq(pr["vc"])
    chunks = lambda a, n: a.reshape(B, S // n, n, a.shape[-1])
    vat = pr["vat"].reshape(B, S // ck_a, W_KV, ck_a)
    vct = pr["vct"].reshape(B, S // NB_KV_BLOCK, W_NBHD, NB_KV_BLOCK)
    if ctx_cache is None:
        out_a = _attention(qa, [(chunks(ka, ck_a), vat)], heads=_HEADS_GQA, name=name + "_attn_a")
        out_c = _attention(qc, [(chunks(kc, NB_KV_BLOCK), vct)], heads=_HEADS_MHA, name=name + "_attn_c")
    else:
        ck_c, cv_c = ctx_cache[2], ctx_cache[3]
        P = ck_c.shape[1]
        assert P == ck_a
        cva_t, cvc_t = _cached_values_t(ctx_cache[1], cv_c, name=name + "_cache_t")
        out_a = _attention(qa, [(chunks(ka, ck_a), vat),
                                (ctx_cache[0].reshape(B, 1, P, W_KV), cva_t.reshape(B, 1, W_KV, P))],
                           heads=_HEADS_GQA, name=name + "_attn_a")
        out_c = _nbhd_attention(qc, chunks(kc, NB_KV_BLOCK), vct, ck_c, cvc_t,
                                lw["rel_bias"], name=name + "_attn_c")
    out_b, sf, mf = _mlstm(seq(pr["qb"]), seq(pr["kb"]), seq(pr["vb"]), seq(pr["g"]), seq(pr["ob"]),
                           state, lw["out_norm"], name=name + "_mlstm")
    x1, h2 = _outproj(out_a.reshape(T, W_ATTN), out_b.reshape(T, W_MLSTM), out_c.reshape(T, W_NBHD),
                      x, mods, lw["w_out"], lw["g_post_mix"], lw["g_pre_ffn"],
                      rows_per_cond=cond_rows, name=name + "_outproj")
    x2 = _ffn(h2, x1, mods, lw["w_up"], lw["w_down"], lw["g_post_ffn"],
              rows_per_cond=cond_rows, name=name + "_ffn")
    return x2, (ka, va, kc, vc, sf, mf)


def kernel(x_prompt, x_sample, c, cache_k_attn, cache_v_attn, cache_k_nbhd, cache_v_nbhd, state_mlstm_C, state_mlstm_n, state_mlstm_m, c_ctx, w_ada, b_ada, g_pre_mix, g_post_mix, g_pre_ffn, g_post_ffn, w_in, q_norm_attn, k_norm_attn, mlstm_gate_bias, mlstm_out_norm, nbhd_rel_bias, w_out, w_ffn_up, w_ffn_down):
    Bc, Sc, _ = x_prompt.shape
    Bl, Sl, _ = x_sample.shape
    P = cache_k_attn.shape[2]
    n_cond = 8
    cond = jnp.concatenate([c_ctx[None, :], c, jnp.zeros((n_cond - 1 - Bl, D_MODEL), F32)], axis=0)
    mods_all = _modulation(cond, w_ada, b_ada)

    layers = []
    for l in range(DEPTH):
        layers.append(dict(
            w_in=_pad_w_in(w_in[l]),
            w_out=w_out[l].astype(BF16),
            w_up=w_ffn_up[l].astype(BF16),
            w_down=w_ffn_down[l].astype(BF16),
            g_pre_mix=g_pre_mix[l].reshape(1, D_MODEL), g_post_mix=g_post_mix[l].reshape(1, D_MODEL),
            g_pre_ffn=g_pre_ffn[l].reshape(1, D_MODEL), g_post_ffn=g_post_ffn[l].reshape(1, D_MODEL),
            q_norm=jnp.tile(q_norm_attn[l], 2).reshape(1, LANES),
            k_norm=jnp.tile(k_norm_attn[l], 2).reshape(1, LANES),
            gate_bias=_pad_gate_bias(mlstm_gate_bias[l]),
            out_norm=mlstm_out_norm[l].reshape(1, W_MLSTM),
            rel_bias=nbhd_rel_bias[l].reshape(-1),
        ))

    xp = x_prompt.reshape(Bc * Sc, D_MODEL)
    ctx = []
    for l in range(DEPTH):
        mods = mods_all[l, 0:1].reshape(1, 1, N_MOD * D_MODEL)
        xp, extras = _layer(xp, mods, layers[l], B=Bc, S=Sc, cond_rows=Bc * Sc, rope_tabs=None,
                            ctx_cache=None, state=None, name=f"ctx{l}")
        ctx.append(extras)
    new_k_attn = jnp.stack([e[0].reshape(Bc, Sc, KV_ATTN, HEAD_DIM) for e in ctx], axis=1)
    new_v_attn = jnp.stack([e[1].reshape(Bc, Sc, KV_ATTN, HEAD_DIM) for e in ctx], axis=1)
    new_k_nbhd = jnp.stack([e[2].reshape(Bc, Sc, H_NBHD, HEAD_DIM) for e in ctx], axis=1)
    new_v_nbhd = jnp.stack([e[3].reshape(Bc, Sc, H_NBHD, HEAD_DIM) for e in ctx], axis=1)
    states = [_unpack_state(e[4], e[5]) for e in ctx]
    new_C = jnp.stack([s[0] for s in states], axis=1)
    new_n = jnp.stack([s[1] for s in states], axis=1)
    new_m = jnp.stack([s[2] for s in states], axis=1)

    xs = x_sample.reshape(Bl * Sl, D_MODEL)
    rope_tabs = _rope_tables(Sl)
    for l in range(DEPTH):
        mods = mods_all[l, 1:1 + Bl].reshape(Bl, 1, N_MOD * D_MODEL)
        cache = (cache_k_attn[:, l].reshape(Bl, P, W_KV), cache_v_attn[:, l].reshape(Bl, P, W_KV),
                 cache_k_nbhd[:, l].reshape(Bl, P, W_NBHD), cache_v_nbhd[:, l].reshape(Bl, P, W_NBHD))
        state = _pack_state(state_mlstm_C[:, l], state_mlstm_n[:, l], state_mlstm_m[:, l])
        xs, _ = _layer(xs, mods, layers[l], B=Bl, S=Sl, cond_rows=Sl, rope_tabs=rope_tabs,
                       ctx_cache=cache, state=state, name=f"lat{l}")

    return (xp.reshape(Bc, Sc, D_MODEL), xs.reshape(Bl, Sl, D_MODEL),
            new_k_attn, new_v_attn, new_k_nbhd, new_v_nbhd, new_C, new_n, new_m)
```

```python
import functools

import jax
import jax.numpy as jnp
import numpy as np
from jax import lax
from jax.experimental import pallas as pl
from jax.experimental.pallas import tpu as pltpu

F32 = jnp.float32
BF16 = jnp.bfloat16

D_MODEL = 1024
DEPTH = 2
GRID_W = 64
HEAD_DIM = 64
H_ATTN = 6
KV_ATTN = 2
H_MLSTM = 4
H_NBHD = 6
D_FF = 4 * D_MODEL
NA_ROWS = 8
NA_COLS = 16
ROPE_THETA = 10000.0
EPS = 1e-6
N_MOD = 6
W_ATTN = H_ATTN * HEAD_DIM
W_KV = KV_ATTN * HEAD_DIM
W_MLSTM = H_MLSTM * HEAD_DIM
W_NBHD = H_NBHD * HEAD_DIM
N_GATES = 4 * H_MLSTM

LANES = 128
V7X_VMEM_BYTES = 64 * 1024 * 1024
VMEM_CAP_BYTES = 56 * 1024 * 1024

TM_PROJ = 512
PROJ_CHUNK = 512
TM_FFN = 512
TF_FFN = 4096
ROW_SPLIT = 4
TQ_ATTN = 512
CK_ATTN = 512
ATTN_LOOKAHEAD = 2
SEQS_PER_STEP = 16
L_CHUNK = 128
MLSTM_GROUP = 2
MLSTM_GROUP_A = 16
MLSTM_STEP_CHUNKS = 32
NB_GROUP = 8
NB_SLAB = 16
NB_STEP_GROUPS = 8
NB_KV_BLOCK = 256
NEG = -1e30
LOG2E = 1.4426950408889634

_COLS = {}
_off = 0
for _name, _w in (("qa", W_ATTN), ("ka", W_KV), ("va", W_KV), ("qb", W_MLSTM), ("kb", W_MLSTM),
                  ("vb", W_MLSTM), ("ob", W_MLSTM), ("gf", LANES), ("gb", LANES),
                  ("qc", W_NBHD), ("kc", W_NBHD), ("vc", W_NBHD)):
    _COLS[_name] = (_off, _off + _w)
    _off += _w
IN_PAD = _off


def _vmem_limit(nbytes):
    return int(min(max(nbytes, 16 * 1024 * 1024), VMEM_CAP_BYTES))


def _dot(a, b):
    return jnp.dot(a, b, preferred_element_type=F32)


def _dot_nt(a, b):
    return lax.dot_general(a, b, (((1,), (1,)), ((), ())), preferred_element_type=F32)


def _lane_lo(shape):
    return (lax.broadcasted_iota(jnp.int32, shape, len(shape) - 1) % LANES) < HEAD_DIM


def _rms(x, g):
    ms = jnp.mean(x * x, axis=-1, keepdims=True)
    return (x * lax.rsqrt(ms + EPS)) * g


def _pair_rms(x, g):
    lo = _lane_lo(x.shape)
    x2 = x * x
    s_lo = jnp.sum(jnp.where(lo, x2, 0.0), axis=-1, keepdims=True)
    s_hi = jnp.sum(jnp.where(lo, 0.0, x2), axis=-1, keepdims=True)
    r = jnp.where(lo, lax.rsqrt(s_lo / HEAD_DIM + EPS), lax.rsqrt(s_hi / HEAD_DIM + EPS))
    return (x * r) * g


def _sigmoid(x):
    return 1.0 / (1.0 + jnp.exp(-x))


def _mods_kernel(c_ref, w_ref, b_ref, o_ref):
    c = c_ref[...]
    s = (c * _sigmoid(c)).astype(BF16)
    o_ref[0] = _dot(s, w_ref[0].astype(BF16)) + b_ref[0]


def _modulation(cond, w_ada, b_ada):
    n = cond.shape[0]
    tn = D_MODEL
    return pl.pallas_call(
        _mods_kernel,
        grid=(DEPTH, N_MOD * D_MODEL // tn),
        in_specs=[pl.BlockSpec((n, D_MODEL), lambda l, j: (0, 0)),
                  pl.BlockSpec((1, D_MODEL, tn), lambda l, j: (l, 0, j)),
                  pl.BlockSpec((1, 1, tn), lambda l, j: (l, 0, j))],
        out_specs=pl.BlockSpec((1, n, tn), lambda l, j: (l, 0, j)),
        out_shape=jax.ShapeDtypeStruct((DEPTH, n, N_MOD * D_MODEL), F32),
        compiler_params=pltpu.CompilerParams(
            dimension_semantics=("arbitrary", "arbitrary"),
            vmem_limit_bytes=_vmem_limit(4 * D_MODEL * tn * 4)),
        name="modulation",
    )(cond, w_ada, b_ada.reshape(DEPTH, 1, N_MOD * D_MODEL))


def _inproj_kernel(*refs, rope):
    if rope:
        (x_ref, mod_ref, g_ref, w_ref, qn_ref, kn_ref, gbias_ref, cos_ref, sin_ref,
         qa_ref, ka_ref, va_ref, qb_ref, kb_ref, vb_ref, ob_ref, gate_ref,
         qc_ref, kc_ref, vc_ref, vat_ref, vct_ref) = refs
    else:
        (x_ref, mod_ref, g_ref, w_ref, qn_ref, kn_ref, gbias_ref,
         qa_ref, ka_ref, va_ref, qb_ref, kb_ref, vb_ref, ob_ref, gate_ref,
         qc_ref, kc_ref, vc_ref, vat_ref, vct_ref) = refs
    x = x_ref[...]
    mod = mod_ref[0]
    sh1 = mod[:, 0:D_MODEL]
    sc1 = mod[:, D_MODEL:2 * D_MODEL]
    hb = (_rms(x, g_ref[...]) * (1.0 + sc1) + sh1).astype(BF16)

    z = [_dot(hb, w_ref[:, c0:c0 + PROJ_CHUNK]) for c0 in range(0, IN_PAD, PROJ_CHUNK)]

    def proj(name, j=0, w=None):
        lo, hi = _COLS[name]
        lo = lo + j
        hi = hi if w is None else lo + w
        parts = []
        while lo < hi:
            c, o = divmod(lo, PROJ_CHUNK)
            n = min(hi - lo, PROJ_CHUNK - o)
            parts.append(z[c][:, o:o + n])
            lo += n
        return parts[0] if len(parts) == 1 else jnp.concatenate(parts, axis=1)

    scale = HEAD_DIM ** -0.5
    q_scale = scale * LOG2E

    def rotary(t):
        first = (lax.broadcasted_iota(jnp.int32, t.shape, 1) % 32) < 16
        partner = jnp.where(first, pltpu.roll(t, LANES - 16, 1), pltpu.roll(t, 16, 1))
        return t * cos_ref[...] + partner * sin_ref[...]

    def store_queries(t, j, q_ref, kv_half):
        lo = _lane_lo(t.shape)
        for e in range(2):
            h = 2 * j + e
            src = t if e == kv_half[h] else pltpu.roll(t, HEAD_DIM, 1)
            keep = lo if kv_half[h] == 0 else jnp.logical_not(lo)
            q_ref[:, h * LANES:(h + 1) * LANES] = jnp.where(keep, src, 0.0).astype(q_ref.dtype)

    for j in range(W_ATTN // LANES):
        t = _pair_rms(proj("qa", j * LANES, LANES), qn_ref[...])
        if rope:
            t = rotary(t)
        store_queries(t * q_scale, j, qa_ref, [kh for _, kh in _HEADS_GQA])
    t = _pair_rms(proj("ka"), kn_ref[...])
    if rope:
        t = rotary(t)
    ka_ref[...] = t.astype(ka_ref.dtype)

    def store_v(v, v_ref, vt_ref):
        v_ref[...] = v.astype(v_ref.dtype)
        nblk, _, blk = vt_ref.shape
        for u in range(nblk):
            vt_ref[u] = v[u * blk:(u + 1) * blk].T.astype(vt_ref.dtype)

    store_v(proj("va"), va_ref, vat_ref)
    qb_ref[...] = proj("qb").astype(qb_ref.dtype)
    kb_ref[...] = (proj("kb") * scale).astype(kb_ref.dtype)
    vb_ref[...] = proj("vb").astype(vb_ref.dtype)
    ob_ref[...] = proj("ob").astype(ob_ref.dtype)
    for j, name in enumerate(("gf", "gb")):
        gt = proj(name) + gbias_ref[:, j * LANES:(j + 1) * LANES]
        lane = lax.broadcasted_iota(jnp.int32, gt.shape, 1)
        is_f = (lane >= H_MLSTM) & (lane < 2 * H_MLSTM)
        logsig = jnp.minimum(gt, 0.0) - jnp.log1p(jnp.exp(-jnp.abs(gt)))
        gate_ref[:, j * LANES:(j + 1) * LANES] = jnp.where(is_f, logsig, gt)
    for j in range(W_NBHD // LANES):
        store_queries(proj("qc", j * LANES, LANES) * q_scale, j, qc_ref, [kh for _, kh in _HEADS_MHA])
    kc_ref[...] = proj("kc").astype(kc_ref.dtype)
    store_v(proj("vc"), vc_ref, vct_ref)


def _inproj(x, mods, g_pre, w_in_p, qn, kn, gbias, rope_tabs, *, rows_per_cond, kv_dtype, vt_blocks, name):
    T = x.shape[0]
    tm = TM_PROJ
    bpc = rows_per_cond // tm
    rope = rope_tabs is not None
    row = lambda i: (i, 0)
    const = lambda i: (0, 0)
    in_specs = [pl.BlockSpec((tm, D_MODEL), row),
                pl.BlockSpec((1, 1, N_MOD * D_MODEL), lambda i: (i // bpc, 0, 0)),
                pl.BlockSpec((1, D_MODEL), const),
                pl.BlockSpec((D_MODEL, IN_PAD), const),
                pl.BlockSpec((1, LANES), const),
                pl.BlockSpec((1, LANES), const),
                pl.BlockSpec((1, 2 * LANES), const)]
    args = [x, mods, g_pre, w_in_p, qn, kn, gbias]
    if rope:
        nblk = rope_tabs[0].shape[0] // tm
        in_specs += [pl.BlockSpec((tm, LANES), lambda i: (i % nblk, 0))] * 2
        args += list(rope_tabs)
    widths = [("qa", H_ATTN * LANES, BF16), ("ka", W_KV, kv_dtype), ("va", W_KV, kv_dtype),
              ("qb", W_MLSTM, BF16), ("kb", W_MLSTM, BF16), ("vb", W_MLSTM, BF16),
              ("ob", W_MLSTM, F32), ("g", 2 * LANES, F32),
              ("qc", H_NBHD * LANES, BF16), ("kc", W_NBHD, kv_dtype), ("vc", W_NBHD, kv_dtype)]
    out_specs = [pl.BlockSpec((tm, w), row) for _, w, _ in widths]
    out_shape = [jax.ShapeDtypeStruct((T, w), dt) for _, w, dt in widths]
    for w, blk in zip((W_KV, W_NBHD), vt_blocks):
        out_specs.append(pl.BlockSpec((tm // blk, w, blk), lambda i: (i, 0, 0)))
        out_shape.append(jax.ShapeDtypeStruct((T // blk, w, blk), BF16))
    est = 2 * (tm * D_MODEL * 4 + D_MODEL * IN_PAD * 2 + tm * IN_PAD * 4) + 3 * tm * IN_PAD * 4
    outs = pl.pallas_call(
        functools.partial(_inproj_kernel, rope=rope),
        grid=(T // tm,),
        in_specs=in_specs, out_specs=out_specs, out_shape=out_shape,
        compiler_params=pltpu.CompilerParams(dimension_semantics=("arbitrary",),
                                             vmem_limit_bytes=_vmem_limit(est)),
        name=name,
    )(*args)
    return dict(zip([n for n, _, _ in widths] + ["vat", "vct"], outs))


SUM_ROWS = 16


def _softmax_step(st, col_max, vt, m_scr, acc_scr, u):
    m_old = m_scr[u]
    m_new = jnp.maximum(m_old, col_max)
    p = jnp.exp2(st - m_new).astype(BF16)
    alpha = jnp.exp2(m_old - m_new)
    vt1 = jnp.concatenate([vt, jnp.ones((SUM_ROWS, vt.shape[1]), BF16)], axis=0)
    acc_scr[u] = alpha * acc_scr[u] + _dot(vt1, p)
    m_scr[u] = m_new


def _normalised_pair(acc_scr, u0, u1):
    halves = [acc_scr[u][0:HEAD_DIM] / acc_scr[u][HEAD_DIM:HEAD_DIM + 1] for u in (u0, u1)]
    return jnp.concatenate(halves, axis=0).T
def _attn_kernel(*refs, heads, part_chunks):
    n_parts = len(part_chunks)
    assert n_parts in (1, 2)
    q_ref, kv_refs = refs[0], refs[1:1 + 2 * n_parts]
    o_ref, m_scr, acc_scr, s_ring, cmax_scr = refs[1 + 2 * n_parts:]
    nch = sum(part_chunks)

    def chunk_of(which, j, rows, cols):
        first = kv_refs[which][0, jnp.minimum(j, part_chunks[0] - 1), rows, cols].astype(BF16)
        if n_parts == 1:
            return first
        j2 = jnp.clip(j - part_chunks[0], 0, part_chunks[1] - 1)
        return jnp.where(j < part_chunks[0], first, kv_refs[2 + which][0, j2, rows, cols].astype(BF16))

    nh = len(heads)
    ring = ATTN_LOOKAHEAD + 1
    assert nh % ring == 0
    m_scr[...] = jnp.full(m_scr.shape, -jnp.inf, F32)
    acc_scr[...] = jnp.zeros(acc_scr.shape, F32)

    def scores(j, item):
        h = item % nh
        kg = heads[h][0]
        kj = chunk_of(0, j, slice(None), slice(kg * LANES, (kg + 1) * LANES))
        st = _dot_nt(kj, q_ref[0, :, h * LANES:(h + 1) * LANES])
        s_ring[item % ring] = st
        cmax_scr[item % ring] = jnp.max(st, axis=0, keepdims=True)

    def chunk(j, carry, last=False):
        for h, (kg, kh) in enumerate(heads):
            ahead = h + ATTN_LOOKAHEAD
            if ahead < nh:
                scores(j, ahead)
            elif not last:
                scores(j + 1, ahead)
            r = kg * LANES + kh * HEAD_DIM
            vt = chunk_of(1, j, slice(r, r + HEAD_DIM), slice(None))
            _softmax_step(s_ring[h % ring], cmax_scr[h % ring], vt, m_scr, acc_scr, h)
        return carry

    for item in range(ATTN_LOOKAHEAD):
        scores(0, item)
    lax.fori_loop(0, nch - 1, chunk, 0)
    chunk(nch - 1, 0, last=True)
    for t in range(len(heads) // 2):
        o_ref[0, :, t * LANES:(t + 1) * LANES] = _normalised_pair(acc_scr, 2 * t, 2 * t + 1).astype(o_ref.dtype)


def _attn_seqs_kernel(q_ref, k_ref, vt_ref, o_ref, s_ring, cmax_scr, *, heads):
    nh = len(heads)
    n_seq = q_ref.shape[0]
    ring = nh
    look = nh - 1

    def scores(s, item):
        h = item % nh
        kg = heads[h][0]
        st = _dot_nt(k_ref[s, 0, :, kg * LANES:(kg + 1) * LANES].astype(BF16),
                     q_ref[s, :, h * LANES:(h + 1) * LANES])
        s_ring[item % ring] = st
        cmax_scr[item % ring] = jnp.max(st, axis=0, keepdims=True)

    def sequence(s, carry, last=False):
        accs = []
        for h, (kg, kh) in enumerate(heads):
            ahead = h + look
            if ahead < nh:
                scores(s, ahead)
            elif not last:
                scores(s + 1, ahead)
            r = kg * LANES + kh * HEAD_DIM
            vt = vt_ref[s, 0, r:r + HEAD_DIM, :].astype(BF16)
            vt1 = jnp.concatenate([vt, jnp.ones((SUM_ROWS, vt.shape[1]), BF16)], axis=0)
            p = jnp.exp2(s_ring[h % ring] - cmax_scr[h % ring]).astype(BF16)
            accs.append(_dot(vt1, p))
        for t in range(nh // 2):
            halves = [a[0:HEAD_DIM] / a[HEAD_DIM:HEAD_DIM + 1] for a in accs[2 * t:2 * t + 2]]
            o_ref[s, :, t * LANES:(t + 1) * LANES] = jnp.concatenate(halves, axis=0).T.astype(o_ref.dtype)
        return carry

    for item in range(look):
        scores(0, item)
    lax.fori_loop(0, n_seq - 1, sequence, 0)
    sequence(n_seq - 1, 0, last=True)


def _attention_seqs(q, k4, vt4, *, heads, name):
    B, Sq, WQ = q.shape
    ck, KW = k4.shape[2], k4.shape[3]
    nh = len(heads)
    W = nh * HEAD_DIM
    bb = min(SEQS_PER_STEP, B)
    ring = nh
    est = (2 * bb * (Sq * (WQ + W) * 2 + ck * KW * (k4.dtype.itemsize + vt4.dtype.itemsize))
           + ring * ck * Sq * 4 + 8 * nh * ck * Sq * 4)
    return pl.pallas_call(
        functools.partial(_attn_seqs_kernel, heads=heads),
        grid=(B // bb,),
        in_specs=[pl.BlockSpec((bb, Sq, WQ), lambda i: (i, 0, 0)),
                  pl.BlockSpec((bb, 1, ck, KW), lambda i: (i, 0, 0, 0)),
                  pl.BlockSpec((bb, 1, KW, ck), lambda i: (i, 0, 0, 0))],
        out_specs=pl.BlockSpec((bb, Sq, W), lambda i: (i, 0, 0)),
        out_shape=jax.ShapeDtypeStruct((B, Sq, W), BF16),
        scratch_shapes=[pltpu.VMEM((ring, ck, Sq), F32),
                        pltpu.VMEM((ring, 1, Sq), F32)],
        compiler_params=pltpu.CompilerParams(dimension_semantics=("arbitrary",),
                                             vmem_limit_bytes=_vmem_limit(est)),
        name=name,
    )(q, k4, vt4)


def _attention(q, parts, *, heads, name):
    B, Sq, WQ = q.shape
    ck, KW = parts[0][0].shape[2], parts[0][0].shape[3]
    tq = min(TQ_ATTN, Sq)
    nh = len(heads)
    W = nh * HEAD_DIM
    part_chunks = tuple(k4.shape[1] for k4, _ in parts)
    if part_chunks == (1,) and tq == Sq:
        return _attention_seqs(q, parts[0][0], parts[0][1], heads=heads, name=name)
    in_specs = [pl.BlockSpec((1, tq, WQ), lambda b, i: (b, i, 0))]
    args = [q]
    kv_bytes = 0
    for k4, vt4 in parts:
        assert k4.shape[2:] == (ck, KW) and vt4.shape[2:] == (KW, ck)
        in_specs += [pl.BlockSpec((1,) + k4.shape[1:], lambda b, i: (b, 0, 0, 0)),
                     pl.BlockSpec((1,) + vt4.shape[1:], lambda b, i: (b, 0, 0, 0))]
        args += [k4, vt4]
        kv_bytes += k4[0].size * k4.dtype.itemsize + vt4[0].size * vt4.dtype.itemsize
    est = (2 * (tq * (WQ + W) * 2 + kv_bytes) + nh * tq * (HEAD_DIM * 4 + 64) + 8 * nh * ck * tq * 4)
    return pl.pallas_call(
        functools.partial(_attn_kernel, heads=heads, part_chunks=part_chunks),
        grid=(B, Sq // tq),
        in_specs=in_specs,
        out_specs=pl.BlockSpec((1, tq, W), lambda b, i: (b, i, 0)),
        out_shape=jax.ShapeDtypeStruct((B, Sq, W), BF16),
        scratch_shapes=[pltpu.VMEM((nh, 1, tq), F32),
                        pltpu.VMEM((nh, HEAD_DIM + SUM_ROWS, tq), F32),
                        pltpu.VMEM((ATTN_LOOKAHEAD + 1, ck, tq), F32),
                        pltpu.VMEM((ATTN_LOOKAHEAD + 1, 1, tq), F32)],
        compiler_params=pltpu.CompilerParams(dimension_semantics=("arbitrary", "arbitrary"),
                                             vmem_limit_bytes=_vmem_limit(est)),
        name=name,
    )(*args)


def _transpose_kernel(a_ref, b_ref, at_ref, bt_ref):
    at_ref[0] = a_ref[0].T.astype(at_ref.dtype)
    bt_ref[0] = b_ref[0].T.astype(bt_ref.dtype)


def _cached_values_t(a, b, *, name):
    B, P, ca = a.shape
    cb = b.shape[2]
    return pl.pallas_call(
        _transpose_kernel,
        grid=(B,),
        in_specs=[pl.BlockSpec((1, P, ca), lambda i: (i, 0, 0)), pl.BlockSpec((1, P, cb), lambda i: (i, 0, 0))],
        out_specs=[pl.BlockSpec((1, ca, P), lambda i: (i, 0, 0)), pl.BlockSpec((1, cb, P), lambda i: (i, 0, 0))],
        out_shape=[jax.ShapeDtypeStruct((B, ca, P), BF16), jax.ShapeDtypeStruct((B, cb, P), BF16)],
        compiler_params=pltpu.CompilerParams(dimension_semantics=("arbitrary",)),
        name=name,
    )(a, b)


_HEADS_GQA = tuple((0, h // (H_ATTN // KV_ATTN)) for h in range(H_ATTN))
_HEADS_MHA = tuple((h // 2, h % 2) for h in range(H_NBHD))


def _nbhd_window(r, rows):
    kr = min(NA_ROWS, rows)
    return min(max(r - kr // 2, 0), rows - kr), kr


def _nbhd_patterns(rows):
    n_groups = rows // NB_GROUP
    pats = []
    for g in (0, 1, n_groups - 1):
        r0 = g * NB_GROUP
        pats.append((r0, min(max(r0 - NA_ROWS // 2, 0), rows - NB_SLAB)))
    return pats


def _nbhd_kernel(rb_ref, q_ref, k_ref, vt_ref, kc_ref, vct_ref, o_ref,
                 bias_scr, m_scr, acc_scr, s_ring, cmax_scr, *, rows):
    hp = pl.program_id(0)
    b = pl.program_id(1)
    gs = pl.program_id(2)
    n_groups = rows // NB_GROUP
    n_dr = 2 * NA_ROWS - 1
    n_dc = 2 * NA_COLS - 1
    tq = NB_GROUP * GRID_W
    ck = tq
    pats = _nbhd_patterns(rows)
    ring = ATTN_LOOKAHEAD + 1
    n_chunks = NB_SLAB * GRID_W // ck + 1
    assert n_chunks % ring == 0 and kc_ref.shape[1] == ck
    blk_rows = NB_KV_BLOCK // GRID_W

    @pl.when((b == 0) & (gs == 0))
    def _build_bias():
        shape = (GRID_W, LANES)
        w = lax.broadcasted_iota(jnp.int32, shape, 0)
        lane = lax.broadcasted_iota(jnp.int32, shape, 1)
        cc = lane % GRID_W
        second = lane >= GRID_W
        cs = jnp.clip(w - NA_COLS // 2, 0, GRID_W - NA_COLS)
        col_ok = (cc >= cs) & (cc < cs + NA_COLS)
        dc = cc - w + (NA_COLS - 1)
        for hh in range(2):
            base = (2 * hp + hh) * (n_dr * n_dc)
            tiles = {}
            for d in range(-1, n_dr):
                acc = jnp.zeros(shape, F32)
                for j in range(n_dc):
                    va = rb_ref[base + d * n_dc + j] * LOG2E if d >= 0 else 0.0
                    vb = rb_ref[base + (d + 1) * n_dc + j] * LOG2E if d + 1 < n_dr else 0.0
                    acc = acc + jnp.where(dc == j, jnp.where(second, vb, va), 0.0)
                tiles[d] = acc
            def query_row_tile(r, kra):
                rs, kr = _nbhd_window(r, rows)
                ok_a = rs <= kra < rs + kr
                ok_b = rs <= kra + 1 < rs + kr
                if not (ok_a or ok_b):
                    return jnp.full(shape, NEG, F32)
                row_ok = (jnp.logical_not(second) if ok_a and not ok_b else
                          second if ok_b and not ok_a else None)
                ok = col_ok if row_ok is None else (col_ok & row_ok)
                return jnp.where(ok, tiles[kra - r + (NA_ROWS - 1)], NEG)

            for pi, (r0, slab0) in enumerate(pats):
                for ip in range(NB_GROUP // 2):
                    for ap in range(NB_SLAB // 2):
                        kra = slab0 + 2 * ap
                        two_rows = jnp.concatenate([query_row_tile(r0 + 2 * ip, kra),
                                                    query_row_tile(r0 + 2 * ip + 1, kra)], axis=0)
                        bias_scr[hh, pi, ap * LANES:(ap + 1) * LANES, ip * LANES:(ip + 1) * LANES] = two_rows.T

    assert (2 * n_chunks) % ring == 0
    m_scr[...] = jnp.full(m_scr.shape, -jnp.inf, F32)
    acc_scr[...] = jnp.zeros(acc_scr.shape, F32)

    def group_of(s):
        g = gs * NB_STEP_GROUPS + s
        pat = jnp.where(g == 0, 0, jnp.where(g == n_groups - 1, 2, 1))
        slab0 = jnp.clip(g * NB_GROUP - NA_ROWS // 2, 0, rows - NB_SLAB)
        return pat, slab0 // blk_rows

    per_group = 2 * n_chunks
    nb = ck // NB_KV_BLOCK

    def scores(s, item):
        hh, c = (item % per_group) // n_chunks, item % n_chunks
        qm = q_ref[0, pl.ds(pl.multiple_of(s * tq, tq), tq), hh * LANES:(hh + 1) * LANES]
        if c < n_chunks - 1:
            pat, blk0 = group_of(s)
            kc = k_ref[0, pl.ds(blk0 + c * nb, nb)].reshape(ck, LANES)
            st = _dot_nt(kc, qm) + bias_scr[hh, pat, c * ck:(c + 1) * ck, :]
        else:
            st = _dot_nt(kc_ref[0].astype(BF16), qm)
        s_ring[item % ring] = st
        cmax_scr[item % ring] = jnp.max(st, axis=0, keepdims=True)

    def group(s, carry, last=False):
        for item in range(per_group):
            ahead = item + ATTN_LOOKAHEAD
            if ahead < per_group:
                scores(s, ahead)
            elif not last:
                scores(s + 1, ahead)
            hh, c = item // n_chunks, item % n_chunks
            if c < n_chunks - 1:
                _, blk0 = group_of(s)
                vt = jnp.concatenate([vt_ref[0, blk0 + c * nb + i, hh * HEAD_DIM:(hh + 1) * HEAD_DIM, :]
                                      for i in range(nb)], axis=1)
            else:
                vt = vct_ref[0, hh * HEAD_DIM:(hh + 1) * HEAD_DIM, :].astype(BF16)
            _softmax_step(s_ring[item % ring], cmax_scr[item % ring], vt, m_scr, acc_scr, 2 * s + hh)
        o_ref[0, pl.ds(pl.multiple_of(s * tq, tq), tq), :] = (
            _normalised_pair(acc_scr, 2 * s, 2 * s + 1).astype(o_ref.dtype))
        return carry

    for item in range(ATTN_LOOKAHEAD):
        scores(0, item)
    lax.fori_loop(0, NB_STEP_GROUPS - 1, group, 0)
    group(jnp.int32(NB_STEP_GROUPS - 1), 0, last=True)


def _nbhd_attention(q, k4, vt4, k_ctx, vct, rel_bias_flat, *, name):
    B, S, _ = q.shape
    W = k4.shape[3]
    P = k_ctx.shape[1]
    rows = S // GRID_W
    tq = NB_GROUP * GRID_W
    nk = NB_SLAB * GRID_W
    nblk = S // NB_KV_BLOCK
    tqs = NB_STEP_GROUPS * tq
    nu = 2 * NB_STEP_GROUPS
    ring = ATTN_LOOKAHEAD + 1
    est =(2 * (2 * tqs * LANES * 2 + 2 * S * LANES * 2 + 2 * P * LANES * 4)
           + 2 * 3 * tq * nk * 4 + ring * tq * tq * 4 + 8 * tq * tq * 4 + nu * tq * 1024)
    return pl.pallas_call(
        functools.partial(_nbhd_kernel, rows=rows),
        grid=(W // LANES, B, rows // (NB_GROUP * NB_STEP_GROUPS)),
        in_specs=[pl.BlockSpec(memory_space=pltpu.SMEM),
                  pl.BlockSpec((1, tqs, 2 * LANES), lambda p, b, g: (b, g, p)),
                  pl.BlockSpec((1, nblk, NB_KV_BLOCK, LANES), lambda p, b, g: (b, 0, 0, p)),
                  pl.BlockSpec((1, nblk, LANES, NB_KV_BLOCK), lambda p, b, g: (b, 0, p, 0)),
                  pl.BlockSpec((1, P, LANES), lambda p, b, g: (b, 0, p)),
                  pl.BlockSpec((1, LANES, P), lambda p, b, g: (b, p, 0))],
        out_specs=pl.BlockSpec((1, tqs, LANES), lambda p, b, g: (b, g, p)),
        out_shape=jax.ShapeDtypeStruct((B, S, W), BF16),
        scratch_shapes=[pltpu.VMEM((2, 3, nk, tq), F32),
                        pltpu.VMEM((nu, 1, tq), F32),
                        pltpu.VMEM((nu, HEAD_DIM + SUM_ROWS, tq), F32),
                        pltpu.VMEM((ring, tq, tq), F32),
                        pltpu.VMEM((ring, 1, tq), F32)],
        compiler_params=pltpu.CompilerParams(dimension_semantics=("arbitrary",) * 3,
                                             vmem_limit_bytes=_vmem_limit(est)),
        name=name,
    )(rel_bias_flat, q, k4, vt4, k_ctx, vct)


def _split3(x):
    hi = x.astype(BF16)
    r = x - hi.astype(F32)
    mid = r.astype(BF16)
    return hi, mid, (r - mid.astype(F32)).astype(BF16)


def _mlstm_kernel(*refs, nc, grp_a, grp, has_state):
    q_ref, k_ref, v_ref, g_ref, ob_ref = refs[:5]
    s0_ref, m0_ref = refs[5:7] if has_state else (None, None)
    (on_ref, out_ref, sf_ref, mf_ref,
     h_scr, nat_scr, rows_scr, stat_scr, mprev_scr, un_scr, st_scr) = refs[7 if has_state else 5:]
    d = pl.program_id(1)
    L = L_CHUNK
    bb = q_ref.shape[0]
    NP = H_MLSTM // 2
    row = lax.broadcasted_iota(jnp.int32, (L, L), 0)
    col = lax.broadcasted_iota(jnp.int32, (L, L), 1)
    sign = 1 - 2 * d
    mask = (col - row) * sign <= 0
    maskb = mask.astype(BF16)
    mask3 = jnp.concatenate([maskb, maskb, maskb], axis=1)
    lane = lax.broadcasted_iota(jnp.int32, (L, LANES), 1)
    lo = lane < HEAD_DIM
    top = row < HEAD_DIM
    row2 = lax.broadcasted_iota(jnp.int32, (L, 2 * LANES), 0)
    col2 = lax.broadcasted_iota(jnp.int32, (L, 2 * LANES), 1)
    keep_state = (row2 < HEAD_DIM) == ((col2 % LANES) < HEAD_DIM)
    top2 = row2 < HEAD_DIM
    ones_b = jnp.ones((L, LANES), BF16)
    ones_lo = lo.astype(BF16)
    ones_hi = jnp.logical_not(lo).astype(BF16)

    def chunk_rows(c):
        return pl.ds(pl.multiple_of(c * L, L), L)

    def tokens(c):
        if bb == 1:
            return 0, chunk_rows(c)
        return c // nc, pl.ds(pl.multiple_of((c % nc) * L, L), L)

    def pass_a(it, carry):
        cs = [it * grp_a + u for u in range(grp_a)]
        gts = [g_ref[tokens(c)[0], tokens(c)[1], :] for c in cs]
        bns = [_dot(mask3, jnp.concatenate(_split3(gt), axis=0)) for gt in gts]
        a_all = []
        for c, gt, bn in zip(cs, gts, bns):
            nat = jnp.where(lane < H_MLSTM, gt, bn)
            nat_scr[chunk_rows(c), :] = nat * (-LOG2E)
            nat_t = nat.T
            b_rows = nat_t[H_MLSTM:2 * H_MLSTM]
            c_rows = nat_t[0:H_MLSTM] - b_rows
            rows_scr[c] = jnp.concatenate([c_rows * LOG2E, b_rows], axis=0)
            c_max = jnp.max(c_rows, axis=1, keepdims=True)
            b_tot = jnp.where(d == 0, b_rows[:, L - 1:L], b_rows[:, 0:1])
            stat_scr[c] = jnp.concatenate([jnp.broadcast_to(c_max, (H_MLSTM, LANES)),
                                           jnp.broadcast_to(b_tot, (H_MLSTM, LANES))], axis=0)
            a_all.append(jnp.exp(c_rows - c_max))
        for c, a_rows in zip(cs, a_all):
            sq, rows = tokens(c)
            for p in range(NP):
                lanes = slice(p * LANES, (p + 1) * LANES)
                k_t = k_ref[sq, rows, lanes].astype(F32).T
                a_sel = jnp.where(top, a_rows[2 * p:2 * p + 1], a_rows[2 * p + 1:2 * p + 2])
                vv = jnp.concatenate([v_ref[sq, rows, lanes], ones_b], axis=1)
                un = _dot((k_t * a_sel).astype(BF16), vv)
                un_scr[c, p] = jnp.where(keep_state, un, 0.0)
        return carry

    lax.fori_loop(0, bb * nc // grp_a, pass_a, 0)

    def pass_b(ci, m, sq):
        c = sq * nc + jnp.where(d == 0, ci, nc - 1 - ci)
        st = stat_scr[c]
        c_max, b_tot = st[0:H_MLSTM], st[H_MLSTM:]
        m_new = jnp.maximum(b_tot + m, b_tot + c_max)
        d_old = jnp.exp(b_tot + m - m_new)
        d_new = jnp.exp(b_tot + c_max - m_new)
        mprev_scr[c] = jnp.concatenate([m, m], axis=0) * LOG2E
        for p in range(NP):
            def rows_of(t, p=p):
                even = jnp.concatenate([t[2 * p:2 * p + 1]] * 2, axis=1)
                odd = jnp.concatenate([t[2 * p + 1:2 * p + 2]] * 2, axis=1)
                return jnp.where(top2, even, odd)
            s_prev = st_scr[p]
            st_scr[p] = rows_of(d_old) * s_prev + rows_of(d_new) * un_scr[c, p]
            un_scr[c, p] = s_prev
        return m_new

    def recurrence(sq, carry):
        if has_state:
            st_scr[...] = s0_ref[sq, 0]
            m_start = m0_ref[sq, 0][0:H_MLSTM]
        else:
            st_scr[...] = jnp.zeros(st_scr.shape, F32)
            m_start = jnp.zeros((H_MLSTM, LANES), F32)
        m_fin = lax.fori_loop(0, nc, functools.partial(pass_b, sq=sq), m_start)
        sf_ref[sq, 0] = st_scr[...]
        mf_ref[sq, 0] = jnp.concatenate([m_fin, m_fin], axis=0)
        return carry

    lax.fori_loop(0, bb, recurrence, 0)

    def pass_c(it, carry):
        cs = [it * grp + u for u in range(grp)]
        units = [(u, p) for u in range(grp) for p in range(NP)]
        early = {}
        for u, p in units:
            c = cs[u]
            sq, rows = tokens(c)
            lanes = slice(p * LANES, (p + 1) * LANES)
            qp = q_ref[sq, rows, lanes]
            kp = k_ref[sq, rows, lanes]
            s_in = un_scr[c, p].astype(BF16)
            qms = [jnp.where(lo if hh == 0 else jnp.logical_not(lo), qp, jnp.zeros_like(qp)) for hh in range(2)]
            early[u, p] = ([_dot_nt(qm, kp) for qm in qms],
                           _dot(qp, s_in))
        mid = {}
        for u, p in units:
            r_t = rows_scr[cs[u]]
            m_in = mprev_scr[cs[u]]
            for hh in range(2):
                h = 2 * p + hh
                cm = jnp.where(mask, r_t[h:h + 1, :], -jnp.inf)
                m_prev = m_in[h:h + 1, :]
                mu = jnp.maximum(jnp.broadcast_to(jnp.max(cm, axis=1, keepdims=True), (L, LANES)), m_prev)
                w = early[u, p][0][hh] * jnp.exp2(cm - mu)
                mid[u, p, hh] = (w.astype(BF16), mu, m_prev)
        for u, p in units:
            sq, rows = tokens(cs[u])
            lanes = slice(p * LANES, (p + 1) * LANES)
            vp = v_ref[sq, rows, lanes]
            zero = jnp.zeros_like(vp)
            vv = jnp.concatenate([jnp.concatenate([jnp.where(lo, vp, zero), ones_lo], axis=1),
                                  jnp.concatenate([jnp.where(lo, zero, vp), ones_hi], axis=1)], axis=0)
            w2 = jnp.concatenate([mid[u, p, 0][0], mid[u, p, 1][0]], axis=1)
            nd = _dot(w2, vv)
            nat = nat_scr[chunk_rows(cs[u]), :]
            nb = [jnp.broadcast_to(nat[:, H_MLSTM + 2 * p + hh:H_MLSTM + 2 * p + hh + 1], (L, LANES))
                  for hh in range(2)]
            fs = early[u, p][1]
            mu = jnp.where(lo, mid[u, p, 0][1], mid[u, p, 1][1])
            m_prev = jnp.where(lo[0:1], mid[u, p, 0][2], mid[u, p, 1][2])
            w_inter = jnp.exp2(m_prev - mu)
            den = nd[:, LANES:] + w_inter * fs[:, LANES:]
            den = jnp.maximum(jnp.abs(den), jnp.exp2(jnp.where(lo, nb[0], nb[1]) - mu))
            h_scr[d, chunk_rows(cs[u]), lanes] = (nd[:, :LANES] + w_inter * fs[:, :LANES]) / den
        return carry

    lax.fori_loop(0, bb * nc // grp, pass_c, 0)

    @pl.when(d == 1)
    def _finish():
        def rows_block(c, carry):
            sq, rows = tokens(c)
            flat = chunk_rows(c)
            for p in range(NP):
                lanes = slice(p * LANES, (p + 1) * LANES)
                hn = _pair_rms(h_scr[0, flat, lanes] + h_scr[1, flat, lanes], on_ref[:, lanes])
                out_ref[sq, rows, lanes] = (_sigmoid(ob_ref[sq, rows, lanes]) * hn).astype(out_ref.dtype)
            return carry

        lax.fori_loop(0, bb * nc, rows_block, 0, unroll=4)


def _mlstm(q, k, v, gates, ob, state, out_norm, *, name):
    assert L_CHUNK == LANES
    B, S, W = q.shape
    nc = S // L_CHUNK
    bb = max(1, min(B, MLSTM_STEP_CHUNKS // nc))
    ncb, sb = bb * nc, bb * S
    grp = min(MLSTM_GROUP, ncb)
    grp_a = min(MLSTM_GROUP_A, ncb)
    npair = H_MLSTM // 2
    seq = lambda b, d: (b, 0, 0)
    est = (2 * (3 * sb * W * 2 + sb * LANES * 4 + sb * W * 4 + sb * W * 2) + 2 * sb * W * 4 + sb * LANES * 4
           + ncb * npair * LANES * 2 * LANES * 4 + 12 * 1024 * 1024)
    state_specs = [pl.BlockSpec((bb, 1, npair, LANES, 2 * LANES), lambda b, d: (b, d, 0, 0, 0)),
                   pl.BlockSpec((bb, 1, 8, LANES), lambda b, d: (b, d, 0, 0))]
    has_state = state is not None
    return pl.pallas_call(
        functools.partial(_mlstm_kernel, nc=nc, grp_a=grp_a, grp=grp, has_state=has_state),
        grid=(B // bb, 2),
        in_specs=[pl.BlockSpec((bb, S, W), seq), pl.BlockSpec((bb, S, W), seq), pl.BlockSpec((bb, S, W), seq),
                  pl.BlockSpec((bb, S, LANES), lambda b, d: (b, 0, d)),
                  pl.BlockSpec((bb, S, W), seq)] + (state_specs if has_state else [])
                 + [pl.BlockSpec((1, W), lambda b, d: (0, 0))],
        out_specs=[pl.BlockSpec((bb, S, W), seq),
                   pl.BlockSpec((bb, 1, npair, LANES, 2 * LANES), lambda b, d: (b, d, 0, 0, 0)),
                   pl.BlockSpec((bb, 1, 8, LANES), lambda b, d: (b, d, 0, 0))],
        out_shape=[jax.ShapeDtypeStruct((B, S, W), BF16),
                   jax.ShapeDtypeStruct((B, 2, npair, LANES, 2 * LANES), F32),
                   jax.ShapeDtypeStruct((B, 2, 8, LANES), F32)],
        scratch_shapes=[pltpu.VMEM((2, sb, W), F32),
                        pltpu.VMEM((sb, LANES), F32),
                        pltpu.VMEM((ncb, 8, L_CHUNK), F32),
                        pltpu.VMEM((ncb, 8, LANES), F32),
                        pltpu.VMEM((ncb, 8, LANES), F32),
                        pltpu.VMEM((ncb, npair, LANES, 2 * LANES), F32),
                        pltpu.VMEM((npair, LANES, 2 * LANES), F32)],
        compiler_params=pltpu.CompilerParams(dimension_semantics=("arbitrary", "arbitrary"),
                                             vmem_limit_bytes=_vmem_limit(est)),
        name=name,
    )(q, k, v, gates, ob, *(state if has_state else ()), out_norm)


def _outproj_kernel(a_ref, b_ref, c_ref, x_ref, mod_ref, w_ref, gpost_ref, gpre_ref, x1_ref, h2_ref):
    mod = mod_ref[0]
    gt1 = mod[:, 2 * D_MODEL:3 * D_MODEL]
    sh2 = mod[:, 3 * D_MODEL:4 * D_MODEL]
    sc2 = mod[:, 4 * D_MODEL:5 * D_MODEL]
    tr = x_ref.shape[0] // ROW_SPLIT
    pieces = [pl.ds(s * tr, tr) for s in range(ROW_SPLIT)]
    mos = [_dot(jnp.concatenate([a_ref[r, :], b_ref[r, :], c_ref[r, :]], axis=1), w_ref[...]) for r in pieces]
    for r, mo in zip(pieces, mos):
        x1 = x_ref[r, :] + gt1 * _rms(mo, gpost_ref[...])
        x1_ref[r, :] = x1
        h2_ref[r, :] = (_rms(x1, gpre_ref[...]) * (1.0 + sc2) + sh2).astype(h2_ref.dtype)


def _outproj(oa, ob, oc, x, mods, w_out, g_post, g_pre, *, rows_per_cond, name):
    T = x.shape[0]
    tm = TM_PROJ
    bpc = rows_per_cond // tm
    row = lambda i: (i, 0)
    const = lambda i: (0, 0)
    est = 2 * (tm * D_MODEL * (2 + 4 + 4 + 2) + D_MODEL * D_MODEL * 2) + 4 * tm * D_MODEL * 4
    return pl.pallas_call(
        _outproj_kernel,
        grid=(T // tm,),
        in_specs=[pl.BlockSpec((tm, W_ATTN), row), pl.BlockSpec((tm, W_MLSTM), row),
                  pl.BlockSpec((tm, W_NBHD), row), pl.BlockSpec((tm, D_MODEL), row),
                  pl.BlockSpec((1, 1, N_MOD * D_MODEL), lambda i: (i // bpc, 0, 0)),
                  pl.BlockSpec((D_MODEL, D_MODEL), const),
                  pl.BlockSpec((1, D_MODEL), const), pl.BlockSpec((1, D_MODEL), const)],
        out_specs=[pl.BlockSpec((tm, D_MODEL), row), pl.BlockSpec((tm, D_MODEL), row)],
        out_shape=[jax.ShapeDtypeStruct((T, D_MODEL), F32), jax.ShapeDtypeStruct((T, D_MODEL), BF16)],
        compiler_params=pltpu.CompilerParams(dimension_semantics=("arbitrary",),
                                             vmem_limit_bytes=_vmem_limit(est)),
        name=name,
    )(oa, ob, oc, x, mods, w_out, g_post, g_pre)


def _ffn_kernel(h_ref, x_ref, mod_ref, wu_ref, wd_ref, g_ref, o_ref, acc_ref):
    j = pl.program_id(1)

    @pl.when(j == 0)
    def _zero():
        acc_ref[...] = jnp.zeros_like(acc_ref)

    u = jnp.maximum(_dot(h_ref[...], wu_ref[...]), 0.0)
    acc_ref[...] += _dot((u * u).astype(BF16), wd_ref[...])

    @pl.when(j == pl.num_programs(1) - 1)
    def _finish():
        gt2 = mod_ref[0][:, 5 * D_MODEL:6 * D_MODEL]
        o_ref[...] = x_ref[...] + gt2 * _rms(acc_ref[...], g_ref[...])


def _ffn(h2, x1, mods, w_up, w_down, g_post, *, rows_per_cond, name):
    T = x1.shape[0]
    tm, tf = TM_FFN, TF_FFN
    tm = min(tm, rows_per_cond)
    bpc = rows_per_cond // tm
    est = 2 * (tm * D_MODEL * (2 + 4 + 4) + 2 * D_MODEL * tf * 2) + tm * D_MODEL * 4 + 3 * tm * tf * 4
    return pl.pallas_call(
        _ffn_kernel,
        grid=(T // tm, D_FF // tf),
        in_specs=[pl.BlockSpec((tm, D_MODEL), lambda i, j: (i, 0)),
                  pl.BlockSpec((tm, D_MODEL), lambda i, j: (i, 0)),
                  pl.BlockSpec((1, 1, N_MOD * D_MODEL), lambda i, j: (i // bpc, 0, 0)),
                  pl.BlockSpec((D_MODEL, tf), lambda i, j: (0, j)),
                  pl.BlockSpec((tf, D_MODEL), lambda i, j: (j, 0)),
                  pl.BlockSpec((1, D_MODEL), lambda i, j: (0, 0))],
        out_specs=pl.BlockSpec((tm, D_MODEL), lambda i, j: (i, 0)),
        out_shape=jax.ShapeDtypeStruct((T, D_MODEL), F32),
        scratch_shapes=[pltpu.VMEM((tm, D_MODEL), F32)],
        compiler_params=pltpu.CompilerParams(dimension_semantics=("arbitrary", "arbitrary"),
                                             vmem_limit_bytes=_vmem_limit(est)),
        name=name,
    )(h2, x1, mods, w_up, w_down, g_post)


def _pad_w_in(w_in_l):
    o = W_ATTN + 2 * W_KV + 4 * W_MLSTM
    pre, gates, post = w_in_l[:, :o], w_in_l[:, o:o + N_GATES], w_in_l[:, o + N_GATES:]
    z = jnp.zeros((D_MODEL, LANES - 2 * H_MLSTM), w_in_l.dtype)
    return jnp.concatenate([pre, gates[:, :2 * H_MLSTM], z, gates[:, 2 * H_MLSTM:], z, post],
                           axis=1).astype(BF16)


def _pad_gate_bias(gb_l):
    z = jnp.zeros((LANES - 2 * H_MLSTM,), gb_l.dtype)
    return jnp.concatenate([gb_l[:2 * H_MLSTM], z, gb_l[2 * H_MLSTM:], z]).reshape(1, 2 * LANES)


def _rope_tables(S):
    quarter = HEAD_DIM // 4
    pos = np.arange(S)
    inv_freq = np.float32(ROPE_THETA) ** (-np.arange(quarter, dtype=np.float32) / np.float32(quarter))

    def tabs(p):
        ang = p.astype(np.float32)[:, None] * inv_freq[None, :]
        return np.cos(ang), np.sin(ang)

    cr, sr = tabs(pos // GRID_W)
    cc, sc = tabs(pos % GRID_W)
    cos = np.concatenate([cr, cr, cc, cc], axis=1)
    sin = np.concatenate([-sr, sr, -sc, sc], axis=1)
    return jnp.asarray(np.tile(cos, (1, 2)), F32), jnp.asarray(np.tile(sin, (1, 2)), F32)


def _pack_state(C, n, m):
    B = C.shape[0]
    Cp = C.reshape(B, 2, 2, 2, HEAD_DIM, HEAD_DIM)
    z = jnp.zeros_like(Cp[:, :, :, 0])
    top = jnp.concatenate([Cp[:, :, :, 0], z], axis=-1)
    bot = jnp.concatenate([z, Cp[:, :, :, 1]], axis=-1)
    Cbd = jnp.concatenate([top, bot], axis=-2)
    n_rep = jnp.broadcast_to(n.reshape(B, 2, 2, LANES, 1), (B, 2, 2, LANES, LANES))
    same_head = (jnp.arange(LANES)[:, None] < HEAD_DIM) == (jnp.arange(LANES)[None, :] < HEAD_DIM)
    n_rep = jnp.where(same_head, n_rep, 0.0)
    m_rows = jnp.broadcast_to(m[..., None], m.shape + (LANES,))
    return jnp.concatenate([Cbd, n_rep], axis=-1), jnp.concatenate([m_rows, m_rows], axis=-2)


def _unpack_state(s_p, m_p):
    B = s_p.shape[0]
    c_even = s_p[:, :, :, :HEAD_DIM, :HEAD_DIM]
    c_odd = s_p[:, :, :, HEAD_DIM:, HEAD_DIM:LANES]
    C = jnp.stack([c_even, c_odd], axis=3).reshape(B, 2, H_MLSTM, HEAD_DIM, HEAD_DIM)
    n = jnp.concatenate([s_p[..., :HEAD_DIM, LANES], s_p[..., HEAD_DIM:, LANES + HEAD_DIM]], axis=-1)
    return C, n.reshape(B, 2, H_MLSTM, HEAD_DIM), m_p[:, :, :H_MLSTM, 0]


def _layer(x, mods, lw, *, B, S, cond_rows, rope_tabs, ctx_cache, state, name):
    T = B * S
    kv_dtype = BF16 if ctx_cache is not None else F32
    ck_a = min(CK_ATTN, S)
    pr = _inproj(x, mods, lw["g_pre_mix"], lw["w_in"], lw["q_norm"], lw["k_norm"], lw["gate_bias"],
                 rope_tabs, rows_per_cond=cond_rows, kv_dtype=kv_dtype,
                 vt_blocks=(ck_a, NB_KV_BLOCK), name=name + "_inproj")
    seq = lambda a: a.reshape(B, S, a.shape[-1])
    qa, ka, va = seq(pr["qa"]), seq(pr["ka"]), seq(pr["va"])
    qc, kc, vc = seq(pr["qc"]), seq(pr["kc"]), seq(pr["vc"])
    chunks = lambda a, n: a.reshape(B, S // n, n, a.shape[-1])
    vat = pr["vat"].reshape(B, S // ck_a, W_KV, ck_a)
    vct = pr["vct"].reshape(B, S // NB_KV_BLOCK, W_NBHD, NB_KV_BLOCK)
    if ctx_cache is None:
        out_a = _attention(qa, [(chunks(ka, ck_a), vat)], heads=_HEADS_GQA, name=name + "_attn_a")
        out_c = _attention(qc, [(chunks(kc, NB_KV_BLOCK), vct)], heads=_HEADS_MHA, name=name + "_attn_c")
    else:
        ck_c, cv_c = ctx_cache[2], ctx_cache[3]
        P = ck_c.shape[1]
        assert P == ck_a
        cva_t, cvc_t = _cached_values_t(ctx_cache[1], cv_c, name=name + "_cache_t")
        out_a = _attention(qa, [(chunks(ka, ck_a), vat),
                                (ctx_cache[0].reshape(B, 1, P, W_KV), cva_t.reshape(B, 1, W_KV, P))],
                           heads=_HEADS_GQA, name=name + "_attn_a")
        out_c = _nbhd_attention(qc, chunks(kc, NB_KV_BLOCK), vct, ck_c, cvc_t,
                                lw["rel_bias"], name=name + "_attn_c")
    out_b, sf, mf = _mlstm(seq(pr["qb"]), seq(pr["kb"]), seq(pr["vb"]), seq(pr["g"]), seq(pr["ob"]),
                           state, lw["out_norm"], name=name + "_mlstm")
    x1, h2 = _outproj(out_a.reshape(T, W_ATTN), out_b.reshape(T, W_MLSTM), out_c.reshape(T, W_NBHD),
                      x, mods, lw["w_out"], lw["g_post_mix"], lw["g_pre_ffn"],
                      rows_per_cond=cond_rows, name=name + "_outproj")
    x2 = _ffn(h2, x1, mods, lw["w_up"], lw["w_down"], lw["g_post_ffn"],
              rows_per_cond=cond_rows, name=name + "_ffn")
    return x2, (ka, va, kc, vc, sf, mf)


def kernel(x_prompt, x_sample, c, cache_k_attn, cache_v_attn, cache_k_nbhd, cache_v_nbhd, state_mlstm_C, state_mlstm_n, state_mlstm_m, c_ctx, w_ada, b_ada, g_pre_mix, g_post_mix, g_pre_ffn, g_post_ffn, w_in, q_norm_attn, k_norm_attn, mlstm_gate_bias, mlstm_out_norm, nbhd_rel_bias, w_out, w_ffn_up, w_ffn_down):
    Bc, Sc, _ = x_prompt.shape
    Bl, Sl, _ = x_sample.shape
    P = cache_k_attn.shape[2]
    n_cond = 8
    cond = jnp.concatenate([c_ctx[None, :], c, jnp.zeros((n_cond - 1 - Bl, D_MODEL), F32)], axis=0)
    mods_all = _modulation(cond, w_ada, b_ada)

    layers = []
    for l in range(DEPTH):
        layers.append(dict(
            w_in=_pad_w_in(w_in[l]),
            w_out=w_out[l].astype(BF16),
            w_up=w_ffn_up[l].astype(BF16),
            w_down=w_ffn_down[l].astype(BF16),
            g_pre_mix=g_pre_mix[l].reshape(1, D_MODEL), g_post_mix=g_post_mix[l].reshape(1, D_MODEL),
            g_pre_ffn=g_pre_ffn[l].reshape(1, D_MODEL), g_post_ffn=g_post_ffn[l].reshape(1, D_MODEL),
            q_norm=jnp.tile(q_norm_attn[l], 2).reshape(1, LANES),
            k_norm=jnp.tile(k_norm_attn[l], 2).reshape(1, LANES),
            gate_bias=_pad_gate_bias(mlstm_gate_bias[l]),
            out_norm=mlstm_out_norm[l].reshape(1, W_MLSTM),
            rel_bias=nbhd_rel_bias[l].reshape(-1),
        ))

    xp = x_prompt.reshape(Bc * Sc, D_MODEL)
    ctx = []
    for l in range(DEPTH):
        mods = mods_all[l, 0:1].reshape(1, 1, N_MOD * D_MODEL)
        xp, extras = _layer(xp, mods, layers[l], B=Bc, S=Sc, cond_rows=Bc * Sc, rope_tabs=None,
                            ctx_cache=None, state=None, name=f"ctx{l}")
        ctx.append(extras)
    new_k_attn = jnp.stack([e[0].reshape(Bc, Sc, KV_ATTN, HEAD_DIM) for e in ctx], axis=1)
    new_v_attn = jnp.stack([e[1].reshape(Bc, Sc, KV_ATTN, HEAD_DIM) for e in ctx], axis=1)
    new_k_nbhd = jnp.stack([e[2].reshape(Bc, Sc, H_NBHD, HEAD_DIM) for e in ctx], axis=1)
    new_v_nbhd = jnp.stack([e[3].reshape(Bc, Sc, H_NBHD, HEAD_DIM) for e in ctx], axis=1)
    states = [_unpack_state(e[4], e[5]) for e in ctx]
    new_C = jnp.stack([s[0] for s in states], axis=1)
    new_n = jnp.stack([s[1] for s in states], axis=1)
    new_m = jnp.stack([s[2] for s in states], axis=1)

    xs = x_sample.reshape(Bl * Sl, D_MODEL)
    rope_tabs = _rope_tables(Sl)
    for l in range(DEPTH):
        mods = mods_all[l, 1:1 + Bl].reshape(Bl, 1, N_MOD * D_MODEL)
        cache = (cache_k_attn[:, l].reshape(Bl, P, W_KV), cache_v_attn[:, l].reshape(Bl, P, W_KV),
                 cache_k_nbhd[:, l].reshape(Bl, P, W_NBHD), cache_v_nbhd[:, l].reshape(Bl, P, W_NBHD))
        state = _pack_state(state_mlstm_C[:, l], state_mlstm_n[:, l], state_mlstm_m[:, l])
        xs, _ = _layer(xs, mods, layers[l], B=Bl, S=Sl, cond_rows=Sl, rope_tabs=rope_tabs,
                       ctx_cache=cache, state=state, name=f"lat{l}")

    return (xp.reshape(Bc, Sc, D_MODEL), xs.reshape(Bl, Sl, D_MODEL),
            new_k_attn, new_v_attn, new_k_nbhd, new_v_nbhd, new_C, new_n, new_m)
```
